```python
import jax, jax.numpy as jnp
from jax import lax
import numpy as np

D_MODEL = 1024
BATCH = 8
SEQ = 4096
DEPTH = 4

N_MIXERS = 2
N_LAYERS_A = (DEPTH + 1) // 2
N_LAYERS_B = DEPTH // 2
MIX_WIDTH = D_MODEL
CONV_WIDTH = 3
CONV_GROUPS = 8
POOL_WINDOWS = (2, 4, 8, 16)
N_POOL_GROUPS = len(POOL_WINDOWS)
POOL_GROUP_DIM = MIX_WIDTH // N_POOL_GROUPS
PLE_DIM = 256
EPS = 1e-6

kernel_name = "hybrid_shortconv_pool_ple_trunk"


def rmsnorm(x, g):
    xf = x.astype(jnp.float32)
    r = lax.rsqrt(jnp.mean(xf * xf, axis=-1, keepdims=True) + EPS)
    return (xf * r).astype(x.dtype) * g


def causal_conv3(u, w):
    s = u.shape[1]
    up = jnp.pad(u, ((0, 0), (CONV_WIDTH - 1, 0), (0, 0)))
    return up[:, 0:s] * w[0] + up[:, 1:s + 1] * w[1] + up[:, 2:s + 2] * w[2]


def short_conv_mixer(h, w_in, w_conv, w_out):
    proj = h @ w_in
    b_g, c_g, v, z = jnp.split(proj, 4, axis=-1)
    y = b_g * causal_conv3(c_g * v, w_conv)
    return (jax.nn.silu(z) * y) @ w_out


def causal_window_mean(u, window):
    s = u.shape[1]
    cs = jnp.cumsum(u.astype(jnp.float32), axis=1)
    csp = jnp.pad(cs, ((0, 0), (window, 0), (0, 0)))
    win_sum = csp[:, window:window + s] - csp[:, 0:s]
    count = jnp.minimum(jnp.arange(1, s + 1, dtype=jnp.float32), float(window))
    return (win_sum / count[None, :, None]).astype(u.dtype)


def pool_mixer(h, w_in, w_grp, scale, w_out):
    proj = h @ w_in
    u, z = jnp.split(proj, 2, axis=-1)
    bsz, s, _ = u.shape
    u4 = u.reshape(bsz, s, N_POOL_GROUPS, POOL_GROUP_DIM)
    pooled = jnp.stack([causal_window_mean(u4[:, :, g], w) for g, w in enumerate(POOL_WINDOWS)], axis=2)
    d = pooled - u4
    mixed = jnp.einsum('bsgc,gcd->bsgd', d, w_grp).reshape(bsz, s, MIX_WIDTH) * scale
    return (jax.nn.silu(z) * mixed) @ w_out


def _fwd_setup_inputs(seed: int = 0) -> dict:
    key = jax.random.key(seed)
    ks = jax.random.split(key, 16)
    f32 = jnp.float32
    E, D, G = MIX_WIDTH, D_MODEL, POOL_GROUP_DIM
    nrm = lambda k, shape, fan_in: jax.random.normal(k, shape, f32) * (fan_in ** -0.5)
    gain = lambda k, shape: 1.0 + 0.02 * jax.random.normal(k, shape, f32)
    return {
        "x": jax.random.normal(ks[0], (BATCH, SEQ, D), f32),
        "p": jax.random.normal(ks[1], (DEPTH, BATCH, SEQ, PLE_DIM), f32),
        "norm_mix": gain(ks[2], (DEPTH, D)),
        "a_w_in": nrm(ks[3], (N_LAYERS_A, D, 4 * E), D),
        "a_w_conv": nrm(ks[4], (N_LAYERS_A, CONV_WIDTH, E), CONV_WIDTH),
        "a_w_out": nrm(ks[5], (N_LAYERS_A, E, D), E),
        "b_w_in": nrm(ks[6], (N_LAYERS_B, D, 2 * E), D),
        "b_w_grp": nrm(ks[7], (N_LAYERS_B, N_POOL_GROUPS, G, G), G),
        "b_scale": gain(ks[8], (N_LAYERS_B, E)),
        "b_w_out": nrm(ks[9], (N_LAYERS_B, E, D), E),
        "ple_norm": gain(ks[10], (DEPTH, D)),
        "ple_w_gate": nrm(ks[11], (DEPTH, D, D), D),
        "ple_w_proj": nrm(ks[12], (DEPTH, PLE_DIM, D), PLE_DIM),
        "final_norm": gain(ks[13], (D,)),
    }


def _fwd_reference(x, p, norm_mix, a_w_in, a_w_conv, a_w_out, b_w_in, b_w_grp, b_scale, b_w_out,
              ple_norm, ple_w_gate, ple_w_proj, final_norm):
    h = x
    for i in range(DEPTH):
        hn = rmsnorm(h, norm_mix[i])
        j = i // N_MIXERS
        if i % N_MIXERS == 0:
            h = h + short_conv_mixer(hn, a_w_in[j], a_w_conv[j], a_w_out[j])
        else:
            h = h + pool_mixer(hn, b_w_in[j], b_w_grp[j], b_scale[j], b_w_out[j])
        gate = jax.nn.sigmoid(rmsnorm(h, ple_norm[i]) @ ple_w_gate[i])
        h = h + gate * (p[i] @ ple_w_proj[i])
    return rmsnorm(h, final_norm)


import jax as _jax
import jax.numpy as _jnp

TWIN_FORMAT = 'train_step'
FWD_PARAMS = ['x', 'p', 'norm_mix', 'a_w_in', 'a_w_conv', 'a_w_out', 'b_w_in', 'b_w_grp', 'b_scale', 'b_w_out', 'ple_norm', 'ple_w_gate', 'ple_w_proj', 'final_norm']
TWIN_WEIGHTS = ['norm_mix', 'a_w_in', 'a_w_conv', 'a_w_out', 'b_w_in', 'b_w_grp', 'b_scale', 'b_w_out', 'ple_norm', 'ple_w_gate', 'ple_w_proj', 'final_norm']
TWIN_DIFF_INPUT = 'x'
TWIN_INPUTS = ['x', 'p', 'norm_mix', 'a_w_in', 'a_w_conv', 'a_w_out', 'b_w_in', 'b_w_grp', 'b_scale', 'b_w_out', 'ple_norm', 'ple_w_gate', 'ple_w_proj', 'final_norm', 'loss_target', 'm_norm_mix', 'm_a_w_in', 'm_a_w_conv', 'm_a_w_out', 'm_b_w_in', 'm_b_w_grp', 'm_b_scale', 'm_b_w_out', 'm_ple_norm', 'm_ple_w_gate', 'm_ple_w_proj', 'm_final_norm', 'v_norm_mix', 'v_a_w_in', 'v_a_w_conv', 'v_a_w_out', 'v_b_w_in', 'v_b_w_grp', 'v_b_scale', 'v_b_w_out', 'v_ple_norm', 'v_ple_w_gate', 'v_ple_w_proj', 'v_final_norm']
TWIN_OUTPUTS = ['loss', 'grad_x', 'grad_norm_mix', 'grad_a_w_in', 'grad_a_w_conv', 'grad_a_w_out', 'grad_b_w_in', 'grad_b_w_grp', 'grad_b_scale', 'grad_b_w_out', 'grad_ple_norm', 'grad_ple_w_gate', 'grad_ple_w_proj', 'grad_final_norm', 'delta_norm_mix', 'delta_a_w_in', 'delta_a_w_conv', 'delta_a_w_out', 'delta_b_w_in', 'delta_b_w_grp', 'delta_b_scale', 'delta_b_w_out', 'delta_ple_norm', 'delta_ple_w_gate', 'delta_ple_w_proj', 'delta_final_norm', 'new_m_norm_mix', 'new_m_a_w_in', 'new_m_a_w_conv', 'new_m_a_w_out', 'new_m_b_w_in', 'new_m_b_w_grp', 'new_m_b_scale', 'new_m_b_w_out', 'new_m_ple_norm', 'new_m_ple_w_gate', 'new_m_ple_w_proj', 'new_m_final_norm', 'new_v_norm_mix', 'new_v_a_w_in', 'new_v_a_w_conv', 'new_v_a_w_out', 'new_v_b_w_in', 'new_v_b_w_grp', 'new_v_b_scale', 'new_v_b_w_out', 'new_v_ple_norm', 'new_v_ple_w_gate', 'new_v_ple_w_proj', 'new_v_final_norm']
TWIN_LEAF_KINDS = {'loss': 'loss', 'grad_x': 'grad_x', 'grad_norm_mix': 'grad_w', 'grad_a_w_in': 'grad_w', 'grad_a_w_conv': 'grad_w', 'grad_a_w_out': 'grad_w', 'grad_b_w_in': 'grad_w', 'grad_b_w_grp': 'grad_w', 'grad_b_scale': 'grad_w', 'grad_b_w_out': 'grad_w', 'grad_ple_norm': 'grad_w', 'grad_ple_w_gate': 'grad_w', 'grad_ple_w_proj': 'grad_w', 'grad_final_norm': 'grad_w', 'delta_norm_mix': 'delta_w', 'delta_a_w_in': 'delta_w', 'delta_a_w_conv': 'delta_w', 'delta_a_w_out': 'delta_w', 'delta_b_w_in': 'delta_w', 'delta_b_w_grp': 'delta_w', 'delta_b_scale': 'delta_w', 'delta_b_w_out': 'delta_w', 'delta_ple_norm': 'delta_w', 'delta_ple_w_gate': 'delta_w', 'delta_ple_w_proj': 'delta_w', 'delta_final_norm': 'delta_w', 'new_m_norm_mix': 'new_m', 'new_m_a_w_in': 'new_m', 'new_m_a_w_conv': 'new_m', 'new_m_a_w_out': 'new_m', 'new_m_b_w_in': 'new_m', 'new_m_b_w_grp': 'new_m', 'new_m_b_scale': 'new_m', 'new_m_b_w_out': 'new_m', 'new_m_ple_norm': 'new_m', 'new_m_ple_w_gate': 'new_m', 'new_m_ple_w_proj': 'new_m', 'new_m_final_norm': 'new_m', 'new_v_norm_mix': 'new_v', 'new_v_a_w_in': 'new_v', 'new_v_a_w_conv': 'new_v', 'new_v_a_w_out': 'new_v', 'new_v_b_w_in': 'new_v', 'new_v_b_w_grp': 'new_v', 'new_v_b_scale': 'new_v', 'new_v_b_w_out': 'new_v', 'new_v_ple_norm': 'new_v', 'new_v_ple_w_gate': 'new_v', 'new_v_ple_w_proj': 'new_v', 'new_v_final_norm': 'new_v'}


def _forward(args):
    return _fwd_reference(*[args[k] for k in FWD_PARAMS])


def _output_shape():
    out = _jax.eval_shape(lambda: _forward(_fwd_setup_inputs(0)))
    return out.shape, out.dtype

N_MICROBATCH = 1
ADAM_LR = 0.001
ADAM_B1 = 0.9
ADAM_B2 = 0.999
ADAM_EPS = 1e-08
ADAM_WD = 0.01
ADAM_STEP = 10
PER_EXAMPLE_BATCH_AXIS = {'x': 0, 'p': 1, 'loss_target': 0}
SHARED_INPUTS = []
_WEIGHT_DTYPES = {'norm_mix': _jnp.float32, 'a_w_in': _jnp.float32, 'a_w_conv': _jnp.float32, 'a_w_out': _jnp.float32, 'b_w_in': _jnp.float32, 'b_w_grp': _jnp.float32, 'b_scale': _jnp.float32, 'b_w_out': _jnp.float32, 'ple_norm': _jnp.float32, 'ple_w_gate': _jnp.float32, 'ple_w_proj': _jnp.float32, 'final_norm': _jnp.float32}
MOMENT_SCALE = {'norm_mix': 1.373024e-01, 'a_w_in': 8.383467e-02, 'a_w_conv': 8.180741e-02, 'a_w_out': 8.189016e-02, 'b_w_in': 6.365100e-02, 'b_w_grp': 6.250405e-02, 'b_scale': 6.374404e-02, 'b_w_out': 6.243455e-02, 'ple_norm': 2.724649e-02, 'ple_w_gate': 2.701956e-02, 'ple_w_proj': 6.929007e-02, 'final_norm': 3.200452e+01}


def _to_microbatches(a, axis):
    t = _jnp.moveaxis(a, axis, 0)
    t = t.reshape((N_MICROBATCH, t.shape[0] // N_MICROBATCH) + t.shape[1:])
    return _jnp.moveaxis(t, 1, axis + 1)


def setup_inputs(seed: int = 0) -> dict:
    inp = _fwd_setup_inputs(seed)
    key = _jax.random.fold_in(_jax.random.key(seed), 7919)
    shape, _ = _output_shape()
    out = dict(inp)
    out["loss_target"] = _jax.random.normal(_jax.random.fold_in(key, 0), shape, _jnp.float32)
    for i, name in enumerate(TWIN_WEIGHTS):
        w = inp[name].astype(_jnp.float32)
        if MOMENT_SCALE is None:
            s = _jnp.sqrt(_jnp.mean(_jnp.square(w)) + 1e-30)
        else:
            s = MOMENT_SCALE[name]
        km, kv = _jax.random.split(_jax.random.fold_in(key, i + 1))
        out[name] = w
        out["m_" + name] = s * _jax.random.normal(km, w.shape, _jnp.float32)
        out["v_" + name] = (s * s) * _jax.random.uniform(kv, w.shape, _jnp.float32, 0.5, 1.5)
    if N_MICROBATCH > 1:
        for name, axis in PER_EXAMPLE_BATCH_AXIS.items():
            out[name] = _to_microbatches(out[name], axis)
    return {'x': out['x'], 'p': out['p'], 'norm_mix': out['norm_mix'], 'a_w_in': out['a_w_in'], 'a_w_conv': out['a_w_conv'], 'a_w_out': out['a_w_out'], 'b_w_in': out['b_w_in'], 'b_w_grp': out['b_w_grp'], 'b_scale': out['b_scale'], 'b_w_out': out['b_w_out'], 'ple_norm': out['ple_norm'], 'ple_w_gate': out['ple_w_gate'], 'ple_w_proj': out['ple_w_proj'], 'final_norm': out['final_norm'], 'loss_target': out['loss_target'], 'm_norm_mix': out['m_norm_mix'], 'm_a_w_in': out['m_a_w_in'], 'm_a_w_conv': out['m_a_w_conv'], 'm_a_w_out': out['m_a_w_out'], 'm_b_w_in': out['m_b_w_in'], 'm_b_w_grp': out['m_b_w_grp'], 'm_b_scale': out['m_b_scale'], 'm_b_w_out': out['m_b_w_out'], 'm_ple_norm': out['m_ple_norm'], 'm_ple_w_gate': out['m_ple_w_gate'], 'm_ple_w_proj': out['m_ple_w_proj'], 'm_final_norm': out['m_final_norm'], 'v_norm_mix': out['v_norm_mix'], 'v_a_w_in': out['v_a_w_in'], 'v_a_w_conv': out['v_a_w_conv'], 'v_a_w_out': out['v_a_w_out'], 'v_b_w_in': out['v_b_w_in'], 'v_b_w_grp': out['v_b_w_grp'], 'v_b_scale': out['v_b_scale'], 'v_b_w_out': out['v_b_w_out'], 'v_ple_norm': out['v_ple_norm'], 'v_ple_w_gate': out['v_ple_w_gate'], 'v_ple_w_proj': out['v_ple_w_proj'], 'v_final_norm': out['v_final_norm']}


def _loss(weights, diff, rest, loss_target):
    with _jax.named_scope("forward"):
        args = {**rest, TWIN_DIFF_INPUT: diff, **{k: w.astype(_WEIGHT_DTYPES[k]) for k, w in weights.items()}}
        y = _forward(args)
    with _jax.named_scope("loss_head"):
        err = _jnp.square(y.astype(_jnp.float32) - loss_target)
        return 0.5 * _jnp.sum(_jnp.mean(err, axis=-1)) if err.ndim else 0.5 * err


def _adamw(w, g, m, v):
    m = ADAM_B1 * m + (1.0 - ADAM_B1) * g
    v = ADAM_B2 * v + (1.0 - ADAM_B2) * _jnp.square(g)
    m_hat = m / (1.0 - ADAM_B1 ** ADAM_STEP)
    v_hat = v / (1.0 - ADAM_B2 ** ADAM_STEP)
    delta = -ADAM_LR * (m_hat / (_jnp.sqrt(v_hat) + ADAM_EPS) + ADAM_WD * w)
    return delta, m, v


def reference(x, p, norm_mix, a_w_in, a_w_conv, a_w_out, b_w_in, b_w_grp, b_scale, b_w_out, ple_norm, ple_w_gate, ple_w_proj, final_norm, loss_target, m_norm_mix, m_a_w_in, m_a_w_conv, m_a_w_out, m_b_w_in, m_b_w_grp, m_b_scale, m_b_w_out, m_ple_norm, m_ple_w_gate, m_ple_w_proj, m_final_norm, v_norm_mix, v_a_w_in, v_a_w_conv, v_a_w_out, v_b_w_in, v_b_w_grp, v_b_scale, v_b_w_out, v_ple_norm, v_ple_w_gate, v_ple_w_proj, v_final_norm):
    given = dict(x=x, p=p, norm_mix=norm_mix, a_w_in=a_w_in, a_w_conv=a_w_conv, a_w_out=a_w_out, b_w_in=b_w_in, b_w_grp=b_w_grp, b_scale=b_scale, b_w_out=b_w_out, ple_norm=ple_norm, ple_w_gate=ple_w_gate, ple_w_proj=ple_w_proj, final_norm=final_norm, loss_target=loss_target, m_norm_mix=m_norm_mix, m_a_w_in=m_a_w_in, m_a_w_conv=m_a_w_conv, m_a_w_out=m_a_w_out, m_b_w_in=m_b_w_in, m_b_w_grp=m_b_w_grp, m_b_scale=m_b_scale, m_b_w_out=m_b_w_out, m_ple_norm=m_ple_norm, m_ple_w_gate=m_ple_w_gate, m_ple_w_proj=m_ple_w_proj, m_final_norm=m_final_norm, v_norm_mix=v_norm_mix, v_a_w_in=v_a_w_in, v_a_w_conv=v_a_w_conv, v_a_w_out=v_a_w_out, v_b_w_in=v_b_w_in, v_b_w_grp=v_b_w_grp, v_b_scale=v_b_scale, v_b_w_out=v_b_w_out, v_ple_norm=v_ple_norm, v_ple_w_gate=v_ple_w_gate, v_ple_w_proj=v_ple_w_proj, v_final_norm=v_final_norm)
    weights = {n: given[n] for n in TWIN_WEIGHTS}
    shared = {n: given[n] for n in SHARED_INPUTS}
    per_example = {n: given[n] for n in ['x', 'p']}
    grad_fn = _jax.value_and_grad(_loss, argnums=(0, 1))

    def one_microbatch(ex, loss_target):
        ex = dict(ex)
        diff = ex.pop(TWIN_DIFF_INPUT)
        return grad_fn(weights, diff, {**shared, **ex}, loss_target)

    if N_MICROBATCH == 1:
        loss, (grad_w, grad_x) = one_microbatch(per_example, given["loss_target"])
    else:
        def body(carry, xs):
            loss_sum, grad_sum = carry
            l_k, (gw_k, gx_k) = one_microbatch(xs[0], xs[1])
            with _jax.named_scope("update"):
                return (loss_sum + l_k, _jax.tree.map(_jnp.add, grad_sum, gw_k)), gx_k

        init = (_jnp.zeros((), _jnp.float32), _jax.tree.map(_jnp.zeros_like, weights))
        (loss, grad_w), grad_x = _jax.lax.scan(body, init, (per_example, given["loss_target"]))
    with _jax.named_scope("update"):
        delta_w, new_m, new_v = {}, {}, {}
        for n in TWIN_WEIGHTS:
            delta_w[n], new_m[n], new_v[n] = _adamw(weights[n], grad_w[n], given["m_" + n], given["v_" + n])
    return (loss, grad_x, *[grad_w[n] for n in TWIN_WEIGHTS], *[delta_w[n] for n in TWIN_WEIGHTS],
            *[new_m[n] for n in TWIN_WEIGHTS], *[new_v[n] for n in TWIN_WEIGHTS])
```

```python
import functools

import jax
import jax.numpy as jnp
from jax import lax
from jax.experimental import pallas as pl
from jax.experimental.pallas import tpu as pltpu

F32 = jnp.float32
BF16 = jnp.bfloat16
MESH = pl.DeviceIdType.MESH

RMS_EPS = 1e-6
POOL_WINDOWS = (2, 4, 8, 16)
N_POOL_GROUPS = len(POOL_WINDOWS)
ADAM_LR = 0.001
ADAM_B1 = 0.9
ADAM_B2 = 0.999
ADAM_EPS = 1e-08
ADAM_WD = 0.01
ADAM_STEP = 10
N_DEV = 8

HALO = 16
ROW_TILE = 512
BWD_ROW_TILE = 256
MID_CHUNK = 256
VMEM_LIMIT = 56 * 1024 * 1024


def _cparams(*sem):
    return pltpu.CompilerParams(dimension_semantics=sem, vmem_limit_bytes=VMEM_LIMIT)


def _dot(a, b):
    return jnp.dot(a, b, preferred_element_type=F32)


def _dot_nt(a, b):
    return lax.dot_general(a, b, (((1,), (1,)), ((), ())), preferred_element_type=F32)


def _dot_tn(a, b):
    return lax.dot_general(a, b, (((0,), (0,)), ((), ())), preferred_element_type=F32)


def _rms_stats(x):
    r = lax.rsqrt(jnp.mean(x * x, axis=-1, keepdims=True) + RMS_EPS)
    return x * r, r


def _rms_bwd(dy, xh, r, g):
    a = dy * g
    return r * (a - xh * jnp.mean(a * xh, axis=-1, keepdims=True))


def _sigmoid(x):
    return 1.0 / (1.0 + jnp.exp(-x))


def _shift_down(x, k):
    return pltpu.roll(x, k, 0)


def _shift_up(x, k):
    return pltpu.roll(x, x.shape[0] - k, 0)


def _norm_proj(h, g, w_all, layer, nsplit):
    s, d = h.shape
    n = w_all.shape[2]
    e = n // nsplit
    ts = min(ROW_TILE, s)

    def body(h_ref, g_ref, w_ref, proj_ref, hn_ref):
        xh, _ = _rms_stats(h_ref[...])
        hn = (xh * g_ref[...]).astype(BF16)
        hn_ref[...] = hn
        for k in range(nsplit):
            proj_ref[k] = _dot(hn, w_ref[:, k * e:(k + 1) * e]).astype(BF16)

    return pl.pallas_call(
        body, name="norm_proj",
        grid=(s // ts,),
        in_specs=[pl.BlockSpec((ts, d), lambda i: (i, 0)),
                  pl.BlockSpec((1, d), lambda i: (0, 0)),
                  pl.BlockSpec((None, d, n), lambda i: (layer, 0, 0))],
        out_specs=[pl.BlockSpec((nsplit, ts, e), lambda i: (0, i, 0)),
                   pl.BlockSpec((ts, d), lambda i: (i, 0))],
        out_shape=[jax.ShapeDtypeStruct((nsplit, s, e), BF16),
                   jax.ShapeDtypeStruct((s, d), BF16)],
        compiler_params=_cparams("arbitrary"),
    )(h, g, w_all)


def _conv_taps(u, u_prev, first):
    uu = jnp.concatenate([jnp.where(first, 0.0, u_prev), u], axis=0)
    return _shift_down(uu, 1)[HALO:], _shift_down(uu, 2)[HALO:]


def _mid_a_fwd(proj4, wconv_all, layer):
    _, s, e = proj4.shape
    cb = e // N_POOL_GROUPS
    rc = min(MID_CHUNK, s)

    def body(p_ref, w_ref, o_ref):
        w0, w1, w2 = w_ref[0:1, :], w_ref[1:2, :], w_ref[2:3, :]

        def chunk(i, carry):
            r0 = pl.multiple_of(i * rc, rc)
            p0 = pl.multiple_of(jnp.maximum(r0 - HALO, 0), HALO)
            ld = lambda k: p_ref[k, pl.ds(r0, rc), :].astype(F32)
            ldp = lambda k: p_ref[k, pl.ds(p0, HALO), :].astype(F32)
            u = ld(1) * ld(2)
            u1, u2 = _conv_taps(u, ldp(1) * ldp(2), i == 0)
            conv = w0 * u2 + w1 * u1 + w2 * u
            z = ld(3)
            o_ref[pl.ds(r0, rc), :] = ((z * _sigmoid(z)) * (ld(0) * conv)).astype(BF16)
            return carry

        lax.fori_loop(0, s // rc, chunk, 0)

    return pl.pallas_call(
        body, name="mid_a_fwd",
        grid=(e // cb,),
        in_specs=[pl.BlockSpec((4, s, cb), lambda c: (0, 0, c)),
                  pl.BlockSpec((None, 3, cb), lambda c: (layer, 0, c))],
        out_specs=pl.BlockSpec((s, cb), lambda c: (0, c)),
        out_shape=jax.ShapeDtypeStruct((s, e), BF16),
        compiler_params=_cparams("arbitrary"),
    )(proj4, wconv_all)


def _window_mean_minus(uu, row0, window):
    acc = uu
    span = 1
    while span < window:
        acc = acc + _shift_down(acc, span)
        span *= 2
    rows = uu.shape[0] - HALO
    t = row0 + lax.broadcasted_iota(jnp.int32, (rows, 1), 0)
    cnt = jnp.minimum(t + 1, window).astype(F32)
    return acc[HALO:] / cnt - uu[HALO:]


def _mid_b_fwd(proj2, wgrp_all, scale_all, layer):
    _, s, e = proj2.shape
    gdim = e // N_POOL_GROUPS
    rc = min(MID_CHUNK, s)

    def body(p_ref, w_ref, sc_ref, o_ref):
        grp = pl.program_id(0)

        def run(window):
            def chunk(i, carry):
                r0 = pl.multiple_of(i * rc, rc)
                p0 = pl.multiple_of(jnp.maximum(r0 - HALO, 0), HALO)
                u = p_ref[0, pl.ds(r0, rc), :].astype(F32)
                up = jnp.where(i == 0, 0.0, p_ref[0, pl.ds(p0, HALO), :].astype(F32))
                d = _window_mean_minus(jnp.concatenate([up, u], axis=0), r0, window)
                mixed = _dot(d.astype(BF16), w_ref[...]) * sc_ref[...]
                z = p_ref[1, pl.ds(r0, rc), :].astype(F32)
                o_ref[pl.ds(r0, rc), :] = ((z * _sigmoid(z)) * mixed).astype(BF16)
                return carry

            lax.fori_loop(0, s // rc, chunk, 0)

        for k, window in enumerate(POOL_WINDOWS):
            pl.when(grp == k)(functools.partial(run, window))

    return pl.pallas_call(
        body, name="mid_b_fwd",
        grid=(N_POOL_GROUPS,),
        in_specs=[pl.BlockSpec((2, s, gdim), lambda g: (0, 0, g)),
                  pl.BlockSpec((None, None, gdim, gdim), lambda g: (layer, g, 0, 0)),
                  pl.BlockSpec((None, 1, gdim), lambda g: (layer, 0, g))],
        out_specs=pl.BlockSpec((s, gdim), lambda g: (0, g)),
        out_shape=jax.ShapeDtypeStruct((s, e), BF16),
        compiler_params=_cparams("arbitrary"),
    )(proj2, wgrp_all, scale_all)


def _out_ple_fwd(h, o, wout_all, mixer_layer, pn, wgate_all, p_all, wproj_all, layer):
    s, d = h.shape
    e = o.shape[1]
    pdim = p_all.shape[2]
    ts = min(ROW_TILE, s)

    def body(h_ref, o_ref, wo_ref, pn_ref, wg_ref, p_ref, wp_ref, h1_ref, h2_ref, gl_ref, pp_ref):
        h1 = h_ref[...] + _dot(o_ref[...], wo_ref[...])
        h1_ref[...] = h1
        xh, _ = _rms_stats(h1)
        gl = _dot((xh * pn_ref[...]).astype(BF16), wg_ref[...])
        pp = _dot(p_ref[...].astype(BF16), wp_ref[...])
        gl_ref[...] = gl.astype(BF16)
        pp_ref[...] = pp.astype(BF16)
        h2_ref[...] = h1 + _sigmoid(gl) * pp

    row = lambda width: pl.BlockSpec((ts, width), lambda i: (i, 0))
    return pl.pallas_call(
        body, name="out_ple_fwd",
        grid=(s // ts,),
        in_specs=[row(d), row(e),
                  pl.BlockSpec((None, e, d), lambda i: (mixer_layer, 0, 0)),
                  pl.BlockSpec((1, d), lambda i: (0, 0)),
                  pl.BlockSpec((None, d, d), lambda i: (layer, 0, 0)),
                  pl.BlockSpec((None, ts, pdim), lambda i: (layer, i, 0)),
                  pl.BlockSpec((None, pdim, d), lambda i: (layer, 0, 0))],
        out_specs=[row(d), row(d), row(d), row(d)],
        out_shape=[jax.ShapeDtypeStruct((s, d), F32), jax.ShapeDtypeStruct((s, d), F32),
                   jax.ShapeDtypeStruct((s, d), BF16), jax.ShapeDtypeStruct((s, d), BF16)],
        compiler_params=_cparams("arbitrary"),
    )(h, o, wout_all, pn, wgate_all, p_all, wproj_all)


def _loss_head(h, target, g):
    s, d = h.shape
    ts = min(ROW_TILE, s)

    def body(h_ref, t_ref, g_ref, dh_ref, loss_ref, dg_ref):
        @pl.when(pl.program_id(0) == 0)
        def _():
            loss_ref[...] = jnp.zeros_like(loss_ref)
            dg_ref[...] = jnp.zeros_like(dg_ref)

        gain = g_ref[...]
        xh, r = _rms_stats(h_ref[...])
        err = xh * gain - t_ref[...]
        loss_ref[...] += jnp.full(loss_ref.shape, (0.5 / d) * jnp.sum(err * err), F32)
        dy = err * (1.0 / d)
        dg_ref[...] += jnp.sum(dy * xh, axis=0, keepdims=True)
        dh_ref[...] = _rms_bwd(dy, xh, r, gain)

    row = pl.BlockSpec((ts, d), lambda i: (i, 0))
    return pl.pallas_call(
        body, name="loss_head",
        grid=(s // ts,),
        in_specs=[row, row, pl.BlockSpec((1, d), lambda i: (0, 0))],
        out_specs=[row, pl.BlockSpec((1, 128), lambda i: (0, 0)), pl.BlockSpec((1, d), lambda i: (0, 0))],
        out_shape=[jax.ShapeDtypeStruct((s, d), F32), jax.ShapeDtypeStruct((1, 128), F32),
                   jax.ShapeDtypeStruct((1, d), F32)],
        compiler_params=_cparams("arbitrary"),
    )(h, target, g)


def _out_ple_bwd(dh2, gl, pp, h1, p_all, o, pn, wgate_all, wout_all, layer, mixer_layer):
    s, d = dh2.shape
    e = o.shape[1]
    pdim = p_all.shape[2]
    ts = min(BWD_ROW_TILE, s)
    last = s // ts - 1

    def body(dh2_ref, gl_ref, pp_ref, h1_ref, p_ref, o_ref, pn_ref, wg_ref, wo_ref,
             dh1_ref, do_ref, dwp_ref, dwg_ref, dwo_ref, dpn_ref, awp, awg, awo):
        i = pl.program_id(0)

        @pl.when(i == 0)
        def _():
            awp[...] = jnp.zeros_like(awp)
            awg[...] = jnp.zeros_like(awg)
            awo[...] = jnp.zeros_like(awo)
            dpn_ref[...] = jnp.zeros_like(dpn_ref)

        dh2 = dh2_ref[...]
        gate = _sigmoid(gl_ref[...].astype(F32))
        dpp = (dh2 * gate).astype(BF16)
        dgl = (dh2 * pp_ref[...].astype(F32) * gate * (1.0 - gate)).astype(BF16)
        xh, r = _rms_stats(h1_ref[...])
        pn = pn_ref[...]
        awp[...] += _dot_tn(p_ref[...].astype(BF16), dpp)
        awg[...] += _dot_tn((xh * pn).astype(BF16), dgl)
        dr = _dot_nt(dgl, wg_ref[...])
        dpn_ref[...] += jnp.sum(dr * xh, axis=0, keepdims=True)
        dh1 = dh2 + _rms_bwd(dr, xh, r, pn)
        dh1_ref[...] = dh1
        dh1b = dh1.astype(BF16)
        do_ref[...] = _dot_nt(dh1b, wo_ref[...]).astype(BF16)
        awo[...] += _dot_tn(o_ref[...], dh1b)

        @pl.when(i == last)
        def _():
            dwp_ref[...] = awp[...].astype(BF16)
            dwg_ref[...] = awg[...].astype(BF16)
            dwo_ref[...] = awo[...].astype(BF16)

    row = lambda width: pl.BlockSpec((ts, width), lambda i: (i, 0))
    whole = lambda a, b: pl.BlockSpec((a, b), lambda i: (0, 0))
    return pl.pallas_call(
        body, name="out_ple_bwd",
        grid=(s // ts,),
        in_specs=[row(d), row(d), row(d), row(d),
                  pl.BlockSpec((None, ts, pdim), lambda i: (layer, i, 0)),
                  row(e), whole(1, d),
                  pl.BlockSpec((None, d, d), lambda i: (layer, 0, 0)),
                  pl.BlockSpec((None, e, d), lambda i: (mixer_layer, 0, 0))],
        out_specs=[row(d), row(e), whole(pdim, d), whole(d, d), whole(e, d), whole(1, d)],
        out_shape=[jax.ShapeDtypeStruct((s, d), F32), jax.ShapeDtypeStruct((s, e), BF16),
                   jax.ShapeDtypeStruct((pdim, d), BF16), jax.ShapeDtypeStruct((d, d), BF16),
                   jax.ShapeDtypeStruct((e, d), BF16), jax.ShapeDtypeStruct((1, d), F32)],
        scratch_shapes=[pltpu.VMEM((pdim, d), F32), pltpu.VMEM((d, d), F32), pltpu.VMEM((e, d), F32)],
        compiler_params=_cparams("arbitrary"),
    )(dh2, gl, pp, h1, p_all, o, pn, wgate_all, wout_all)


def _mid_a_bwd(do, proj4, wconv_all, layer):
    _, s, e = proj4.shape
    cb = e // N_POOL_GROUPS
    rc = min(MID_CHUNK, s)
    nchunk = s // rc

    def body(do_ref, p_ref, w_ref, dp_ref, dw_ref, dconv_ref):
        w0, w1, w2 = w_ref[0:1, :], w_ref[1:2, :], w_ref[2:3, :]

        def first_pass(i, acc):
            r0 = pl.multiple_of(i * rc, rc)
            p0 = pl.multiple_of(jnp.maximum(r0 - HALO, 0), HALO)
            ld = lambda k: p_ref[k, pl.ds(r0, rc), :].astype(F32)
            ldp = lambda k: p_ref[k, pl.ds(p0, HALO), :].astype(F32)
            u = ld(1) * ld(2)
            u1, u2 = _conv_taps(u, ldp(1) * ldp(2), i == 0)
            conv = w0 * u2 + w1 * u1 + w2 * u
            b, z = ld(0), ld(3)
            sig = _sigmoid(z)
            dof = do_ref[pl.ds(r0, rc), :].astype(F32)
            dy = dof * (z * sig)
            dp_ref[3, pl.ds(r0, rc), :] = (dof * (b * conv) * (sig * (1.0 + z * (1.0 - sig)))).astype(BF16)
            dp_ref[0, pl.ds(r0, rc), :] = (dy * conv).astype(BF16)
            dconv = dy * b
            dconv_ref[pl.ds(r0, rc), :] = dconv
            a0, a1, a2 = acc
            return (a0 + jnp.sum(dconv * u2, axis=0, keepdims=True),
                    a1 + jnp.sum(dconv * u1, axis=0, keepdims=True),
                    a2 + jnp.sum(dconv * u, axis=0, keepdims=True))

        zero = jnp.zeros((1, cb), F32)
        a0, a1, a2 = lax.fori_loop(0, nchunk, first_pass, (zero, zero, zero))
        dw_ref[0:1, :] = a0
        dw_ref[1:2, :] = a1
        dw_ref[2:3, :] = a2
        dconv_ref[pl.ds(s, HALO), :] = jnp.zeros((HALO, cb), F32)

        def second_pass(i, carry):
            r0 = pl.multiple_of(i * rc, rc)
            dc = dconv_ref[pl.ds(r0, rc + HALO), :]
            du = (w2 * dc + w1 * _shift_up(dc, 1) + w0 * _shift_up(dc, 2))[:rc]
            cg = p_ref[1, pl.ds(r0, rc), :].astype(F32)
            v = p_ref[2, pl.ds(r0, rc), :].astype(F32)
            dp_ref[1, pl.ds(r0, rc), :] = (du * v).astype(BF16)
            dp_ref[2, pl.ds(r0, rc), :] = (du * cg).astype(BF16)
            return carry

        lax.fori_loop(0, nchunk, second_pass, 0)

    return pl.pallas_call(
        body, name="mid_a_bwd",
        grid=(e // cb,),
        in_specs=[pl.BlockSpec((s, cb), lambda c: (0, c)),
                  pl.BlockSpec((4, s, cb), lambda c: (0, 0, c)),
                  pl.BlockSpec((None, 3, cb), lambda c: (layer, 0, c))],
        out_specs=[pl.BlockSpec((4, s, cb), lambda c: (0, 0, c)),
                   pl.BlockSpec((3, cb), lambda c: (0, c))],
        out_shape=[jax.ShapeDtypeStruct((4, s, e), BF16), jax.ShapeDtypeStruct((3, e), F32)],
        scratch_shapes=[pltpu.VMEM((s + HALO, cb), F32)],
        compiler_params=_cparams("arbitrary"),
    )(do, proj4, wconv_all)


def _mid_b_bwd(do, proj2, wgrp_all, scale_all, layer):
    _, s, e = proj2.shape
    gdim = e // N_POOL_GROUPS
    rc = min(MID_CHUNK, s)
    nchunk = s // rc

    def body(do_ref, p_ref, w_ref, sc_ref, dp_ref, dw_ref, dsc_ref, ddq_ref, dd_ref, aw):
        grp = pl.program_id(0)

        def run(window):
            aw[...] = jnp.zeros_like(aw)
            scale = sc_ref[...]
            w = w_ref[...]

            def first_pass(i, dsc):
                r0 = pl.multiple_of(i * rc, rc)
                p0 = pl.multiple_of(jnp.maximum(r0 - HALO, 0), HALO)
                u = p_ref[0, pl.ds(r0, rc), :].astype(F32)
                up = jnp.where(i == 0, 0.0, p_ref[0, pl.ds(p0, HALO), :].astype(F32))
                db = _window_mean_minus(jnp.concatenate([up, u], axis=0), r0, window).astype(BF16)
                mr = _dot(db, w)
                z = p_ref[1, pl.ds(r0, rc), :].astype(F32)
                sig = _sigmoid(z)
                dof = do_ref[pl.ds(r0, rc), :].astype(F32)
                dp_ref[1, pl.ds(r0, rc), :] = (dof * (mr * scale) * (sig * (1.0 + z * (1.0 - sig)))).astype(BF16)
                dm = dof * (z * sig)
                dmr = (dm * scale).astype(BF16)
                aw[...] += _dot_tn(db, dmr)
                dd = _dot_nt(dmr, w)
                t = r0 + lax.broadcasted_iota(jnp.int32, (rc, 1), 0)
                dd_ref[pl.ds(r0, rc), :] = dd
                ddq_ref[pl.ds(r0, rc), :] = dd / jnp.minimum(t + 1, window).astype(F32)
                return dsc + jnp.sum(dm * mr, axis=0, keepdims=True)

            dsc_ref[...] = lax.fori_loop(0, nchunk, first_pass, jnp.zeros((1, gdim), F32))
            dw_ref[...] = aw[...].astype(BF16)
            ddq_ref[pl.ds(s, HALO), :] = jnp.zeros((HALO, gdim), F32)

            def second_pass(i, carry):
                r0 = pl.multiple_of(i * rc, rc)
                acc = ddq_ref[pl.ds(r0, rc + HALO), :]
                span = 1
                while span < window:
                    acc = acc + _shift_up(acc, span)
                    span *= 2
                dp_ref[0, pl.ds(r0, rc), :] = (acc[:rc] - dd_ref[pl.ds(r0, rc), :]).astype(BF16)
                return carry

            lax.fori_loop(0, nchunk, second_pass, 0)

        for k, window in enumerate(POOL_WINDOWS):
            pl.when(grp == k)(functools.partial(run, window))

    return pl.pallas_call(
        body, name="mid_b_bwd",
        grid=(N_POOL_GROUPS,),
        in_specs=[pl.BlockSpec((s, gdim), lambda g: (0, g)),
                  pl.BlockSpec((2, s, gdim), lambda g: (0, 0, g)),
                  pl.BlockSpec((None, None, gdim, gdim), lambda g: (layer, g, 0, 0)),
                  pl.BlockSpec((None, 1, gdim), lambda g: (layer, 0, g))],
        out_specs=[pl.BlockSpec((2, s, gdim), lambda g: (0, 0, g)),
                   pl.BlockSpec((None, gdim, gdim), lambda g: (g, 0, 0)),
                   pl.BlockSpec((1, gdim), lambda g: (0, g))],
        out_shape=[jax.ShapeDtypeStruct((2, s, e), BF16),
                   jax.ShapeDtypeStruct((N_POOL_GROUPS, gdim, gdim), BF16),
                   jax.ShapeDtypeStruct((1, e), F32)],
        scratch_shapes=[pltpu.VMEM((s + HALO, gdim), F32), pltpu.VMEM((s, gdim), F32),
                        pltpu.VMEM((gdim, gdim), F32)],
        compiler_params=_cparams("arbitrary"),
    )(do, proj2, wgrp_all, scale_all)


def _proj_wgrad(hn, dproj):
    s, d = hn.shape
    nsplit, _, e = dproj.shape
    ts = min(ROW_TILE, s)
    last = s // ts - 1

    def body(hn_ref, dp_ref, dw_ref, acc):
        i = pl.program_id(1)

        @pl.when(i == 0)
        def _():
            acc[...] = jnp.zeros_like(acc)

        acc[...] += _dot_tn(hn_ref[...], dp_ref[...])

        @pl.when(i == last)
        def _():
            dw_ref[...] = acc[...].astype(BF16)

    return pl.pallas_call(
        body, name="proj_wgrad",
        grid=(nsplit, s // ts),
        in_specs=[pl.BlockSpec((ts, d), lambda k, i: (i, 0)),
                  pl.BlockSpec((None, ts, e), lambda k, i: (k, i, 0))],
        out_specs=pl.BlockSpec((d, e), lambda k, i: (0, k)),
        out_shape=jax.ShapeDtypeStruct((d, nsplit * e), BF16),
        scratch_shapes=[pltpu.VMEM((d, e), F32)],
        compiler_params=_cparams("parallel", "arbitrary"),
    )(hn, dproj)


def _proj_xgrad(dproj, w_all, layer, h, g, dh1):
    s, d = h.shape
    nsplit, _, e = dproj.shape
    ts = min(ROW_TILE, s)

    def body(dp_ref, w_ref, h_ref, g_ref, dh1_ref, dh_ref, dg_ref):
        @pl.when(pl.program_id(0) == 0)
        def _():
            dg_ref[...] = jnp.zeros_like(dg_ref)

        dhn = _dot_nt(dp_ref[0], w_ref[:, 0:e])
        for k in range(1, nsplit):
            dhn += _dot_nt(dp_ref[k], w_ref[:, k * e:(k + 1) * e])
        xh, r = _rms_stats(h_ref[...])
        dg_ref[...] += jnp.sum(dhn * xh, axis=0, keepdims=True)
        dh_ref[...] = dh1_ref[...] + _rms_bwd(dhn, xh, r, g_ref[...])

    row = pl.BlockSpec((ts, d), lambda i: (i, 0))
    return pl.pallas_call(
        body, name="proj_xgrad",
        grid=(s // ts,),
        in_specs=[pl.BlockSpec((nsplit, ts, e), lambda i: (0, i, 0)),
                  pl.BlockSpec((None, d, nsplit * e), lambda i: (layer, 0, 0)),
                  row, pl.BlockSpec((1, d), lambda i: (0, 0)), row],
        out_specs=[row, pl.BlockSpec((1, d), lambda i: (0, 0))],
        out_shape=[jax.ShapeDtypeStruct((s, d), F32), jax.ShapeDtypeStruct((1, d), F32)],
        compiler_params=_cparams("arbitrary"),
    )(dproj, w_all, h, g, dh1)


def _local_step(x, p, target, w):
    depth = p.shape[0]
    row = lambda a, i: a[i][None, :]
    scale3 = w["b_scale"][:, None, :]
    saved = []
    h = x
    for i in range(depth):
        j = i // 2
        if i % 2 == 0:
            proj, hn = _norm_proj(h, row(w["norm_mix"], i), w["a_w_in"], j, 4)
            o = _mid_a_fwd(proj, w["a_w_conv"], j)
            wout = w["a_w_out"]
        else:
            proj, hn = _norm_proj(h, row(w["norm_mix"], i), w["b_w_in"], j, 2)
            o = _mid_b_fwd(proj, w["b_w_grp"], scale3, j)
            wout = w["b_w_out"]
        h1, h2, gl, pp = _out_ple_fwd(h, o, wout, j, row(w["ple_norm"], i), w["ple_w_gate"], p, w["ple_w_proj"], i)
        saved.append((h, proj, hn, o, h1, gl, pp))
        h = h2

    dh, loss_row, d_final = _loss_head(h, target, w["final_norm"][None, :])

    g = {k: [None] * n for k, n in (("norm_mix", depth), ("ple_norm", depth), ("ple_w_gate", depth),
                                    ("ple_w_proj", depth), ("a_w_in", depth // 2), ("a_w_conv", depth // 2),
                                    ("a_w_out", depth // 2), ("b_w_in", depth // 2), ("b_w_grp", depth // 2),
                                    ("b_scale", depth // 2), ("b_w_out", depth // 2))}
    g["final_norm"] = d_final
    for i in reversed(range(depth)):
        j = i // 2
        h_in, proj, hn, o, h1, gl, pp = saved[i]
        is_a = i % 2 == 0
        wout = w["a_w_out"] if is_a else w["b_w_out"]
        dh1, do, g["ple_w_proj"][i], g["ple_w_gate"][i], dwo, g["ple_norm"][i] = _out_ple_bwd(
            dh, gl, pp, h1, p, o, row(w["ple_norm"], i), w["ple_w_gate"], wout, i, j)
        if is_a:
            g["a_w_out"][j] = dwo
            dproj, g["a_w_conv"][j] = _mid_a_bwd(do, proj, w["a_w_conv"], j)
            g["a_w_in"][j] = _proj_wgrad(hn, dproj)
            dh, g["norm_mix"][i] = _proj_xgrad(dproj, w["a_w_in"], j, h_in, row(w["norm_mix"], i), dh1)
        else:
            g["b_w_out"][j] = dwo
            dproj, g["b_w_grp"][j], g["b_scale"][j] = _mid_b_bwd(do, proj, w["b_w_grp"], scale3, j)
            g["b_w_in"][j] = _proj_wgrad(hn, dproj)
            dh, g["norm_mix"][i] = _proj_xgrad(dproj, w["b_w_in"], j, h_in, row(w["norm_mix"], i), dh1)
    return loss_row, dh, g


ANY = pl.BlockSpec(memory_space=pl.ANY)
VMEM_SPEC = pl.BlockSpec(memory_space=pltpu.VMEM)

FLIPS = [(fx, fy, fc) for fx in (0, 1) for fy in (0, 1) for fc in (0, 1)][1:]
SHARD_AXIS = {"a_w_in": 2, "a_w_out": 1, "b_w_in": 2, "b_w_grp": 2, "b_w_out": 1, "ple_w_gate": 1, "ple_w_proj": 2}
MATRICES = tuple(SHARD_AXIS)


def _my_place():
    return lax.axis_index("x"), lax.axis_index("y"), lax.axis_index("c")


def _position(place):
    x, y, c = place
    return 4 * x + 2 * y + c


def _flip(place, flips):
    return tuple(1 - v if f else v for v, f in zip(place, flips))


def _shard_of(ref, axis, pos, n):
    idx = [slice(None)] * len(ref.shape)
    idx[axis] = pl.ds(pl.multiple_of(pos * n, n), n)
    return ref.at[tuple(idx)]


def _all_gather_matrices(shards):
    names = list(shards)
    nt = len(names)
    axes = [SHARD_AXIS[k] for k in names]
    widths = [shards[k].shape[SHARD_AXIS[k]] for k in names]

    def full_shape(k):
        shp = list(shards[k].shape)
        shp[SHARD_AXIS[k]] *= N_DEV
        return tuple(shp)

    def body(*refs):
        ins, outs = refs[:nt], refs[nt:2 * nt]
        send_sems, recv_sems, local_sems = refs[2 * nt:]
        me = _my_place()
        x, y, c = me
        sibling = (x, y, 1 - c)
        chips = [(1 - x, y), (x, 1 - y), (1 - x, 1 - y)]

        def block(t, place):
            return _shard_of(outs[t], axes[t], _position(place), widths[t])

        def copy(t, k, place, to, src=None):
            return pltpu.make_async_remote_copy(
                src_ref=block(t, place) if src is None else src, dst_ref=block(t, place),
                send_sem=send_sems.at[t, k], recv_sem=recv_sems.at[t, k], device_id=to, device_id_type=MESH)

        mine = [pltpu.make_async_copy(ins[t], block(t, me), local_sems.at[t]) for t in range(nt)]
        for cp in mine:
            cp.start()
        first = []
        for j, chip in enumerate(chips):
            first += [copy(t, 1 + j, me, (*chip, c), src=ins[t]) for t in range(nt)]
        first += [copy(t, 0, me, sibling, src=ins[t]) for t in range(nt)]
        for cp in first:
            cp.start()
        passed = []
        for j, chip in enumerate(chips):
            for t in range(nt):
                copy(t, 1 + j, (*chip, c), me).wait_recv()
                fwd = copy(t, 4 + j, (*chip, c), sibling)
                fwd.start()
                passed.append(fwd)
        for t in range(nt):
            copy(t, 0, sibling, me).wait_recv()
        for j, chip in enumerate(chips):
            for t in range(nt):
                copy(t, 4 + j, (*chip, 1 - c), me).wait_recv()
        for cp in first + passed:
            cp.wait_send()
        for cp in mine:
            cp.wait()

    outs = pl.pallas_call(
        body, name="all_gather_matrices",
        in_specs=[ANY] * nt, out_specs=[ANY] * nt,
        out_shape=[jax.ShapeDtypeStruct(full_shape(k), shards[k].dtype) for k in names],
        scratch_shapes=[pltpu.SemaphoreType.DMA((nt, 7)), pltpu.SemaphoreType.DMA((nt, 7)),
                        pltpu.SemaphoreType.DMA((nt,))],
    )(*[shards[k] for k in names])
    return dict(zip(names, outs))


def _gather_rows(buf, reduce):
    r, c = buf.shape

    def body(in_ref, out_ref, *scratch):
        if reduce:
            all_ref, send_sems, recv_sems = scratch
        else:
            all_ref = out_ref
            send_sems, recv_sems = scratch
        me = _my_place()
        all_ref[_position(me)] = in_ref[...]
        copies = []
        for k, flips in enumerate(FLIPS):
            peer = _flip(me, flips)
            cp = pltpu.make_async_remote_copy(
                src_ref=in_ref, dst_ref=all_ref.at[_position(me)],
                send_sem=send_sems.at[k], recv_sem=recv_sems.at[k], device_id=peer, device_id_type=MESH)
            cp.start()
            copies.append(cp)
        for cp in copies:
            cp.wait()
        if reduce:
            total = all_ref[0]
            for j in range(1, N_DEV):
                total = total + all_ref[j]
            out_ref[...] = total

    return pl.pallas_call(
        body, name="sum_rows" if reduce else "gather_rows",
        in_specs=[VMEM_SPEC], out_specs=VMEM_SPEC,
        out_shape=jax.ShapeDtypeStruct((r, c) if reduce else (N_DEV, r, c), F32),
        scratch_shapes=([pltpu.VMEM((N_DEV, r, c), F32)] if reduce else [])
        + [pltpu.SemaphoreType.DMA((7,)), pltpu.SemaphoreType.DMA((7,))],
    )(buf)


def _exchange_gradients(grads):
    names = list(grads)
    nt = len(names)
    axes = [SHARD_AXIS[k] - 1 for k in names]
    counts = [len(grads[k]) for k in names]
    widths = [grads[k][0].shape[SHARD_AXIS[k] - 1] // N_DEV for k in names]
    flat = [a for k in names for a in grads[k]]
    first = [sum(counts[:t]) for t in range(nt)]

    def slot_shape(t):
        shp = list(grads[names[t]][0].shape)
        shp[axes[t]] = widths[t]
        return (N_DEV, counts[t], *shp)

    def body(*refs):
        ins, outs = refs[:len(flat)], refs[len(flat):len(flat) + nt]
        send_sems, recv_sems, local_sems = refs[len(flat) + nt:]
        me = _my_place()
        mine = _position(me)
        started = []
        for t in range(nt):
            for l in range(counts[t]):
                cp = pltpu.make_async_copy(_shard_of(ins[first[t] + l], axes[t], mine, widths[t]),
                                           outs[t].at[mine, l], local_sems.at[t])
                cp.start()
                started.append(cp)
        for k, flips in enumerate(FLIPS):
            peer = _flip(me, flips)
            for t in range(nt):
                for l in range(counts[t]):
                    pltpu.make_async_remote_copy(
                        src_ref=_shard_of(ins[first[t] + l], axes[t], _position(peer), widths[t]),
                        dst_ref=outs[t].at[mine, l],
                        send_sem=send_sems.at[t, k], recv_sem=recv_sems.at[t, k],
                        device_id=peer, device_id_type=MESH).start()
        for k, flips in enumerate(FLIPS):
            peer = _flip(me, flips)
            for t in range(nt):
                slot = outs[t].at[_position(peer)]
                pltpu.make_async_remote_copy(
                    src_ref=slot, dst_ref=slot, send_sem=send_sems.at[t, k], recv_sem=recv_sems.at[t, k],
                    device_id=peer, device_id_type=MESH).wait()
        for cp in started:
            cp.wait()

    outs = pl.pallas_call(
        body, name="exchange_gradients",
        in_specs=[ANY] * len(flat), out_specs=[ANY] * nt,
        out_shape=[jax.ShapeDtypeStruct(slot_shape(t), BF16) for t in range(nt)],
        scratch_shapes=[pltpu.SemaphoreType.DMA((nt, 7)), pltpu.SemaphoreType.DMA((nt, 7)),
                        pltpu.SemaphoreType.DMA((nt,))],
    )(*flat)
    return dict(zip(names, outs))


def _adamw_math(w, g, m, v):
    m = ADAM_B1 * m + (1.0 - ADAM_B1) * g
    v = ADAM_B2 * v + (1.0 - ADAM_B2) * (g * g)
    m_hat = m / (1.0 - ADAM_B1 ** ADAM_STEP)
    v_hat = v / (1.0 - ADAM_B2 ** ADAM_STEP)
    delta = -ADAM_LR * (m_hat / (jnp.sqrt(v_hat) + ADAM_EPS) + ADAM_WD * w)
    return delta, m, v


def _adamw_pieces(pieces, w, m, v):
    shape = w.shape
    cols = shape[-1]
    rows = w.size // cols
    tr = min(256, rows)
    flat2 = lambda a: a.reshape(rows, cols)

    def body(p_ref, w_ref, m_ref, v_ref, g_ref, d_ref, nm_ref, nv_ref):
        g = p_ref[0].astype(F32)
        for j in range(1, N_DEV):
            g = g + p_ref[j].astype(F32)
        g_ref[...] = g
        d_ref[...], nm_ref[...], nv_ref[...] = _adamw_math(w_ref[...], g, m_ref[...], v_ref[...])

    blk = pl.BlockSpec((tr, cols), lambda i: (i, 0))
    outs = pl.pallas_call(
        body, name="adamw_pieces",
        grid=(rows // tr,),
        in_specs=[pl.BlockSpec((N_DEV, tr, cols), lambda i: (0, i, 0)), blk, blk, blk],
        out_specs=[blk] * 4,
        out_shape=[jax.ShapeDtypeStruct((rows, cols), F32)] * 4,
        compiler_params=_cparams("parallel"),
    )(pieces.reshape(N_DEV, rows, cols), flat2(w), flat2(m), flat2(v))
    return [a.reshape(shape) for a in outs]


def _adamw_small(g, w, m, v):
    shape = w.shape
    two = lambda a: a.reshape(-1, shape[-1])

    def body(g_ref, w_ref, m_ref, v_ref, d_ref, nm_ref, nv_ref):
        d_ref[...], nm_ref[...], nv_ref[...] = _adamw_math(w_ref[...], g_ref[...], m_ref[...], v_ref[...])

    outs = pl.pallas_call(
        body, name="adamw_small",
        in_specs=[VMEM_SPEC] * 4, out_specs=[VMEM_SPEC] * 3,
        out_shape=[jax.ShapeDtypeStruct(two(w).shape, F32)] * 3,
    )(two(g), two(w), two(m), two(v))
    return [a.reshape(shape) for a in outs]


WEIGHTS = ("norm_mix", "a_w_in", "a_w_conv", "a_w_out", "b_w_in", "b_w_grp", "b_scale", "b_w_out",
           "ple_norm", "ple_w_gate", "ple_w_proj", "final_norm")
SMALL_ROWS = 24


def kernel(x, p, norm_mix, a_w_in, a_w_conv, a_w_out, b_w_in, b_w_grp, b_scale, b_w_out, ple_norm, ple_w_gate, ple_w_proj, final_norm, loss_target, m_norm_mix, m_a_w_in, m_a_w_conv, m_a_w_out, m_b_w_in, m_b_w_grp, m_b_scale, m_b_w_out, m_ple_norm, m_ple_w_gate, m_ple_w_proj, m_final_norm, v_norm_mix, v_a_w_in, v_a_w_conv, v_a_w_out, v_b_w_in, v_b_w_grp, v_b_scale, v_b_w_out, v_ple_norm, v_ple_w_gate, v_ple_w_proj, v_final_norm):
    wts = dict(norm_mix=norm_mix, a_w_in=a_w_in, a_w_conv=a_w_conv, a_w_out=a_w_out, b_w_in=b_w_in, b_w_grp=b_w_grp,
               b_scale=b_scale, b_w_out=b_w_out, ple_norm=ple_norm, ple_w_gate=ple_w_gate, ple_w_proj=ple_w_proj,
               final_norm=final_norm)
    mom = dict(norm_mix=m_norm_mix, a_w_in=m_a_w_in, a_w_conv=m_a_w_conv, a_w_out=m_a_w_out, b_w_in=m_b_w_in,
               b_w_grp=m_b_w_grp, b_scale=m_b_scale, b_w_out=m_b_w_out, ple_norm=m_ple_norm, ple_w_gate=m_ple_w_gate,
               ple_w_proj=m_ple_w_proj, final_norm=m_final_norm)
    var = dict(norm_mix=v_norm_mix, a_w_in=v_a_w_in, a_w_conv=v_a_w_conv, a_w_out=v_a_w_out, b_w_in=v_b_w_in,
               b_w_grp=v_b_w_grp, b_scale=v_b_scale, b_w_out=v_b_w_out, ple_norm=v_ple_norm, ple_w_gate=v_ple_w_gate,
               ple_w_proj=v_ple_w_proj, final_norm=v_final_norm)
    s, d = x.shape[1], x.shape[2]
    n_a = a_w_conv.shape[0]
    cw = a_w_conv.shape[2]
    pos = _position(_my_place())

    w = _all_gather_matrices({k: wts[k].astype(BF16) for k in MATRICES})
    vec_rows = jnp.concatenate([a_w_conv.reshape(-1, cw), b_scale], axis=0)
    vecs = _gather_rows(vec_rows, reduce=False)
    n_conv = 3 * n_a
    w["a_w_conv"] = vecs[:, :n_conv].transpose(1, 0, 2).reshape(n_a, 3, N_DEV * cw)
    w["b_scale"] = vecs[:, n_conv:].transpose(1, 0, 2).reshape(b_scale.shape[0], N_DEV * cw)
    w["norm_mix"], w["ple_norm"], w["final_norm"] = norm_mix, ple_norm, final_norm

    loss_row, dx, g = _local_step(x[0], p[:, 0], loss_target[0], w)

    pad = lambda a: jnp.pad(a, ((0, 0), (0, d - a.shape[1])))
    small = jnp.concatenate(g["norm_mix"] + g["ple_norm"] + [g["final_norm"]] + g["a_w_conv"] + g["b_scale"]
                            + [pad(loss_row)], axis=0)
    small = jnp.pad(small, ((0, SMALL_ROWS - small.shape[0]), (0, 0)))
    total = _gather_rows(small, reduce=True)
    depth = norm_mix.shape[0]
    o = 0
    gsum = {}
    gsum["norm_mix"] = total[o:o + depth]; o += depth
    gsum["ple_norm"] = total[o:o + depth]; o += depth
    gsum["final_norm"] = total[o]; o += 1
    conv_full = total[o:o + n_conv].reshape(n_a, 3, d); o += n_conv
    scale_full = total[o:o + b_scale.shape[0]]; o += b_scale.shape[0]
    loss = total[o, 0]
    gsum["a_w_conv"] = lax.dynamic_slice_in_dim(conv_full, pos * cw, cw, axis=2)
    gsum["b_scale"] = lax.dynamic_slice_in_dim(scale_full, pos * cw, cw, axis=1)

    pieces = _exchange_gradients({k: g[k] for k in MATRICES})

    grad, delta, new_m, new_v = {}, {}, {}, {}
    for k in WEIGHTS:
        if k in SHARD_AXIS:
            grad[k], delta[k], new_m[k], new_v[k] = _adamw_pieces(pieces[k], wts[k], mom[k], var[k])
        else:
            grad[k] = gsum[k]
            delta[k], new_m[k], new_v[k] = _adamw_small(gsum[k], wts[k], mom[k], var[k])
    return (loss, dx[None], *[grad[k] for k in WEIGHTS], *[delta[k] for k in WEIGHTS],
            *[new_m[k] for k in WEIGHTS], *[new_v[k] for k in WEIGHTS])
```

```python
import functools

import jax
import jax.numpy as jnp
from jax import lax
from jax.experimental import pallas as pl
from jax.experimental.pallas import tpu as pltpu
from jax.experimental.pallas import tpu_sc as plsc

F32 = jnp.float32
BF16 = jnp.bfloat16
MESH = pl.DeviceIdType.MESH

RMS_EPS = 1e-6
POOL_WINDOWS = (2, 4, 8, 16)
N_POOL_GROUPS = len(POOL_WINDOWS)
ADAM_LR = 0.001
ADAM_B1 = 0.9
ADAM_B2 = 0.999
ADAM_EPS = 1e-08
ADAM_WD = 0.01
ADAM_STEP = 10
N_DEV = 8

HALO = 16
ROW_TILE = 512
BWD_ROW_TILE = 256
MID_CHUNK = 256
VMEM_LIMIT = 56 * 1024 * 1024


def _cparams(*sem):
    return pltpu.CompilerParams(dimension_semantics=sem, vmem_limit_bytes=VMEM_LIMIT)


def _dot(a, b):
    return jnp.dot(a, b, preferred_element_type=F32)


def _dot_nt(a, b):
    return lax.dot_general(a, b, (((1,), (1,)), ((), ())), preferred_element_type=F32)


def _dot_tn(a, b):
    return lax.dot_general(a, b, (((0,), (0,)), ((), ())), preferred_element_type=F32)


def _rms_stats(x):
    r = lax.rsqrt(jnp.mean(x * x, axis=-1, keepdims=True) + RMS_EPS)
    return x * r, r


def _rms_bwd(dy, xh, r, g):
    a = dy * g
    return r * (a - xh * jnp.mean(a * xh, axis=-1, keepdims=True))


def _sigmoid(x):
    return 1.0 / (1.0 + jnp.exp(-x))


def _shift_down(x, k):
    return pltpu.roll(x, k, 0)


def _shift_up(x, k):
    return pltpu.roll(x, x.shape[0] - k, 0)


def _norm_proj(h, g, w, nsplit):
    s, d = h.shape
    n = w.shape[1]
    e = n // nsplit
    ts = min(ROW_TILE, s)

    def body(h_ref, g_ref, w_ref, proj_ref, hn_ref):
        xh, _ = _rms_stats(h_ref[...])
        hn = (xh * g_ref[...]).astype(BF16)
        hn_ref[...] = hn
        for k in range(nsplit):
            proj_ref[k] = _dot(hn, w_ref[:, k * e:(k + 1) * e]).astype(BF16)

    return pl.pallas_call(
        body, name="norm_proj",
        grid=(s // ts,),
        in_specs=[pl.BlockSpec((ts, d), lambda i: (i, 0)),
                  pl.BlockSpec((1, d), lambda i: (0, 0)),
                  pl.BlockSpec((d, n), lambda i: (0, 0))],
        out_specs=[pl.BlockSpec((nsplit, ts, e), lambda i: (0, i, 0)),
                   pl.BlockSpec((ts, d), lambda i: (i, 0))],
        out_shape=[jax.ShapeDtypeStruct((nsplit, s, e), BF16),
                   jax.ShapeDtypeStruct((s, d), BF16)],
        compiler_params=_cparams("arbitrary"),
    )(h, g, w)


def _conv_taps(u, u_prev, first):
    uu = jnp.concatenate([jnp.where(first, 0.0, u_prev), u], axis=0)
    return _shift_down(uu, 1)[HALO:], _shift_down(uu, 2)[HALO:]


def _mid_a_fwd(proj4, wconv):
    _, s, e = proj4.shape
    cb = e // N_POOL_GROUPS
    rc = min(MID_CHUNK, s)

    def body(p_ref, w_ref, o_ref):
        w0, w1, w2 = w_ref[0:1, :], w_ref[1:2, :], w_ref[2:3, :]

        def chunk(i, carry):
            r0 = pl.multiple_of(i * rc, rc)
            p0 = pl.multiple_of(jnp.maximum(r0 - HALO, 0), HALO)
            ld = lambda k: p_ref[k, pl.ds(r0, rc), :].astype(F32)
            ldp = lambda k: p_ref[k, pl.ds(p0, HALO), :].astype(F32)
            u = ld(1) * ld(2)
            u1, u2 = _conv_taps(u, ldp(1) * ldp(2), i == 0)
            conv = w0 * u2 + w1 * u1 + w2 * u
            z = ld(3)
            o_ref[pl.ds(r0, rc), :] = ((z * _sigmoid(z)) * (ld(0) * conv)).astype(BF16)
            return carry

        lax.fori_loop(0, s // rc, chunk, 0)

    return pl.pallas_call(
        body, name="mid_a_fwd",
        grid=(e // cb,),
        in_specs=[pl.BlockSpec((4, s, cb), lambda c: (0, 0, c)),
                  pl.BlockSpec((3, cb), lambda c: (0, c))],
        out_specs=pl.BlockSpec((s, cb), lambda c: (0, c)),
        out_shape=jax.ShapeDtypeStruct((s, e), BF16),
        compiler_params=_cparams("arbitrary"),
    )(proj4, wconv)


def _window_mean_minus(uu, row0, window):
    acc = uu
    span = 1
    while span < window:
        acc = acc + _shift_down(acc, span)
        span *= 2
    rows = uu.shape[0] - HALO
    t = row0 + lax.broadcasted_iota(jnp.int32, (rows, 1), 0)
    cnt = jnp.minimum(t + 1, window).astype(F32)
    return acc[HALO:] / cnt - uu[HALO:]


def _mid_b_fwd(proj2, wgrp, scale):
    _, s, e = proj2.shape
    gdim = e // N_POOL_GROUPS
    rc = min(MID_CHUNK, s)

    def body(p_ref, w_ref, sc_ref, o_ref):
        grp = pl.program_id(0)

        def run(window):
            def chunk(i, carry):
                r0 = pl.multiple_of(i * rc, rc)
                p0 = pl.multiple_of(jnp.maximum(r0 - HALO, 0), HALO)
                u = p_ref[0, pl.ds(r0, rc), :].astype(F32)
                up = jnp.where(i == 0, 0.0, p_ref[0, pl.ds(p0, HALO), :].astype(F32))
                d = _window_mean_minus(jnp.concatenate([up, u], axis=0), r0, window)
                mixed = _dot(d.astype(BF16), w_ref[...]) * sc_ref[...]
                z = p_ref[1, pl.ds(r0, rc), :].astype(F32)
                o_ref[pl.ds(r0, rc), :] = ((z * _sigmoid(z)) * mixed).astype(BF16)
                return carry

            lax.fori_loop(0, s // rc, chunk, 0)

        for k, window in enumerate(POOL_WINDOWS):
            pl.when(grp == k)(functools.partial(run, window))

    return pl.pallas_call(
        body, name="mid_b_fwd",
        grid=(N_POOL_GROUPS,),
        in_specs=[pl.BlockSpec((2, s, gdim), lambda g: (0, 0, g)),
                  pl.BlockSpec((None, gdim, gdim), lambda g: (g, 0, 0)),
                  pl.BlockSpec((1, gdim), lambda g: (0, g))],
        out_specs=pl.BlockSpec((s, gdim), lambda g: (0, g)),
        out_shape=jax.ShapeDtypeStruct((s, e), BF16),
        compiler_params=_cparams("arbitrary"),
    )(proj2, wgrp, scale)


def _out_ple_fwd(h, o, wout, pn, wgate, p_all, wproj, layer):
    s, d = h.shape
    e = o.shape[1]
    pdim = p_all.shape[2]
    ts = min(ROW_TILE, s)

    def body(h_ref, o_ref, wo_ref, pn_ref, wg_ref, p_ref, wp_ref, h1_ref, h2_ref, gl_ref, pp_ref):
        h1 = h_ref[...] + _dot(o_ref[...], wo_ref[...])
        h1_ref[...] = h1
        xh, _ = _rms_stats(h1)
        gl = _dot((xh * pn_ref[...]).astype(BF16), wg_ref[...])
        pp = _dot(p_ref[...].astype(BF16), wp_ref[...])
        gl_ref[...] = gl.astype(BF16)
        pp_ref[...] = pp.astype(BF16)
        h2_ref[...] = h1 + _sigmoid(gl) * pp

    row = lambda width: pl.BlockSpec((ts, width), lambda i: (i, 0))
    whole = lambda a, b: pl.BlockSpec((a, b), lambda i: (0, 0))
    return pl.pallas_call(
        body, name="out_ple_fwd",
        grid=(s // ts,),
        in_specs=[row(d), row(e), whole(e, d), whole(1, d), whole(d, d),
                  pl.BlockSpec((None, ts, pdim), lambda i: (layer, i, 0)),
                  whole(pdim, d)],
        out_specs=[row(d), row(d), row(d), row(d)],
        out_shape=[jax.ShapeDtypeStruct((s, d), F32), jax.ShapeDtypeStruct((s, d), F32),
                   jax.ShapeDtypeStruct((s, d), BF16), jax.ShapeDtypeStruct((s, d), BF16)],
        compiler_params=_cparams("arbitrary"),
    )(h, o, wout, pn, wgate, p_all, wproj)


def _loss_head(h, target, g):
    s, d = h.shape
    ts = min(ROW_TILE, s)

    def body(h_ref, t_ref, g_ref, dh_ref, loss_ref, dg_ref):
        @pl.when(pl.program_id(0) == 0)
        def _():
            loss_ref[...] = jnp.zeros_like(loss_ref)
            dg_ref[...] = jnp.zeros_like(dg_ref)

        gain = g_ref[...]
        xh, r = _rms_stats(h_ref[...])
        err = xh * gain - t_ref[...]
        loss_ref[...] += jnp.full(loss_ref.shape, (0.5 / d) * jnp.sum(err * err), F32)
        dy = err * (1.0 / d)
        dg_ref[...] += jnp.sum(dy * xh, axis=0, keepdims=True)
        dh_ref[...] = _rms_bwd(dy, xh, r, gain)

    row = pl.BlockSpec((ts, d), lambda i: (i, 0))
    return pl.pallas_call(
        body, name="loss_head",
        grid=(s // ts,),
        in_specs=[row, row, pl.BlockSpec((1, d), lambda i: (0, 0))],
        out_specs=[row, pl.BlockSpec((1, 128), lambda i: (0, 0)), pl.BlockSpec((1, d), lambda i: (0, 0))],
        out_shape=[jax.ShapeDtypeStruct((s, d), F32), jax.ShapeDtypeStruct((1, 128), F32),
                   jax.ShapeDtypeStruct((1, d), F32)],
        compiler_params=_cparams("arbitrary"),
    )(h, target, g)


def _out_ple_bwd(dh2, gl, pp, h1, p_all, o, pn, wgate, wout, layer):
    s, d = dh2.shape
    e = o.shape[1]
    pdim = p_all.shape[2]
    ts = min(BWD_ROW_TILE, s)
    last = s // ts - 1

    def body(dh2_ref, gl_ref, pp_ref, h1_ref, p_ref, o_ref, pn_ref, wg_ref, wo_ref,
             dh1_ref, do_ref, dwp_ref, dwg_ref, dwo_ref, dpn_ref, awp, awg, awo):
        i = pl.program_id(0)

        @pl.when(i == 0)
        def _():
            awp[...] = jnp.zeros_like(awp)
            awg[...] = jnp.zeros_like(awg)
            awo[...] = jnp.zeros_like(awo)
            dpn_ref[...] = jnp.zeros_like(dpn_ref)

        dh2 = dh2_ref[...]
        gate = _sigmoid(gl_ref[...].astype(F32))
        dpp = (dh2 * gate).astype(BF16)
        dgl = (dh2 * pp_ref[...].astype(F32) * gate * (1.0 - gate)).astype(BF16)
        xh, r = _rms_stats(h1_ref[...])
        pn = pn_ref[...]
        awp[...] += _dot_tn(p_ref[...].astype(BF16), dpp)
        awg[...] += _dot_tn((xh * pn).astype(BF16), dgl)
        dr = _dot_nt(dgl, wg_ref[...])
        dpn_ref[...] += jnp.sum(dr * xh, axis=0, keepdims=True)
        dh1 = dh2 + _rms_bwd(dr, xh, r, pn)
        dh1_ref[...] = dh1
        dh1b = dh1.astype(BF16)
        do_ref[...] = _dot_nt(dh1b, wo_ref[...]).astype(BF16)
        awo[...] += _dot_tn(o_ref[...], dh1b)

        @pl.when(i == last)
        def _():
            dwp_ref[...] = awp[...].astype(BF16)
            dwg_ref[...] = awg[...].astype(BF16)
            dwo_ref[...] = awo[...].astype(BF16)

    row = lambda width: pl.BlockSpec((ts, width), lambda i: (i, 0))
    whole = lambda a, b: pl.BlockSpec((a, b), lambda i: (0, 0))
    return pl.pallas_call(
        body, name="out_ple_bwd",
        grid=(s // ts,),
        in_specs=[row(d), row(d), row(d), row(d),
                  pl.BlockSpec((None, ts, pdim), lambda i: (layer, i, 0)),
                  row(e), whole(1, d), whole(d, d), whole(e, d)],
        out_specs=[row(d), row(e), whole(pdim, d), whole(d, d), whole(e, d), whole(1, d)],
        out_shape=[jax.ShapeDtypeStruct((s, d), F32), jax.ShapeDtypeStruct((s, e), BF16),
                   jax.ShapeDtypeStruct((pdim, d), BF16), jax.ShapeDtypeStruct((d, d), BF16),
                   jax.ShapeDtypeStruct((e, d), BF16), jax.ShapeDtypeStruct((1, d), F32)],
        scratch_shapes=[pltpu.VMEM((pdim, d), F32), pltpu.VMEM((d, d), F32), pltpu.VMEM((e, d), F32)],
        compiler_params=_cparams("arbitrary"),
    )(dh2, gl, pp, h1, p_all, o, pn, wgate, wout)


def _mid_a_bwd(do, proj4, wconv):
    _, s, e = proj4.shape
    cb = e // N_POOL_GROUPS
    rc = min(MID_CHUNK, s)
    nchunk = s // rc

    def body(do_ref, p_ref, w_ref, dp_ref, dw_ref, dconv_ref):
        w0, w1, w2 = w_ref[0:1, :], w_ref[1:2, :], w_ref[2:3, :]

        def first_pass(i, acc):
            r0 = pl.multiple_of(i * rc, rc)
            p0 = pl.multiple_of(jnp.maximum(r0 - HALO, 0), HALO)
            ld = lambda k: p_ref[k, pl.ds(r0, rc), :].astype(F32)
            ldp = lambda k: p_ref[k, pl.ds(p0, HALO), :].astype(F32)
            u = ld(1) * ld(2)
            u1, u2 = _conv_taps(u, ldp(1) * ldp(2), i == 0)
            conv = w0 * u2 + w1 * u1 + w2 * u
            b, z = ld(0), ld(3)
            sig = _sigmoid(z)
            dof = do_ref[pl.ds(r0, rc), :].astype(F32)
            dy = dof * (z * sig)
            dp_ref[3, pl.ds(r0, rc), :] = (dof * (b * conv) * (sig * (1.0 + z * (1.0 - sig)))).astype(BF16)
            dp_ref[0, pl.ds(r0, rc), :] = (dy * conv).astype(BF16)
            dconv = dy * b
            dconv_ref[pl.ds(r0, rc), :] = dconv
            a0, a1, a2 = acc
            return (a0 + jnp.sum(dconv * u2, axis=0, keepdims=True),
                    a1 + jnp.sum(dconv * u1, axis=0, keepdims=True),
                    a2 + jnp.sum(dconv * u, axis=0, keepdims=True))

        zero = jnp.zeros((1, cb), F32)
        a0, a1, a2 = lax.fori_loop(0, nchunk, first_pass, (zero, zero, zero))
        dw_ref[0:1, :] = a0
        dw_ref[1:2, :] = a1
        dw_ref[2:3, :] = a2
        dconv_ref[pl.ds(s, HALO), :] = jnp.zeros((HALO, cb), F32)

        def second_pass(i, carry):
            r0 = pl.multiple_of(i * rc, rc)
            dc = dconv_ref[pl.ds(r0, rc + HALO), :]
            du = (w2 * dc + w1 * _shift_up(dc, 1) + w0 * _shift_up(dc, 2))[:rc]
            cg = p_ref[1, pl.ds(r0, rc), :].astype(F32)
            v = p_ref[2, pl.ds(r0, rc), :].astype(F32)
            dp_ref[1, pl.ds(r0, rc), :] = (du * v).astype(BF16)
            dp_ref[2, pl.ds(r0, rc), :] = (du * cg).astype(BF16)
            return carry

        lax.fori_loop(0, nchunk, second_pass, 0)

    return pl.pallas_call(
        body, name="mid_a_bwd",
        grid=(e // cb,),
        in_specs=[pl.BlockSpec((s, cb), lambda c: (0, c)),
                  pl.BlockSpec((4, s, cb), lambda c: (0, 0, c)),
                  pl.BlockSpec((3, cb), lambda c: (0, c))],
        out_specs=[pl.BlockSpec((4, s, cb), lambda c: (0, 0, c)),
                   pl.BlockSpec((3, cb), lambda c: (0, c))],
        out_shape=[jax.ShapeDtypeStruct((4, s, e), BF16), jax.ShapeDtypeStruct((3, e), F32)],
        scratch_shapes=[pltpu.VMEM((s + HALO, cb), F32)],
        compiler_params=_cparams("arbitrary"),
    )(do, proj4, wconv)


def _mid_b_bwd(do, proj2, wgrp, scale):
    _, s, e = proj2.shape
    gdim = e // N_POOL_GROUPS
    rc = min(MID_CHUNK, s)
    nchunk = s // rc

    def body(do_ref, p_ref, w_ref, sc_ref, dp_ref, dw_ref, dsc_ref, ddq_ref, dd_ref, aw):
        grp = pl.program_id(0)

        def run(window):
            aw[...] = jnp.zeros_like(aw)
            scale_row = sc_ref[...]
            w = w_ref[...]

            def first_pass(i, dsc):
                r0 = pl.multiple_of(i * rc, rc)
                p0 = pl.multiple_of(jnp.maximum(r0 - HALO, 0), HALO)
                u = p_ref[0, pl.ds(r0, rc), :].astype(F32)
                up = jnp.where(i == 0, 0.0, p_ref[0, pl.ds(p0, HALO), :].astype(F32))
                db = _window_mean_minus(jnp.concatenate([up, u], axis=0), r0, window).astype(BF16)
                mr = _dot(db, w)
                z = p_ref[1, pl.ds(r0, rc), :].astype(F32)
                sig = _sigmoid(z)
                dof = do_ref[pl.ds(r0, rc), :].astype(F32)
                dp_ref[1, pl.ds(r0, rc), :] = (dof * (mr * scale_row) * (sig * (1.0 + z * (1.0 - sig)))).astype(BF16)
                dm = dof * (z * sig)
                dmr = (dm * scale_row).astype(BF16)
                aw[...] += _dot_tn(db, dmr)
                dd = _dot_nt(dmr, w)
                t = r0 + lax.broadcasted_iota(jnp.int32, (rc, 1), 0)
                dd_ref[pl.ds(r0, rc), :] = dd
                ddq_ref[pl.ds(r0, rc), :] = dd / jnp.minimum(t + 1, window).astype(F32)
                return dsc + jnp.sum(dm * mr, axis=0, keepdims=True)

            dsc_ref[...] = lax.fori_loop(0, nchunk, first_pass, jnp.zeros((1, gdim), F32))
            dw_ref[...] = aw[...].astype(BF16)
            ddq_ref[pl.ds(s, HALO), :] = jnp.zeros((HALO, gdim), F32)

            def second_pass(i, carry):
                r0 = pl.multiple_of(i * rc, rc)
                acc = ddq_ref[pl.ds(r0, rc + HALO), :]
                span = 1
                while span < window:
                    acc = acc + _shift_up(acc, span)
                    span *= 2
                dp_ref[0, pl.ds(r0, rc), :] = (acc[:rc] - dd_ref[pl.ds(r0, rc), :]).astype(BF16)
                return carry

            lax.fori_loop(0, nchunk, second_pass, 0)

        for k, window in enumerate(POOL_WINDOWS):
            pl.when(grp == k)(functools.partial(run, window))

    return pl.pallas_call(
        body, name="mid_b_bwd",
        grid=(N_POOL_GROUPS,),
        in_specs=[pl.BlockSpec((s, gdim), lambda g: (0, g)),
                  pl.BlockSpec((2, s, gdim), lambda g: (0, 0, g)),
                  pl.BlockSpec((None, gdim, gdim), lambda g: (g, 0, 0)),
                  pl.BlockSpec((1, gdim), lambda g: (0, g))],
        out_specs=[pl.BlockSpec((2, s, gdim), lambda g: (0, 0, g)),
                   pl.BlockSpec((None, gdim, gdim), lambda g: (g, 0, 0)),
                   pl.BlockSpec((1, gdim), lambda g: (0, g))],
        out_shape=[jax.ShapeDtypeStruct((2, s, e), BF16),
                   jax.ShapeDtypeStruct((N_POOL_GROUPS, gdim, gdim), BF16),
                   jax.ShapeDtypeStruct((1, e), F32)],
        scratch_shapes=[pltpu.VMEM((s + HALO, gdim), F32), pltpu.VMEM((s, gdim), F32),
                        pltpu.VMEM((gdim, gdim), F32)],
        compiler_params=_cparams("arbitrary"),
    )(do, proj2, wgrp, scale)


def _proj_wgrad(hn, dproj):
    s, d = hn.shape
    nsplit, _, e = dproj.shape
    ts = min(ROW_TILE, s)
    last = s // ts - 1

    def body(hn_ref, dp_ref, dw_ref, acc):
        i = pl.program_id(1)

        @pl.when(i == 0)
        def _():
            acc[...] = jnp.zeros_like(acc)

        acc[...] += _dot_tn(hn_ref[...], dp_ref[...])

        @pl.when(i == last)
        def _():
            dw_ref[...] = acc[...].astype(BF16)

    return pl.pallas_call(
        body, name="proj_wgrad",
        grid=(nsplit, s // ts),
        in_specs=[pl.BlockSpec((ts, d), lambda k, i: (i, 0)),
                  pl.BlockSpec((None, ts, e), lambda k, i: (k, i, 0))],
        out_specs=pl.BlockSpec((d, e), lambda k, i: (0, k)),
        out_shape=jax.ShapeDtypeStruct((d, nsplit * e), BF16),
        scratch_shapes=[pltpu.VMEM((d, e), F32)],
        compiler_params=_cparams("parallel", "arbitrary"),
    )(hn, dproj)


def _proj_xgrad(dproj, w, h, g, dh1):
    s, d = h.shape
    nsplit, _, e = dproj.shape
    ts = min(ROW_TILE, s)

    def body(dp_ref, w_ref, h_ref, g_ref, dh1_ref, dh_ref, dg_ref):
        @pl.when(pl.program_id(0) == 0)
        def _():
            dg_ref[...] = jnp.zeros_like(dg_ref)

        dhn = _dot_nt(dp_ref[0], w_ref[:, 0:e])
        for k in range(1, nsplit):
            dhn += _dot_nt(dp_ref[k], w_ref[:, k * e:(k + 1) * e])
        xh, r = _rms_stats(h_ref[...])
        dg_ref[...] += jnp.sum(dhn * xh, axis=0, keepdims=True)
        dh_ref[...] = dh1_ref[...] + _rms_bwd(dhn, xh, r, g_ref[...])

    row = pl.BlockSpec((ts, d), lambda i: (i, 0))
    return pl.pallas_call(
        body, name="proj_xgrad",
        grid=(s // ts,),
        in_specs=[pl.BlockSpec((nsplit, ts, e), lambda i: (0, i, 0)),
                  pl.BlockSpec((d, nsplit * e), lambda i: (0, 0)),
                  row, pl.BlockSpec((1, d), lambda i: (0, 0)), row],
        out_specs=[row, pl.BlockSpec((1, d), lambda i: (0, 0))],
        out_shape=[jax.ShapeDtypeStruct((s, d), F32), jax.ShapeDtypeStruct((1, d), F32)],
        compiler_params=_cparams("arbitrary"),
    )(dproj, w, h, g, dh1)


VMEM_SPEC = pl.BlockSpec(memory_space=pltpu.VMEM)

FLIPS = [(fx, fy, fc) for fx in (0, 1) for fy in (0, 1) for fc in (0, 1)][1:]
SHARD_AXIS = {"w_in": 1, "w_out": 0, "w_grp": 1, "gate": 0, "proj": 1}


def _my_place():
    return lax.axis_index("x"), lax.axis_index("y"), lax.axis_index("c")


def _position(place):
    x, y, c = place
    return 4 * x + 2 * y + c


def _flip(place, flips):
    return tuple(1 - v if f else v for v, f in zip(place, flips))


def _shard_of(ref, axis, pos, n):
    idx = [slice(None)] * len(ref.shape)
    idx[axis] = pl.ds(pl.multiple_of(pos * n, n), n)
    return ref.at[tuple(idx)]


def _sequencer_mesh():
    return plsc.ScalarSubcoreMesh(axis_name="sequencer", num_cores=1)


def _handshake(peers):
    barrier = pltpu.get_barrier_semaphore()
    for peer in peers:
        pl.semaphore_signal(barrier, inc=1, device_id=peer, device_id_type=MESH)
    pl.semaphore_wait(barrier, len(peers))


def _all_gather_layer(shards, collective_id):
    names = list(shards)
    nt = len(names)
    axes = [SHARD_AXIS[k] for k in names]
    widths = [shards[k].shape[SHARD_AXIS[k]] for k in names]

    def full_shape(k):
        shp = list(shards[k].shape)
        shp[SHARD_AXIS[k]] *= N_DEV
        return tuple(shp)

    def body(*refs):
        ins, outs = refs[:nt], refs[nt:2 * nt]
        send_sems, recv_sems, local_sem = refs[2 * nt:]
        me = _my_place()
        x, y, c = me
        sibling = (x, y, 1 - c)
        chips = [(1 - x, y), (x, 1 - y), (1 - x, 1 - y)]
        _handshake([sibling] + [(*chip, c) for chip in chips])

        def block(t, place):
            return _shard_of(outs[t], axes[t], _position(place), widths[t])

        def copy(t, k, place, to, src=None):
            return pltpu.make_async_remote_copy(
                src_ref=block(t, place) if src is None else src, dst_ref=block(t, place),
                send_sem=send_sems.at[k], recv_sem=recv_sems.at[k], device_id=to, device_id_type=MESH)

        mine = [pltpu.make_async_copy(ins[t], block(t, me), local_sem) for t in range(nt)]
        for cp in mine:
            cp.start()
        first = []
        for j, chip in enumerate(chips):
            first += [copy(t, 1 + j, me, (*chip, c), src=ins[t]) for t in range(nt)]
        first += [copy(t, 0, me, sibling, src=ins[t]) for t in range(nt)]
        for cp in first:
            cp.start()
        passed = []
        for j, chip in enumerate(chips):
            for t in range(nt):
                copy(t, 1 + j, (*chip, c), me).wait_recv()
            for t in range(nt):
                fwd = copy(t, 4 + j, (*chip, c), sibling)
                fwd.start()
                passed.append(fwd)
        for t in range(nt):
            copy(t, 0, sibling, me).wait_recv()
        for j, chip in enumerate(chips):
            for t in range(nt):
                copy(t, 4 + j, (*chip, 1 - c), me).wait_recv()
        for cp in first + passed:
            cp.wait_send()
        for cp in mine:
            cp.wait()

    outs = pl.kernel(
        body, name=f"all_gather_layer_{collective_id}",
        out_type=[jax.ShapeDtypeStruct(full_shape(k), shards[k].dtype) for k in names],
        mesh=_sequencer_mesh(),
        scratch_types=[pltpu.SemaphoreType.DMA((7,)), pltpu.SemaphoreType.DMA((7,)), pltpu.SemaphoreType.DMA],
        compiler_params=pltpu.CompilerParams(collective_id=collective_id),
    )(*[shards[k] for k in names])
    return dict(zip(names, outs))


def _exchange_layer(grads, collective_id):
    names = list(grads)
    nt = len(names)
    axes = [SHARD_AXIS[k] for k in names]
    widths = [grads[k].shape[SHARD_AXIS[k]] // N_DEV for k in names]

    def slot_shape(t):
        shp = list(grads[names[t]].shape)
        shp[axes[t]] = widths[t]
        return (N_DEV, *shp)

    def body(*refs):
        ins, outs = refs[:nt], refs[nt:2 * nt]
        send_sems, recv_sems, local_sem = refs[2 * nt:]
        me = _my_place()
        mine = _position(me)
        _handshake([_flip(me, flips) for flips in FLIPS])
        local = [pltpu.make_async_copy(_shard_of(ins[t], axes[t], mine, widths[t]), outs[t].at[mine], local_sem)
                 for t in range(nt)]
        for cp in local:
            cp.start()
        copies = []
        for k, flips in enumerate(FLIPS):
            peer = _flip(me, flips)
            for t in range(nt):
                cp = pltpu.make_async_remote_copy(
                    src_ref=_shard_of(ins[t], axes[t], _position(peer), widths[t]), dst_ref=outs[t].at[mine],
                    send_sem=send_sems.at[k], recv_sem=recv_sems.at[k], device_id=peer, device_id_type=MESH)
                cp.start()
                copies.append(cp)
        for cp in copies:
            cp.wait()
        for cp in local:
            cp.wait()

    outs = pl.kernel(
        body, name=f"exchange_layer_{collective_id}",
        out_type=[jax.ShapeDtypeStruct(slot_shape(t), BF16) for t in range(nt)],
        mesh=_sequencer_mesh(),
        scratch_types=[pltpu.SemaphoreType.DMA((7,)), pltpu.SemaphoreType.DMA((7,)), pltpu.SemaphoreType.DMA],
        compiler_params=pltpu.CompilerParams(collective_id=collective_id),
    )(*[grads[k] for k in names])
    return dict(zip(names, outs))


def _gather_rows(buf, reduce):
    r, c = buf.shape

    def body(in_ref, out_ref, *scratch):
        if reduce:
            all_ref, send_sems, recv_sems = scratch
        else:
            all_ref = out_ref
            send_sems, recv_sems = scratch
        me = _my_place()
        all_ref[_position(me)] = in_ref[...]
        copies = []
        for k, flips in enumerate(FLIPS):
            cp = pltpu.make_async_remote_copy(
                src_ref=in_ref, dst_ref=all_ref.at[_position(me)],
                send_sem=send_sems.at[k], recv_sem=recv_sems.at[k], device_id=_flip(me, flips), device_id_type=MESH)
            cp.start()
            copies.append(cp)
        for cp in copies:
            cp.wait()
        if reduce:
            total = all_ref[0]
            for j in range(1, N_DEV):
                total = total + all_ref[j]
            out_ref[...] = total

    return pl.pallas_call(
        body, name="sum_rows" if reduce else "gather_rows",
        in_specs=[VMEM_SPEC], out_specs=VMEM_SPEC,
        out_shape=jax.ShapeDtypeStruct((r, c) if reduce else (N_DEV, r, c), F32),
        scratch_shapes=([pltpu.VMEM((N_DEV, r, c), F32)] if reduce else [])
        + [pltpu.SemaphoreType.DMA((7,)), pltpu.SemaphoreType.DMA((7,))],
    )(buf)


def _adamw_math(w, g, m, v):
    m = ADAM_B1 * m + (1.0 - ADAM_B1) * g
    v = ADAM_B2 * v + (1.0 - ADAM_B2) * (g * g)
    m_hat = m / (1.0 - ADAM_B1 ** ADAM_STEP)
    v_hat = v / (1.0 - ADAM_B2 ** ADAM_STEP)
    delta = -ADAM_LR * (m_hat / (jnp.sqrt(v_hat) + ADAM_EPS) + ADAM_WD * w)
    return delta, m, v


def _adamw_pieces(pieces, w, m, v):
    shape = w.shape
    nl = shape[0]
    cols = shape[-1]
    rows = w.size // (nl * cols)
    tr = min(256 // nl, rows)
    flat3 = lambda a: a.reshape(nl, rows, cols)

    def body(*refs):
        p_refs = refs[:nl]
        w_ref, m_ref, v_ref, g_ref, d_ref, nm_ref, nv_ref = refs[nl:]
        for l in range(nl):
            g = p_refs[l][0].astype(F32)
            for j in range(1, N_DEV):
                g = g + p_refs[l][j].astype(F32)
            g_ref[l] = g
            d_ref[l], nm_ref[l], nv_ref[l] = _adamw_math(w_ref[l], g, m_ref[l], v_ref[l])

    blk = pl.BlockSpec((nl, tr, cols), lambda i: (0, i, 0))
    outs = pl.pallas_call(
        body, name="adamw_pieces",
        grid=(rows // tr,),
        in_specs=[pl.BlockSpec((N_DEV, tr, cols), lambda i: (0, i, 0))] * nl + [blk, blk, blk],
        out_specs=[blk] * 4,
        out_shape=[jax.ShapeDtypeStruct((nl, rows, cols), F32)] * 4,
        compiler_params=_cparams("parallel"),
    )(*[a.reshape(N_DEV, rows, cols) for a in pieces], flat3(w), flat3(m), flat3(v))
    return [a.reshape(shape) for a in outs]


def _adamw_small(g, w, m, v):
    shape = w.shape
    two = lambda a: a.reshape(-1, shape[-1])

    def body(g_ref, w_ref, m_ref, v_ref, d_ref, nm_ref, nv_ref):
        d_ref[...], nm_ref[...], nv_ref[...] = _adamw_math(w_ref[...], g_ref[...], m_ref[...], v_ref[...])

    outs = pl.pallas_call(
        body, name="adamw_small",
        in_specs=[VMEM_SPEC] * 4, out_specs=[VMEM_SPEC] * 3,
        out_shape=[jax.ShapeDtypeStruct(two(w).shape, F32)] * 3,
    )(two(g), two(w), two(m), two(v))
    return [a.reshape(shape) for a in outs]


WEIGHTS = ("norm_mix", "a_w_in", "a_w_conv", "a_w_out", "b_w_in", "b_w_grp", "b_scale", "b_w_out",
           "ple_norm", "ple_w_gate", "ple_w_proj", "final_norm")
SMALL_ROWS = 24
GATHER_ID = 0
EXCHANGE_ID = 4


def kernel(x, p, norm_mix, a_w_in, a_w_conv, a_w_out, b_w_in, b_w_grp, b_scale, b_w_out, ple_norm, ple_w_gate, ple_w_proj, final_norm, loss_target, m_norm_mix, m_a_w_in, m_a_w_conv, m_a_w_out, m_b_w_in, m_b_w_grp, m_b_scale, m_b_w_out, m_ple_norm, m_ple_w_gate, m_ple_w_proj, m_final_norm, v_norm_mix, v_a_w_in, v_a_w_conv, v_a_w_out, v_b_w_in, v_b_w_grp, v_b_scale, v_b_w_out, v_ple_norm, v_ple_w_gate, v_ple_w_proj, v_final_norm):
    wts = dict(norm_mix=norm_mix, a_w_in=a_w_in, a_w_conv=a_w_conv, a_w_out=a_w_out, b_w_in=b_w_in, b_w_grp=b_w_grp,
               b_scale=b_scale, b_w_out=b_w_out, ple_norm=ple_norm, ple_w_gate=ple_w_gate, ple_w_proj=ple_w_proj,
               final_norm=final_norm)
    mom = dict(norm_mix=m_norm_mix, a_w_in=m_a_w_in, a_w_conv=m_a_w_conv, a_w_out=m_a_w_out, b_w_in=m_b_w_in,
               b_w_grp=m_b_w_grp, b_scale=m_b_scale, b_w_out=m_b_w_out, ple_norm=m_ple_norm, ple_w_gate=m_ple_w_gate,
               ple_w_proj=m_ple_w_proj, final_norm=m_final_norm)
    var = dict(norm_mix=v_norm_mix, a_w_in=v_a_w_in, a_w_conv=v_a_w_conv, a_w_out=v_a_w_out, b_w_in=v_b_w_in,
               b_w_grp=v_b_w_grp, b_scale=v_b_scale, b_w_out=v_b_w_out, ple_norm=v_ple_norm, ple_w_gate=v_ple_w_gate,
               ple_w_proj=v_ple_w_proj, final_norm=v_final_norm)
    d = x.shape[2]
    depth = norm_mix.shape[0]
    n_a, n_b = a_w_conv.shape[0], b_scale.shape[0]
    cw = a_w_conv.shape[2]
    pos = _position(_my_place())
    xs, ps, target = x[0], p[:, 0], loss_target[0]
    row = lambda a, i: a[i][None, :]

    def layer_matrices(i):
        j = i // 2
        mixer = {"w_in": ("a_w_in", j), "w_out": ("a_w_out", j)} if i % 2 == 0 else \
                {"w_in": ("b_w_in", j), "w_grp": ("b_w_grp", j), "w_out": ("b_w_out", j)}
        return {**mixer, "gate": ("ple_w_gate", i), "proj": ("ple_w_proj", i)}

    full = [_all_gather_layer({k: wts[name][idx].astype(BF16) for k, (name, idx) in layer_matrices(i).items()},
                              GATHER_ID + i) for i in range(depth)]
    vec_rows = jnp.concatenate([a_w_conv.reshape(-1, cw), b_scale], axis=0)
    vecs = _gather_rows(vec_rows, reduce=False)
    n_conv = 3 * n_a
    conv_w = vecs[:, :n_conv].transpose(1, 0, 2).reshape(n_a, 3, N_DEV * cw)
    scale_w = vecs[:, n_conv:].transpose(1, 0, 2).reshape(n_b, N_DEV * cw)

    saved = []
    h = xs
    for i in range(depth):
        j, w = i // 2, full[i]
        if i % 2 == 0:
            proj, hn = _norm_proj(h, row(norm_mix, i), w["w_in"], 4)
            o = _mid_a_fwd(proj, conv_w[j])
        else:
            proj, hn = _norm_proj(h, row(norm_mix, i), w["w_in"], 2)
            o = _mid_b_fwd(proj, w["w_grp"], row(scale_w, j))
        h1, h2, gl, pp = _out_ple_fwd(h, o, w["w_out"], row(ple_norm, i), w["gate"], ps, w["proj"], i)
        saved.append((h, proj, hn, o, h1, gl, pp))
        h = h2
    dh, loss_row, d_final = _loss_head(h, target, final_norm[None, :])

    d_norm, d_ple_norm, d_conv, d_scale = [None] * depth, [None] * depth, [None] * n_a, [None] * n_b
    pieces = {name: [None] * wts[name].shape[0] for name in WEIGHTS if wts[name].ndim >= 3 and name != "a_w_conv"}
    for i in reversed(range(depth)):
        j, w = i // 2, full[i]
        h_in, proj, hn, o, h1, gl, pp = saved[i]
        g = {}
        dh1, do, g["proj"], g["gate"], g["w_out"], d_ple_norm[i] = _out_ple_bwd(
            dh, gl, pp, h1, ps, o, row(ple_norm, i), w["gate"], w["w_out"], i)
        if i % 2 == 0:
            dproj, d_conv[j] = _mid_a_bwd(do, proj, conv_w[j])
        else:
            dproj, g["w_grp"], d_scale[j] = _mid_b_bwd(do, proj, w["w_grp"], row(scale_w, j))
        g["w_in"] = _proj_wgrad(hn, dproj)
        dh, d_norm[i] = _proj_xgrad(dproj, w["w_in"], h_in, row(norm_mix, i), dh1)
        got = _exchange_layer(g, EXCHANGE_ID + i)
        for k, (name, idx) in layer_matrices(i).items():
            pieces[name][idx] = got[k]

    pad = lambda a: jnp.pad(a, ((0, 0), (0, d - a.shape[1])))
    small = jnp.concatenate(d_norm + d_ple_norm + [d_final] + d_conv + d_scale + [pad(loss_row)], axis=0)
    small = jnp.pad(small, ((0, SMALL_ROWS - small.shape[0]), (0, 0)))
    total = _gather_rows(small, reduce=True)
    o = 0
    gsum = {}
    gsum["norm_mix"] = total[o:o + depth]; o += depth
    gsum["ple_norm"] = total[o:o + depth]; o += depth
    gsum["final_norm"] = total[o]; o += 1
    conv_full = total[o:o + n_conv].reshape(n_a, 3, d); o += n_conv
    scale_full = total[o:o + n_b]; o += n_b
    loss = total[o, 0]
    gsum["a_w_conv"] = lax.dynamic_slice_in_dim(conv_full, pos * cw, cw, axis=2)
    gsum["b_scale"] = lax.dynamic_slice_in_dim(scale_full, pos * cw, cw, axis=1)

    grad, delta, new_m, new_v = {}, {}, {}, {}
    for k in WEIGHTS:
        if k in pieces:
            grad[k], delta[k], new_m[k], new_v[k] = _adamw_pieces(pieces[k], wts[k], mom[k], var[k])
        else:
            grad[k] = gsum[k]
            delta[k], new_m[k], new_v[k] = _adamw_small(gsum[k], wts[k], mom[k], var[k])
    return (loss, dh[None], *[grad[k] for k in WEIGHTS], *[delta[k] for k in WEIGHTS],
            *[new_m[k] for k in WEIGHTS], *[new_v[k] for k in WEIGHTS])
```

```python
import functools

import jax
import jax.numpy as jnp
from jax import lax
from jax.experimental import pallas as pl
from jax.experimental.pallas import tpu as pltpu
from jax.experimental.pallas import tpu_sc as plsc

F32 = jnp.float32
BF16 = jnp.bfloat16
MESH = pl.DeviceIdType.MESH

RMS_EPS = 1e-6
POOL_WINDOWS = (2, 4, 8, 16)
N_POOL_GROUPS = len(POOL_WINDOWS)
ADAM_LR = 0.001
ADAM_B1 = 0.9
ADAM_B2 = 0.999
ADAM_EPS = 1e-08
ADAM_WD = 0.01
ADAM_STEP = 10
N_DEV = 8

HALO = 16
ROW_TILE = 512
BWD_ROW_TILE = 512
WGRAD_ROW_TILE = 1024
MID_CHUNK = 256
VMEM_LIMIT = 56 * 1024 * 1024


def _cparams(*sem):
    return pltpu.CompilerParams(dimension_semantics=sem, vmem_limit_bytes=VMEM_LIMIT)


def _dot(a, b):
    return jnp.dot(a, b, preferred_element_type=F32)


def _dot_nt(a, b):
    return lax.dot_general(a, b, (((1,), (1,)), ((), ())), preferred_element_type=F32)


def _dot_tn(a, b):
    return lax.dot_general(a, b, (((0,), (0,)), ((), ())), preferred_element_type=F32)


def _rms_stats(x):
    r = lax.rsqrt(jnp.mean(x * x, axis=-1, keepdims=True) + RMS_EPS)
    return x * r, r


def _rms_bwd(dy, xh, r, g):
    a = dy * g
    return r * (a - xh * jnp.mean(a * xh, axis=-1, keepdims=True))


def _sigmoid(x):
    return 1.0 / (1.0 + jnp.exp(-x))


def _shift_down(x, k):
    return pltpu.roll(x, k, 0)


def _shift_up(x, k):
    return pltpu.roll(x, x.shape[0] - k, 0)


def _norm_proj(h, g, w, nsplit):
    s, d = h.shape
    n = w.shape[1]
    e = n // nsplit
    ts = min(ROW_TILE, s)

    def body(h_ref, g_ref, w_ref, proj_ref, hn_ref):
        xh, _ = _rms_stats(h_ref[...])
        hn = (xh * g_ref[...]).astype(BF16)
        hn_ref[...] = hn
        for k in range(nsplit):
            proj_ref[k] = _dot(hn, w_ref[:, k * e:(k + 1) * e]).astype(BF16)

    return pl.pallas_call(
        body, name="norm_proj",
        grid=(s // ts,),
        in_specs=[pl.BlockSpec((ts, d), lambda i: (i, 0)),
                  pl.BlockSpec((1, d), lambda i: (0, 0)),
                  pl.BlockSpec((d, n), lambda i: (0, 0))],
        out_specs=[pl.BlockSpec((nsplit, ts, e), lambda i: (0, i, 0)),
                   pl.BlockSpec((ts, d), lambda i: (i, 0))],
        out_shape=[jax.ShapeDtypeStruct((nsplit, s, e), BF16),
                   jax.ShapeDtypeStruct((s, d), BF16)],
        compiler_params=_cparams("arbitrary"),
    )(h, g, w)


def _conv_taps(u, u_prev, first):
    uu = jnp.concatenate([jnp.where(first, 0.0, u_prev), u], axis=0)
    return _shift_down(uu, 1)[HALO:], _shift_down(uu, 2)[HALO:]


def _mid_a_fwd(proj4, wconv):
    _, s, e = proj4.shape
    cb = e // N_POOL_GROUPS
    rc = min(MID_CHUNK, s)

    def body(p_ref, w_ref, o_ref):
        w0, w1, w2 = w_ref[0:1, :], w_ref[1:2, :], w_ref[2:3, :]

        def chunk(i, carry):
            r0 = pl.multiple_of(i * rc, rc)
            p0 = pl.multiple_of(jnp.maximum(r0 - HALO, 0), HALO)
            ld = lambda k: p_ref[k, pl.ds(r0, rc), :].astype(F32)
            ldp = lambda k: p_ref[k, pl.ds(p0, HALO), :].astype(F32)
            u = ld(1) * ld(2)
            u1, u2 = _conv_taps(u, ldp(1) * ldp(2), i == 0)
            conv = w0 * u2 + w1 * u1 + w2 * u
            z = ld(3)
            o_ref[pl.ds(r0, rc), :] = ((z * _sigmoid(z)) * (ld(0) * conv)).astype(BF16)
            return carry

        lax.fori_loop(0, s // rc, chunk, 0)

    return pl.pallas_call(
        body, name="mid_a_fwd",
        grid=(e // cb,),
        in_specs=[pl.BlockSpec((4, s, cb), lambda c: (0, 0, c)),
                  pl.BlockSpec((3, cb), lambda c: (0, c))],
        out_specs=pl.BlockSpec((s, cb), lambda c: (0, c)),
        out_shape=jax.ShapeDtypeStruct((s, e), BF16),
        compiler_params=_cparams("arbitrary"),
    )(proj4, wconv)


def _window_mean_minus(uu, row0, window):
    acc = uu
    span = 1
    while span < window:
        acc = acc + _shift_down(acc, span)
        span *= 2
    rows = uu.shape[0] - HALO
    t = row0 + lax.broadcasted_iota(jnp.int32, (rows, 1), 0)
    cnt = jnp.minimum(t + 1, window).astype(F32)
    return acc[HALO:] / cnt - uu[HALO:]


def _mid_b_fwd(proj2, wgrp, scale):
    _, s, e = proj2.shape
    gdim = e // N_POOL_GROUPS
    rc = min(MID_CHUNK, s)

    def body(p_ref, w_ref, sc_ref, o_ref):
        grp = pl.program_id(0)

        def run(window):
            def chunk(i, carry):
                r0 = pl.multiple_of(i * rc, rc)
                p0 = pl.multiple_of(jnp.maximum(r0 - HALO, 0), HALO)
                u = p_ref[0, pl.ds(r0, rc), :].astype(F32)
                up = jnp.where(i == 0, 0.0, p_ref[0, pl.ds(p0, HALO), :].astype(F32))
                d = _window_mean_minus(jnp.concatenate([up, u], axis=0), r0, window)
                mixed = _dot(d.astype(BF16), w_ref[...]) * sc_ref[...]
                z = p_ref[1, pl.ds(r0, rc), :].astype(F32)
                o_ref[pl.ds(r0, rc), :] = ((z * _sigmoid(z)) * mixed).astype(BF16)
                return carry

            lax.fori_loop(0, s // rc, chunk, 0)

        for k, window in enumerate(POOL_WINDOWS):
            pl.when(grp == k)(functools.partial(run, window))

    return pl.pallas_call(
        body, name="mid_b_fwd",
        grid=(N_POOL_GROUPS,),
        in_specs=[pl.BlockSpec((2, s, gdim), lambda g: (0, 0, g)),
                  pl.BlockSpec((None, gdim, gdim), lambda g: (g, 0, 0)),
                  pl.BlockSpec((1, gdim), lambda g: (0, g))],
        out_specs=pl.BlockSpec((s, gdim), lambda g: (0, g)),
        out_shape=jax.ShapeDtypeStruct((s, e), BF16),
        compiler_params=_cparams("arbitrary"),
    )(proj2, wgrp, scale)


def _out_ple_fwd(h, o, wout, pn, wgate, p_all, wproj, layer):
    s, d = h.shape
    e = o.shape[1]
    pdim = p_all.shape[2]
    ts = min(ROW_TILE, s)

    def body(h_ref, o_ref, wo_ref, pn_ref, wg_ref, p_ref, wp_ref, h1_ref, h2_ref, gl_ref, pp_ref):
        h1 = h_ref[...] + _dot(o_ref[...], wo_ref[...])
        h1_ref[...] = h1
        xh, _ = _rms_stats(h1)
        gl = _dot((xh * pn_ref[...]).astype(BF16), wg_ref[...])
        pp = _dot(p_ref[...].astype(BF16), wp_ref[...])
        gl_ref[...] = gl.astype(BF16)
        pp_ref[...] = pp.astype(BF16)
        h2_ref[...] = h1 + _sigmoid(gl) * pp

    row = lambda width: pl.BlockSpec((ts, width), lambda i: (i, 0))
    whole = lambda a, b: pl.BlockSpec((a, b), lambda i: (0, 0))
    return pl.pallas_call(
        body, name="out_ple_fwd",
        grid=(s // ts,),
        in_specs=[row(d), row(e), whole(e, d), whole(1, d), whole(d, d),
                  pl.BlockSpec((None, ts, pdim), lambda i: (layer, i, 0)),
                  whole(pdim, d)],
        out_specs=[row(d), row(d), row(d), row(d)],
        out_shape=[jax.ShapeDtypeStruct((s, d), F32), jax.ShapeDtypeStruct((s, d), F32),
                   jax.ShapeDtypeStruct((s, d), BF16), jax.ShapeDtypeStruct((s, d), BF16)],
        compiler_params=_cparams("arbitrary"),
    )(h, o, wout, pn, wgate, p_all, wproj)


def _loss_head(h, target, g):
    s, d = h.shape
    ts = min(ROW_TILE, s)

    def body(h_ref, t_ref, g_ref, dh_ref, loss_ref, dg_ref):
        @pl.when(pl.program_id(0) == 0)
        def _():
            loss_ref[...] = jnp.zeros_like(loss_ref)
            dg_ref[...] = jnp.zeros_like(dg_ref)

        gain = g_ref[...]
        xh, r = _rms_stats(h_ref[...])
        err = xh * gain - t_ref[...]
        loss_ref[...] += jnp.full(loss_ref.shape, (0.5 / d) * jnp.sum(err * err), F32)
        dy = err * (1.0 / d)
        dg_ref[...] += jnp.sum(dy * xh, axis=0, keepdims=True)
        dh_ref[...] = _rms_bwd(dy, xh, r, gain)

    row = pl.BlockSpec((ts, d), lambda i: (i, 0))
    return pl.pallas_call(
        body, name="loss_head",
        grid=(s // ts,),
        in_specs=[row, row, pl.BlockSpec((1, d), lambda i: (0, 0))],
        out_specs=[row, pl.BlockSpec((1, 128), lambda i: (0, 0)), pl.BlockSpec((1, d), lambda i: (0, 0))],
        out_shape=[jax.ShapeDtypeStruct((s, d), F32), jax.ShapeDtypeStruct((1, 128), F32),
                   jax.ShapeDtypeStruct((1, d), F32)],
        compiler_params=_cparams("arbitrary"),
    )(h, target, g)


def _out_ple_bwd(dh2, gl, pp, h1, p_all, o, pn, wgate, wout, layer):
    s, d = dh2.shape
    e = o.shape[1]
    pdim = p_all.shape[2]
    ts = min(BWD_ROW_TILE, s)
    last = s // ts - 1

    def body(dh2_ref, gl_ref, pp_ref, h1_ref, p_ref, o_ref, pn_ref, wg_ref, wo_ref,
             dh1_ref, do_ref, dwp_ref, dwg_ref, dwo_ref, dpn_ref, awp, awg, awo):
        i = pl.program_id(0)

        @pl.when(i == 0)
        def _():
            awp[...] = jnp.zeros_like(awp)
            awg[...] = jnp.zeros_like(awg)
            awo[...] = jnp.zeros_like(awo)
            dpn_ref[...] = jnp.zeros_like(dpn_ref)

        dh2 = dh2_ref[...]
        gate = _sigmoid(gl_ref[...].astype(F32))
        dpp = (dh2 * gate).astype(BF16)
        dgl = (dh2 * pp_ref[...].astype(F32) * gate * (1.0 - gate)).astype(BF16)
        xh, r = _rms_stats(h1_ref[...])
        pn = pn_ref[...]
        awp[...] += _dot_tn(p_ref[...].astype(BF16), dpp)
        awg[...] += _dot_tn((xh * pn).astype(BF16), dgl)
        dr = _dot_nt(dgl, wg_ref[...])
        dpn_ref[...] += jnp.sum(dr * xh, axis=0, keepdims=True)
        dh1 = dh2 + _rms_bwd(dr, xh, r, pn)
        dh1_ref[...] = dh1
        dh1b = dh1.astype(BF16)
        do_ref[...] = _dot_nt(dh1b, wo_ref[...]).astype(BF16)
        awo[...] += _dot_tn(o_ref[...], dh1b)

        @pl.when(i == last)
        def _():
            dwp_ref[...] = awp[...].astype(BF16)
            dwg_ref[...] = awg[...].astype(BF16)
            dwo_ref[...] = awo[...].astype(BF16)

    row = lambda width: pl.BlockSpec((ts, width), lambda i: (i, 0))
    whole = lambda a, b: pl.BlockSpec((a, b), lambda i: (0, 0), pipeline_mode=pl.Buffered(1))
    return pl.pallas_call(
        body, name="out_ple_bwd",
        grid=(s // ts,),
        in_specs=[row(d), row(d), row(d), row(d),
                  pl.BlockSpec((None, ts, pdim), lambda i: (layer, i, 0)),
                  row(e), whole(1, d), whole(d, d), whole(e, d)],
        out_specs=[row(d), row(e), whole(pdim, d), whole(d, d), whole(e, d), whole(1, d)],
        out_shape=[jax.ShapeDtypeStruct((s, d), F32), jax.ShapeDtypeStruct((s, e), BF16),
                   jax.ShapeDtypeStruct((pdim, d), BF16), jax.ShapeDtypeStruct((d, d), BF16),
                   jax.ShapeDtypeStruct((e, d), BF16), jax.ShapeDtypeStruct((1, d), F32)],
        scratch_shapes=[pltpu.VMEM((pdim, d), F32), pltpu.VMEM((d, d), F32), pltpu.VMEM((e, d), F32)],
        compiler_params=_cparams("arbitrary"),
    )(dh2, gl, pp, h1, p_all, o, pn, wgate, wout)


def _mid_a_bwd(do, proj4, wconv):
    _, s, e = proj4.shape
    cb = e // N_POOL_GROUPS
    rc = min(MID_CHUNK, s)
    nchunk = s // rc

    def body(do_ref, p_ref, w_ref, dp_ref, dw_ref, dconv_ref):
        w0, w1, w2 = w_ref[0:1, :], w_ref[1:2, :], w_ref[2:3, :]

        def first_pass(i, acc):
            r0 = pl.multiple_of(i * rc, rc)
            p0 = pl.multiple_of(jnp.maximum(r0 - HALO, 0), HALO)
            ld = lambda k: p_ref[k, pl.ds(r0, rc), :].astype(F32)
            ldp = lambda k: p_ref[k, pl.ds(p0, HALO), :].astype(F32)
            u = ld(1) * ld(2)
            u1, u2 = _conv_taps(u, ldp(1) * ldp(2), i == 0)
            conv = w0 * u2 + w1 * u1 + w2 * u
            b, z = ld(0), ld(3)
            sig = _sigmoid(z)
            dof = do_ref[pl.ds(r0, rc), :].astype(F32)
            dy = dof * (z * sig)
            dp_ref[3, pl.ds(r0, rc), :] = (dof * (b * conv) * (sig * (1.0 + z * (1.0 - sig)))).astype(BF16)
            dp_ref[0, pl.ds(r0, rc), :] = (dy * conv).astype(BF16)
            dconv = dy * b
            dconv_ref[pl.ds(r0, rc), :] = dconv
            a0, a1, a2 = acc
            return (a0 + jnp.sum(dconv * u2, axis=0, keepdims=True),
                    a1 + jnp.sum(dconv * u1, axis=0, keepdims=True),
                    a2 + jnp.sum(dconv * u, axis=0, keepdims=True))

        zero = jnp.zeros((1, cb), F32)
        a0, a1, a2 = lax.fori_loop(0, nchunk, first_pass, (zero, zero, zero))
        dw_ref[0:1, :] = a0
        dw_ref[1:2, :] = a1
        dw_ref[2:3, :] = a2
        dconv_ref[pl.ds(s, HALO), :] = jnp.zeros((HALO, cb), F32)

        def second_pass(i, carry):
            r0 = pl.multiple_of(i * rc, rc)
            dc = dconv_ref[pl.ds(r0, rc + HALO), :]
            du = (w2 * dc + w1 * _shift_up(dc, 1) + w0 * _shift_up(dc, 2))[:rc]
            cg = p_ref[1, pl.ds(r0, rc), :].astype(F32)
            v = p_ref[2, pl.ds(r0, rc), :].astype(F32)
            dp_ref[1, pl.ds(r0, rc), :] = (du * v).astype(BF16)
            dp_ref[2, pl.ds(r0, rc), :] = (du * cg).astype(BF16)
            return carry

        lax.fori_loop(0, nchunk, second_pass, 0)

    return pl.pallas_call(
        body, name="mid_a_bwd",
        grid=(e // cb,),
        in_specs=[pl.BlockSpec((s, cb), lambda c: (0, c)),
                  pl.BlockSpec((4, s, cb), lambda c: (0, 0, c)),
                  pl.BlockSpec((3, cb), lambda c: (0, c))],
        out_specs=[pl.BlockSpec((4, s, cb), lambda c: (0, 0, c)),
                   pl.BlockSpec((3, cb), lambda c: (0, c))],
        out_shape=[jax.ShapeDtypeStruct((4, s, e), BF16), jax.ShapeDtypeStruct((3, e), F32)],
        scratch_shapes=[pltpu.VMEM((s + HALO, cb), F32)],
        compiler_params=_cparams("arbitrary"),
    )(do, proj4, wconv)


def _mid_b_bwd(do, proj2, wgrp, scale):
    _, s, e = proj2.shape
    gdim = e // N_POOL_GROUPS
    rc = min(MID_CHUNK, s)
    nchunk = s // rc

    def body(do_ref, p_ref, w_ref, sc_ref, dp_ref, dw_ref, dsc_ref, ddq_ref, dd_ref, aw):
        grp = pl.program_id(0)

        def run(window):
            aw[...] = jnp.zeros_like(aw)
            scale_row = sc_ref[...]
            w = w_ref[...]

            def first_pass(i, dsc):
                r0 = pl.multiple_of(i * rc, rc)
                p0 = pl.multiple_of(jnp.maximum(r0 - HALO, 0), HALO)
                u = p_ref[0, pl.ds(r0, rc), :].astype(F32)
                up = jnp.where(i == 0, 0.0, p_ref[0, pl.ds(p0, HALO), :].astype(F32))
                db = _window_mean_minus(jnp.concatenate([up, u], axis=0), r0, window).astype(BF16)
                mr = _dot(db, w)
                z = p_ref[1, pl.ds(r0, rc), :].astype(F32)
                sig = _sigmoid(z)
                dof = do_ref[pl.ds(r0, rc), :].astype(F32)
                dp_ref[1, pl.ds(r0, rc), :] = (dof * (mr * scale_row) * (sig * (1.0 + z * (1.0 - sig)))).astype(BF16)
                dm = dof * (z * sig)
                dmr = (dm * scale_row).astype(BF16)
                aw[...] += _dot_tn(db, dmr)
                dd = _dot_nt(dmr, w)
                t = r0 + lax.broadcasted_iota(jnp.int32, (rc, 1), 0)
                dd_ref[pl.ds(r0, rc), :] = dd
                ddq_ref[pl.ds(r0, rc), :] = dd / jnp.minimum(t + 1, window).astype(F32)
                return dsc + jnp.sum(dm * mr, axis=0, keepdims=True)

            dsc_ref[...] = lax.fori_loop(0, nchunk, first_pass, jnp.zeros((1, gdim), F32))
            dw_ref[...] = aw[...].astype(BF16)
            ddq_ref[pl.ds(s, HALO), :] = jnp.zeros((HALO, gdim), F32)

            def second_pass(i, carry):
                r0 = pl.multiple_of(i * rc, rc)
                acc = ddq_ref[pl.ds(r0, rc + HALO), :]
                span = 1
                while span < window:
                    acc = acc + _shift_up(acc, span)
                    span *= 2
                dp_ref[0, pl.ds(r0, rc), :] = (acc[:rc] - dd_ref[pl.ds(r0, rc), :]).astype(BF16)
                return carry

            lax.fori_loop(0, nchunk, second_pass, 0)

        for k, window in enumerate(POOL_WINDOWS):
            pl.when(grp == k)(functools.partial(run, window))

    return pl.pallas_call(
        body, name="mid_b_bwd",
        grid=(N_POOL_GROUPS,),
        in_specs=[pl.BlockSpec((s, gdim), lambda g: (0, g)),
                  pl.BlockSpec((2, s, gdim), lambda g: (0, 0, g)),
                  pl.BlockSpec((None, gdim, gdim), lambda g: (g, 0, 0)),
                  pl.BlockSpec((1, gdim), lambda g: (0, g))],
        out_specs=[pl.BlockSpec((2, s, gdim), lambda g: (0, 0, g)),
                   pl.BlockSpec((None, gdim, gdim), lambda g: (g, 0, 0)),
                   pl.BlockSpec((1, gdim), lambda g: (0, g))],
        out_shape=[jax.ShapeDtypeStruct((2, s, e), BF16),
                   jax.ShapeDtypeStruct((N_POOL_GROUPS, gdim, gdim), BF16),
                   jax.ShapeDtypeStruct((1, e), F32)],
        scratch_shapes=[pltpu.VMEM((s + HALO, gdim), F32), pltpu.VMEM((s, gdim), F32),
                        pltpu.VMEM((gdim, gdim), F32)],
        compiler_params=_cparams("arbitrary"),
    )(do, proj2, wgrp, scale)


def _proj_wgrad(hn, dproj):
    s, d = hn.shape
    nsplit, _, e = dproj.shape
    ts = min(WGRAD_ROW_TILE, s)
    last = s // ts - 1

    def body(hn_ref, dp_ref, dw_ref, acc):
        i = pl.program_id(1)

        @pl.when(i == 0)
        def _():
            acc[...] = jnp.zeros_like(acc)

        acc[...] += _dot_tn(hn_ref[...], dp_ref[...])

        @pl.when(i == last)
        def _():
            dw_ref[...] = acc[...].astype(BF16)

    return pl.pallas_call(
        body, name="proj_wgrad",
        grid=(nsplit, s // ts),
        in_specs=[pl.BlockSpec((ts, d), lambda k, i: (i, 0)),
                  pl.BlockSpec((None, ts, e), lambda k, i: (k, i, 0))],
        out_specs=pl.BlockSpec((d, e), lambda k, i: (0, k)),
        out_shape=jax.ShapeDtypeStruct((d, nsplit * e), BF16),
        scratch_shapes=[pltpu.VMEM((d, e), F32)],
        compiler_params=_cparams("parallel", "arbitrary"),
    )(hn, dproj)


def _proj_xgrad(dproj, w, h, g, dh1):
    s, d = h.shape
    nsplit, _, e = dproj.shape
    ts = min(ROW_TILE, s)

    def body(dp_ref, w_ref, h_ref, g_ref, dh1_ref, dh_ref, dg_ref):
        @pl.when(pl.program_id(0) == 0)
        def _():
            dg_ref[...] = jnp.zeros_like(dg_ref)

        dhn = _dot_nt(dp_ref[0], w_ref[:, 0:e])
        for k in range(1, nsplit):
            dhn += _dot_nt(dp_ref[k], w_ref[:, k * e:(k + 1) * e])
        xh, r = _rms_stats(h_ref[...])
        dg_ref[...] += jnp.sum(dhn * xh, axis=0, keepdims=True)
        dh_ref[...] = dh1_ref[...] + _rms_bwd(dhn, xh, r, g_ref[...])

    row = pl.BlockSpec((ts, d), lambda i: (i, 0))
    return pl.pallas_call(
        body, name="proj_xgrad",
        grid=(s // ts,),
        in_specs=[pl.BlockSpec((nsplit, ts, e), lambda i: (0, i, 0)),
                  pl.BlockSpec((d, nsplit * e), lambda i: (0, 0)),
                  row, pl.BlockSpec((1, d), lambda i: (0, 0)), row],
        out_specs=[row, pl.BlockSpec((1, d), lambda i: (0, 0))],
        out_shape=[jax.ShapeDtypeStruct((s, d), F32), jax.ShapeDtypeStruct((1, d), F32)],
        compiler_params=_cparams("arbitrary"),
    )(dproj, w, h, g, dh1)


VMEM_SPEC = pl.BlockSpec(memory_space=pltpu.VMEM)

FLIPS = [(fx, fy, fc) for fx in (0, 1) for fy in (0, 1) for fc in (0, 1)][1:]
SHARD_AXIS = {"w_in": 1, "w_out": 0, "w_grp": 1, "gate": 0, "proj": 1}


def _my_place():
    return lax.axis_index("x"), lax.axis_index("y"), lax.axis_index("c")


def _position(place):
    x, y, c = place
    return 4 * x + 2 * y + c


def _flip(place, flips):
    return tuple(1 - v if f else v for v, f in zip(place, flips))


def _shard_of(ref, axis, pos, n):
    idx = [slice(None)] * len(ref.shape)
    idx[axis] = pl.ds(pl.multiple_of(pos * n, n), n)
    return ref.at[tuple(idx)]


def _sequencer_mesh():
    return plsc.ScalarSubcoreMesh(axis_name="sequencer", num_cores=1)


def _handshake(peers):
    barrier = pltpu.get_barrier_semaphore()
    for peer in peers:
        pl.semaphore_signal(barrier, inc=1, device_id=peer, device_id_type=MESH)
    pl.semaphore_wait(barrier, len(peers))


def _all_gather_layer(shards, collective_id):
    names = list(shards)
    nt = len(names)
    axes = [SHARD_AXIS[k] for k in names]
    widths = [shards[k].shape[SHARD_AXIS[k]] for k in names]

    def full_shape(k):
        shp = list(shards[k].shape)
        shp[SHARD_AXIS[k]] *= N_DEV
        return tuple(shp)

    def body(*refs):
        ins, outs = refs[:nt], refs[nt:2 * nt]
        send_sems, recv_sems, local_sem = refs[2 * nt:]
        me = _my_place()
        x, y, c = me
        sibling = (x, y, 1 - c)
        chips = [(1 - x, y), (x, 1 - y), (1 - x, 1 - y)]
        _handshake([sibling] + [(*chip, c) for chip in chips])

        def block(t, place):
            return _shard_of(outs[t], axes[t], _position(place), widths[t])

        def copy(t, k, place, to, src=None):
            return pltpu.make_async_remote_copy(
                src_ref=block(t, place) if src is None else src, dst_ref=block(t, place),
                send_sem=send_sems.at[k], recv_sem=recv_sems.at[k], device_id=to, device_id_type=MESH)

        mine = [pltpu.make_async_copy(ins[t], block(t, me), local_sem) for t in range(nt)]
        for cp in mine:
            cp.start()
        first = []
        for j, chip in enumerate(chips):
            first += [copy(t, 1 + j, me, (*chip, c), src=ins[t]) for t in range(nt)]
        first += [copy(t, 0, me, sibling, src=ins[t]) for t in range(nt)]
        for cp in first:
            cp.start()
        passed = []
        for j, chip in enumerate(chips):
            for t in range(nt):
                copy(t, 1 + j, (*chip, c), me).wait_recv()
            for t in range(nt):
                fwd = copy(t, 4 + j, (*chip, c), sibling)
                fwd.start()
                passed.append(fwd)
        for t in range(nt):
            copy(t, 0, sibling, me).wait_recv()
        for j, chip in enumerate(chips):
            for t in range(nt):
                copy(t, 4 + j, (*chip, 1 - c), me).wait_recv()
        for cp in first + passed:
            cp.wait_send()
        for cp in mine:
            cp.wait()

    outs = pl.kernel(
        body, name=f"all_gather_layer_{collective_id}",
        out_type=[jax.ShapeDtypeStruct(full_shape(k), shards[k].dtype) for k in names],
        mesh=_sequencer_mesh(),
        scratch_types=[pltpu.SemaphoreType.DMA((7,)), pltpu.SemaphoreType.DMA((7,)), pltpu.SemaphoreType.DMA],
        compiler_params=pltpu.CompilerParams(collective_id=collective_id),
    )(*[shards[k] for k in names])
    return dict(zip(names, outs))


def _exchange_layer(grads, collective_id):
    names = list(grads)
    nt = len(names)
    axes = [SHARD_AXIS[k] for k in names]
    widths = [grads[k].shape[SHARD_AXIS[k]] // N_DEV for k in names]

    def slot_shape(t):
        shp = list(grads[names[t]].shape)
        shp[axes[t]] = widths[t]
        return (N_DEV, *shp)

    def body(*refs):
        ins, outs = refs[:nt], refs[nt:2 * nt]
        send_sems, recv_sems, local_sem = refs[2 * nt:]
        me = _my_place()
        mine = _position(me)
        _handshake([_flip(me, flips) for flips in FLIPS])
        local = [pltpu.make_async_copy(_shard_of(ins[t], axes[t], mine, widths[t]), outs[t].at[mine], local_sem)
                 for t in range(nt)]
        for cp in local:
            cp.start()
        copies = []
        for k, flips in enumerate(FLIPS):
            peer = _flip(me, flips)
            for t in range(nt):
                cp = pltpu.make_async_remote_copy(
                    src_ref=_shard_of(ins[t], axes[t], _position(peer), widths[t]), dst_ref=outs[t].at[mine],
                    send_sem=send_sems.at[k], recv_sem=recv_sems.at[k], device_id=peer, device_id_type=MESH)
                cp.start()
                copies.append(cp)
        for cp in copies:
            cp.wait()
        for cp in local:
            cp.wait()

    outs = pl.kernel(
        body, name=f"exchange_layer_{collective_id}",
        out_type=[jax.ShapeDtypeStruct(slot_shape(t), BF16) for t in range(nt)],
        mesh=_sequencer_mesh(),
        scratch_types=[pltpu.SemaphoreType.DMA((7,)), pltpu.SemaphoreType.DMA((7,)), pltpu.SemaphoreType.DMA],
        compiler_params=pltpu.CompilerParams(collective_id=collective_id),
    )(*[grads[k] for k in names])
    return dict(zip(names, outs))


def _gather_rows(buf, reduce):
    r, c = buf.shape

    def body(in_ref, out_ref, *scratch):
        if reduce:
            all_ref, send_sems, recv_sems = scratch
        else:
            all_ref = out_ref
            send_sems, recv_sems = scratch
        me = _my_place()
        all_ref[_position(me)] = in_ref[...]
        copies = []
        for k, flips in enumerate(FLIPS):
            cp = pltpu.make_async_remote_copy(
                src_ref=in_ref, dst_ref=all_ref.at[_position(me)],
                send_sem=send_sems.at[k], recv_sem=recv_sems.at[k], device_id=_flip(me, flips), device_id_type=MESH)
            cp.start()
            copies.append(cp)
        for cp in copies:
            cp.wait()
        if reduce:
            total = all_ref[0]
            for j in range(1, N_DEV):
                total = total + all_ref[j]
            out_ref[...] = total

    return pl.pallas_call(
        body, name="sum_rows" if reduce else "gather_rows",
        in_specs=[VMEM_SPEC], out_specs=VMEM_SPEC,
        out_shape=jax.ShapeDtypeStruct((r, c) if reduce else (N_DEV, r, c), F32),
        scratch_shapes=([pltpu.VMEM((N_DEV, r, c), F32)] if reduce else [])
        + [pltpu.SemaphoreType.DMA((7,)), pltpu.SemaphoreType.DMA((7,))],
    )(buf)


def _adamw_math(w, g, m, v):
    m = ADAM_B1 * m + (1.0 - ADAM_B1) * g
    v = ADAM_B2 * v + (1.0 - ADAM_B2) * (g * g)
    m_hat = m / (1.0 - ADAM_B1 ** ADAM_STEP)
    v_hat = v / (1.0 - ADAM_B2 ** ADAM_STEP)
    delta = -ADAM_LR * (m_hat / (jnp.sqrt(v_hat) + ADAM_EPS) + ADAM_WD * w)
    return delta, m, v


def _adamw_pieces(pieces, w, m, v):
    shape = w.shape
    nl = shape[0]
    cols = shape[-1]
    rows = w.size // (nl * cols)
    tr = min(256 // nl, rows)
    flat3 = lambda a: a.reshape(nl, rows, cols)

    def body(*refs):
        p_refs = refs[:nl]
        w_ref, m_ref, v_ref, g_ref, d_ref, nm_ref, nv_ref = refs[nl:]
        for l in range(nl):
            g = p_refs[l][0].astype(F32)
            for j in range(1, N_DEV):
                g = g + p_refs[l][j].astype(F32)
            g_ref[l] = g
            d_ref[l], nm_ref[l], nv_ref[l] = _adamw_math(w_ref[l], g, m_ref[l], v_ref[l])

    blk = pl.BlockSpec((nl, tr, cols), lambda i: (0, i, 0))
    outs = pl.pallas_call(
        body, name="adamw_pieces",
        grid=(rows // tr,),
        in_specs=[pl.BlockSpec((N_DEV, tr, cols), lambda i: (0, i, 0))] * nl + [blk, blk, blk],
        out_specs=[blk] * 4,
        out_shape=[jax.ShapeDtypeStruct((nl, rows, cols), F32)] * 4,
        compiler_params=_cparams("parallel"),
    )(*[a.reshape(N_DEV, rows, cols) for a in pieces], flat3(w), flat3(m), flat3(v))
    return [a.reshape(shape) for a in outs]


def _adamw_small(g, w, m, v):
    shape = w.shape
    two = lambda a: a.reshape(-1, shape[-1])

    def body(g_ref, w_ref, m_ref, v_ref, d_ref, nm_ref, nv_ref):
        d_ref[...], nm_ref[...], nv_ref[...] = _adamw_math(w_ref[...], g_ref[...], m_ref[...], v_ref[...])

    outs = pl.pallas_call(
        body, name="adamw_small",
        in_specs=[VMEM_SPEC] * 4, out_specs=[VMEM_SPEC] * 3,
        out_shape=[jax.ShapeDtypeStruct(two(w).shape, F32)] * 3,
    )(two(g), two(w), two(m), two(v))
    return [a.reshape(shape) for a in outs]


WEIGHTS = ("norm_mix", "a_w_in", "a_w_conv", "a_w_out", "b_w_in", "b_w_grp", "b_scale", "b_w_out",
           "ple_norm", "ple_w_gate", "ple_w_proj", "final_norm")
SMALL_ROWS = 24
GATHER_ID = 0
EXCHANGE_ID = 4


def kernel(x, p, norm_mix, a_w_in, a_w_conv, a_w_out, b_w_in, b_w_grp, b_scale, b_w_out, ple_norm, ple_w_gate, ple_w_proj, final_norm, loss_target, m_norm_mix, m_a_w_in, m_a_w_conv, m_a_w_out, m_b_w_in, m_b_w_grp, m_b_scale, m_b_w_out, m_ple_norm, m_ple_w_gate, m_ple_w_proj, m_final_norm, v_norm_mix, v_a_w_in, v_a_w_conv, v_a_w_out, v_b_w_in, v_b_w_grp, v_b_scale, v_b_w_out, v_ple_norm, v_ple_w_gate, v_ple_w_proj, v_final_norm):
    wts = dict(norm_mix=norm_mix, a_w_in=a_w_in, a_w_conv=a_w_conv, a_w_out=a_w_out, b_w_in=b_w_in, b_w_grp=b_w_grp,
               b_scale=b_scale, b_w_out=b_w_out, ple_norm=ple_norm, ple_w_gate=ple_w_gate, ple_w_proj=ple_w_proj,
               final_norm=final_norm)
    mom = dict(norm_mix=m_norm_mix, a_w_in=m_a_w_in, a_w_conv=m_a_w_conv, a_w_out=m_a_w_out, b_w_in=m_b_w_in,
               b_w_grp=m_b_w_grp, b_scale=m_b_scale, b_w_out=m_b_w_out, ple_norm=m_ple_norm, ple_w_gate=m_ple_w_gate,
               ple_w_proj=m_ple_w_proj, final_norm=m_final_norm)
    var = dict(norm_mix=v_norm_mix, a_w_in=v_a_w_in, a_w_conv=v_a_w_conv, a_w_out=v_a_w_out, b_w_in=v_b_w_in,
               b_w_grp=v_b_w_grp, b_scale=v_b_scale, b_w_out=v_b_w_out, ple_norm=v_ple_norm, ple_w_gate=v_ple_w_gate,
               ple_w_proj=v_ple_w_proj, final_norm=v_final_norm)
    d = x.shape[2]
    depth = norm_mix.shape[0]
    n_a, n_b = a_w_conv.shape[0], b_scale.shape[0]
    cw = a_w_conv.shape[2]
    pos = _position(_my_place())
    xs, ps, target = x[0], p[:, 0], loss_target[0]
    row = lambda a, i: a[i][None, :]

    def layer_matrices(i):
        j = i // 2
        mixer = {"w_in": ("a_w_in", j), "w_out": ("a_w_out", j)} if i % 2 == 0 else \
                {"w_in": ("b_w_in", j), "w_grp": ("b_w_grp", j), "w_out": ("b_w_out", j)}
        return {**mixer, "gate": ("ple_w_gate", i), "proj": ("ple_w_proj", i)}

    full = [_all_gather_layer({k: wts[name][idx].astype(BF16) for k, (name, idx) in layer_matrices(i).items()},
                              GATHER_ID + i) for i in range(depth)]
    vec_rows = jnp.concatenate([a_w_conv.reshape(-1, cw), b_scale], axis=0)
    vecs = _gather_rows(vec_rows, reduce=False)
    n_conv = 3 * n_a
    conv_w = vecs[:, :n_conv].transpose(1, 0, 2).reshape(n_a, 3, N_DEV * cw)
    scale_w = vecs[:, n_conv:].transpose(1, 0, 2).reshape(n_b, N_DEV * cw)

    saved = []
    h = xs
    for i in range(depth):
        j, w = i // 2, full[i]
        if i % 2 == 0:
            proj, hn = _norm_proj(h, row(norm_mix, i), w["w_in"], 4)
            o = _mid_a_fwd(proj, conv_w[j])
        else:
            proj, hn = _norm_proj(h, row(norm_mix, i), w["w_in"], 2)
            o = _mid_b_fwd(proj, w["w_grp"], row(scale_w, j))
        h1, h2, gl, pp = _out_ple_fwd(h, o, w["w_out"], row(ple_norm, i), w["gate"], ps, w["proj"], i)
        saved.append((h, proj, hn, o, h1, gl, pp))
        h = h2
    dh, loss_row, d_final = _loss_head(h, target, final_norm[None, :])

    d_norm, d_ple_norm, d_conv, d_scale = [None] * depth, [None] * depth, [None] * n_a, [None] * n_b
    pieces = {name: [None] * wts[name].shape[0] for name in WEIGHTS if wts[name].ndim >= 3 and name != "a_w_conv"}
    for i in reversed(range(depth)):
        j, w = i // 2, full[i]
        h_in, proj, hn, o, h1, gl, pp = saved[i]
        g = {}
        dh1, do, g["proj"], g["gate"], g["w_out"], d_ple_norm[i] = _out_ple_bwd(
            dh, gl, pp, h1, ps, o, row(ple_norm, i), w["gate"], w["w_out"], i)
        if i % 2 == 0:
            dproj, d_conv[j] = _mid_a_bwd(do, proj, conv_w[j])
        else:
            dproj, g["w_grp"], d_scale[j] = _mid_b_bwd(do, proj, w["w_grp"], row(scale_w, j))
        g["w_in"] = _proj_wgrad(hn, dproj)
        dh, d_norm[i] = _proj_xgrad(dproj, w["w_in"], h_in, row(norm_mix, i), dh1)
        got = _exchange_layer(g, EXCHANGE_ID + i)
        for k, (name, idx) in layer_matrices(i).items():
            pieces[name][idx] = got[k]

    pad = lambda a: jnp.pad(a, ((0, 0), (0, d - a.shape[1])))
    small = jnp.concatenate(d_norm + d_ple_norm + [d_final] + d_conv + d_scale + [pad(loss_row)], axis=0)
    small = jnp.pad(small, ((0, SMALL_ROWS - small.shape[0]), (0, 0)))
    total = _gather_rows(small, reduce=True)
    o = 0
    gsum = {}
    gsum["norm_mix"] = total[o:o + depth]; o += depth
    gsum["ple_norm"] = total[o:o + depth]; o += depth
    gsum["final_norm"] = total[o]; o += 1
    conv_full = total[o:o + n_conv].reshape(n_a, 3, d); o += n_conv
    scale_full = total[o:o + n_b]; o += n_b
    loss = total[o, 0]
    gsum["a_w_conv"] = lax.dynamic_slice_in_dim(conv_full, pos * cw, cw, axis=2)
    gsum["b_scale"] = lax.dynamic_slice_in_dim(scale_full, pos * cw, cw, axis=1)

    grad, delta, new_m, new_v = {}, {}, {}, {}
    for k in WEIGHTS:
        if k in pieces:
            grad[k], delta[k], new_m[k], new_v[k] = _adamw_pieces(pieces[k], wts[k], mom[k], var[k])
        else:
            grad[k] = gsum[k]
            delta[k], new_m[k], new_v[k] = _adamw_small(gsum[k], wts[k], mom[k], var[k])
    return (loss, dh[None], *[grad[k] for k in WEIGHTS], *[delta[k] for k in WEIGHTS],
            *[new_m[k] for k in WEIGHTS], *[new_v[k] for k in WEIGHTS])
```

```python
import jax
import jax.numpy as jnp
from jax import lax
from jax.experimental import pallas as pl
from jax.experimental.pallas import tpu as pltpu
from jax.experimental.pallas import tpu_sc as plsc

F32 = jnp.float32
BF16 = jnp.bfloat16
MESH = pl.DeviceIdType.MESH

RMS_EPS = 1e-6
POOL_WINDOWS = (2, 4, 8, 16)
N_POOL_GROUPS = len(POOL_WINDOWS)
ADAM_LR = 0.001
ADAM_B1 = 0.9
ADAM_B2 = 0.999
ADAM_EPS = 1e-08
ADAM_WD = 0.01
ADAM_STEP = 10
N_DEV = 8

HALO = 16
ROW_TILE = 512
LAYER_ROW_TILE = 256
BWD_ROW_TILE = 512
WGRAD_ROW_TILE = 1024
VMEM_LIMIT = 56 * 1024 * 1024


def _cparams(*sem):
    return pltpu.CompilerParams(dimension_semantics=sem, vmem_limit_bytes=VMEM_LIMIT)


def _dot(a, b):
    return jnp.dot(a, b, preferred_element_type=F32)


def _dot_nt(a, b):
    return lax.dot_general(a, b, (((1,), (1,)), ((), ())), preferred_element_type=F32)


def _dot_tn(a, b):
    return lax.dot_general(a, b, (((0,), (0,)), ((), ())), preferred_element_type=F32)


def _rms_stats(x):
    r = lax.rsqrt(jnp.mean(x * x, axis=-1, keepdims=True) + RMS_EPS)
    return x * r, r


def _rms_bwd(dy, xh, r, g):
    a = dy * g
    return r * (a - xh * jnp.mean(a * xh, axis=-1, keepdims=True))


def _sigmoid(x):
    return 1.0 / (1.0 + jnp.exp(-x))


def _shift_down(x, k):
    return pltpu.roll(x, k, 0)


def _shift_up(x, k):
    return pltpu.roll(x, x.shape[0] - k, 0)


def _conv_taps(u, u_prev):
    uu = jnp.concatenate([u_prev, u], axis=0)
    return _shift_down(uu, 1)[HALO:], _shift_down(uu, 2)[HALO:]


def _window_mean_minus(uu, row0, window):
    acc = uu
    span = 1
    while span < window:
        acc = acc + _shift_down(acc, span)
        span *= 2
    rows = uu.shape[0] - HALO
    t = row0 + lax.broadcasted_iota(jnp.int32, (rows, 1), 0)
    cnt = jnp.minimum(t + 1, window).astype(F32)
    return acc[HALO:] / cnt - uu[HALO:]


def _whole(*shape):
    return pl.BlockSpec(shape, lambda i: (0,) * len(shape), pipeline_mode=pl.Buffered(1))


def _layer_fwd(h, g, w_in, mixer, w_out, pn, w_gate, p_all, w_proj, layer):
    s, d = h.shape
    n = w_in.shape[1]
    e = w_out.shape[0]
    nsplit = n // e
    gdim = e // N_POOL_GROUPS
    pdim = p_all.shape[2]
    ts = min(LAYER_ROW_TILE, s)
    is_conv = mixer[0] == "conv"
    params = mixer[1:]

    def body(*refs):
        h_ref, g_ref, win_ref = refs[:3]
        mix_refs = refs[3:3 + len(params)]
        wo_ref, pn_ref, wg_ref, p_ref, wp_ref = refs[3 + len(params):8 + len(params)]
        proj_ref, hn_ref, o_ref, h1_ref, h2_ref, gl_ref, pp_ref, carry_ref = refs[8 + len(params):]
        i = pl.program_id(0)

        @pl.when(i == 0)
        def _():
            carry_ref[...] = jnp.zeros_like(carry_ref)

        x = h_ref[...]
        xh, _ = _rms_stats(x)
        hn = (xh * g_ref[...]).astype(BF16)
        hn_ref[...] = hn
        parts = []
        for k in range(nsplit):
            part = _dot(hn, win_ref[:, k * e:(k + 1) * e])
            proj_ref[k] = part.astype(BF16)
            parts.append(part)
        prev = carry_ref[...]
        if is_conv:
            b, c, v, z = parts
            w_ref, = mix_refs
            u = c * v
            u1, u2 = _conv_taps(u, prev)
            mixed = b * (w_ref[0:1, :] * u2 + w_ref[1:2, :] * u1 + w_ref[2:3, :] * u)
        else:
            u, z = parts
            wgrp_ref, sc_ref = mix_refs
            uu = jnp.concatenate([prev, u], axis=0)
            cols = []
            for gi, window in enumerate(POOL_WINDOWS):
                dg = _window_mean_minus(uu[:, gi * gdim:(gi + 1) * gdim], i * ts, window)
                cols.append(_dot(dg.astype(BF16), wgrp_ref[gi]))
            mixed = jnp.concatenate(cols, axis=1) * sc_ref[...]
        carry_ref[...] = u[ts - HALO:]
        o = ((z * _sigmoid(z)) * mixed).astype(BF16)
        o_ref[...] = o
        h1 = x + _dot(o, wo_ref[...])
        h1_ref[...] = h1
        xh1, _ = _rms_stats(h1)
        gl = _dot((xh1 * pn_ref[...]).astype(BF16), wg_ref[...])
        pp = _dot(p_ref[...].astype(BF16), wp_ref[...])
        gl_ref[...] = gl.astype(BF16)
        pp_ref[...] = pp.astype(BF16)
        h2_ref[...] = h1 + _sigmoid(gl) * pp

    row = lambda width: pl.BlockSpec((ts, width), lambda i: (i, 0))
    mix_specs = [_whole(*a.shape) for a in params]
    outs = pl.pallas_call(
        body, name="layer_fwd",
        grid=(s // ts,),
        in_specs=[row(d), _whole(1, d), _whole(d, n)] + mix_specs
        + [_whole(e, d), _whole(1, d), _whole(d, d),
           pl.BlockSpec((None, ts, pdim), lambda i: (layer, i, 0)), _whole(pdim, d)],
        out_specs=[pl.BlockSpec((nsplit, ts, e), lambda i: (0, i, 0)),
                   row(d), row(e), row(d), row(d), row(d), row(d)],
        out_shape=[jax.ShapeDtypeStruct((nsplit, s, e), BF16), jax.ShapeDtypeStruct((s, d), BF16),
                   jax.ShapeDtypeStruct((s, e), BF16), jax.ShapeDtypeStruct((s, d), F32),
                   jax.ShapeDtypeStruct((s, d), F32), jax.ShapeDtypeStruct((s, d), BF16),
                   jax.ShapeDtypeStruct((s, d), BF16)],
        scratch_shapes=[pltpu.VMEM((HALO, e), F32)],
        compiler_params=_cparams("arbitrary"),
    )(h, g, w_in, *params, w_out, pn, w_gate, p_all, w_proj)
    proj, hn, o, h1, h2, gl, pp = outs
    return h2, (proj, hn, o, h1, gl, pp)


def _loss_head(h, target, g):
    s, d = h.shape
    ts = min(ROW_TILE, s)

    def body(h_ref, t_ref, g_ref, dh_ref, loss_ref, dg_ref):
        @pl.when(pl.program_id(0) == 0)
        def _():
            loss_ref[...] = jnp.zeros_like(loss_ref)
            dg_ref[...] = jnp.zeros_like(dg_ref)

        gain = g_ref[...]
        xh, r = _rms_stats(h_ref[...])
        err = xh * gain - t_ref[...]
        loss_ref[...] += jnp.full(loss_ref.shape, (0.5 / d) * jnp.sum(err * err), F32)
        dy = err * (1.0 / d)
        dg_ref[...] += jnp.sum(dy * xh, axis=0, keepdims=True)
        dh_ref[...] = _rms_bwd(dy, xh, r, gain)

    row = pl.BlockSpec((ts, d), lambda i: (i, 0))
    return pl.pallas_call(
        body, name="loss_head",
        grid=(s // ts,),
        in_specs=[row, row, pl.BlockSpec((1, d), lambda i: (0, 0))],
        out_specs=[row, pl.BlockSpec((1, 128), lambda i: (0, 0)), pl.BlockSpec((1, d), lambda i: (0, 0))],
        out_shape=[jax.ShapeDtypeStruct((s, d), F32), jax.ShapeDtypeStruct((1, 128), F32),
                   jax.ShapeDtypeStruct((1, d), F32)],
        compiler_params=_cparams("arbitrary"),
    )(h, target, g)


def _out_ple_bwd(dh2, gl, pp, h1, p_all, o, pn, wgate, wout, layer):
    s, d = dh2.shape
    e = o.shape[1]
    pdim = p_all.shape[2]
    ts = min(BWD_ROW_TILE, s)
    last = s // ts - 1

    def body(dh2_ref, gl_ref, pp_ref, h1_ref, p_ref, o_ref, pn_ref, wg_ref, wo_ref,
             dh1_ref, do_ref, dwp_ref, dwg_ref, dwo_ref, dpn_ref, awp, awg, awo):
        i = pl.program_id(0)

        @pl.when(i == 0)
        def _():
            awp[...] = jnp.zeros_like(awp)
            awg[...] = jnp.zeros_like(awg)
            awo[...] = jnp.zeros_like(awo)
            dpn_ref[...] = jnp.zeros_like(dpn_ref)

        dh2 = dh2_ref[...]
        gate = _sigmoid(gl_ref[...].astype(F32))
        dpp = (dh2 * gate).astype(BF16)
        dgl = (dh2 * pp_ref[...].astype(F32) * gate * (1.0 - gate)).astype(BF16)
        xh, r = _rms_stats(h1_ref[...])
        pn = pn_ref[...]
        awp[...] += _dot_tn(p_ref[...].astype(BF16), dpp)
        awg[...] += _dot_tn((xh * pn).astype(BF16), dgl)
        dr = _dot_nt(dgl, wg_ref[...])
        dpn_ref[...] += jnp.sum(dr * xh, axis=0, keepdims=True)
        dh1 = dh2 + _rms_bwd(dr, xh, r, pn)
        dh1_ref[...] = dh1
        dh1b = dh1.astype(BF16)
        do_ref[...] = _dot_nt(dh1b, wo_ref[...]).astype(BF16)
        awo[...] += _dot_tn(o_ref[...], dh1b)

        @pl.when(i == last)
        def _():
            dwp_ref[...] = awp[...].astype(BF16)
            dwg_ref[...] = awg[...].astype(BF16)
            dwo_ref[...] = awo[...].astype(BF16)

    row = lambda width: pl.BlockSpec((ts, width), lambda i: (i, 0))
    return pl.pallas_call(
        body, name="out_ple_bwd",
        grid=(s // ts,),
        in_specs=[row(d), row(d), row(d), row(d),
                  pl.BlockSpec((None, ts, pdim), lambda i: (layer, i, 0)),
                  row(e), _whole(1, d), _whole(d, d), _whole(e, d)],
        out_specs=[row(d), row(e), _whole(pdim, d), _whole(d, d), _whole(e, d), _whole(1, d)],
        out_shape=[jax.ShapeDtypeStruct((s, d), F32), jax.ShapeDtypeStruct((s, e), BF16),
                   jax.ShapeDtypeStruct((pdim, d), BF16), jax.ShapeDtypeStruct((d, d), BF16),
                   jax.ShapeDtypeStruct((e, d), BF16), jax.ShapeDtypeStruct((1, d), F32)],
        scratch_shapes=[pltpu.VMEM((pdim, d), F32), pltpu.VMEM((d, d), F32), pltpu.VMEM((e, d), F32)],
        compiler_params=_cparams("arbitrary"),
    )(dh2, gl, pp, h1, p_all, o, pn, wgate, wout)


def _mixer_bwd(do, proj, mixer, w_in, h, g, dh1):
    s, d = h.shape
    nsplit, _, e = proj.shape
    gdim = e // N_POOL_GROUPS
    ts = min(LAYER_ROW_TILE, s)
    nt = s // ts
    is_conv = mixer[0] == "conv"
    params = mixer[1:]
    n_mix_out = 1 if is_conv else 2

    def body(*refs):
        do_ref, p_ref, ph_ref = refs[:3]
        mix_refs = refs[3:3 + len(params)]
        win_ref, h_ref, g_ref, dh1_ref = refs[3 + len(params):7 + len(params)]
        dp_ref, dh_ref, dg_ref = refs[7 + len(params):10 + len(params)]
        mix_out = refs[10 + len(params):10 + len(params) + n_mix_out]
        scratch = refs[10 + len(params) + n_mix_out:]
        carry_ref = scratch[0]
        i = pl.program_id(0)
        tile = nt - 1 - i

        @pl.when(i == 0)
        def _():
            carry_ref[...] = jnp.zeros_like(carry_ref)
            dg_ref[...] = jnp.zeros_like(dg_ref)
            for ref in mix_out[-1:] + scratch[1:]:
                ref[...] = jnp.zeros_like(ref)

        dof = do_ref[...].astype(F32)
        nxt = carry_ref[...]
        if is_conv:
            w_ref, = mix_refs
            dw_ref, = mix_out
            w0, w1, w2 = w_ref[0:1, :], w_ref[1:2, :], w_ref[2:3, :]
            b, c, v, z = [p_ref[k].astype(F32) for k in range(4)]
            u = c * v
            u_prev = jnp.where(tile == 0, 0.0, ph_ref[1].astype(F32) * ph_ref[2].astype(F32))
            u1, u2 = _conv_taps(u, u_prev)
            conv = w0 * u2 + w1 * u1 + w2 * u
            sig = _sigmoid(z)
            sz = z * sig
            dy = dof * sz
            dp_ref[3] = (dof * (b * conv) * (sig + sz * (1.0 - sig))).astype(BF16)
            dp_ref[0] = (dy * conv).astype(BF16)
            dconv = dy * b
            dw_ref[0:1, :] += jnp.sum(dconv * u2, axis=0, keepdims=True)
            dw_ref[1:2, :] += jnp.sum(dconv * u1, axis=0, keepdims=True)
            dw_ref[2:3, :] += jnp.sum(dconv * u, axis=0, keepdims=True)
            dcc = jnp.concatenate([dconv, nxt], axis=0)
            du = w2 * dconv + w1 * _shift_up(dcc, 1)[:ts] + w0 * _shift_up(dcc, 2)[:ts]
            carry_ref[...] = dconv[:HALO]
            dp_ref[1] = (du * v).astype(BF16)
            dp_ref[2] = (du * c).astype(BF16)
        else:
            wgrp_ref, sc_ref = mix_refs
            dwg_ref, dsc_ref = mix_out
            agrp = scratch[1]
            u = p_ref[0].astype(F32)
            z = p_ref[1].astype(F32)
            u_prev = jnp.where(tile == 0, 0.0, ph_ref[0].astype(F32))
            uu = jnp.concatenate([u_prev, u], axis=0)
            sig = _sigmoid(z)
            sz = z * sig
            dm = dof * sz
            dsilu = dof * (sig + sz * (1.0 - sig))
            t = tile * ts + lax.broadcasted_iota(jnp.int32, (ts, 1), 0)
            for gi, window in enumerate(POOL_WINDOWS):
                cols = slice(gi * gdim, (gi + 1) * gdim)
                w = wgrp_ref[gi]
                scale = sc_ref[:, cols]
                db = _window_mean_minus(uu[:, cols], tile * ts, window).astype(BF16)
                mr = _dot(db, w)
                dp_ref[1, :, cols] = (dsilu[:, cols] * (mr * scale)).astype(BF16)
                dmg = dm[:, cols]
                dmr = (dmg * scale).astype(BF16)
                agrp[gi] += _dot_tn(db, dmr)
                dsc_ref[:, cols] += jnp.sum(dmg * mr, axis=0, keepdims=True)
                dd = _dot_nt(dmr, w)
                ddq = dd / jnp.minimum(t + 1, window).astype(F32)
                acc = jnp.concatenate([ddq, nxt[:, cols]], axis=0)
                span = 1
                while span < window:
                    acc = acc + _shift_up(acc, span)
                    span *= 2
                carry_ref[:, cols] = ddq[:HALO]
                dp_ref[0, :, cols] = (acc[:ts] - dd).astype(BF16)

            @pl.when(i == nt - 1)
            def _():
                dwg_ref[...] = agrp[...].astype(BF16)

        dhn = _dot_nt(dp_ref[0], win_ref[:, 0:e])
        for k in range(1, nsplit):
            dhn += _dot_nt(dp_ref[k], win_ref[:, k * e:(k + 1) * e])
        xh, r = _rms_stats(h_ref[...])
        dg_ref[...] += jnp.sum(dhn * xh, axis=0, keepdims=True)
        dh_ref[...] = dh1_ref[...] + _rms_bwd(dhn, xh, r, g_ref[...])

    rev = lambda width: pl.BlockSpec((ts, width), lambda i: (nt - 1 - i, 0))
    halo_blocks = ts // HALO
    in_specs = [rev(e),
                pl.BlockSpec((nsplit, ts, e), lambda i: (0, nt - 1 - i, 0)),
                pl.BlockSpec((nsplit, HALO, e), lambda i: (0, jnp.maximum((nt - 1 - i) * halo_blocks - 1, 0), 0))]
    in_specs += [_whole(*a.shape) for a in params]
    in_specs += [_whole(d, nsplit * e), rev(d), _whole(1, d), rev(d)]
    out_specs = [pl.BlockSpec((nsplit, ts, e), lambda i: (0, nt - 1 - i, 0)), rev(d), _whole(1, d)]
    out_shape = [jax.ShapeDtypeStruct((nsplit, s, e), BF16), jax.ShapeDtypeStruct((s, d), F32),
                 jax.ShapeDtypeStruct((1, d), F32)]
    scratch = [pltpu.VMEM((HALO, e), F32)]
    if is_conv:
        out_specs += [_whole(3, e)]
        out_shape += [jax.ShapeDtypeStruct((3, e), F32)]
    else:
        out_specs += [_whole(N_POOL_GROUPS, gdim, gdim), _whole(1, e)]
        out_shape += [jax.ShapeDtypeStruct((N_POOL_GROUPS, gdim, gdim), BF16), jax.ShapeDtypeStruct((1, e), F32)]
        scratch += [pltpu.VMEM((N_POOL_GROUPS, gdim, gdim), F32)]
    return pl.pallas_call(
        body, name="mixer_bwd",
        grid=(nt,),
        in_specs=in_specs, out_specs=out_specs, out_shape=out_shape, scratch_shapes=scratch,
        compiler_params=_cparams("arbitrary"),
    )(do, proj, proj, *params, w_in, h, g, dh1)


def _proj_wgrad(hn, dproj):
    s, d = hn.shape
    nsplit, _, e = dproj.shape
    ts = min(WGRAD_ROW_TILE, s)
    last = s // ts - 1

    def body(hn_ref, dp_ref, dw_ref, acc):
        i = pl.program_id(1)

        @pl.when(i == 0)
        def _():
            acc[...] = jnp.zeros_like(acc)

        acc[...] += _dot_tn(hn_ref[...], dp_ref[...])

        @pl.when(i == last)
        def _():
            dw_ref[...] = acc[...].astype(BF16)

    return pl.pallas_call(
        body, name="proj_wgrad",
        grid=(nsplit, s // ts),
        in_specs=[pl.BlockSpec((ts, d), lambda k, i: (i, 0)),
                  pl.BlockSpec((None, ts, e), lambda k, i: (k, i, 0))],
        out_specs=pl.BlockSpec((d, e), lambda k, i: (0, k)),
        out_shape=jax.ShapeDtypeStruct((d, nsplit * e), BF16),
        scratch_shapes=[pltpu.VMEM((d, e), F32)],
        compiler_params=_cparams("parallel", "arbitrary"),
    )(hn, dproj)


def _forward_backward(xs, ps, target, full, conv_w, scale_w, norm_mix, ple_norm, final_norm, exchange):
    depth = len(full)
    row = lambda a, i: a[i][None, :]
    mixer_of = lambda i: ("conv", conv_w[i // 2]) if i % 2 == 0 else ("pool", full[i]["w_grp"], row(scale_w, i // 2))

    saved = []
    h = xs
    for i in range(depth):
        w = full[i]
        h_next, acts = _layer_fwd(h, row(norm_mix, i), w["w_in"], mixer_of(i), w["w_out"], row(ple_norm, i),
                                  w["gate"], ps, w["proj"], i)
        saved.append((h, *acts))
        h = h_next
    dh, loss_row, d_final = _loss_head(h, target, final_norm[None, :])

    d_norm, d_ple_norm, d_conv, d_scale, sent = [None] * depth, [None] * depth, [], [], [None] * depth
    for i in reversed(range(depth)):
        w = full[i]
        h_in, proj, hn, o, h1, gl, pp = saved[i]
        g = {}
        dh1, do, g["proj"], g["gate"], g["w_out"], d_ple_norm[i] = _out_ple_bwd(
            dh, gl, pp, h1, ps, o, row(ple_norm, i), w["gate"], w["w_out"], i)
        outs = _mixer_bwd(do, proj, mixer_of(i), w["w_in"], h_in, row(norm_mix, i), dh1)
        dproj, dh, d_norm[i] = outs[:3]
        if i % 2 == 0:
            d_conv.insert(0, outs[3])
        else:
            g["w_grp"] = outs[3]
            d_scale.insert(0, outs[4])
        g["w_in"] = _proj_wgrad(hn, dproj)
        sent[i] = exchange(i, g)
    return loss_row, dh, sent, (d_norm, d_ple_norm, d_final, d_conv, d_scale)


VMEM_SPEC = pl.BlockSpec(memory_space=pltpu.VMEM)

FLIPS = [(fx, fy, fc) for fx in (0, 1) for fy in (0, 1) for fc in (0, 1)][1:]
SHARD_AXIS = {"w_in": 1, "w_out": 0, "w_grp": 1, "gate": 0, "proj": 1}


def _my_place():
    return lax.axis_index("x"), lax.axis_index("y"), lax.axis_index("c")


def _position(place):
    x, y, c = place
    return 4 * x + 2 * y + c


def _flip(place, flips):
    return tuple(1 - v if f else v for v, f in zip(place, flips))


def _shard_of(ref, axis, pos, n):
    idx = [slice(None)] * len(ref.shape)
    idx[axis] = pl.ds(pl.multiple_of(pos * n, n), n)
    return ref.at[tuple(idx)]


def _sequencer_mesh():
    return plsc.ScalarSubcoreMesh(axis_name="sequencer", num_cores=1)


def _handshake(peers):
    barrier = pltpu.get_barrier_semaphore()
    for peer in peers:
        pl.semaphore_signal(barrier, inc=1, device_id=peer, device_id_type=MESH)
    pl.semaphore_wait(barrier, len(peers))


def _all_gather_layer(shards, collective_id):
    names = list(shards)
    nt = len(names)
    axes = [SHARD_AXIS[k] for k in names]
    widths = [shards[k].shape[SHARD_AXIS[k]] for k in names]

    def full_shape(k):
        shp = list(shards[k].shape)
        shp[SHARD_AXIS[k]] *= N_DEV
        return tuple(shp)

    def body(*refs):
        ins, outs = refs[:nt], refs[nt:2 * nt]
        send_sems, recv_sems, local_sem = refs[2 * nt:]
        me = _my_place()
        x, y, c = me
        sibling = (x, y, 1 - c)
        chips = [(1 - x, y), (x, 1 - y), (1 - x, 1 - y)]
        _handshake([sibling] + [(*chip, c) for chip in chips])

        def block(t, place):
            return _shard_of(outs[t], axes[t], _position(place), widths[t])

        def copy(t, k, place, to, src=None):
            return pltpu.make_async_remote_copy(
                src_ref=block(t, place) if src is None else src, dst_ref=block(t, place),
                send_sem=send_sems.at[k], recv_sem=recv_sems.at[k], device_id=to, device_id_type=MESH)

        mine = [pltpu.make_async_copy(ins[t], block(t, me), local_sem) for t in range(nt)]
        for cp in mine:
            cp.start()
        first = []
        for j, chip in enumerate(chips):
            first += [copy(t, 1 + j, me, (*chip, c), src=ins[t]) for t in range(nt)]
        first += [copy(t, 0, me, sibling, src=ins[t]) for t in range(nt)]
        for cp in first:
            cp.start()
        passed = []
        for j, chip in enumerate(chips):
            for t in range(nt):
                copy(t, 1 + j, (*chip, c), me).wait_recv()
            for t in range(nt):
                fwd = copy(t, 4 + j, (*chip, c), sibling)
                fwd.start()
                passed.append(fwd)
        for t in range(nt):
            copy(t, 0, sibling, me).wait_recv()
        for j, chip in enumerate(chips):
            for t in range(nt):
                copy(t, 4 + j, (*chip, 1 - c), me).wait_recv()
        for cp in first + passed:
            cp.wait_send()
        for cp in mine:
            cp.wait()

    outs = pl.kernel(
        body, name=f"all_gather_layer_{collective_id}",
        out_type=[jax.ShapeDtypeStruct(full_shape(k), shards[k].dtype) for k in names],
        mesh=_sequencer_mesh(),
        scratch_types=[pltpu.SemaphoreType.DMA((7,)), pltpu.SemaphoreType.DMA((7,)), pltpu.SemaphoreType.DMA],
        compiler_params=pltpu.CompilerParams(collective_id=collective_id),
    )(*[shards[k] for k in names])
    return dict(zip(names, outs))


def _exchange_layer(grads, collective_id):
    names = list(grads)
    nt = len(names)
    axes = [SHARD_AXIS[k] for k in names]
    widths = [grads[k].shape[SHARD_AXIS[k]] // N_DEV for k in names]

    def slot_shape(t):
        shp = list(grads[names[t]].shape)
        shp[axes[t]] = widths[t]
        return (N_DEV, *shp)

    def body(*refs):
        ins, outs = refs[:nt], refs[nt:2 * nt]
        send_sems, recv_sems, local_sem = refs[2 * nt:]
        me = _my_place()
        mine = _position(me)
        _handshake([_flip(me, flips) for flips in FLIPS])
        local = [pltpu.make_async_copy(_shard_of(ins[t], axes[t], mine, widths[t]), outs[t].at[mine], local_sem)
                 for t in range(nt)]
        for cp in local:
            cp.start()
        copies = []
        for k, flips in enumerate(FLIPS):
            peer = _flip(me, flips)
            for t in range(nt):
                cp = pltpu.make_async_remote_copy(
                    src_ref=_shard_of(ins[t], axes[t], _position(peer), widths[t]), dst_ref=outs[t].at[mine],
                    send_sem=send_sems.at[k], recv_sem=recv_sems.at[k], device_id=peer, device_id_type=MESH)
                cp.start()
                copies.append(cp)
        for cp in copies:
            cp.wait()
        for cp in local:
            cp.wait()

    outs = pl.kernel(
        body, name=f"exchange_layer_{collective_id}",
        out_type=[jax.ShapeDtypeStruct(slot_shape(t), BF16) for t in range(nt)],
        mesh=_sequencer_mesh(),
        scratch_types=[pltpu.SemaphoreType.DMA((7,)), pltpu.SemaphoreType.DMA((7,)), pltpu.SemaphoreType.DMA],
        compiler_params=pltpu.CompilerParams(collective_id=collective_id),
    )(*[grads[k] for k in names])
    return dict(zip(names, outs))


def _gather_rows(buf, reduce):
    r, c = buf.shape

    def body(in_ref, out_ref, *scratch):
        if reduce:
            all_ref, send_sems, recv_sems = scratch
        else:
            all_ref = out_ref
            send_sems, recv_sems = scratch
        me = _my_place()
        all_ref[_position(me)] = in_ref[...]
        copies = []
        for k, flips in enumerate(FLIPS):
            cp = pltpu.make_async_remote_copy(
                src_ref=in_ref, dst_ref=all_ref.at[_position(me)],
                send_sem=send_sems.at[k], recv_sem=recv_sems.at[k], device_id=_flip(me, flips), device_id_type=MESH)
            cp.start()
            copies.append(cp)
        for cp in copies:
            cp.wait()
        if reduce:
            total = all_ref[0]
            for j in range(1, N_DEV):
                total = total + all_ref[j]
            out_ref[...] = total

    return pl.pallas_call(
        body, name="sum_rows" if reduce else "gather_rows",
        in_specs=[VMEM_SPEC], out_specs=VMEM_SPEC,
        out_shape=jax.ShapeDtypeStruct((r, c) if reduce else (N_DEV, r, c), F32),
        scratch_shapes=([pltpu.VMEM((N_DEV, r, c), F32)] if reduce else [])
        + [pltpu.SemaphoreType.DMA((7,)), pltpu.SemaphoreType.DMA((7,))],
    )(buf)


def _adamw_math(w, g, m, v):
    m = ADAM_B1 * m + (1.0 - ADAM_B1) * g
    v = ADAM_B2 * v + (1.0 - ADAM_B2) * (g * g)
    m_hat = m / (1.0 - ADAM_B1 ** ADAM_STEP)
    v_hat = v / (1.0 - ADAM_B2 ** ADAM_STEP)
    delta = -ADAM_LR * (m_hat / (jnp.sqrt(v_hat) + ADAM_EPS) + ADAM_WD * w)
    return delta, m, v


def _adamw_pieces(pieces, w, m, v):
    shape = w.shape
    nl = shape[0]
    cols = shape[-1]
    rows = w.size // (nl * cols)
    tr = min(256 // nl, rows)
    flat3 = lambda a: a.reshape(nl, rows, cols)

    def body(*refs):
        p_refs = refs[:nl]
        w_ref, m_ref, v_ref, g_ref, d_ref, nm_ref, nv_ref = refs[nl:]
        for l in range(nl):
            g = p_refs[l][0].astype(F32)
            for j in range(1, N_DEV):
                g = g + p_refs[l][j].astype(F32)
            g_ref[l] = g
            d_ref[l], nm_ref[l], nv_ref[l] = _adamw_math(w_ref[l], g, m_ref[l], v_ref[l])

    blk = pl.BlockSpec((nl, tr, cols), lambda i: (0, i, 0))
    outs = pl.pallas_call(
        body, name="adamw_pieces",
        grid=(rows // tr,),
        in_specs=[pl.BlockSpec((N_DEV, tr, cols), lambda i: (0, i, 0))] * nl + [blk, blk, blk],
        out_specs=[blk] * 4,
        out_shape=[jax.ShapeDtypeStruct((nl, rows, cols), F32)] * 4,
        compiler_params=_cparams("parallel"),
    )(*[a.reshape(N_DEV, rows, cols) for a in pieces], flat3(w), flat3(m), flat3(v))
    return [a.reshape(shape) for a in outs]


def _adamw_small(g, w, m, v):
    shape = w.shape
    two = lambda a: a.reshape(-1, shape[-1])

    def body(g_ref, w_ref, m_ref, v_ref, d_ref, nm_ref, nv_ref):
        d_ref[...], nm_ref[...], nv_ref[...] = _adamw_math(w_ref[...], g_ref[...], m_ref[...], v_ref[...])

    outs = pl.pallas_call(
        body, name="adamw_small",
        in_specs=[VMEM_SPEC] * 4, out_specs=[VMEM_SPEC] * 3,
        out_shape=[jax.ShapeDtypeStruct(two(w).shape, F32)] * 3,
    )(two(g), two(w), two(m), two(v))
    return [a.reshape(shape) for a in outs]


WEIGHTS = ("norm_mix", "a_w_in", "a_w_conv", "a_w_out", "b_w_in", "b_w_grp", "b_scale", "b_w_out",
           "ple_norm", "ple_w_gate", "ple_w_proj", "final_norm")
SMALL_ROWS = 24
GATHER_ID = 0
EXCHANGE_ID = 4


def kernel(x, p, norm_mix, a_w_in, a_w_conv, a_w_out, b_w_in, b_w_grp, b_scale, b_w_out, ple_norm, ple_w_gate, ple_w_proj, final_norm, loss_target, m_norm_mix, m_a_w_in, m_a_w_conv, m_a_w_out, m_b_w_in, m_b_w_grp, m_b_scale, m_b_w_out, m_ple_norm, m_ple_w_gate, m_ple_w_proj, m_final_norm, v_norm_mix, v_a_w_in, v_a_w_conv, v_a_w_out, v_b_w_in, v_b_w_grp, v_b_scale, v_b_w_out, v_ple_norm, v_ple_w_gate, v_ple_w_proj, v_final_norm):
    wts = dict(norm_mix=norm_mix, a_w_in=a_w_in, a_w_conv=a_w_conv, a_w_out=a_w_out, b_w_in=b_w_in, b_w_grp=b_w_grp,
               b_scale=b_scale, b_w_out=b_w_out, ple_norm=ple_norm, ple_w_gate=ple_w_gate, ple_w_proj=ple_w_proj,
               final_norm=final_norm)
    mom = dict(norm_mix=m_norm_mix, a_w_in=m_a_w_in, a_w_conv=m_a_w_conv, a_w_out=m_a_w_out, b_w_in=m_b_w_in,
               b_w_grp=m_b_w_grp, b_scale=m_b_scale, b_w_out=m_b_w_out, ple_norm=m_ple_norm, ple_w_gate=m_ple_w_gate,
               ple_w_proj=m_ple_w_proj, final_norm=m_final_norm)
    var = dict(norm_mix=v_norm_mix, a_w_in=v_a_w_in, a_w_conv=v_a_w_conv, a_w_out=v_a_w_out, b_w_in=v_b_w_in,
               b_w_grp=v_b_w_grp, b_scale=v_b_scale, b_w_out=v_b_w_out, ple_norm=v_ple_norm, ple_w_gate=v_ple_w_gate,
               ple_w_proj=v_ple_w_proj, final_norm=v_final_norm)
    d = x.shape[2]
    depth = norm_mix.shape[0]
    n_a, n_b = a_w_conv.shape[0], b_scale.shape[0]
    cw = a_w_conv.shape[2]
    pos = _position(_my_place())

    def layer_matrices(i):
        j = i // 2
        mixer = {"w_in": ("a_w_in", j), "w_out": ("a_w_out", j)} if i % 2 == 0 else \
                {"w_in": ("b_w_in", j), "w_grp": ("b_w_grp", j), "w_out": ("b_w_out", j)}
        return {**mixer, "gate": ("ple_w_gate", i), "proj": ("ple_w_proj", i)}

    full = [_all_gather_layer({k: wts[name][idx].astype(BF16) for k, (name, idx) in layer_matrices(i).items()},
                              GATHER_ID + i) for i in range(depth)]
    vec_rows = jnp.concatenate([a_w_conv.reshape(-1, cw), b_scale], axis=0)
    vecs = _gather_rows(vec_rows, reduce=False)
    n_conv = 3 * n_a
    conv_w = vecs[:, :n_conv].transpose(1, 0, 2).reshape(n_a, 3, N_DEV * cw)
    scale_w = vecs[:, n_conv:].transpose(1, 0, 2).reshape(n_b, N_DEV * cw)

    loss_row, dx, sent, (d_norm, d_ple_norm, d_final, d_conv, d_scale) = _forward_backward(
        x[0], p[:, 0], loss_target[0], full, conv_w, scale_w, norm_mix, ple_norm, final_norm,
        lambda i, g: _exchange_layer(g, EXCHANGE_ID + i))
    pieces = {name: [None] * wts[name].shape[0] for name in WEIGHTS if wts[name].ndim >= 3 and name != "a_w_conv"}
    for i in range(depth):
        for k, (name, idx) in layer_matrices(i).items():
            pieces[name][idx] = sent[i][k]

    pad = lambda a: jnp.pad(a, ((0, 0), (0, d - a.shape[1])))
    small = jnp.concatenate(d_norm + d_ple_norm + [d_final] + d_conv + d_scale + [pad(loss_row)], axis=0)
    small = jnp.pad(small, ((0, SMALL_ROWS - small.shape[0]), (0, 0)))
    total = _gather_rows(small, reduce=True)
    o = 0
    gsum = {}
    gsum["norm_mix"] = total[o:o + depth]; o += depth
    gsum["ple_norm"] = total[o:o + depth]; o += depth
    gsum["final_norm"] = total[o]; o += 1
    conv_full = total[o:o + n_conv].reshape(n_a, 3, d); o += n_conv
    scale_full = total[o:o + n_b]; o += n_b
    loss = total[o, 0]
    gsum["a_w_conv"] = lax.dynamic_slice_in_dim(conv_full, pos * cw, cw, axis=2)
    gsum["b_scale"] = lax.dynamic_slice_in_dim(scale_full, pos * cw, cw, axis=1)

    grad, delta, new_m, new_v = {}, {}, {}, {}
    for k in WEIGHTS:
        if k in pieces:
            grad[k], delta[k], new_m[k], new_v[k] = _adamw_pieces(pieces[k], wts[k], mom[k], var[k])
        else:
            grad[k] = gsum[k]
            delta[k], new_m[k], new_v[k] = _adamw_small(gsum[k], wts[k], mom[k], var[k])
    return (loss, dx[None], *[grad[k] for k in WEIGHTS], *[delta[k] for k in WEIGHTS],
            *[new_m[k] for k in WEIGHTS], *[new_v[k] for k in WEIGHTS])
```

```python
import jax
import jax.numpy as jnp
from jax import lax
from jax.experimental import pallas as pl
from jax.experimental.pallas import tpu as pltpu
from jax.experimental.pallas import tpu_sc as plsc

F32 = jnp.float32
BF16 = jnp.bfloat16
MESH = pl.DeviceIdType.MESH

RMS_EPS = 1e-6
POOL_WINDOWS = (2, 4, 8, 16)
N_POOL_GROUPS = len(POOL_WINDOWS)
ADAM_LR = 0.001
ADAM_B1 = 0.9
ADAM_B2 = 0.999
ADAM_EPS = 1e-08
ADAM_WD = 0.01
ADAM_STEP = 10
N_DEV = 8

HALO = 16
ROW_TILE = 512
LAYER_ROW_TILE = 256
BWD_ROW_TILE = 512
WGRAD_ROW_TILE = 1024
VMEM_LIMIT = 56 * 1024 * 1024


def _cparams(*sem):
    return pltpu.CompilerParams(dimension_semantics=sem, vmem_limit_bytes=VMEM_LIMIT)


def _dot(a, b):
    return jnp.dot(a, b, preferred_element_type=F32)


def _dot_nt(a, b):
    return lax.dot_general(a, b, (((1,), (1,)), ((), ())), preferred_element_type=F32)


def _dot_tn(a, b):
    return lax.dot_general(a, b, (((0,), (0,)), ((), ())), preferred_element_type=F32)


def _rms_stats(x):
    r = lax.rsqrt(jnp.mean(x * x, axis=-1, keepdims=True) + RMS_EPS)
    return x * r, r


def _rms_bwd(dy, xh, r, g):
    a = dy * g
    return r * (a - xh * jnp.mean(a * xh, axis=-1, keepdims=True))


def _sigmoid(x):
    return 1.0 / (1.0 + jnp.exp(-x))


def _shift_down(x, k):
    return pltpu.roll(x, k, 0)


def _shift_up(x, k):
    return pltpu.roll(x, x.shape[0] - k, 0)


def _conv_taps(u, u_prev):
    uu = jnp.concatenate([u_prev, u], axis=0)
    return _shift_down(uu, 1)[HALO:], _shift_down(uu, 2)[HALO:]


def _window_mean_minus(uu, row0, window):
    acc = uu
    span = 1
    while span < window:
        acc = acc + _shift_down(acc, span)
        span *= 2
    rows = uu.shape[0] - HALO
    t = row0 + lax.broadcasted_iota(jnp.int32, (rows, 1), 0)
    cnt = jnp.minimum(t + 1, window).astype(F32)
    return acc[HALO:] / cnt - uu[HALO:]


def _whole(*shape):
    return pl.BlockSpec(shape, lambda i: (0,) * len(shape), pipeline_mode=pl.Buffered(1))


def _layer_fwd(h, g, w_in, mixer, w_out, pn, w_gate, p_all, w_proj, layer):
    s, d = h.shape
    n = w_in.shape[1]
    e = w_out.shape[0]
    nsplit = n // e
    gdim = e // N_POOL_GROUPS
    pdim = p_all.shape[2]
    ts = min(LAYER_ROW_TILE, s)
    is_conv = mixer[0] == "conv"
    params = mixer[1:]

    def body(*refs):
        h_ref, g_ref, win_ref = refs[:3]
        mix_refs = refs[3:3 + len(params)]
        wo_ref, pn_ref, wg_ref, p_ref, wp_ref = refs[3 + len(params):8 + len(params)]
        proj_ref, hn_ref, o_ref, h1_ref, h2_ref, gl_ref, pp_ref, carry_ref = refs[8 + len(params):]
        i = pl.program_id(0)

        @pl.when(i == 0)
        def _():
            carry_ref[...] = jnp.zeros_like(carry_ref)

        x = h_ref[...]
        xh, _ = _rms_stats(x)
        hn = (xh * g_ref[...]).astype(BF16)
        hn_ref[...] = hn
        parts = []
        for k in range(nsplit):
            part = _dot(hn, win_ref[:, k * e:(k + 1) * e])
            proj_ref[k] = part.astype(BF16)
            parts.append(part)
        prev = carry_ref[...]
        if is_conv:
            b, c, v, z = parts
            w_ref, = mix_refs
            u = c * v
            u1, u2 = _conv_taps(u, prev)
            mixed = b * (w_ref[0:1, :] * u2 + w_ref[1:2, :] * u1 + w_ref[2:3, :] * u)
        else:
            u, z = parts
            wgrp_ref, sc_ref = mix_refs
            uu = jnp.concatenate([prev, u], axis=0)
            cols = []
            for gi, window in enumerate(POOL_WINDOWS):
                dg = _window_mean_minus(uu[:, gi * gdim:(gi + 1) * gdim], i * ts, window)
                cols.append(_dot(dg.astype(BF16), wgrp_ref[gi]))
            mixed = jnp.concatenate(cols, axis=1) * sc_ref[...]
        carry_ref[...] = u[ts - HALO:]
        o = ((z * _sigmoid(z)) * mixed).astype(BF16)
        o_ref[...] = o
        h1 = x + _dot(o, wo_ref[...])
        h1_ref[...] = h1
        xh1, _ = _rms_stats(h1)
        gl = _dot((xh1 * pn_ref[...]).astype(BF16), wg_ref[...])
        pp = _dot(p_ref[...].astype(BF16), wp_ref[...])
        gl_ref[...] = gl.astype(BF16)
        pp_ref[...] = pp.astype(BF16)
        h2_ref[...] = h1 + _sigmoid(gl) * pp

    row = lambda width: pl.BlockSpec((ts, width), lambda i: (i, 0))
    mix_specs = [_whole(*a.shape) for a in params]
    outs = pl.pallas_call(
        body, name="layer_fwd",
        grid=(s // ts,),
        in_specs=[row(d), _whole(1, d), _whole(d, n)] + mix_specs
        + [_whole(e, d), _whole(1, d), _whole(d, d),
           pl.BlockSpec((None, ts, pdim), lambda i: (layer, i, 0)), _whole(pdim, d)],
        out_specs=[pl.BlockSpec((nsplit, ts, e), lambda i: (0, i, 0)),
                   row(d), row(e), row(d), row(d), row(d), row(d)],
        out_shape=[jax.ShapeDtypeStruct((nsplit, s, e), BF16), jax.ShapeDtypeStruct((s, d), BF16),
                   jax.ShapeDtypeStruct((s, e), BF16), jax.ShapeDtypeStruct((s, d), F32),
                   jax.ShapeDtypeStruct((s, d), F32), jax.ShapeDtypeStruct((s, d), BF16),
                   jax.ShapeDtypeStruct((s, d), BF16)],
        scratch_shapes=[pltpu.VMEM((HALO, e), F32)],
        compiler_params=_cparams("arbitrary"),
    )(h, g, w_in, *params, w_out, pn, w_gate, p_all, w_proj)
    proj, hn, o, h1, h2, gl, pp = outs
    return h2, (proj, hn, o, h1, gl, pp)


def _loss_head(h, target, g):
    s, d = h.shape
    ts = min(ROW_TILE, s)

    def body(h_ref, t_ref, g_ref, dh_ref, loss_ref, dg_ref):
        @pl.when(pl.program_id(0) == 0)
        def _():
            loss_ref[...] = jnp.zeros_like(loss_ref)
            dg_ref[...] = jnp.zeros_like(dg_ref)

        gain = g_ref[...]
        xh, r = _rms_stats(h_ref[...])
        err = xh * gain - t_ref[...]
        loss_ref[...] += jnp.full(loss_ref.shape, (0.5 / d) * jnp.sum(err * err), F32)
        dy = err * (1.0 / d)
        dg_ref[...] += jnp.sum(dy * xh, axis=0, keepdims=True)
        dh_ref[...] = _rms_bwd(dy, xh, r, gain)

    row = pl.BlockSpec((ts, d), lambda i: (i, 0))
    return pl.pallas_call(
        body, name="loss_head",
        grid=(s // ts,),
        in_specs=[row, row, pl.BlockSpec((1, d), lambda i: (0, 0))],
        out_specs=[row, pl.BlockSpec((1, 128), lambda i: (0, 0)), pl.BlockSpec((1, d), lambda i: (0, 0))],
        out_shape=[jax.ShapeDtypeStruct((s, d), F32), jax.ShapeDtypeStruct((1, 128), F32),
                   jax.ShapeDtypeStruct((1, d), F32)],
        compiler_params=_cparams("arbitrary"),
    )(h, target, g)


def _out_ple_bwd(dh2, gl, pp, h1, p_all, o, pn, wgate, wout, layer):
    s, d = dh2.shape
    e = o.shape[1]
    pdim = p_all.shape[2]
    ts = min(BWD_ROW_TILE, s)
    last = s // ts - 1

    def body(dh2_ref, gl_ref, pp_ref, h1_ref, p_ref, o_ref, pn_ref, wg_ref, wo_ref,
             dh1_ref, do_ref, dwp_ref, dwg_ref, dwo_ref, dpn_ref, awp, awg, awo):
        i = pl.program_id(0)

        @pl.when(i == 0)
        def _():
            awp[...] = jnp.zeros_like(awp)
            awg[...] = jnp.zeros_like(awg)
            awo[...] = jnp.zeros_like(awo)
            dpn_ref[...] = jnp.zeros_like(dpn_ref)

        dh2 = dh2_ref[...]
        gate = _sigmoid(gl_ref[...].astype(F32))
        dpp = (dh2 * gate).astype(BF16)
        dgl = (dh2 * pp_ref[...].astype(F32) * gate * (1.0 - gate)).astype(BF16)
        xh, r = _rms_stats(h1_ref[...])
        pn = pn_ref[...]
        awp[...] += _dot_tn(p_ref[...].astype(BF16), dpp)
        awg[...] += _dot_tn((xh * pn).astype(BF16), dgl)
        dr = _dot_nt(dgl, wg_ref[...])
        dpn_ref[...] += jnp.sum(dr * xh, axis=0, keepdims=True)
        dh1 = dh2 + _rms_bwd(dr, xh, r, pn)
        dh1_ref[...] = dh1
        dh1b = dh1.astype(BF16)
        do_ref[...] = _dot_nt(dh1b, wo_ref[...]).astype(BF16)
        awo[...] += _dot_tn(o_ref[...], dh1b)

        @pl.when(i == last)
        def _():
            dwp_ref[...] = awp[...].astype(BF16)
            dwg_ref[...] = awg[...].astype(BF16)
            dwo_ref[...] = awo[...].astype(BF16)

    row = lambda width: pl.BlockSpec((ts, width), lambda i: (i, 0))
    return pl.pallas_call(
        body, name="out_ple_bwd",
        grid=(s // ts,),
        in_specs=[row(d), row(d), row(d), row(d),
                  pl.BlockSpec((None, ts, pdim), lambda i: (layer, i, 0)),
                  row(e), _whole(1, d), _whole(d, d), _whole(e, d)],
        out_specs=[row(d), row(e), _whole(pdim, d), _whole(d, d), _whole(e, d), _whole(1, d)],
        out_shape=[jax.ShapeDtypeStruct((s, d), F32), jax.ShapeDtypeStruct((s, e), BF16),
                   jax.ShapeDtypeStruct((pdim, d), BF16), jax.ShapeDtypeStruct((d, d), BF16),
                   jax.ShapeDtypeStruct((e, d), BF16), jax.ShapeDtypeStruct((1, d), F32)],
        scratch_shapes=[pltpu.VMEM((pdim, d), F32), pltpu.VMEM((d, d), F32), pltpu.VMEM((e, d), F32)],
        compiler_params=_cparams("arbitrary"),
    )(dh2, gl, pp, h1, p_all, o, pn, wgate, wout)


def _mixer_bwd(do, proj, mixer, w_in, h, g, dh1):
    s, d = h.shape
    nsplit, _, e = proj.shape
    gdim = e // N_POOL_GROUPS
    ts = min(LAYER_ROW_TILE, s)
    nt = s // ts
    is_conv = mixer[0] == "conv"
    params = mixer[1:]
    n_mix_out = 1 if is_conv else 2

    def body(*refs):
        do_ref, p_ref, ph_ref = refs[:3]
        mix_refs = refs[3:3 + len(params)]
        win_ref, h_ref, g_ref, dh1_ref = refs[3 + len(params):7 + len(params)]
        dp_ref, dh_ref, dg_ref = refs[7 + len(params):10 + len(params)]
        mix_out = refs[10 + len(params):10 + len(params) + n_mix_out]
        scratch = refs[10 + len(params) + n_mix_out:]
        carry_ref = scratch[0]
        i = pl.program_id(0)
        tile = nt - 1 - i

        @pl.when(i == 0)
        def _():
            carry_ref[...] = jnp.zeros_like(carry_ref)
            dg_ref[...] = jnp.zeros_like(dg_ref)
            for ref in mix_out[-1:] + scratch[1:]:
                ref[...] = jnp.zeros_like(ref)

        dof = do_ref[...].astype(F32)
        nxt = carry_ref[...]
        if is_conv:
            w_ref, = mix_refs
            dw_ref, = mix_out
            w0, w1, w2 = w_ref[0:1, :], w_ref[1:2, :], w_ref[2:3, :]
            b, c, v, z = [p_ref[k].astype(F32) for k in range(4)]
            u = c * v
            u_prev = jnp.where(tile == 0, 0.0, ph_ref[1].astype(F32) * ph_ref[2].astype(F32))
            u1, u2 = _conv_taps(u, u_prev)
            conv = w0 * u2 + w1 * u1 + w2 * u
            sig = _sigmoid(z)
            sz = z * sig
            dy = dof * sz
            dp_ref[3] = (dof * (b * conv) * (sig + sz * (1.0 - sig))).astype(BF16)
            dp_ref[0] = (dy * conv).astype(BF16)
            dconv = dy * b
            dw_ref[0:1, :] += jnp.sum(dconv * u2, axis=0, keepdims=True)
            dw_ref[1:2, :] += jnp.sum(dconv * u1, axis=0, keepdims=True)
            dw_ref[2:3, :] += jnp.sum(dconv * u, axis=0, keepdims=True)
            dcc = jnp.concatenate([dconv, nxt], axis=0)
            du = w2 * dconv + w1 * _shift_up(dcc, 1)[:ts] + w0 * _shift_up(dcc, 2)[:ts]
            carry_ref[...] = dconv[:HALO]
            dp_ref[1] = (du * v).astype(BF16)
            dp_ref[2] = (du * c).astype(BF16)
        else:
            wgrp_ref, sc_ref = mix_refs
            dwg_ref, dsc_ref = mix_out
            agrp = scratch[1]
            u = p_ref[0].astype(F32)
            z = p_ref[1].astype(F32)
            u_prev = jnp.where(tile == 0, 0.0, ph_ref[0].astype(F32))
            uu = jnp.concatenate([u_prev, u], axis=0)
            sig = _sigmoid(z)
            sz = z * sig
            dm = dof * sz
            dsilu = dof * (sig + sz * (1.0 - sig))
            t = tile * ts + lax.broadcasted_iota(jnp.int32, (ts, 1), 0)
            for gi, window in enumerate(POOL_WINDOWS):
                cols = slice(gi * gdim, (gi + 1) * gdim)
                w = wgrp_ref[gi]
                scale = sc_ref[:, cols]
                db = _window_mean_minus(uu[:, cols], tile * ts, window).astype(BF16)
                mr = _dot(db, w)
                dp_ref[1, :, cols] = (dsilu[:, cols] * (mr * scale)).astype(BF16)
                dmg = dm[:, cols]
                dmr = (dmg * scale).astype(BF16)
                agrp[gi] += _dot_tn(db, dmr)
                dsc_ref[:, cols] += jnp.sum(dmg * mr, axis=0, keepdims=True)
                dd = _dot_nt(dmr, w)
                ddq = dd / jnp.minimum(t + 1, window).astype(F32)
                acc = jnp.concatenate([ddq, nxt[:, cols]], axis=0)
                span = 1
                while span < window:
                    acc = acc + _shift_up(acc, span)
                    span *= 2
                carry_ref[:, cols] = ddq[:HALO]
                dp_ref[0, :, cols] = (acc[:ts] - dd).astype(BF16)

            @pl.when(i == nt - 1)
            def _():
                dwg_ref[...] = agrp[...].astype(BF16)

        dhn = _dot_nt(dp_ref[0], win_ref[:, 0:e])
        for k in range(1, nsplit):
            dhn += _dot_nt(dp_ref[k], win_ref[:, k * e:(k + 1) * e])
        xh, r = _rms_stats(h_ref[...])
        dg_ref[...] += jnp.sum(dhn * xh, axis=0, keepdims=True)
        dh_ref[...] = dh1_ref[...] + _rms_bwd(dhn, xh, r, g_ref[...])

    rev = lambda width: pl.BlockSpec((ts, width), lambda i: (nt - 1 - i, 0))
    halo_blocks = ts // HALO
    in_specs = [rev(e),
                pl.BlockSpec((nsplit, ts, e), lambda i: (0, nt - 1 - i, 0)),
                pl.BlockSpec((nsplit, HALO, e), lambda i: (0, jnp.maximum((nt - 1 - i) * halo_blocks - 1, 0), 0))]
    in_specs += [_whole(*a.shape) for a in params]
    in_specs += [_whole(d, nsplit * e), rev(d), _whole(1, d), rev(d)]
    out_specs = [pl.BlockSpec((nsplit, ts, e), lambda i: (0, nt - 1 - i, 0)), rev(d), _whole(1, d)]
    out_shape = [jax.ShapeDtypeStruct((nsplit, s, e), BF16), jax.ShapeDtypeStruct((s, d), F32),
                 jax.ShapeDtypeStruct((1, d), F32)]
    scratch = [pltpu.VMEM((HALO, e), F32)]
    if is_conv:
        out_specs += [_whole(3, e)]
        out_shape += [jax.ShapeDtypeStruct((3, e), F32)]
    else:
        out_specs += [_whole(N_POOL_GROUPS, gdim, gdim), _whole(1, e)]
        out_shape += [jax.ShapeDtypeStruct((N_POOL_GROUPS, gdim, gdim), BF16), jax.ShapeDtypeStruct((1, e), F32)]
        scratch += [pltpu.VMEM((N_POOL_GROUPS, gdim, gdim), F32)]
    return pl.pallas_call(
        body, name="mixer_bwd",
        grid=(nt,),
        in_specs=in_specs, out_specs=out_specs, out_shape=out_shape, scratch_shapes=scratch,
        compiler_params=_cparams("arbitrary"),
    )(do, proj, proj, *params, w_in, h, g, dh1)


def _proj_wgrad(hn, dproj):
    s, d = hn.shape
    nsplit, _, e = dproj.shape
    ts = min(WGRAD_ROW_TILE, s)
    last = s // ts - 1

    def body(hn_ref, dp_ref, dw_ref, acc):
        i = pl.program_id(1)

        @pl.when(i == 0)
        def _():
            acc[...] = jnp.zeros_like(acc)

        acc[...] += _dot_tn(hn_ref[...], dp_ref[...])

        @pl.when(i == last)
        def _():
            dw_ref[...] = acc[...].astype(BF16)

    return pl.pallas_call(
        body, name="proj_wgrad",
        grid=(nsplit, s // ts),
        in_specs=[pl.BlockSpec((ts, d), lambda k, i: (i, 0)),
                  pl.BlockSpec((None, ts, e), lambda k, i: (k, i, 0))],
        out_specs=pl.BlockSpec((d, e), lambda k, i: (0, k)),
        out_shape=jax.ShapeDtypeStruct((d, nsplit * e), BF16),
        scratch_shapes=[pltpu.VMEM((d, e), F32)],
        compiler_params=_cparams("parallel", "arbitrary"),
    )(hn, dproj)


def _forward_backward(xs, ps, target, full, conv_w, scale_w, norm_mix, ple_norm, final_norm, exchange):
    depth = len(full)
    row = lambda a, i: a[i][None, :]
    mixer_of = lambda i: ("conv", conv_w[i // 2]) if i % 2 == 0 else ("pool", full[i]["w_grp"], row(scale_w, i // 2))

    saved = []
    h = xs
    for i in range(depth):
        w = full[i]
        h_next, acts = _layer_fwd(h, row(norm_mix, i), w["w_in"], mixer_of(i), w["w_out"], row(ple_norm, i),
                                  w["gate"], ps, w["proj"], i)
        saved.append((h, *acts))
        h = h_next
    dh, loss_row, d_final = _loss_head(h, target, final_norm[None, :])

    d_norm, d_ple_norm, d_conv, d_scale, sent = [None] * depth, [None] * depth, [], [], [None] * depth
    for i in reversed(range(depth)):
        w = full[i]
        h_in, proj, hn, o, h1, gl, pp = saved[i]
        g = {}
        dh1, do, g["proj"], g["gate"], g["w_out"], d_ple_norm[i] = _out_ple_bwd(
            dh, gl, pp, h1, ps, o, row(ple_norm, i), w["gate"], w["w_out"], i)
        outs = _mixer_bwd(do, proj, mixer_of(i), w["w_in"], h_in, row(norm_mix, i), dh1)
        dproj, dh, d_norm[i] = outs[:3]
        if i % 2 == 0:
            d_conv.insert(0, outs[3])
        else:
            g["w_grp"] = outs[3]
            d_scale.insert(0, outs[4])
        g["w_in"] = _proj_wgrad(hn, dproj)
        sent[i] = exchange(i, g)
    return loss_row, dh, sent, (d_norm, d_ple_norm, d_final, d_conv, d_scale)


VMEM_SPEC = pl.BlockSpec(memory_space=pltpu.VMEM)

FLIPS = [(fx, fy, fc) for fx in (0, 1) for fy in (0, 1) for fc in (0, 1)][1:]
SHARD_AXIS = {"w_in": 1, "w_out": 0, "w_grp": 1, "gate": 0, "proj": 1}


def _my_place():
    return lax.axis_index("x"), lax.axis_index("y"), lax.axis_index("c")


def _position(place):
    x, y, c = place
    return 4 * x + 2 * y + c


def _flip(place, flips):
    return tuple(1 - v if f else v for v, f in zip(place, flips))


def _shard_of(ref, axis, pos, n):
    idx = [slice(None)] * len(ref.shape)
    idx[axis] = pl.ds(pl.multiple_of(pos * n, n), n)
    return ref.at[tuple(idx)]


def _sequencer_mesh():
    return plsc.ScalarSubcoreMesh(axis_name="sequencer", num_cores=1)


def _handshake(peers):
    barrier = pltpu.get_barrier_semaphore()
    for peer in peers:
        pl.semaphore_signal(barrier, inc=1, device_id=peer, device_id_type=MESH)
    pl.semaphore_wait(barrier, len(peers))


def _all_gather_layer(shards, collective_id):
    names = list(shards)
    nt = len(names)
    axes = [SHARD_AXIS[k] for k in names]
    widths = [shards[k].shape[SHARD_AXIS[k]] for k in names]

    def full_shape(k):
        shp = list(shards[k].shape)
        shp[SHARD_AXIS[k]] *= N_DEV
        return tuple(shp)

    def body(*refs):
        ins, outs = refs[:nt], refs[nt:2 * nt]
        send_sems, recv_sems, local_sem = refs[2 * nt:]
        me = _my_place()
        x, y, c = me
        sibling = (x, y, 1 - c)
        chips = [(1 - x, y), (x, 1 - y), (1 - x, 1 - y)]
        _handshake([sibling] + [(*chip, c) for chip in chips])

        def block(t, place):
            return _shard_of(outs[t], axes[t], _position(place), widths[t])

        def copy(t, k, place, to, src=None):
            return pltpu.make_async_remote_copy(
                src_ref=block(t, place) if src is None else src, dst_ref=block(t, place),
                send_sem=send_sems.at[k], recv_sem=recv_sems.at[k], device_id=to, device_id_type=MESH)

        mine = [pltpu.make_async_copy(ins[t], block(t, me), local_sem) for t in range(nt)]
        for cp in mine:
            cp.start()
        first = []
        for j, chip in enumerate(chips):
            first += [copy(t, 1 + j, me, (*chip, c), src=ins[t]) for t in range(nt)]
        first += [copy(t, 0, me, sibling, src=ins[t]) for t in range(nt)]
        for cp in first:
            cp.start()
        passed = []
        for j, chip in enumerate(chips):
            for t in range(nt):
                copy(t, 1 + j, (*chip, c), me).wait_recv()
            for t in range(nt):
                fwd = copy(t, 4 + j, (*chip, c), sibling)
                fwd.start()
                passed.append(fwd)
        for t in range(nt):
            copy(t, 0, sibling, me).wait_recv()
        for j, chip in enumerate(chips):
            for t in range(nt):
                copy(t, 4 + j, (*chip, 1 - c), me).wait_recv()
        for cp in first + passed:
            cp.wait_send()
        for cp in mine:
            cp.wait()

    outs = pl.kernel(
        body, name=f"all_gather_layer_{collective_id}",
        out_type=[jax.ShapeDtypeStruct(full_shape(k), shards[k].dtype) for k in names],
        mesh=_sequencer_mesh(),
        scratch_types=[pltpu.SemaphoreType.DMA((7,)), pltpu.SemaphoreType.DMA((7,)), pltpu.SemaphoreType.DMA],
        compiler_params=pltpu.CompilerParams(collective_id=collective_id),
    )(*[shards[k] for k in names])
    return dict(zip(names, outs))


def _exchange_layer(grads, collective_id):
    names = list(grads)
    nt = len(names)
    axes = [SHARD_AXIS[k] for k in names]
    widths = [grads[k].shape[SHARD_AXIS[k]] // N_DEV for k in names]

    def slot_shape(t):
        shp = list(grads[names[t]].shape)
        shp[axes[t]] = widths[t]
        return (N_DEV, *shp)

    def body(*refs):
        ins, outs = refs[:nt], refs[nt:2 * nt]
        send_sems, recv_sems, local_sem = refs[2 * nt:]
        me = _my_place()
        mine = _position(me)
        _handshake([_flip(me, flips) for flips in FLIPS])
        local = [pltpu.make_async_copy(_shard_of(ins[t], axes[t], mine, widths[t]), outs[t].at[mine], local_sem)
                 for t in range(nt)]
        for cp in local:
            cp.start()
        copies = []
        for k, flips in enumerate(FLIPS):
            peer = _flip(me, flips)
            for t in range(nt):
                cp = pltpu.make_async_remote_copy(
                    src_ref=_shard_of(ins[t], axes[t], _position(peer), widths[t]), dst_ref=outs[t].at[mine],
                    send_sem=send_sems.at[k], recv_sem=recv_sems.at[k], device_id=peer, device_id_type=MESH)
                cp.start()
                copies.append(cp)
        for cp in copies:
            cp.wait()
        for cp in local:
            cp.wait()

    outs = pl.kernel(
        body, name=f"exchange_layer_{collective_id}",
        out_type=[jax.ShapeDtypeStruct(slot_shape(t), BF16) for t in range(nt)],
        mesh=_sequencer_mesh(),
        scratch_types=[pltpu.SemaphoreType.DMA((7,)), pltpu.SemaphoreType.DMA((7,)), pltpu.SemaphoreType.DMA],
        compiler_params=pltpu.CompilerParams(collective_id=collective_id),
    )(*[grads[k] for k in names])
    return dict(zip(names, outs))


def _gather_rows(buf, reduce):
    r, c = buf.shape

    def body(in_ref, out_ref, *scratch):
        if reduce:
            all_ref, send_sems, recv_sems = scratch
        else:
            all_ref = out_ref
            send_sems, recv_sems = scratch
        me = _my_place()
        all_ref[_position(me)] = in_ref[...]
        copies = []
        for k, flips in enumerate(FLIPS):
            cp = pltpu.make_async_remote_copy(
                src_ref=in_ref, dst_ref=all_ref.at[_position(me)],
                send_sem=send_sems.at[k], recv_sem=recv_sems.at[k], device_id=_flip(me, flips), device_id_type=MESH)
            cp.start()
            copies.append(cp)
        for cp in copies:
            cp.wait()
        if reduce:
            total = all_ref[0]
            for j in range(1, N_DEV):
                total = total + all_ref[j]
            out_ref[...] = total

    return pl.pallas_call(
        body, name="sum_rows" if reduce else "gather_rows",
        in_specs=[VMEM_SPEC], out_specs=VMEM_SPEC,
        out_shape=jax.ShapeDtypeStruct((r, c) if reduce else (N_DEV, r, c), F32),
        scratch_shapes=([pltpu.VMEM((N_DEV, r, c), F32)] if reduce else [])
        + [pltpu.SemaphoreType.DMA((7,)), pltpu.SemaphoreType.DMA((7,))],
    )(buf)


def _adamw_math(w, g, m, v):
    m = ADAM_B1 * m + (1.0 - ADAM_B1) * g
    v = ADAM_B2 * v + (1.0 - ADAM_B2) * (g * g)
    m_hat = m / (1.0 - ADAM_B1 ** ADAM_STEP)
    v_hat = v / (1.0 - ADAM_B2 ** ADAM_STEP)
    delta = -ADAM_LR * (m_hat / (jnp.sqrt(v_hat) + ADAM_EPS) + ADAM_WD * w)
    return delta, m, v


def _adamw_pieces(pieces, w, m, v):
    shape = w.shape
    nl = shape[0]
    cols = shape[-1]
    rows = w.size // (nl * cols)
    tr = min(256 // nl, rows)
    flat3 = lambda a: a.reshape(nl, rows, cols)

    def body(*refs):
        p_refs = refs[:nl]
        w_ref, m_ref, v_ref, g_ref, d_ref, nm_ref, nv_ref = refs[nl:]
        for l in range(nl):
            g = p_refs[l][0].astype(F32)
            for j in range(1, N_DEV):
                g = g + p_refs[l][j].astype(F32)
            g_ref[l] = g
            d_ref[l], nm_ref[l], nv_ref[l] = _adamw_math(w_ref[l], g, m_ref[l], v_ref[l])

    blk = pl.BlockSpec((nl, tr, cols), lambda i: (0, i, 0))
    outs = pl.pallas_call(
        body, name="adamw_pieces",
        grid=(rows // tr,),
        in_specs=[pl.BlockSpec((N_DEV, tr, cols), lambda i: (0, i, 0))] * nl + [blk, blk, blk],
        out_specs=[blk] * 4,
        out_shape=[jax.ShapeDtypeStruct((nl, rows, cols), F32)] * 4,
        compiler_params=_cparams("parallel"),
    )(*[a.reshape(N_DEV, rows, cols) for a in pieces], flat3(w), flat3(m), flat3(v))
    return [a.reshape(shape) for a in outs]


def _adamw_small(g, w, m, v):
    shape = w.shape
    two = lambda a: a.reshape(-1, shape[-1])

    def body(g_ref, w_ref, m_ref, v_ref, d_ref, nm_ref, nv_ref):
        d_ref[...], nm_ref[...], nv_ref[...] = _adamw_math(w_ref[...], g_ref[...], m_ref[...], v_ref[...])

    outs = pl.pallas_call(
        body, name="adamw_small",
        in_specs=[VMEM_SPEC] * 4, out_specs=[VMEM_SPEC] * 3,
        out_shape=[jax.ShapeDtypeStruct(two(w).shape, F32)] * 3,
    )(two(g), two(w), two(m), two(v))
    return [a.reshape(shape) for a in outs]


WEIGHTS = ("norm_mix", "a_w_in", "a_w_conv", "a_w_out", "b_w_in", "b_w_grp", "b_scale", "b_w_out",
           "ple_norm", "ple_w_gate", "ple_w_proj", "final_norm")
SMALL_ROWS = 24
GATHER_ID = 0
EXCHANGE_ID = 4
LAST_EXCHANGE_ID = 8


def kernel(x, p, norm_mix, a_w_in, a_w_conv, a_w_out, b_w_in, b_w_grp, b_scale, b_w_out, ple_norm, ple_w_gate, ple_w_proj, final_norm, loss_target, m_norm_mix, m_a_w_in, m_a_w_conv, m_a_w_out, m_b_w_in, m_b_w_grp, m_b_scale, m_b_w_out, m_ple_norm, m_ple_w_gate, m_ple_w_proj, m_final_norm, v_norm_mix, v_a_w_in, v_a_w_conv, v_a_w_out, v_b_w_in, v_b_w_grp, v_b_scale, v_b_w_out, v_ple_norm, v_ple_w_gate, v_ple_w_proj, v_final_norm):
    wts = dict(norm_mix=norm_mix, a_w_in=a_w_in, a_w_conv=a_w_conv, a_w_out=a_w_out, b_w_in=b_w_in, b_w_grp=b_w_grp,
               b_scale=b_scale, b_w_out=b_w_out, ple_norm=ple_norm, ple_w_gate=ple_w_gate, ple_w_proj=ple_w_proj,
               final_norm=final_norm)
    mom = dict(norm_mix=m_norm_mix, a_w_in=m_a_w_in, a_w_conv=m_a_w_conv, a_w_out=m_a_w_out, b_w_in=m_b_w_in,
               b_w_grp=m_b_w_grp, b_scale=m_b_scale, b_w_out=m_b_w_out, ple_norm=m_ple_norm, ple_w_gate=m_ple_w_gate,
               ple_w_proj=m_ple_w_proj, final_norm=m_final_norm)
    var = dict(norm_mix=v_norm_mix, a_w_in=v_a_w_in, a_w_conv=v_a_w_conv, a_w_out=v_a_w_out, b_w_in=v_b_w_in,
               b_w_grp=v_b_w_grp, b_scale=v_b_scale, b_w_out=v_b_w_out, ple_norm=v_ple_norm, ple_w_gate=v_ple_w_gate,
               ple_w_proj=v_ple_w_proj, final_norm=v_final_norm)
    d = x.shape[2]
    depth = norm_mix.shape[0]
    n_a, n_b = a_w_conv.shape[0], b_scale.shape[0]
    cw = a_w_conv.shape[2]
    pos = _position(_my_place())

    def layer_matrices(i):
        j = i // 2
        mixer = {"w_in": ("a_w_in", j), "w_out": ("a_w_out", j)} if i % 2 == 0 else \
                {"w_in": ("b_w_in", j), "w_grp": ("b_w_grp", j), "w_out": ("b_w_out", j)}
        return {**mixer, "gate": ("ple_w_gate", i), "proj": ("ple_w_proj", i)}

    full = [_all_gather_layer({k: wts[name][idx].astype(BF16) for k, (name, idx) in layer_matrices(i).items()},
                              GATHER_ID + i) for i in range(depth)]
    vec_rows = jnp.concatenate([a_w_conv.reshape(-1, cw), b_scale], axis=0)
    vecs = _gather_rows(vec_rows, reduce=False)
    n_conv = 3 * n_a
    conv_w = vecs[:, :n_conv].transpose(1, 0, 2).reshape(n_a, 3, N_DEV * cw)
    scale_w = vecs[:, n_conv:].transpose(1, 0, 2).reshape(n_b, N_DEV * cw)

    def exchange(i, g):
        if i > 0:
            return _exchange_layer(g, EXCHANGE_ID + i)
        early = {k: a for k, a in g.items() if k != "w_in"}
        return {**_exchange_layer(early, EXCHANGE_ID), **_exchange_layer({"w_in": g["w_in"]}, LAST_EXCHANGE_ID)}

    loss_row, dx, sent, (d_norm, d_ple_norm, d_final, d_conv, d_scale) = _forward_backward(
        x[0], p[:, 0], loss_target[0], full, conv_w, scale_w, norm_mix, ple_norm, final_norm, exchange)
    pieces = {name: [None] * wts[name].shape[0] for name in WEIGHTS if wts[name].ndim >= 3 and name != "a_w_conv"}
    for i in range(depth):
        for k, (name, idx) in layer_matrices(i).items():
            pieces[name][idx] = sent[i][k]

    pad = lambda a: jnp.pad(a, ((0, 0), (0, d - a.shape[1])))
    small = jnp.concatenate(d_norm + d_ple_norm + [d_final] + d_conv + d_scale + [pad(loss_row)], axis=0)
    small = jnp.pad(small, ((0, SMALL_ROWS - small.shape[0]), (0, 0)))
    total = _gather_rows(small, reduce=True)
    o = 0
    gsum = {}
    gsum["norm_mix"] = total[o:o + depth]; o += depth
    gsum["ple_norm"] = total[o:o + depth]; o += depth
    gsum["final_norm"] = total[o]; o += 1
    conv_full = total[o:o + n_conv].reshape(n_a, 3, d); o += n_conv
    scale_full = total[o:o + n_b]; o += n_b
    loss = total[o, 0]
    gsum["a_w_conv"] = lax.dynamic_slice_in_dim(conv_full, pos * cw, cw, axis=2)
    gsum["b_scale"] = lax.dynamic_slice_in_dim(scale_full, pos * cw, cw, axis=1)

    grad, delta, new_m, new_v = {}, {}, {}, {}
    for k in WEIGHTS:
        if k in pieces:
            grad[k], delta[k], new_m[k], new_v[k] = _adamw_pieces(pieces[k], wts[k], mom[k], var[k])
        else:
            grad[k] = gsum[k]
            delta[k], new_m[k], new_v[k] = _adamw_small(gsum[k], wts[k], mom[k], var[k])
    return (loss, dx[None], *[grad[k] for k in WEIGHTS], *[delta[k] for k in WEIGHTS],
            *[new_m[k] for k in WEIGHTS], *[new_v[k] for k in WEIGHTS])
```

```python
import jax
import jax.numpy as jnp
from jax import lax
from jax.experimental import pallas as pl
from jax.experimental.pallas import tpu as pltpu
from jax.experimental.pallas import tpu_sc as plsc

F32 = jnp.float32
BF16 = jnp.bfloat16
MESH = pl.DeviceIdType.MESH

RMS_EPS = 1e-6
POOL_WINDOWS = (2, 4, 8, 16)
N_POOL_GROUPS = len(POOL_WINDOWS)
ADAM_LR = 0.001
ADAM_B1 = 0.9
ADAM_B2 = 0.999
ADAM_EPS = 1e-08
ADAM_WD = 0.01
ADAM_STEP = 10
N_DEV = 8

HALO = 16
ROW_TILE = 512
LAYER_ROW_TILE = 256
BWD_ROW_TILE = 512
WGRAD_ROW_TILE = 1024
VMEM_LIMIT = 56 * 1024 * 1024


def _cparams(*sem):
    return pltpu.CompilerParams(dimension_semantics=sem, vmem_limit_bytes=VMEM_LIMIT)


def _dot(a, b):
    return jnp.dot(a, b, preferred_element_type=F32)


def _dot_nt(a, b):
    return lax.dot_general(a, b, (((1,), (1,)), ((), ())), preferred_element_type=F32)


def _dot_tn(a, b):
    return lax.dot_general(a, b, (((0,), (0,)), ((), ())), preferred_element_type=F32)


def _rms_stats(x):
    r = lax.rsqrt(jnp.mean(x * x, axis=-1, keepdims=True) + RMS_EPS)
    return x * r, r


def _rms_bwd(dy, xh, r, g):
    a = dy * g
    return r * (a - xh * jnp.mean(a * xh, axis=-1, keepdims=True))


def _sigmoid(x):
    return 1.0 / (1.0 + jnp.exp(-x))


def _shift_down(x, k):
    return pltpu.roll(x, k, 0)


def _shift_up(x, k):
    return pltpu.roll(x, x.shape[0] - k, 0)


def _conv_taps(u, u_prev):
    uu = jnp.concatenate([u_prev, u], axis=0)
    return _shift_down(uu, 1)[HALO:], _shift_down(uu, 2)[HALO:]


def _window_mean_minus(uu, row0, window):
    acc = uu
    span = 1
    while span < window:
        acc = acc + _shift_down(acc, span)
        span *= 2
    return acc[HALO:] * _inv_count(uu.shape[0] - HALO, row0, window) - uu[HALO:]


def _inv_count(rows, row0, window):
    t = row0 + lax.broadcasted_iota(jnp.int32, (rows, 1), 0)
    return 1.0 / jnp.minimum(t + 1, window).astype(F32)


def _whole(*shape):
    return pl.BlockSpec(shape, lambda i: (0,) * len(shape), pipeline_mode=pl.Buffered(1))


def _layer_fwd(h, g, w_in, mixer, w_out, pn, w_gate, p_all, w_proj, layer):
    s, d = h.shape
    n = w_in.shape[1]
    e = w_out.shape[0]
    nsplit = n // e
    gdim = e // N_POOL_GROUPS
    pdim = p_all.shape[2]
    ts = min(LAYER_ROW_TILE, s)
    is_conv = mixer[0] == "conv"
    params = mixer[1:]

    def body(*refs):
        h_ref, g_ref, win_ref = refs[:3]
        mix_refs = refs[3:3 + len(params)]
        wo_ref, pn_ref, wg_ref, p_ref, wp_ref = refs[3 + len(params):8 + len(params)]
        proj_ref, hn_ref, o_ref, h1_ref, h2_ref, gl_ref, pp_ref, carry_ref = refs[8 + len(params):]
        i = pl.program_id(0)

        @pl.when(i == 0)
        def _():
            carry_ref[...] = jnp.zeros_like(carry_ref)

        x = h_ref[...]
        xh, _ = _rms_stats(x)
        hn = (xh * g_ref[...]).astype(BF16)
        hn_ref[...] = hn
        parts = []
        for k in range(nsplit):
            part = _dot(hn, win_ref[:, k * e:(k + 1) * e])
            proj_ref[k] = part
            parts.append(part)
        prev = carry_ref[...]
        if is_conv:
            b, c, v, z = parts
            w_ref, = mix_refs
            u = c * v
            u1, u2 = _conv_taps(u, prev)
            mixed = b * (w_ref[0:1, :] * u2 + w_ref[1:2, :] * u1 + w_ref[2:3, :] * u)
        else:
            u, z = parts
            wgrp_ref, sc_ref = mix_refs
            uu = jnp.concatenate([prev, u], axis=0)
            cols = []
            for gi, window in enumerate(POOL_WINDOWS):
                dg = _window_mean_minus(uu[:, gi * gdim:(gi + 1) * gdim], i * ts, window)
                cols.append(_dot(dg.astype(BF16), wgrp_ref[gi]))
            mixed = jnp.concatenate(cols, axis=1) * sc_ref[...]
        carry_ref[...] = u[ts - HALO:]
        o = ((z * _sigmoid(z)) * mixed).astype(BF16)
        o_ref[...] = o
        h1 = x + _dot(o, wo_ref[...])
        h1_ref[...] = h1
        xh1, _ = _rms_stats(h1)
        gl = _dot((xh1 * pn_ref[...]).astype(BF16), wg_ref[...])
        pp = _dot(p_ref[...].astype(BF16), wp_ref[...])
        gl_ref[...] = gl.astype(BF16)
        pp_ref[...] = pp.astype(BF16)
        h2_ref[...] = h1 + _sigmoid(gl) * pp

    row = lambda width: pl.BlockSpec((ts, width), lambda i: (i, 0))
    mix_specs = [_whole(*a.shape) for a in params]
    outs = pl.pallas_call(
        body, name="layer_fwd",
        grid=(s // ts,),
        in_specs=[row(d), _whole(1, d), _whole(d, n)] + mix_specs
        + [_whole(e, d), _whole(1, d), _whole(d, d),
           pl.BlockSpec((None, ts, pdim), lambda i: (layer, i, 0)), _whole(pdim, d)],
        out_specs=[pl.BlockSpec((nsplit, ts, e), lambda i: (0, i, 0)),
                   row(d), row(e), row(d), row(d), row(d), row(d)],
        out_shape=[jax.ShapeDtypeStruct((nsplit, s, e), F32), jax.ShapeDtypeStruct((s, d), BF16),
                   jax.ShapeDtypeStruct((s, e), BF16), jax.ShapeDtypeStruct((s, d), F32),
                   jax.ShapeDtypeStruct((s, d), F32), jax.ShapeDtypeStruct((s, d), BF16),
                   jax.ShapeDtypeStruct((s, d), BF16)],
        scratch_shapes=[pltpu.VMEM((HALO, e), F32)],
        compiler_params=_cparams("arbitrary"),
    )(h, g, w_in, *params, w_out, pn, w_gate, p_all, w_proj)
    proj, hn, o, h1, h2, gl, pp = outs
    return h2, (proj, hn, o, h1, gl, pp)


def _loss_head(h, target, g):
    s, d = h.shape
    ts = min(ROW_TILE, s)

    def body(h_ref, t_ref, g_ref, dh_ref, loss_ref, dg_ref):
        @pl.when(pl.program_id(0) == 0)
        def _():
            loss_ref[...] = jnp.zeros_like(loss_ref)
            dg_ref[...] = jnp.zeros_like(dg_ref)

        gain = g_ref[...]
        xh, r = _rms_stats(h_ref[...])
        err = xh * gain - t_ref[...]
        loss_ref[...] += jnp.full(loss_ref.shape, (0.5 / d) * jnp.sum(err * err), F32)
        dy = err * (1.0 / d)
        dg_ref[...] += jnp.sum(dy * xh, axis=0, keepdims=True)
        dh_ref[...] = _rms_bwd(dy, xh, r, gain)

    row = pl.BlockSpec((ts, d), lambda i: (i, 0))
    return pl.pallas_call(
        body, name="loss_head",
        grid=(s // ts,),
        in_specs=[row, row, pl.BlockSpec((1, d), lambda i: (0, 0))],
        out_specs=[row, pl.BlockSpec((1, 128), lambda i: (0, 0)), pl.BlockSpec((1, d), lambda i: (0, 0))],
        out_shape=[jax.ShapeDtypeStruct((s, d), F32), jax.ShapeDtypeStruct((1, 128), F32),
                   jax.ShapeDtypeStruct((1, d), F32)],
        compiler_params=_cparams("arbitrary"),
    )(h, target, g)


def _out_ple_bwd(dh2, gl, pp, h1, p_all, o, pn, wgate, wout, layer):
    s, d = dh2.shape
    e = o.shape[1]
    pdim = p_all.shape[2]
    ts = min(BWD_ROW_TILE, s)
    last = s // ts - 1

    def body(dh2_ref, gl_ref, pp_ref, h1_ref, p_ref, o_ref, pn_ref, wg_ref, wo_ref,
             dh1_ref, do_ref, dwp_ref, dwg_ref, dwo_ref, dpn_ref, awp, awg, awo):
        i = pl.program_id(0)

        @pl.when(i == 0)
        def _():
            awp[...] = jnp.zeros_like(awp)
            awg[...] = jnp.zeros_like(awg)
            awo[...] = jnp.zeros_like(awo)
            dpn_ref[...] = jnp.zeros_like(dpn_ref)

        dh2 = dh2_ref[...]
        gate = _sigmoid(gl_ref[...].astype(F32))
        dpp = (dh2 * gate).astype(BF16)
        dgl = (dh2 * pp_ref[...].astype(F32) * gate * (1.0 - gate)).astype(BF16)
        xh, r = _rms_stats(h1_ref[...])
        pn = pn_ref[...]
        awp[...] += _dot_tn(p_ref[...].astype(BF16), dpp)
        awg[...] += _dot_tn((xh * pn).astype(BF16), dgl)
        dr = _dot_nt(dgl, wg_ref[...])
        dpn_ref[...] += jnp.sum(dr * xh, axis=0, keepdims=True)
        dh1 = dh2 + _rms_bwd(dr, xh, r, pn)
        dh1_ref[...] = dh1
        dh1b = dh1.astype(BF16)
        do_ref[...] = _dot_nt(dh1b, wo_ref[...]).astype(BF16)
        awo[...] += _dot_tn(o_ref[...], dh1b)

        @pl.when(i == last)
        def _():
            dwp_ref[...] = awp[...].astype(BF16)
            dwg_ref[...] = awg[...].astype(BF16)
            dwo_ref[...] = awo[...].astype(BF16)

    row = lambda width: pl.BlockSpec((ts, width), lambda i: (i, 0))
    return pl.pallas_call(
        body, name="out_ple_bwd",
        grid=(s // ts,),
        in_specs=[row(d), row(d), row(d), row(d),
                  pl.BlockSpec((None, ts, pdim), lambda i: (layer, i, 0)),
                  row(e), _whole(1, d), _whole(d, d), _whole(e, d)],
        out_specs=[row(d), row(e), _whole(pdim, d), _whole(d, d), _whole(e, d), _whole(1, d)],
        out_shape=[jax.ShapeDtypeStruct((s, d), F32), jax.ShapeDtypeStruct((s, e), BF16),
                   jax.ShapeDtypeStruct((pdim, d), BF16), jax.ShapeDtypeStruct((d, d), BF16),
                   jax.ShapeDtypeStruct((e, d), BF16), jax.ShapeDtypeStruct((1, d), F32)],
        scratch_shapes=[pltpu.VMEM((pdim, d), F32), pltpu.VMEM((d, d), F32), pltpu.VMEM((e, d), F32)],
        compiler_params=_cparams("arbitrary"),
    )(dh2, gl, pp, h1, p_all, o, pn, wgate, wout)


def _mixer_bwd(do, proj, mixer, w_in, h, g, dh1):
    s, d = h.shape
    nsplit, _, e = proj.shape
    gdim = e // N_POOL_GROUPS
    ts = min(LAYER_ROW_TILE, s)
    nt = s // ts
    is_conv = mixer[0] == "conv"
    params = mixer[1:]
    n_mix_out = 1 if is_conv else 2

    def body(*refs):
        do_ref, p_ref, ph_ref = refs[:3]
        mix_refs = refs[3:3 + len(params)]
        win_ref, h_ref, g_ref, dh1_ref = refs[3 + len(params):7 + len(params)]
        dp_ref, dh_ref, dg_ref = refs[7 + len(params):10 + len(params)]
        mix_out = refs[10 + len(params):10 + len(params) + n_mix_out]
        scratch = refs[10 + len(params) + n_mix_out:]
        carry_ref = scratch[0]
        i = pl.program_id(0)
        tile = nt - 1 - i

        @pl.when(i == 0)
        def _():
            carry_ref[...] = jnp.zeros_like(carry_ref)
            dg_ref[...] = jnp.zeros_like(dg_ref)
            for ref in mix_out[-1:] + scratch[1:]:
                ref[...] = jnp.zeros_like(ref)

        dof = do_ref[...].astype(F32)
        nxt = carry_ref[...]
        if is_conv:
            w_ref, = mix_refs
            dw_ref, = mix_out
            w0, w1, w2 = w_ref[0:1, :], w_ref[1:2, :], w_ref[2:3, :]
            b, c, v, z = [p_ref[k] for k in range(4)]
            u = c * v
            u_prev = jnp.where(tile == 0, 0.0, ph_ref[1] * ph_ref[2])
            u1, u2 = _conv_taps(u, u_prev)
            conv = w0 * u2 + w1 * u1 + w2 * u
            sig = _sigmoid(z)
            sz = z * sig
            dy = dof * sz
            dp_ref[3] = (dof * (b * conv) * (sig + sz * (1.0 - sig))).astype(BF16)
            dp_ref[0] = (dy * conv).astype(BF16)
            dconv = dy * b
            dw_ref[0:1, :] += jnp.sum(dconv * u2, axis=0, keepdims=True)
            dw_ref[1:2, :] += jnp.sum(dconv * u1, axis=0, keepdims=True)
            dw_ref[2:3, :] += jnp.sum(dconv * u, axis=0, keepdims=True)
            dcc = jnp.concatenate([dconv, nxt], axis=0)
            du = w2 * dconv + w1 * _shift_up(dcc, 1)[:ts] + w0 * _shift_up(dcc, 2)[:ts]
            carry_ref[...] = dconv[:HALO]
            dp_ref[1] = (du * v).astype(BF16)
            dp_ref[2] = (du * c).astype(BF16)
        else:
            wgrp_ref, sc_ref = mix_refs
            dwg_ref, dsc_ref = mix_out
            agrp = scratch[1]
            u = p_ref[0]
            z = p_ref[1]
            u_prev = jnp.where(tile == 0, 0.0, ph_ref[0])
            uu = jnp.concatenate([u_prev, u], axis=0)
            sig = _sigmoid(z)
            sz = z * sig
            dm = dof * sz
            dsilu = dof * (sig + sz * (1.0 - sig))
            for gi, window in enumerate(POOL_WINDOWS):
                cols = slice(gi * gdim, (gi + 1) * gdim)
                w = wgrp_ref[gi]
                scale = sc_ref[:, cols]
                db = _window_mean_minus(uu[:, cols], tile * ts, window).astype(BF16)
                mr = _dot(db, w)
                dp_ref[1, :, cols] = (dsilu[:, cols] * (mr * scale)).astype(BF16)
                dmg = dm[:, cols]
                dmr = (dmg * scale).astype(BF16)
                agrp[gi] += _dot_tn(db, dmr)
                dsc_ref[:, cols] += jnp.sum(dmg * mr, axis=0, keepdims=True)
                dd = _dot_nt(dmr, w)
                ddq = dd * _inv_count(ts, tile * ts, window)
                acc = jnp.concatenate([ddq, nxt[:, cols]], axis=0)
                span = 1
                while span < window:
                    acc = acc + _shift_up(acc, span)
                    span *= 2
                carry_ref[:, cols] = ddq[:HALO]
                dp_ref[0, :, cols] = (acc[:ts] - dd).astype(BF16)

            @pl.when(i == nt - 1)
            def _():
                dwg_ref[...] = agrp[...].astype(BF16)

        dhn = _dot_nt(dp_ref[0], win_ref[:, 0:e])
        for k in range(1, nsplit):
            dhn += _dot_nt(dp_ref[k], win_ref[:, k * e:(k + 1) * e])
        xh, r = _rms_stats(h_ref[...])
        dg_ref[...] += jnp.sum(dhn * xh, axis=0, keepdims=True)
        dh_ref[...] = dh1_ref[...] + _rms_bwd(dhn, xh, r, g_ref[...])

    rev = lambda width: pl.BlockSpec((ts, width), lambda i: (nt - 1 - i, 0))
    halo_blocks = ts // HALO
    in_specs = [rev(e),
                pl.BlockSpec((nsplit, ts, e), lambda i: (0, nt - 1 - i, 0)),
                pl.BlockSpec((nsplit, HALO, e), lambda i: (0, jnp.maximum((nt - 1 - i) * halo_blocks - 1, 0), 0))]
    in_specs += [_whole(*a.shape) for a in params]
    in_specs += [_whole(d, nsplit * e), rev(d), _whole(1, d), rev(d)]
    out_specs = [pl.BlockSpec((nsplit, ts, e), lambda i: (0, nt - 1 - i, 0)), rev(d), _whole(1, d)]
    out_shape = [jax.ShapeDtypeStruct((nsplit, s, e), BF16), jax.ShapeDtypeStruct((s, d), F32),
                 jax.ShapeDtypeStruct((1, d), F32)]
    scratch = [pltpu.VMEM((HALO, e), F32)]
    if is_conv:
        out_specs += [_whole(3, e)]
        out_shape += [jax.ShapeDtypeStruct((3, e), F32)]
    else:
        out_specs += [_whole(N_POOL_GROUPS, gdim, gdim), _whole(1, e)]
        out_shape += [jax.ShapeDtypeStruct((N_POOL_GROUPS, gdim, gdim), BF16), jax.ShapeDtypeStruct((1, e), F32)]
        scratch += [pltpu.VMEM((N_POOL_GROUPS, gdim, gdim), F32)]
    return pl.pallas_call(
        body, name="mixer_bwd",
        grid=(nt,),
        in_specs=in_specs, out_specs=out_specs, out_shape=out_shape, scratch_shapes=scratch,
        compiler_params=_cparams("arbitrary"),
    )(do, proj, proj, *params, w_in, h, g, dh1)


def _proj_wgrad(hn, dproj):
    s, d = hn.shape
    nsplit, _, e = dproj.shape
    ts = min(WGRAD_ROW_TILE, s)
    last = s // ts - 1

    def body(hn_ref, dp_ref, dw_ref, acc):
        i = pl.program_id(1)

        @pl.when(i == 0)
        def _():
            acc[...] = jnp.zeros_like(acc)

        acc[...] += _dot_tn(hn_ref[...], dp_ref[...])

        @pl.when(i == last)
        def _():
            dw_ref[...] = acc[...].astype(BF16)

    return pl.pallas_call(
        body, name="proj_wgrad",
        grid=(nsplit, s // ts),
        in_specs=[pl.BlockSpec((ts, d), lambda k, i: (i, 0)),
                  pl.BlockSpec((None, ts, e), lambda k, i: (k, i, 0))],
        out_specs=pl.BlockSpec((d, e), lambda k, i: (0, k)),
        out_shape=jax.ShapeDtypeStruct((d, nsplit * e), BF16),
        scratch_shapes=[pltpu.VMEM((d, e), F32)],
        compiler_params=_cparams("parallel", "arbitrary"),
    )(hn, dproj)


def _forward_backward(xs, ps, target, full, conv_w, scale_w, norm_mix, ple_norm, final_norm, exchange):
    depth = len(full)
    row = lambda a, i: a[i][None, :]
    mixer_of = lambda i: ("conv", conv_w[i // 2]) if i % 2 == 0 else ("pool", full[i]["w_grp"], row(scale_w, i // 2))

    saved = []
    h = xs
    for i in range(depth):
        w = full[i]
        h_next, acts = _layer_fwd(h, row(norm_mix, i), w["w_in"], mixer_of(i), w["w_out"], row(ple_norm, i),
                                  w["gate"], ps, w["proj"], i)
        saved.append((h, *acts))
        h = h_next
    dh, loss_row, d_final = _loss_head(h, target, final_norm[None, :])

    d_norm, d_ple_norm, d_conv, d_scale, sent = [None] * depth, [None] * depth, [], [], [None] * depth
    for i in reversed(range(depth)):
        w = full[i]
        h_in, proj, hn, o, h1, gl, pp = saved[i]
        g = {}
        dh1, do, g["proj"], g["gate"], g["w_out"], d_ple_norm[i] = _out_ple_bwd(
            dh, gl, pp, h1, ps, o, row(ple_norm, i), w["gate"], w["w_out"], i)
        outs = _mixer_bwd(do, proj, mixer_of(i), w["w_in"], h_in, row(norm_mix, i), dh1)
        dproj, dh, d_norm[i] = outs[:3]
        if i % 2 == 0:
            d_conv.insert(0, outs[3])
        else:
            g["w_grp"] = outs[3]
            d_scale.insert(0, outs[4])
        g["w_in"] = _proj_wgrad(hn, dproj)
        sent[i] = exchange(i, g)
    return loss_row, dh, sent, (d_norm, d_ple_norm, d_final, d_conv, d_scale)


VMEM_SPEC = pl.BlockSpec(memory_space=pltpu.VMEM)

FLIPS = [(fx, fy, fc) for fx in (0, 1) for fy in (0, 1) for fc in (0, 1)][1:]
SHARD_AXIS = {"w_in": 1, "w_out": 0, "w_grp": 1, "gate": 0, "proj": 1}


def _my_place():
    return lax.axis_index("x"), lax.axis_index("y"), lax.axis_index("c")


def _position(place):
    x, y, c = place
    return 4 * x + 2 * y + c


def _flip(place, flips):
    return tuple(1 - v if f else v for v, f in zip(place, flips))


def _shard_of(ref, axis, pos, n):
    idx = [slice(None)] * len(ref.shape)
    idx[axis] = pl.ds(pl.multiple_of(pos * n, n), n)
    return ref.at[tuple(idx)]


def _sequencer_mesh():
    return plsc.ScalarSubcoreMesh(axis_name="sequencer", num_cores=1)


def _handshake(peers):
    barrier = pltpu.get_barrier_semaphore()
    for peer in peers:
        pl.semaphore_signal(barrier, inc=1, device_id=peer, device_id_type=MESH)
    pl.semaphore_wait(barrier, len(peers))


def _all_gather_layer(shards, collective_id):
    names = list(shards)
    nt = len(names)
    axes = [SHARD_AXIS[k] for k in names]
    widths = [shards[k].shape[SHARD_AXIS[k]] for k in names]

    def full_shape(k):
        shp = list(shards[k].shape)
        shp[SHARD_AXIS[k]] *= N_DEV
        return tuple(shp)

    def body(*refs):
        ins, outs = refs[:nt], refs[nt:2 * nt]
        send_sems, recv_sems, local_sem = refs[2 * nt:]
        me = _my_place()
        x, y, c = me
        sibling = (x, y, 1 - c)
        chips = [(1 - x, y), (x, 1 - y), (1 - x, 1 - y)]
        _handshake([sibling] + [(*chip, c) for chip in chips])

        def block(t, place):
            return _shard_of(outs[t], axes[t], _position(place), widths[t])

        def copy(t, k, place, to, src=None):
            return pltpu.make_async_remote_copy(
                src_ref=block(t, place) if src is None else src, dst_ref=block(t, place),
                send_sem=send_sems.at[k], recv_sem=recv_sems.at[k], device_id=to, device_id_type=MESH)

        mine = [pltpu.make_async_copy(ins[t], block(t, me), local_sem) for t in range(nt)]
        for cp in mine:
            cp.start()
        first = []
        for j, chip in enumerate(chips):
            first += [copy(t, 1 + j, me, (*chip, c), src=ins[t]) for t in range(nt)]
        first += [copy(t, 0, me, sibling, src=ins[t]) for t in range(nt)]
        for cp in first:
            cp.start()
        passed = []
        for j, chip in enumerate(chips):
            for t in range(nt):
                copy(t, 1 + j, (*chip, c), me).wait_recv()
            for t in range(nt):
                fwd = copy(t, 4 + j, (*chip, c), sibling)
                fwd.start()
                passed.append(fwd)
        for t in range(nt):
            copy(t, 0, sibling, me).wait_recv()
        for j, chip in enumerate(chips):
            for t in range(nt):
                copy(t, 4 + j, (*chip, 1 - c), me).wait_recv()
        for cp in first + passed:
            cp.wait_send()
        for cp in mine:
            cp.wait()

    outs = pl.kernel(
        body, name=f"all_gather_layer_{collective_id}",
        out_type=[jax.ShapeDtypeStruct(full_shape(k), shards[k].dtype) for k in names],
        mesh=_sequencer_mesh(),
        scratch_types=[pltpu.SemaphoreType.DMA((7,)), pltpu.SemaphoreType.DMA((7,)), pltpu.SemaphoreType.DMA],
        compiler_params=pltpu.CompilerParams(collective_id=collective_id),
    )(*[shards[k] for k in names])
    return dict(zip(names, outs))


def _exchange_layer(grads, collective_id):
    names = list(grads)
    nt = len(names)
    axes = [SHARD_AXIS[k] for k in names]
    widths = [grads[k].shape[SHARD_AXIS[k]] // N_DEV for k in names]

    def slot_shape(t):
        shp = list(grads[names[t]].shape)
        shp[axes[t]] = widths[t]
        return (N_DEV, *shp)

    def body(*refs):
        ins, outs = refs[:nt], refs[nt:2 * nt]
        send_sems, recv_sems, local_sem = refs[2 * nt:]
        me = _my_place()
        mine = _position(me)
        _handshake([_flip(me, flips) for flips in FLIPS])
        local = [pltpu.make_async_copy(_shard_of(ins[t], axes[t], mine, widths[t]), outs[t].at[mine], local_sem)
                 for t in range(nt)]
        for cp in local:
            cp.start()
        copies = []
        for k, flips in enumerate(FLIPS):
            peer = _flip(me, flips)
            for t in range(nt):
                cp = pltpu.make_async_remote_copy(
                    src_ref=_shard_of(ins[t], axes[t], _position(peer), widths[t]), dst_ref=outs[t].at[mine],
                    send_sem=send_sems.at[k], recv_sem=recv_sems.at[k], device_id=peer, device_id_type=MESH)
                cp.start()
                copies.append(cp)
        for cp in copies:
            cp.wait()
        for cp in local:
            cp.wait()

    outs = pl.kernel(
        body, name=f"exchange_layer_{collective_id}",
        out_type=[jax.ShapeDtypeStruct(slot_shape(t), BF16) for t in range(nt)],
        mesh=_sequencer_mesh(),
        scratch_types=[pltpu.SemaphoreType.DMA((7,)), pltpu.SemaphoreType.DMA((7,)), pltpu.SemaphoreType.DMA],
        compiler_params=pltpu.CompilerParams(collective_id=collective_id),
    )(*[grads[k] for k in names])
    return dict(zip(names, outs))


def _gather_rows(buf, reduce):
    r, c = buf.shape

    def body(in_ref, out_ref, *scratch):
        if reduce:
            all_ref, send_sems, recv_sems = scratch
        else:
            all_ref = out_ref
            send_sems, recv_sems = scratch
        me = _my_place()
        all_ref[_position(me)] = in_ref[...]
        copies = []
        for k, flips in enumerate(FLIPS):
            cp = pltpu.make_async_remote_copy(
                src_ref=in_ref, dst_ref=all_ref.at[_position(me)],
                send_sem=send_sems.at[k], recv_sem=recv_sems.at[k], device_id=_flip(me, flips), device_id_type=MESH)
            cp.start()
            copies.append(cp)
        for cp in copies:
            cp.wait()
        if reduce:
            total = all_ref[0]
            for j in range(1, N_DEV):
                total = total + all_ref[j]
            out_ref[...] = total

    return pl.pallas_call(
        body, name="sum_rows" if reduce else "gather_rows",
        in_specs=[VMEM_SPEC], out_specs=VMEM_SPEC,
        out_shape=jax.ShapeDtypeStruct((r, c) if reduce else (N_DEV, r, c), F32),
        scratch_shapes=([pltpu.VMEM((N_DEV, r, c), F32)] if reduce else [])
        + [pltpu.SemaphoreType.DMA((7,)), pltpu.SemaphoreType.DMA((7,))],
    )(buf)


def _adamw_math(w, g, m, v):
    m = ADAM_B1 * m + (1.0 - ADAM_B1) * g
    v = ADAM_B2 * v + (1.0 - ADAM_B2) * (g * g)
    m_hat = m / (1.0 - ADAM_B1 ** ADAM_STEP)
    v_hat = v / (1.0 - ADAM_B2 ** ADAM_STEP)
    delta = -ADAM_LR * (m_hat / (jnp.sqrt(v_hat) + ADAM_EPS) + ADAM_WD * w)
    return delta, m, v


def _adamw_pieces(pieces, w, m, v):
    shape = w.shape
    nl = shape[0]
    cols = shape[-1]
    rows = w.size // (nl * cols)
    tr = min(256 // nl, rows)
    flat3 = lambda a: a.reshape(nl, rows, cols)

    def body(*refs):
        p_refs = refs[:nl]
        w_ref, m_ref, v_ref, g_ref, d_ref, nm_ref, nv_ref = refs[nl:]
        for l in range(nl):
            g = p_refs[l][0].astype(F32)
            for j in range(1, N_DEV):
                g = g + p_refs[l][j].astype(F32)
            g_ref[l] = g
            d_ref[l], nm_ref[l], nv_ref[l] = _adamw_math(w_ref[l], g, m_ref[l], v_ref[l])

    blk = pl.BlockSpec((nl, tr, cols), lambda i: (0, i, 0))
    outs = pl.pallas_call(
        body, name="adamw_pieces",
        grid=(rows // tr,),
        in_specs=[pl.BlockSpec((N_DEV, tr, cols), lambda i: (0, i, 0))] * nl + [blk, blk, blk],
        out_specs=[blk] * 4,
        out_shape=[jax.ShapeDtypeStruct((nl, rows, cols), F32)] * 4,
        compiler_params=_cparams("parallel"),
    )(*[a.reshape(N_DEV, rows, cols) for a in pieces], flat3(w), flat3(m), flat3(v))
    return [a.reshape(shape) for a in outs]


def _adamw_small(g, w, m, v):
    shape = w.shape
    two = lambda a: a.reshape(-1, shape[-1])

    def body(g_ref, w_ref, m_ref, v_ref, d_ref, nm_ref, nv_ref):
        d_ref[...], nm_ref[...], nv_ref[...] = _adamw_math(w_ref[...], g_ref[...], m_ref[...], v_ref[...])

    outs = pl.pallas_call(
        body, name="adamw_small",
        in_specs=[VMEM_SPEC] * 4, out_specs=[VMEM_SPEC] * 3,
        out_shape=[jax.ShapeDtypeStruct(two(w).shape, F32)] * 3,
    )(two(g), two(w), two(m), two(v))
    return [a.reshape(shape) for a in outs]


WEIGHTS = ("norm_mix", "a_w_in", "a_w_conv", "a_w_out", "b_w_in", "b_w_grp", "b_scale", "b_w_out",
           "ple_norm", "ple_w_gate", "ple_w_proj", "final_norm")
SMALL_ROWS = 24
GATHER_ID = 0
EXCHANGE_ID = 4
LAST_EXCHANGE_ID = 8


def kernel(x, p, norm_mix, a_w_in, a_w_conv, a_w_out, b_w_in, b_w_grp, b_scale, b_w_out, ple_norm, ple_w_gate, ple_w_proj, final_norm, loss_target, m_norm_mix, m_a_w_in, m_a_w_conv, m_a_w_out, m_b_w_in, m_b_w_grp, m_b_scale, m_b_w_out, m_ple_norm, m_ple_w_gate, m_ple_w_proj, m_final_norm, v_norm_mix, v_a_w_in, v_a_w_conv, v_a_w_out, v_b_w_in, v_b_w_grp, v_b_scale, v_b_w_out, v_ple_norm, v_ple_w_gate, v_ple_w_proj, v_final_norm):
    wts = dict(norm_mix=norm_mix, a_w_in=a_w_in, a_w_conv=a_w_conv, a_w_out=a_w_out, b_w_in=b_w_in, b_w_grp=b_w_grp,
               b_scale=b_scale, b_w_out=b_w_out, ple_norm=ple_norm, ple_w_gate=ple_w_gate, ple_w_proj=ple_w_proj,
               final_norm=final_norm)
    mom = dict(norm_mix=m_norm_mix, a_w_in=m_a_w_in, a_w_conv=m_a_w_conv, a_w_out=m_a_w_out, b_w_in=m_b_w_in,
               b_w_grp=m_b_w_grp, b_scale=m_b_scale, b_w_out=m_b_w_out, ple_norm=m_ple_norm, ple_w_gate=m_ple_w_gate,
               ple_w_proj=m_ple_w_proj, final_norm=m_final_norm)
    var = dict(norm_mix=v_norm_mix, a_w_in=v_a_w_in, a_w_conv=v_a_w_conv, a_w_out=v_a_w_out, b_w_in=v_b_w_in,
               b_w_grp=v_b_w_grp, b_scale=v_b_scale, b_w_out=v_b_w_out, ple_norm=v_ple_norm, ple_w_gate=v_ple_w_gate,
               ple_w_proj=v_ple_w_proj, final_norm=v_final_norm)
    d = x.shape[2]
    depth = norm_mix.shape[0]
    n_a, n_b = a_w_conv.shape[0], b_scale.shape[0]
    cw = a_w_conv.shape[2]
    pos = _position(_my_place())

    def layer_matrices(i):
        j = i // 2
        mixer = {"w_in": ("a_w_in", j), "w_out": ("a_w_out", j)} if i % 2 == 0 else \
                {"w_in": ("b_w_in", j), "w_grp": ("b_w_grp", j), "w_out": ("b_w_out", j)}
        return {**mixer, "gate": ("ple_w_gate", i), "proj": ("ple_w_proj", i)}

    full = [_all_gather_layer({k: wts[name][idx].astype(BF16) for k, (name, idx) in layer_matrices(i).items()},
                              GATHER_ID + i) for i in range(depth)]
    vec_rows = jnp.concatenate([a_w_conv.reshape(-1, cw), b_scale], axis=0)
    vecs = _gather_rows(vec_rows, reduce=False)
    n_conv = 3 * n_a
    conv_w = vecs[:, :n_conv].transpose(1, 0, 2).reshape(n_a, 3, N_DEV * cw)
    scale_w = vecs[:, n_conv:].transpose(1, 0, 2).reshape(n_b, N_DEV * cw)

    def exchange(i, g):
        if i > 0:
            return _exchange_layer(g, EXCHANGE_ID + i)
        early = {k: a for k, a in g.items() if k != "w_in"}
        return {**_exchange_layer(early, EXCHANGE_ID), **_exchange_layer({"w_in": g["w_in"]}, LAST_EXCHANGE_ID)}

    loss_row, dx, sent, (d_norm, d_ple_norm, d_final, d_conv, d_scale) = _forward_backward(
        x[0], p[:, 0], loss_target[0], full, conv_w, scale_w, norm_mix, ple_norm, final_norm, exchange)
    pieces = {name: [None] * wts[name].shape[0] for name in WEIGHTS if wts[name].ndim >= 3 and name != "a_w_conv"}
    for i in range(depth):
        for k, (name, idx) in layer_matrices(i).items():
            pieces[name][idx] = sent[i][k]

    pad = lambda a: jnp.pad(a, ((0, 0), (0, d - a.shape[1])))
    small = jnp.concatenate(d_norm + d_ple_norm + [d_final] + d_conv + d_scale + [pad(loss_row)], axis=0)
    small = jnp.pad(small, ((0, SMALL_ROWS - small.shape[0]), (0, 0)))
    total = _gather_rows(small, reduce=True)
    o = 0
    gsum = {}
    gsum["norm_mix"] = total[o:o + depth]; o += depth
    gsum["ple_norm"] = total[o:o + depth]; o += depth
    gsum["final_norm"] = total[o]; o += 1
    conv_full = total[o:o + n_conv].reshape(n_a, 3, d); o += n_conv
    scale_full = total[o:o + n_b]; o += n_b
    loss = total[o, 0]
    gsum["a_w_conv"] = lax.dynamic_slice_in_dim(conv_full, pos * cw, cw, axis=2)
    gsum["b_scale"] = lax.dynamic_slice_in_dim(scale_full, pos * cw, cw, axis=1)

    grad, delta, new_m, new_v = {}, {}, {}, {}
    for k in sorted(WEIGHTS, key=lambda name: name == "a_w_in"):
        if k in pieces:
            grad[k], delta[k], new_m[k], new_v[k] = _adamw_pieces(pieces[k], wts[k], mom[k], var[k])
        else:
            grad[k] = gsum[k]
            delta[k], new_m[k], new_v[k] = _adamw_small(gsum[k], wts[k], mom[k], var[k])
    return (loss, dx[None], *[grad[k] for k in WEIGHTS], *[delta[k] for k in WEIGHTS],
            *[new_m[k] for k in WEIGHTS], *[new_v[k] for k in WEIGHTS])
```

```python
import jax
import jax.numpy as jnp
from jax import lax
from jax.experimental import pallas as pl
from jax.experimental.pallas import tpu as pltpu
from jax.experimental.pallas import tpu_sc as plsc

F32 = jnp.float32
BF16 = jnp.bfloat16
MESH = pl.DeviceIdType.MESH

RMS_EPS = 1e-6
POOL_WINDOWS = (2, 4, 8, 16)
N_POOL_GROUPS = len(POOL_WINDOWS)
ADAM_LR = 0.001
ADAM_B1 = 0.9
ADAM_B2 = 0.999
ADAM_EPS = 1e-08
ADAM_WD = 0.01
ADAM_STEP = 10
N_DEV = 8

HALO = 16
ROW_TILE = 512
LAYER_ROW_TILE = 256
BWD_ROW_TILE = 512
WGRAD_ROW_TILE = 1024
VMEM_LIMIT = 56 * 1024 * 1024


def _cparams(*sem):
    return pltpu.CompilerParams(dimension_semantics=sem, vmem_limit_bytes=VMEM_LIMIT)


def _dot(a, b):
    return jnp.dot(a, b, preferred_element_type=F32)


def _dot_nt(a, b):
    return lax.dot_general(a, b, (((1,), (1,)), ((), ())), preferred_element_type=F32)


def _dot_tn(a, b):
    return lax.dot_general(a, b, (((0,), (0,)), ((), ())), preferred_element_type=F32)


def _rms_stats(x):
    r = lax.rsqrt(jnp.mean(x * x, axis=-1, keepdims=True) + RMS_EPS)
    return x * r, r


def _rms_bwd(dy, xh, r, g):
    a = dy * g
    return r * (a - xh * jnp.mean(a * xh, axis=-1, keepdims=True))


def _sigmoid(x):
    return 1.0 / (1.0 + jnp.exp(-x))


def _shift_down(x, k):
    return pltpu.roll(x, k, 0)


def _shift_up(x, k):
    return pltpu.roll(x, x.shape[0] - k, 0)


def _conv_taps(u, u_prev):
    uu = jnp.concatenate([u_prev, u], axis=0)
    return _shift_down(uu, 1)[HALO:], _shift_down(uu, 2)[HALO:]


def _window_mean_minus(uu, row0, window):
    acc = uu
    span = 1
    while span < window:
        acc = acc + _shift_down(acc, span)
        span *= 2
    return acc[HALO:] * _inv_count(uu.shape[0] - HALO, row0, window) - uu[HALO:]


def _inv_count(rows, row0, window):
    t = row0 + lax.broadcasted_iota(jnp.int32, (rows, 1), 0)
    return 1.0 / jnp.minimum(t + 1, window).astype(F32)


def _whole(*shape):
    return pl.BlockSpec(shape, lambda i: (0,) * len(shape), pipeline_mode=pl.Buffered(1))


def _layer_fwd(h, g, w_in, mixer, w_out, pn, w_gate, p_all, w_proj, layer):
    s, d = h.shape
    n = w_in.shape[1]
    e = w_out.shape[0]
    nsplit = n // e
    gdim = e // N_POOL_GROUPS
    pdim = p_all.shape[2]
    ts = min(LAYER_ROW_TILE, s)
    is_conv = mixer[0] == "conv"
    params = mixer[1:]

    def body(*refs):
        h_ref, g_ref, win_ref = refs[:3]
        mix_refs = refs[3:3 + len(params)]
        wo_ref, pn_ref, wg_ref, p_ref, wp_ref = refs[3 + len(params):8 + len(params)]
        proj_ref, hn_ref, o_ref, h1_ref, h2_ref, gl_ref, pp_ref, carry_ref = refs[8 + len(params):]
        i = pl.program_id(0)

        @pl.when(i == 0)
        def _():
            carry_ref[...] = jnp.zeros_like(carry_ref)

        x = h_ref[...]
        xh, _ = _rms_stats(x)
        hn = (xh * g_ref[...]).astype(BF16)
        hn_ref[...] = hn
        parts = []
        for k in range(nsplit):
            part = _dot(hn, win_ref[:, k * e:(k + 1) * e])
            proj_ref[k] = part.astype(BF16)
            parts.append(part)
        prev = carry_ref[...]
        if is_conv:
            b, c, v, z = parts
            w_ref, = mix_refs
            u = c * v
            u1, u2 = _conv_taps(u, prev)
            mixed = b * (w_ref[0:1, :] * u2 + w_ref[1:2, :] * u1 + w_ref[2:3, :] * u)
        else:
            u, z = parts
            wgrp_ref, sc_ref = mix_refs
            uu = jnp.concatenate([prev, u], axis=0)
            cols = []
            for gi, window in enumerate(POOL_WINDOWS):
                dg = _window_mean_minus(uu[:, gi * gdim:(gi + 1) * gdim], i * ts, window)
                cols.append(_dot(dg.astype(BF16), wgrp_ref[gi]))
            mixed = jnp.concatenate(cols, axis=1) * sc_ref[...]
        carry_ref[...] = u[ts - HALO:]
        o = ((z * _sigmoid(z)) * mixed).astype(BF16)
        o_ref[...] = o
        h1 = x + _dot(o, wo_ref[...])
        h1_ref[...] = h1
        xh1, _ = _rms_stats(h1)
        gl = _dot((xh1 * pn_ref[...]).astype(BF16), wg_ref[...])
        pp = _dot(p_ref[...].astype(BF16), wp_ref[...])
        gl_ref[...] = gl.astype(BF16)
        pp_ref[...] = pp.astype(BF16)
        h2_ref[...] = h1 + _sigmoid(gl) * pp

    row = lambda width: pl.BlockSpec((ts, width), lambda i: (i, 0))
    mix_specs = [_whole(*a.shape) for a in params]
    outs = pl.pallas_call(
        body, name="layer_fwd",
        grid=(s // ts,),
        in_specs=[row(d), _whole(1, d), _whole(d, n)] + mix_specs
        + [_whole(e, d), _whole(1, d), _whole(d, d),
           pl.BlockSpec((None, ts, pdim), lambda i: (layer, i, 0)), _whole(pdim, d)],
        out_specs=[pl.BlockSpec((nsplit, ts, e), lambda i: (0, i, 0)),
                   row(d), row(e), row(d), row(d), row(d), row(d)],
        out_shape=[jax.ShapeDtypeStruct((nsplit, s, e), BF16), jax.ShapeDtypeStruct((s, d), BF16),
                   jax.ShapeDtypeStruct((s, e), BF16), jax.ShapeDtypeStruct((s, d), F32),
                   jax.ShapeDtypeStruct((s, d), F32), jax.ShapeDtypeStruct((s, d), BF16),
                   jax.ShapeDtypeStruct((s, d), BF16)],
        scratch_shapes=[pltpu.VMEM((HALO, e), F32)],
        compiler_params=_cparams("arbitrary"),
    )(h, g, w_in, *params, w_out, pn, w_gate, p_all, w_proj)
    proj, hn, o, h1, h2, gl, pp = outs
    return h2, (proj, hn, o, h1, gl, pp)


def _loss_head(h, target, g):
    s, d = h.shape
    ts = min(ROW_TILE, s)

    def body(h_ref, t_ref, g_ref, dh_ref, loss_ref, dg_ref):
        @pl.when(pl.program_id(0) == 0)
        def _():
            loss_ref[...] = jnp.zeros_like(loss_ref)
            dg_ref[...] = jnp.zeros_like(dg_ref)

        gain = g_ref[...]
        xh, r = _rms_stats(h_ref[...])
        err = xh * gain - t_ref[...]
        loss_ref[...] += jnp.full(loss_ref.shape, (0.5 / d) * jnp.sum(err * err), F32)
        dy = err * (1.0 / d)
        dg_ref[...] += jnp.sum(dy * xh, axis=0, keepdims=True)
        dh_ref[...] = _rms_bwd(dy, xh, r, gain)

    row = pl.BlockSpec((ts, d), lambda i: (i, 0))
    return pl.pallas_call(
        body, name="loss_head",
        grid=(s // ts,),
        in_specs=[row, row, pl.BlockSpec((1, d), lambda i: (0, 0))],
        out_specs=[row, pl.BlockSpec((1, 128), lambda i: (0, 0)), pl.BlockSpec((1, d), lambda i: (0, 0))],
        out_shape=[jax.ShapeDtypeStruct((s, d), F32), jax.ShapeDtypeStruct((1, 128), F32),
                   jax.ShapeDtypeStruct((1, d), F32)],
        compiler_params=_cparams("arbitrary"),
    )(h, target, g)


def _out_ple_bwd(dh2, gl, pp, h1, p_all, o, pn, wgate, wout, layer):
    s, d = dh2.shape
    e = o.shape[1]
    pdim = p_all.shape[2]
    ts = min(BWD_ROW_TILE, s)
    last = s // ts - 1

    def body(dh2_ref, gl_ref, pp_ref, h1_ref, p_ref, o_ref, pn_ref, wg_ref, wo_ref,
             dh1_ref, do_ref, dwp_ref, dwg_ref, dwo_ref, dpn_ref, awp, awg, awo):
        i = pl.program_id(0)

        @pl.when(i == 0)
        def _():
            awp[...] = jnp.zeros_like(awp)
            awg[...] = jnp.zeros_like(awg)
            awo[...] = jnp.zeros_like(awo)
            dpn_ref[...] = jnp.zeros_like(dpn_ref)

        dh2 = dh2_ref[...]
        gate = _sigmoid(gl_ref[...].astype(F32))
        dpp = (dh2 * gate).astype(BF16)
        dgl = (dh2 * pp_ref[...].astype(F32) * gate * (1.0 - gate)).astype(BF16)
        xh, r = _rms_stats(h1_ref[...])
        pn = pn_ref[...]
        awp[...] += _dot_tn(p_ref[...].astype(BF16), dpp)
        awg[...] += _dot_tn((xh * pn).astype(BF16), dgl)
        dr = _dot_nt(dgl, wg_ref[...])
        dpn_ref[...] += jnp.sum(dr * xh, axis=0, keepdims=True)
        dh1 = dh2 + _rms_bwd(dr, xh, r, pn)
        dh1_ref[...] = dh1
        dh1b = dh1.astype(BF16)
        do_ref[...] = _dot_nt(dh1b, wo_ref[...]).astype(BF16)
        awo[...] += _dot_tn(o_ref[...], dh1b)

        @pl.when(i == last)
        def _():
            dwp_ref[...] = awp[...].astype(BF16)
            dwg_ref[...] = awg[...].astype(BF16)
            dwo_ref[...] = awo[...].astype(BF16)

    row = lambda width: pl.BlockSpec((ts, width), lambda i: (i, 0))
    return pl.pallas_call(
        body, name="out_ple_bwd",
        grid=(s // ts,),
        in_specs=[row(d), row(d), row(d), row(d),
                  pl.BlockSpec((None, ts, pdim), lambda i: (layer, i, 0)),
                  row(e), _whole(1, d), _whole(d, d), _whole(e, d)],
        out_specs=[row(d), row(e), _whole(pdim, d), _whole(d, d), _whole(e, d), _whole(1, d)],
        out_shape=[jax.ShapeDtypeStruct((s, d), F32), jax.ShapeDtypeStruct((s, e), BF16),
                   jax.ShapeDtypeStruct((pdim, d), BF16), jax.ShapeDtypeStruct((d, d), BF16),
                   jax.ShapeDtypeStruct((e, d), BF16), jax.ShapeDtypeStruct((1, d), F32)],
        scratch_shapes=[pltpu.VMEM((pdim, d), F32), pltpu.VMEM((d, d), F32), pltpu.VMEM((e, d), F32)],
        compiler_params=_cparams("arbitrary"),
    )(dh2, gl, pp, h1, p_all, o, pn, wgate, wout)


def _mixer_bwd(do, proj, mixer, w_in, h, g, dh1, hn=None):
    s, d = h.shape
    nsplit, _, e = proj.shape
    gdim = e // N_POOL_GROUPS
    ts = min(LAYER_ROW_TILE, s)
    nt = s // ts
    is_conv = mixer[0] == "conv"
    params = mixer[1:]
    n_mix_out = 1 if is_conv else 2
    with_wgrad = hn is not None

    def body(*refs):
        refs = list(refs)
        take = lambda n: [refs.pop(0) for _ in range(n)]
        do_ref, p_ref, ph_ref = take(3)
        mix_refs = take(len(params))
        win_ref, h_ref, g_ref, dh1_ref = take(4)
        hn_ref, = take(1) if with_wgrad else [None]
        first_out, dh_ref, dg_ref = take(3)
        mix_out = take(n_mix_out)
        carry_ref, = take(1)
        dp_ref, acc_ref = take(2) if with_wgrad else (first_out, None)
        accumulators = refs + ([acc_ref] if with_wgrad else [])
        i = pl.program_id(0)
        tile = nt - 1 - i

        @pl.when(i == 0)
        def _():
            carry_ref[...] = jnp.zeros_like(carry_ref)
            dg_ref[...] = jnp.zeros_like(dg_ref)
            for ref in mix_out[-1:] + accumulators:
                ref[...] = jnp.zeros_like(ref)

        dof = do_ref[...].astype(F32)
        nxt = carry_ref[...]
        if is_conv:
            w_ref, = mix_refs
            dw_ref, = mix_out
            w0, w1, w2 = w_ref[0:1, :], w_ref[1:2, :], w_ref[2:3, :]
            b, c, v, z = [p_ref[k].astype(F32) for k in range(4)]
            u = c * v
            u_prev = jnp.where(tile == 0, 0.0, ph_ref[1].astype(F32) * ph_ref[2].astype(F32))
            u1, u2 = _conv_taps(u, u_prev)
            conv = w0 * u2 + w1 * u1 + w2 * u
            sig = _sigmoid(z)
            sz = z * sig
            dy = dof * sz
            dp_ref[3] = (dof * (b * conv) * (sig + sz * (1.0 - sig))).astype(BF16)
            dp_ref[0] = (dy * conv).astype(BF16)
            dconv = dy * b
            dw_ref[0:1, :] += jnp.sum(dconv * u2, axis=0, keepdims=True)
            dw_ref[1:2, :] += jnp.sum(dconv * u1, axis=0, keepdims=True)
            dw_ref[2:3, :] += jnp.sum(dconv * u, axis=0, keepdims=True)
            dcc = jnp.concatenate([dconv, nxt], axis=0)
            du = w2 * dconv + w1 * _shift_up(dcc, 1)[:ts] + w0 * _shift_up(dcc, 2)[:ts]
            carry_ref[...] = dconv[:HALO]
            dp_ref[1] = (du * v).astype(BF16)
            dp_ref[2] = (du * c).astype(BF16)
        else:
            wgrp_ref, sc_ref = mix_refs
            dwg_ref, dsc_ref = mix_out
            agrp, = refs
            u = p_ref[0].astype(F32)
            z = p_ref[1].astype(F32)
            u_prev = jnp.where(tile == 0, 0.0, ph_ref[0].astype(F32))
            uu = jnp.concatenate([u_prev, u], axis=0)
            sig = _sigmoid(z)
            sz = z * sig
            dm = dof * sz
            dsilu = dof * (sig + sz * (1.0 - sig))
            for gi, window in enumerate(POOL_WINDOWS):
                cols = slice(gi * gdim, (gi + 1) * gdim)
                w = wgrp_ref[gi]
                scale = sc_ref[:, cols]
                db = _window_mean_minus(uu[:, cols], tile * ts, window).astype(BF16)
                mr = _dot(db, w)
                dp_ref[1, :, cols] = (dsilu[:, cols] * (mr * scale)).astype(BF16)
                dmg = dm[:, cols]
                dmr = (dmg * scale).astype(BF16)
                agrp[gi] += _dot_tn(db, dmr)
                dsc_ref[:, cols] += jnp.sum(dmg * mr, axis=0, keepdims=True)
                dd = _dot_nt(dmr, w)
                ddq = dd * _inv_count(ts, tile * ts, window)
                acc = jnp.concatenate([ddq, nxt[:, cols]], axis=0)
                span = 1
                while span < window:
                    acc = acc + _shift_up(acc, span)
                    span *= 2
                carry_ref[:, cols] = ddq[:HALO]
                dp_ref[0, :, cols] = (acc[:ts] - dd).astype(BF16)

            @pl.when(i == nt - 1)
            def _():
                dwg_ref[...] = agrp[...].astype(BF16)

        dhn = _dot_nt(dp_ref[0], win_ref[:, 0:e])
        for k in range(1, nsplit):
            dhn += _dot_nt(dp_ref[k], win_ref[:, k * e:(k + 1) * e])
        xh, r = _rms_stats(h_ref[...])
        dg_ref[...] += jnp.sum(dhn * xh, axis=0, keepdims=True)
        dh_ref[...] = dh1_ref[...] + _rms_bwd(dhn, xh, r, g_ref[...])
        if with_wgrad:
            hn_tile = hn_ref[...]
            for k in range(nsplit):
                acc_ref[:, k * e:(k + 1) * e] += _dot_tn(hn_tile, dp_ref[k])

            @pl.when(i == nt - 1)
            def _():
                first_out[...] = acc_ref[...].astype(BF16)

    rev = lambda width: pl.BlockSpec((ts, width), lambda i: (nt - 1 - i, 0))
    halo_blocks = ts // HALO
    in_specs = [rev(e),
                pl.BlockSpec((nsplit, ts, e), lambda i: (0, nt - 1 - i, 0)),
                pl.BlockSpec((nsplit, HALO, e), lambda i: (0, jnp.maximum((nt - 1 - i) * halo_blocks - 1, 0), 0))]
    in_specs += [_whole(*a.shape) for a in params]
    in_specs += [_whole(d, nsplit * e), rev(d), _whole(1, d), rev(d)]
    scratch = [pltpu.VMEM((HALO, e), F32)]
    if with_wgrad:
        in_specs += [rev(d)]
        out_specs = [_whole(d, nsplit * e)]
        out_shape = [jax.ShapeDtypeStruct((d, nsplit * e), BF16)]
        scratch += [pltpu.VMEM((nsplit, ts, e), BF16), pltpu.VMEM((d, nsplit * e), F32)]
    else:
        out_specs = [pl.BlockSpec((nsplit, ts, e), lambda i: (0, nt - 1 - i, 0))]
        out_shape = [jax.ShapeDtypeStruct((nsplit, s, e), BF16)]
    out_specs += [rev(d), _whole(1, d)]
    out_shape += [jax.ShapeDtypeStruct((s, d), F32), jax.ShapeDtypeStruct((1, d), F32)]
    if is_conv:
        out_specs += [_whole(3, e)]
        out_shape += [jax.ShapeDtypeStruct((3, e), F32)]
    else:
        out_specs += [_whole(N_POOL_GROUPS, gdim, gdim), _whole(1, e)]
        out_shape += [jax.ShapeDtypeStruct((N_POOL_GROUPS, gdim, gdim), BF16), jax.ShapeDtypeStruct((1, e), F32)]
        scratch += [pltpu.VMEM((N_POOL_GROUPS, gdim, gdim), F32)]
    return pl.pallas_call(
        body, name="mixer_bwd",
        grid=(nt,),
        in_specs=in_specs, out_specs=out_specs, out_shape=out_shape, scratch_shapes=scratch,
        compiler_params=_cparams("arbitrary"),
    )(do, proj, proj, *params, w_in, h, g, dh1, *([hn] if with_wgrad else []))


def _proj_wgrad(hn, dproj):
    s, d = hn.shape
    nsplit, _, e = dproj.shape
    ts = min(WGRAD_ROW_TILE, s)
    last = s // ts - 1

    def body(hn_ref, dp_ref, dw_ref, acc):
        i = pl.program_id(1)

        @pl.when(i == 0)
        def _():
            acc[...] = jnp.zeros_like(acc)

        acc[...] += _dot_tn(hn_ref[...], dp_ref[...])

        @pl.when(i == last)
        def _():
            dw_ref[...] = acc[...].astype(BF16)

    return pl.pallas_call(
        body, name="proj_wgrad",
        grid=(nsplit, s // ts),
        in_specs=[pl.BlockSpec((ts, d), lambda k, i: (i, 0)),
                  pl.BlockSpec((None, ts, e), lambda k, i: (k, i, 0))],
        out_specs=pl.BlockSpec((d, e), lambda k, i: (0, k)),
        out_shape=jax.ShapeDtypeStruct((d, nsplit * e), BF16),
        scratch_shapes=[pltpu.VMEM((d, e), F32)],
        compiler_params=_cparams("parallel", "arbitrary"),
    )(hn, dproj)


def _forward_backward(xs, ps, target, full, conv_w, scale_w, norm_mix, ple_norm, final_norm, exchange):
    depth = len(full)
    row = lambda a, i: a[i][None, :]
    mixer_of = lambda i: ("conv", conv_w[i // 2]) if i % 2 == 0 else ("pool", full[i]["w_grp"], row(scale_w, i // 2))

    saved = []
    h = xs
    for i in range(depth):
        w = full[i]
        h_next, acts = _layer_fwd(h, row(norm_mix, i), w["w_in"], mixer_of(i), w["w_out"], row(ple_norm, i),
                                  w["gate"], ps, w["proj"], i)
        saved.append((h, *acts))
        h = h_next
    dh, loss_row, d_final = _loss_head(h, target, final_norm[None, :])

    d_norm, d_ple_norm, d_conv, d_scale, sent = [None] * depth, [None] * depth, [], [], [None] * depth
    for i in reversed(range(depth)):
        w = full[i]
        h_in, proj, hn, o, h1, gl, pp = saved[i]
        g = {}
        dh1, do, g["proj"], g["gate"], g["w_out"], d_ple_norm[i] = _out_ple_bwd(
            dh, gl, pp, h1, ps, o, row(ple_norm, i), w["gate"], w["w_out"], i)
        if i % 2 == 0:
            dproj, dh, d_norm[i], dw_conv = _mixer_bwd(do, proj, mixer_of(i), w["w_in"], h_in, row(norm_mix, i), dh1)
            d_conv.insert(0, dw_conv)
            g["w_in"] = _proj_wgrad(hn, dproj)
        else:
            g["w_in"], dh, d_norm[i], g["w_grp"], dw_scale = _mixer_bwd(
                do, proj, mixer_of(i), w["w_in"], h_in, row(norm_mix, i), dh1, hn=hn)
            d_scale.insert(0, dw_scale)
        sent[i] = exchange(i, g)
    return loss_row, dh, sent, (d_norm, d_ple_norm, d_final, d_conv, d_scale)


VMEM_SPEC = pl.BlockSpec(memory_space=pltpu.VMEM)

FLIPS = [(fx, fy, fc) for fx in (0, 1) for fy in (0, 1) for fc in (0, 1)][1:]
SHARD_AXIS = {"w_in": 1, "w_out": 0, "w_grp": 1, "gate": 0, "proj": 1}


def _my_place():
    return lax.axis_index("x"), lax.axis_index("y"), lax.axis_index("c")


def _position(place):
    x, y, c = place
    return 4 * x + 2 * y + c


def _flip(place, flips):
    return tuple(1 - v if f else v for v, f in zip(place, flips))


def _shard_of(ref, axis, pos, n):
    idx = [slice(None)] * len(ref.shape)
    idx[axis] = pl.ds(pl.multiple_of(pos * n, n), n)
    return ref.at[tuple(idx)]


def _sequencer_mesh():
    return plsc.ScalarSubcoreMesh(axis_name="sequencer", num_cores=1)


def _handshake(peers):
    barrier = pltpu.get_barrier_semaphore()
    for peer in peers:
        pl.semaphore_signal(barrier, inc=1, device_id=peer, device_id_type=MESH)
    pl.semaphore_wait(barrier, len(peers))


def _all_gather_layer(shards, collective_id):
    names = list(shards)
    nt = len(names)
    axes = [SHARD_AXIS[k] for k in names]
    widths = [shards[k].shape[SHARD_AXIS[k]] for k in names]

    def full_shape(k):
        shp = list(shards[k].shape)
        shp[SHARD_AXIS[k]] *= N_DEV
        return tuple(shp)

    def body(*refs):
        ins, outs = refs[:nt], refs[nt:2 * nt]
        send_sems, recv_sems, local_sem = refs[2 * nt:]
        me = _my_place()
        x, y, c = me
        sibling = (x, y, 1 - c)
        chips = [(1 - x, y), (x, 1 - y), (1 - x, 1 - y)]
        _handshake([sibling] + [(*chip, c) for chip in chips])

        def block(t, place):
            return _shard_of(outs[t], axes[t], _position(place), widths[t])

        def copy(t, k, place, to, src=None):
            return pltpu.make_async_remote_copy(
                src_ref=block(t, place) if src is None else src, dst_ref=block(t, place),
                send_sem=send_sems.at[k], recv_sem=recv_sems.at[k], device_id=to, device_id_type=MESH)

        mine = [pltpu.make_async_copy(ins[t], block(t, me), local_sem) for t in range(nt)]
        for cp in mine:
            cp.start()
        first = []
        for j, chip in enumerate(chips):
            first += [copy(t, 1 + j, me, (*chip, c), src=ins[t]) for t in range(nt)]
        first += [copy(t, 0, me, sibling, src=ins[t]) for t in range(nt)]
        for cp in first:
            cp.start()
        passed = []
        for j, chip in enumerate(chips):
            for t in range(nt):
                copy(t, 1 + j, (*chip, c), me).wait_recv()
            for t in range(nt):
                fwd = copy(t, 4 + j, (*chip, c), sibling)
                fwd.start()
                passed.append(fwd)
        for t in range(nt):
            copy(t, 0, sibling, me).wait_recv()
        for j, chip in enumerate(chips):
            for t in range(nt):
                copy(t, 4 + j, (*chip, 1 - c), me).wait_recv()
        for cp in first + passed:
            cp.wait_send()
        for cp in mine:
            cp.wait()

    outs = pl.kernel(
        body, name=f"all_gather_layer_{collective_id}",
        out_type=[jax.ShapeDtypeStruct(full_shape(k), shards[k].dtype) for k in names],
        mesh=_sequencer_mesh(),
        scratch_types=[pltpu.SemaphoreType.DMA((7,)), pltpu.SemaphoreType.DMA((7,)), pltpu.SemaphoreType.DMA],
        compiler_params=pltpu.CompilerParams(collective_id=collective_id),
    )(*[shards[k] for k in names])
    return dict(zip(names, outs))


def _exchange_layer(grads, collective_id):
    names = list(grads)
    nt = len(names)
    axes = [SHARD_AXIS[k] for k in names]
    widths = [grads[k].shape[SHARD_AXIS[k]] // N_DEV for k in names]

    def slot_shape(t):
        shp = list(grads[names[t]].shape)
        shp[axes[t]] = widths[t]
        return (N_DEV, *shp)

    def body(*refs):
        ins, outs = refs[:nt], refs[nt:2 * nt]
        send_sems, recv_sems, local_sem = refs[2 * nt:]
        me = _my_place()
        mine = _position(me)
        _handshake([_flip(me, flips) for flips in FLIPS])
        local = [pltpu.make_async_copy(_shard_of(ins[t], axes[t], mine, widths[t]), outs[t].at[mine], local_sem)
                 for t in range(nt)]
        for cp in local:
            cp.start()
        copies = []
        for k, flips in enumerate(FLIPS):
            peer = _flip(me, flips)
            for t in range(nt):
                cp = pltpu.make_async_remote_copy(
                    src_ref=_shard_of(ins[t], axes[t], _position(peer), widths[t]), dst_ref=outs[t].at[mine],
                    send_sem=send_sems.at[k], recv_sem=recv_sems.at[k], device_id=peer, device_id_type=MESH)
                cp.start()
                copies.append(cp)
        for cp in copies:
            cp.wait()
        for cp in local:
            cp.wait()

    outs = pl.kernel(
        body, name=f"exchange_layer_{collective_id}",
        out_type=[jax.ShapeDtypeStruct(slot_shape(t), BF16) for t in range(nt)],
        mesh=_sequencer_mesh(),
        scratch_types=[pltpu.SemaphoreType.DMA((7,)), pltpu.SemaphoreType.DMA((7,)), pltpu.SemaphoreType.DMA],
        compiler_params=pltpu.CompilerParams(collective_id=collective_id),
    )(*[grads[k] for k in names])
    return dict(zip(names, outs))


def _gather_rows(buf, reduce):
    r, c = buf.shape

    def body(in_ref, out_ref, *scratch):
        if reduce:
            all_ref, send_sems, recv_sems = scratch
        else:
            all_ref = out_ref
            send_sems, recv_sems = scratch
        me = _my_place()
        all_ref[_position(me)] = in_ref[...]
        copies = []
        for k, flips in enumerate(FLIPS):
            cp = pltpu.make_async_remote_copy(
                src_ref=in_ref, dst_ref=all_ref.at[_position(me)],
                send_sem=send_sems.at[k], recv_sem=recv_sems.at[k], device_id=_flip(me, flips), device_id_type=MESH)
            cp.start()
            copies.append(cp)
        for cp in copies:
            cp.wait()
        if reduce:
            total = all_ref[0]
            for j in range(1, N_DEV):
                total = total + all_ref[j]
            out_ref[...] = total

    return pl.pallas_call(
        body, name="sum_rows" if reduce else "gather_rows",
        in_specs=[VMEM_SPEC], out_specs=VMEM_SPEC,
        out_shape=jax.ShapeDtypeStruct((r, c) if reduce else (N_DEV, r, c), F32),
        scratch_shapes=([pltpu.VMEM((N_DEV, r, c), F32)] if reduce else [])
        + [pltpu.SemaphoreType.DMA((7,)), pltpu.SemaphoreType.DMA((7,))],
    )(buf)


def _adamw_math(w, g, m, v):
    m = ADAM_B1 * m + (1.0 - ADAM_B1) * g
    v = ADAM_B2 * v + (1.0 - ADAM_B2) * (g * g)
    m_hat = m / (1.0 - ADAM_B1 ** ADAM_STEP)
    v_hat = v / (1.0 - ADAM_B2 ** ADAM_STEP)
    delta = -ADAM_LR * (m_hat / (jnp.sqrt(v_hat) + ADAM_EPS) + ADAM_WD * w)
    return delta, m, v


def _adamw_pieces(pieces, w, m, v):
    shape = w.shape
    nl = shape[0]
    cols = shape[-1]
    rows = w.size // (nl * cols)
    tr = min(256 // nl, rows)
    flat3 = lambda a: a.reshape(nl, rows, cols)

    def body(*refs):
        p_refs = refs[:nl]
        w_ref, m_ref, v_ref, g_ref, d_ref, nm_ref, nv_ref = refs[nl:]
        for l in range(nl):
            g = p_refs[l][0].astype(F32)
            for j in range(1, N_DEV):
                g = g + p_refs[l][j].astype(F32)
            g_ref[l] = g
            d_ref[l], nm_ref[l], nv_ref[l] = _adamw_math(w_ref[l], g, m_ref[l], v_ref[l])

    blk = pl.BlockSpec((nl, tr, cols), lambda i: (0, i, 0))
    outs = pl.pallas_call(
        body, name="adamw_pieces",
        grid=(rows // tr,),
        in_specs=[pl.BlockSpec((N_DEV, tr, cols), lambda i: (0, i, 0))] * nl + [blk, blk, blk],
        out_specs=[blk] * 4,
        out_shape=[jax.ShapeDtypeStruct((nl, rows, cols), F32)] * 4,
        compiler_params=_cparams("parallel"),
    )(*[a.reshape(N_DEV, rows, cols) for a in pieces], flat3(w), flat3(m), flat3(v))
    return [a.reshape(shape) for a in outs]


def _adamw_small(g, w, m, v):
    shape = w.shape
    two = lambda a: a.reshape(-1, shape[-1])

    def body(g_ref, w_ref, m_ref, v_ref, d_ref, nm_ref, nv_ref):
        d_ref[...], nm_ref[...], nv_ref[...] = _adamw_math(w_ref[...], g_ref[...], m_ref[...], v_ref[...])

    outs = pl.pallas_call(
        body, name="adamw_small",
        in_specs=[VMEM_SPEC] * 4, out_specs=[VMEM_SPEC] * 3,
        out_shape=[jax.ShapeDtypeStruct(two(w).shape, F32)] * 3,
    )(two(g), two(w), two(m), two(v))
    return [a.reshape(shape) for a in outs]


WEIGHTS = ("norm_mix", "a_w_in", "a_w_conv", "a_w_out", "b_w_in", "b_w_grp", "b_scale", "b_w_out",
           "ple_norm", "ple_w_gate", "ple_w_proj", "final_norm")
SMALL_ROWS = 24
GATHER_ID = 0
EXCHANGE_ID = 4
LAST_EXCHANGE_ID = 8


def kernel(x, p, norm_mix, a_w_in, a_w_conv, a_w_out, b_w_in, b_w_grp, b_scale, b_w_out, ple_norm, ple_w_gate, ple_w_proj, final_norm, loss_target, m_norm_mix, m_a_w_in, m_a_w_conv, m_a_w_out, m_b_w_in, m_b_w_grp, m_b_scale, m_b_w_out, m_ple_norm, m_ple_w_gate, m_ple_w_proj, m_final_norm, v_norm_mix, v_a_w_in, v_a_w_conv, v_a_w_out, v_b_w_in, v_b_w_grp, v_b_scale, v_b_w_out, v_ple_norm, v_ple_w_gate, v_ple_w_proj, v_final_norm):
    wts = dict(norm_mix=norm_mix, a_w_in=a_w_in, a_w_conv=a_w_conv, a_w_out=a_w_out, b_w_in=b_w_in, b_w_grp=b_w_grp,
               b_scale=b_scale, b_w_out=b_w_out, ple_norm=ple_norm, ple_w_gate=ple_w_gate, ple_w_proj=ple_w_proj,
               final_norm=final_norm)
    mom = dict(norm_mix=m_norm_mix, a_w_in=m_a_w_in, a_w_conv=m_a_w_conv, a_w_out=m_a_w_out, b_w_in=m_b_w_in,
               b_w_grp=m_b_w_grp, b_scale=m_b_scale, b_w_out=m_b_w_out, ple_norm=m_ple_norm, ple_w_gate=m_ple_w_gate,
               ple_w_proj=m_ple_w_proj, final_norm=m_final_norm)
    var = dict(norm_mix=v_norm_mix, a_w_in=v_a_w_in, a_w_conv=v_a_w_conv, a_w_out=v_a_w_out, b_w_in=v_b_w_in,
               b_w_grp=v_b_w_grp, b_scale=v_b_scale, b_w_out=v_b_w_out, ple_norm=v_ple_norm, ple_w_gate=v_ple_w_gate,
               ple_w_proj=v_ple_w_proj, final_norm=v_final_norm)
    d = x.shape[2]
    depth = norm_mix.shape[0]
    n_a, n_b = a_w_conv.shape[0], b_scale.shape[0]
    cw = a_w_conv.shape[2]
    pos = _position(_my_place())

    def layer_matrices(i):
        j = i // 2
        mixer = {"w_in": ("a_w_in", j), "w_out": ("a_w_out", j)} if i % 2 == 0 else \
                {"w_in": ("b_w_in", j), "w_grp": ("b_w_grp", j), "w_out": ("b_w_out", j)}
        return {**mixer, "gate": ("ple_w_gate", i), "proj": ("ple_w_proj", i)}

    full = [_all_gather_layer({k: wts[name][idx].astype(BF16) for k, (name, idx) in layer_matrices(i).items()},
                              GATHER_ID + i) for i in range(depth)]
    vec_rows = jnp.concatenate([a_w_conv.reshape(-1, cw), b_scale], axis=0)
    vecs = _gather_rows(vec_rows, reduce=False)
    n_conv = 3 * n_a
    conv_w = vecs[:, :n_conv].transpose(1, 0, 2).reshape(n_a, 3, N_DEV * cw)
    scale_w = vecs[:, n_conv:].transpose(1, 0, 2).reshape(n_b, N_DEV * cw)

    def exchange(i, g):
        if i > 0:
            return _exchange_layer(g, EXCHANGE_ID + i)
        early = {k: a for k, a in g.items() if k != "w_in"}
        return {**_exchange_layer(early, EXCHANGE_ID), **_exchange_layer({"w_in": g["w_in"]}, LAST_EXCHANGE_ID)}

    loss_row, dx, sent, (d_norm, d_ple_norm, d_final, d_conv, d_scale) = _forward_backward(
        x[0], p[:, 0], loss_target[0], full, conv_w, scale_w, norm_mix, ple_norm, final_norm, exchange)
    pieces = {name: [None] * wts[name].shape[0] for name in WEIGHTS if wts[name].ndim >= 3 and name != "a_w_conv"}
    for i in range(depth):
        for k, (name, idx) in layer_matrices(i).items():
            pieces[name][idx] = sent[i][k]

    pad = lambda a: jnp.pad(a, ((0, 0), (0, d - a.shape[1])))
    small = jnp.concatenate(d_norm + d_ple_norm + [d_final] + d_conv + d_scale + [pad(loss_row)], axis=0)
    small = jnp.pad(small, ((0, SMALL_ROWS - small.shape[0]), (0, 0)))
    total = _gather_rows(small, reduce=True)
    o = 0
    gsum = {}
    gsum["norm_mix"] = total[o:o + depth]; o += depth
    gsum["ple_norm"] = total[o:o + depth]; o += depth
    gsum["final_norm"] = total[o]; o += 1
    conv_full = total[o:o + n_conv].reshape(n_a, 3, d); o += n_conv
    scale_full = total[o:o + n_b]; o += n_b
    loss = total[o, 0]
    gsum["a_w_conv"] = lax.dynamic_slice_in_dim(conv_full, pos * cw, cw, axis=2)
    gsum["b_scale"] = lax.dynamic_slice_in_dim(scale_full, pos * cw, cw, axis=1)

    grad, delta, new_m, new_v = {}, {}, {}, {}
    for k in sorted(WEIGHTS, key=lambda name: name == "a_w_in"):
        if k in pieces:
            grad[k], delta[k], new_m[k], new_v[k] = _adamw_pieces(pieces[k], wts[k], mom[k], var[k])
        else:
            grad[k] = gsum[k]
            delta[k], new_m[k], new_v[k] = _adamw_small(gsum[k], wts[k], mom[k], var[k])
    return (loss, dx[None], *[grad[k] for k in WEIGHTS], *[delta[k] for k in WEIGHTS],
            *[new_m[k] for k in WEIGHTS], *[new_v[k] for k in WEIGHTS])
```

```python
import jax
import jax.numpy as jnp
from jax import lax
from jax.experimental import pallas as pl
from jax.experimental.pallas import tpu as pltpu
from jax.experimental.pallas import tpu_sc as plsc

F32 = jnp.float32
BF16 = jnp.bfloat16
MESH = pl.DeviceIdType.MESH

RMS_EPS = 1e-6
POOL_WINDOWS = (2, 4, 8, 16)
N_POOL_GROUPS = len(POOL_WINDOWS)
ADAM_LR = 0.001
ADAM_B1 = 0.9
ADAM_B2 = 0.999
ADAM_EPS = 1e-08
ADAM_WD = 0.01
ADAM_STEP = 10
N_DEV = 8

HALO = 16
ROW_TILE = 512
LAYER_ROW_TILE = 256
BWD_ROW_TILE = 512
WGRAD_ROW_TILE = 1024
VMEM_LIMIT = 56 * 1024 * 1024


def _cparams(*sem):
    return pltpu.CompilerParams(dimension_semantics=sem, vmem_limit_bytes=VMEM_LIMIT)


def _dot(a, b):
    return jnp.dot(a, b, preferred_element_type=F32)


def _dot_nt(a, b):
    return lax.dot_general(a, b, (((1,), (1,)), ((), ())), preferred_element_type=F32)


def _dot_tn(a, b):
    return lax.dot_general(a, b, (((0,), (0,)), ((), ())), preferred_element_type=F32)


def _rms_stats(x):
    r = lax.rsqrt(jnp.mean(x * x, axis=-1, keepdims=True) + RMS_EPS)
    return x * r, r


def _rms_bwd(dy, xh, r, g):
    a = dy * g
    return r * (a - xh * jnp.mean(a * xh, axis=-1, keepdims=True))


def _sigmoid(x):
    return 1.0 / (1.0 + jnp.exp(-x))


def _shift_down(x, k):
    return pltpu.roll(x, k, 0)


def _shift_up(x, k):
    return pltpu.roll(x, x.shape[0] - k, 0)


def _conv_taps(u, u_prev):
    uu = jnp.concatenate([u_prev, u], axis=0)
    return _shift_down(uu, 1)[HALO:], _shift_down(uu, 2)[HALO:]


def _window_mean_minus(uu, row0, window):
    acc = uu
    span = 1
    while span < window:
        acc = acc + _shift_down(acc, span)
        span *= 2
    return acc[HALO:] * _inv_count(uu.shape[0] - HALO, row0, window) - uu[HALO:]


def _inv_count(rows, row0, window):
    t = row0 + lax.broadcasted_iota(jnp.int32, (rows, 1), 0)
    return 1.0 / jnp.minimum(t + 1, window).astype(F32)


def _whole(*shape):
    return pl.BlockSpec(shape, lambda i: (0,) * len(shape), pipeline_mode=pl.Buffered(1))


def _layer_fwd(h, g, w_in, mixer, w_out, pn, w_gate, p_all, w_proj, layer):
    s, d = h.shape
    n = w_in.shape[1]
    e = w_out.shape[0]
    nsplit = n // e
    gdim = e // N_POOL_GROUPS
    pdim = p_all.shape[2]
    ts = min(LAYER_ROW_TILE, s)
    is_conv = mixer[0] == "conv"
    params = mixer[1:]

    def body(*refs):
        h_ref, g_ref, win_ref = refs[:3]
        mix_refs = refs[3:3 + len(params)]
        wo_ref, pn_ref, wg_ref, p_ref, wp_ref = refs[3 + len(params):8 + len(params)]
        proj_ref, hn_ref, o_ref, h1_ref, h2_ref, gl_ref, pp_ref, carry_ref = refs[8 + len(params):]
        i = pl.program_id(0)

        @pl.when(i == 0)
        def _():
            carry_ref[...] = jnp.zeros_like(carry_ref)

        x = h_ref[...]
        xh, _ = _rms_stats(x)
        hn = (xh * g_ref[...]).astype(BF16)
        hn_ref[...] = hn
        parts = []
        for k in range(nsplit):
            part = _dot(hn, win_ref[:, k * e:(k + 1) * e])
            proj_ref[k] = part.astype(BF16)
            parts.append(part)
        prev = carry_ref[...]
        if is_conv:
            b, c, v, z = parts
            w_ref, = mix_refs
            u = c * v
            u1, u2 = _conv_taps(u, prev)
            mixed = b * (w_ref[0:1, :] * u2 + w_ref[1:2, :] * u1 + w_ref[2:3, :] * u)
        else:
            u, z = parts
            wgrp_ref, sc_ref = mix_refs
            uu = jnp.concatenate([prev, u], axis=0)
            cols = []
            for gi, window in enumerate(POOL_WINDOWS):
                dg = _window_mean_minus(uu[:, gi * gdim:(gi + 1) * gdim], i * ts, window)
                cols.append(_dot(dg.astype(BF16), wgrp_ref[gi]))
            mixed = jnp.concatenate(cols, axis=1) * sc_ref[...]
        carry_ref[...] = u[ts - HALO:]
        o = ((z * _sigmoid(z)) * mixed).astype(BF16)
        o_ref[...] = o
        h1 = x + _dot(o, wo_ref[...])
        h1_ref[...] = h1
        xh1, _ = _rms_stats(h1)
        gl = _dot((xh1 * pn_ref[...]).astype(BF16), wg_ref[...])
        pp = _dot(p_ref[...].astype(BF16), wp_ref[...])
        gl_ref[...] = gl.astype(BF16)
        pp_ref[...] = pp.astype(BF16)
        h2_ref[...] = h1 + _sigmoid(gl) * pp

    row = lambda width: pl.BlockSpec((ts, width), lambda i: (i, 0))
    mix_specs = [_whole(*a.shape) for a in params]
    outs = pl.pallas_call(
        body, name="layer_fwd",
        grid=(s // ts,),
        in_specs=[row(d), _whole(1, d), _whole(d, n)] + mix_specs
        + [_whole(e, d), _whole(1, d), _whole(d, d),
           pl.BlockSpec((None, ts, pdim), lambda i: (layer, i, 0)), _whole(pdim, d)],
        out_specs=[pl.BlockSpec((nsplit, ts, e), lambda i: (0, i, 0)),
                   row(d), row(e), row(d), row(d), row(d), row(d)],
        out_shape=[jax.ShapeDtypeStruct((nsplit, s, e), BF16), jax.ShapeDtypeStruct((s, d), BF16),
                   jax.ShapeDtypeStruct((s, e), BF16), jax.ShapeDtypeStruct((s, d), F32),
                   jax.ShapeDtypeStruct((s, d), F32), jax.ShapeDtypeStruct((s, d), BF16),
                   jax.ShapeDtypeStruct((s, d), BF16)],
        scratch_shapes=[pltpu.VMEM((HALO, e), F32)],
        compiler_params=_cparams("arbitrary"),
    )(h, g, w_in, *params, w_out, pn, w_gate, p_all, w_proj)
    proj, hn, o, h1, h2, gl, pp = outs
    return h2, (proj, hn, o, h1, gl, pp)


def _loss_head(h, target, g):
    s, d = h.shape
    ts = min(ROW_TILE, s)

    def body(h_ref, t_ref, g_ref, dh_ref, loss_ref, dg_ref):
        @pl.when(pl.program_id(0) == 0)
        def _():
            loss_ref[...] = jnp.zeros_like(loss_ref)
            dg_ref[...] = jnp.zeros_like(dg_ref)

        gain = g_ref[...]
        xh, r = _rms_stats(h_ref[...])
        err = xh * gain - t_ref[...]
        loss_ref[...] += jnp.full(loss_ref.shape, (0.5 / d) * jnp.sum(err * err), F32)
        dy = err * (1.0 / d)
        dg_ref[...] += jnp.sum(dy * xh, axis=0, keepdims=True)
        dh_ref[...] = _rms_bwd(dy, xh, r, gain)

    row = pl.BlockSpec((ts, d), lambda i: (i, 0))
    return pl.pallas_call(
        body, name="loss_head",
        grid=(s // ts,),
        in_specs=[row, row, pl.BlockSpec((1, d), lambda i: (0, 0))],
        out_specs=[row, pl.BlockSpec((1, 128), lambda i: (0, 0)), pl.BlockSpec((1, d), lambda i: (0, 0))],
        out_shape=[jax.ShapeDtypeStruct((s, d), F32), jax.ShapeDtypeStruct((1, 128), F32),
                   jax.ShapeDtypeStruct((1, d), F32)],
        compiler_params=_cparams("arbitrary"),
    )(h, target, g)


def _out_ple_bwd(dh2, gl, pp, h1, p_all, o, pn, wgate, wout, layer, after):
    s, d = dh2.shape
    e = o.shape[1]
    pdim = p_all.shape[2]
    ts = min(BWD_ROW_TILE, s)
    last = s // ts - 1

    def body(dh2_ref, gl_ref, pp_ref, h1_ref, p_ref, o_ref, pn_ref, wg_ref, wo_ref, after_ref,
             dh1_ref, do_ref, dwp_ref, dwg_ref, dwo_ref, dpn_ref, awp, awg, awo):
        i = pl.program_id(0)

        @pl.when(i == 0)
        def _():
            awp[...] = jnp.zeros_like(awp)
            awg[...] = jnp.zeros_like(awg)
            awo[...] = jnp.zeros_like(awo)
            dpn_ref[...] = jnp.zeros_like(dpn_ref)

        dh2 = dh2_ref[...]
        gate = _sigmoid(gl_ref[...].astype(F32))
        dpp = (dh2 * gate).astype(BF16)
        dgl = (dh2 * pp_ref[...].astype(F32) * gate * (1.0 - gate)).astype(BF16)
        xh, r = _rms_stats(h1_ref[...])
        pn = pn_ref[...]
        awp[...] += _dot_tn(p_ref[...].astype(BF16), dpp)
        awg[...] += _dot_tn((xh * pn).astype(BF16), dgl)
        dr = _dot_nt(dgl, wg_ref[...])
        dpn_ref[...] += jnp.sum(dr * xh, axis=0, keepdims=True)
        dh1 = dh2 + _rms_bwd(dr, xh, r, pn)
        dh1_ref[...] = dh1
        dh1b = dh1.astype(BF16)
        do_ref[...] = _dot_nt(dh1b, wo_ref[...]).astype(BF16)
        awo[...] += _dot_tn(o_ref[...], dh1b)

        @pl.when(i == last)
        def _():
            dwp_ref[...] = awp[...].astype(BF16)
            dwg_ref[...] = awg[...].astype(BF16)
            dwo_ref[...] = awo[...].astype(BF16)

    row = lambda width: pl.BlockSpec((ts, width), lambda i: (i, 0))
    return pl.pallas_call(
        body, name="out_ple_bwd",
        grid=(s // ts,),
        in_specs=[row(d), row(d), row(d), row(d),
                  pl.BlockSpec((None, ts, pdim), lambda i: (layer, i, 0)),
                  row(e), _whole(1, d), _whole(d, d), _whole(e, d), pl.BlockSpec(memory_space=pl.ANY)],
        out_specs=[row(d), row(e), _whole(pdim, d), _whole(d, d), _whole(e, d), _whole(1, d)],
        out_shape=[jax.ShapeDtypeStruct((s, d), F32), jax.ShapeDtypeStruct((s, e), BF16),
                   jax.ShapeDtypeStruct((pdim, d), BF16), jax.ShapeDtypeStruct((d, d), BF16),
                   jax.ShapeDtypeStruct((e, d), BF16), jax.ShapeDtypeStruct((1, d), F32)],
        scratch_shapes=[pltpu.VMEM((pdim, d), F32), pltpu.VMEM((d, d), F32), pltpu.VMEM((e, d), F32)],
        compiler_params=_cparams("arbitrary"),
    )(dh2, gl, pp, h1, p_all, o, pn, wgate, wout, after)


def _mixer_bwd(do, proj, mixer, w_in, h, g, dh1, hn=None):
    s, d = h.shape
    nsplit, _, e = proj.shape
    gdim = e // N_POOL_GROUPS
    ts = min(LAYER_ROW_TILE, s)
    nt = s // ts
    is_conv = mixer[0] == "conv"
    params = mixer[1:]
    n_mix_out = 1 if is_conv else 2
    with_wgrad = hn is not None

    def body(*refs):
        refs = list(refs)
        take = lambda n: [refs.pop(0) for _ in range(n)]
        do_ref, p_ref, ph_ref = take(3)
        mix_refs = take(len(params))
        win_ref, h_ref, g_ref, dh1_ref = take(4)
        hn_ref, = take(1) if with_wgrad else [None]
        first_out, dh_ref, dg_ref = take(3)
        mix_out = take(n_mix_out)
        carry_ref, = take(1)
        dp_ref, acc_ref = take(2) if with_wgrad else (first_out, None)
        accumulators = refs + ([acc_ref] if with_wgrad else [])
        i = pl.program_id(0)
        tile = nt - 1 - i

        @pl.when(i == 0)
        def _():
            carry_ref[...] = jnp.zeros_like(carry_ref)
            dg_ref[...] = jnp.zeros_like(dg_ref)
            for ref in mix_out[-1:] + accumulators:
                ref[...] = jnp.zeros_like(ref)

        dof = do_ref[...].astype(F32)
        nxt = carry_ref[...]
        if is_conv:
            w_ref, = mix_refs
            dw_ref, = mix_out
            w0, w1, w2 = w_ref[0:1, :], w_ref[1:2, :], w_ref[2:3, :]
            b, c, v, z = [p_ref[k].astype(F32) for k in range(4)]
            u = c * v
            u_prev = jnp.where(tile == 0, 0.0, ph_ref[1].astype(F32) * ph_ref[2].astype(F32))
            u1, u2 = _conv_taps(u, u_prev)
            conv = w0 * u2 + w1 * u1 + w2 * u
            sig = _sigmoid(z)
            sz = z * sig
            dy = dof * sz
            dp_ref[3] = (dof * (b * conv) * (sig + sz * (1.0 - sig))).astype(BF16)
            dp_ref[0] = (dy * conv).astype(BF16)
            dconv = dy * b
            dw_ref[0:1, :] += jnp.sum(dconv * u2, axis=0, keepdims=True)
            dw_ref[1:2, :] += jnp.sum(dconv * u1, axis=0, keepdims=True)
            dw_ref[2:3, :] += jnp.sum(dconv * u, axis=0, keepdims=True)
            dcc = jnp.concatenate([dconv, nxt], axis=0)
            du = w2 * dconv + w1 * _shift_up(dcc, 1)[:ts] + w0 * _shift_up(dcc, 2)[:ts]
            carry_ref[...] = dconv[:HALO]
            dp_ref[1] = (du * v).astype(BF16)
            dp_ref[2] = (du * c).astype(BF16)
        else:
            wgrp_ref, sc_ref = mix_refs
            dwg_ref, dsc_ref = mix_out
            agrp, = refs
            u = p_ref[0].astype(F32)
            z = p_ref[1].astype(F32)
            u_prev = jnp.where(tile == 0, 0.0, ph_ref[0].astype(F32))
            uu = jnp.concatenate([u_prev, u], axis=0)
            sig = _sigmoid(z)
            sz = z * sig
            dm = dof * sz
            dsilu = dof * (sig + sz * (1.0 - sig))
            for gi, window in enumerate(POOL_WINDOWS):
                cols = slice(gi * gdim, (gi + 1) * gdim)
                w = wgrp_ref[gi]
                scale = sc_ref[:, cols]
                db = _window_mean_minus(uu[:, cols], tile * ts, window).astype(BF16)
                mr = _dot(db, w)
                dp_ref[1, :, cols] = (dsilu[:, cols] * (mr * scale)).astype(BF16)
                dmg = dm[:, cols]
                dmr = (dmg * scale).astype(BF16)
                agrp[gi] += _dot_tn(db, dmr)
                dsc_ref[:, cols] += jnp.sum(dmg * mr, axis=0, keepdims=True)
                dd = _dot_nt(dmr, w)
                ddq = dd * _inv_count(ts, tile * ts, window)
                acc = jnp.concatenate([ddq, nxt[:, cols]], axis=0)
                span = 1
                while span < window:
                    acc = acc + _shift_up(acc, span)
                    span *= 2
                carry_ref[:, cols] = ddq[:HALO]
                dp_ref[0, :, cols] = (acc[:ts] - dd).astype(BF16)

            @pl.when(i == nt - 1)
            def _():
                dwg_ref[...] = agrp[...].astype(BF16)

        dhn = _dot_nt(dp_ref[0], win_ref[:, 0:e])
        for k in range(1, nsplit):
            dhn += _dot_nt(dp_ref[k], win_ref[:, k * e:(k + 1) * e])
        xh, r = _rms_stats(h_ref[...])
        dg_ref[...] += jnp.sum(dhn * xh, axis=0, keepdims=True)
        dh_ref[...] = dh1_ref[...] + _rms_bwd(dhn, xh, r, g_ref[...])
        if with_wgrad:
            hn_tile = hn_ref[...]
            for k in range(nsplit):
                acc_ref[:, k * e:(k + 1) * e] += _dot_tn(hn_tile, dp_ref[k])

            @pl.when(i == nt - 1)
            def _():
                first_out[...] = acc_ref[...].astype(BF16)

    rev = lambda width: pl.BlockSpec((ts, width), lambda i: (nt - 1 - i, 0))
    halo_blocks = ts // HALO
    in_specs = [rev(e),
                pl.BlockSpec((nsplit, ts, e), lambda i: (0, nt - 1 - i, 0)),
                pl.BlockSpec((nsplit, HALO, e), lambda i: (0, jnp.maximum((nt - 1 - i) * halo_blocks - 1, 0), 0))]
    in_specs += [_whole(*a.shape) for a in params]
    in_specs += [_whole(d, nsplit * e), rev(d), _whole(1, d), rev(d)]
    scratch = [pltpu.VMEM((HALO, e), F32)]
    if with_wgrad:
        in_specs += [rev(d)]
        out_specs = [_whole(d, nsplit * e)]
        out_shape = [jax.ShapeDtypeStruct((d, nsplit * e), BF16)]
        scratch += [pltpu.VMEM((nsplit, ts, e), BF16), pltpu.VMEM((d, nsplit * e), F32)]
    else:
        out_specs = [pl.BlockSpec((nsplit, ts, e), lambda i: (0, nt - 1 - i, 0))]
        out_shape = [jax.ShapeDtypeStruct((nsplit, s, e), BF16)]
    out_specs += [rev(d), _whole(1, d)]
    out_shape += [jax.ShapeDtypeStruct((s, d), F32), jax.ShapeDtypeStruct((1, d), F32)]
    if is_conv:
        out_specs += [_whole(3, e)]
        out_shape += [jax.ShapeDtypeStruct((3, e), F32)]
    else:
        out_specs += [_whole(N_POOL_GROUPS, gdim, gdim), _whole(1, e)]
        out_shape += [jax.ShapeDtypeStruct((N_POOL_GROUPS, gdim, gdim), BF16), jax.ShapeDtypeStruct((1, e), F32)]
        scratch += [pltpu.VMEM((N_POOL_GROUPS, gdim, gdim), F32)]
    return pl.pallas_call(
        body, name="mixer_bwd",
        grid=(nt,),
        in_specs=in_specs, out_specs=out_specs, out_shape=out_shape, scratch_shapes=scratch,
        compiler_params=_cparams("arbitrary"),
    )(do, proj, proj, *params, w_in, h, g, dh1, *([hn] if with_wgrad else []))


def _proj_wgrad(hn, dproj):
    s, d = hn.shape
    nsplit, _, e = dproj.shape
    ts = min(WGRAD_ROW_TILE, s)
    last = s // ts - 1

    def body(hn_ref, dp_ref, dw_ref, acc):
        i = pl.program_id(1)

        @pl.when(i == 0)
        def _():
            acc[...] = jnp.zeros_like(acc)

        acc[...] += _dot_tn(hn_ref[...], dp_ref[...])

        @pl.when(i == last)
        def _():
            dw_ref[...] = acc[...].astype(BF16)

    return pl.pallas_call(
        body, name="proj_wgrad",
        grid=(nsplit, s // ts),
        in_specs=[pl.BlockSpec((ts, d), lambda k, i: (i, 0)),
                  pl.BlockSpec((None, ts, e), lambda k, i: (k, i, 0))],
        out_specs=pl.BlockSpec((d, e), lambda k, i: (0, k)),
        out_shape=jax.ShapeDtypeStruct((d, nsplit * e), BF16),
        scratch_shapes=[pltpu.VMEM((d, e), F32)],
        compiler_params=_cparams("parallel", "arbitrary"),
    )(hn, dproj)


def _forward_backward(xs, ps, target, full, conv_w, scale_w, norm_mix, ple_norm, final_norm, exchange):
    depth = len(full)
    row = lambda a, i: a[i][None, :]
    mixer_of = lambda i: ("conv", conv_w[i // 2]) if i % 2 == 0 else ("pool", full[i]["w_grp"], row(scale_w, i // 2))

    saved = []
    h = xs
    for i in range(depth):
        w = full[i]
        h_next, acts = _layer_fwd(h, row(norm_mix, i), w["w_in"], mixer_of(i), w["w_out"], row(ple_norm, i),
                                  w["gate"], ps, w["proj"], i)
        saved.append((h, *acts))
        h = h_next
    dh, loss_row, d_final = _loss_head(h, target, final_norm[None, :])

    d_norm, d_ple_norm, d_conv, d_scale, sent = [None] * depth, [None] * depth, [], [], [None] * depth
    after = d_final
    for i in reversed(range(depth)):
        w = full[i]
        h_in, proj, hn, o, h1, gl, pp = saved[i]
        g = {}
        dh1, do, g["proj"], g["gate"], g["w_out"], d_ple_norm[i] = _out_ple_bwd(
            dh, gl, pp, h1, ps, o, row(ple_norm, i), w["gate"], w["w_out"], i, after)
        if i % 2 == 0:
            dproj, dh, d_norm[i], dw_conv = _mixer_bwd(do, proj, mixer_of(i), w["w_in"], h_in, row(norm_mix, i), dh1)
            d_conv.insert(0, dw_conv)
            g["w_in"] = _proj_wgrad(hn, dproj)
        else:
            g["w_in"], dh, d_norm[i], g["w_grp"], dw_scale = _mixer_bwd(
                do, proj, mixer_of(i), w["w_in"], h_in, row(norm_mix, i), dh1, hn=hn)
            d_scale.insert(0, dw_scale)
        after = g["w_in"]
        sent[i] = exchange(i, g)
    return loss_row, dh, sent, (d_norm, d_ple_norm, d_final, d_conv, d_scale)


VMEM_SPEC = pl.BlockSpec(memory_space=pltpu.VMEM)

FLIPS = [(fx, fy, fc) for fx in (0, 1) for fy in (0, 1) for fc in (0, 1)][1:]
SHARD_AXIS = {"w_in": 1, "w_out": 0, "w_grp": 1, "gate": 0, "proj": 1}


def _my_place():
    return lax.axis_index("x"), lax.axis_index("y"), lax.axis_index("c")


def _position(place):
    x, y, c = place
    return 4 * x + 2 * y + c


def _flip(place, flips):
    return tuple(1 - v if f else v for v, f in zip(place, flips))


def _shard_of(ref, axis, pos, n):
    idx = [slice(None)] * len(ref.shape)
    idx[axis] = pl.ds(pl.multiple_of(pos * n, n), n)
    return ref.at[tuple(idx)]


def _sequencer_mesh():
    return plsc.ScalarSubcoreMesh(axis_name="sequencer", num_cores=1)


def _handshake(peers):
    barrier = pltpu.get_barrier_semaphore()
    for peer in peers:
        pl.semaphore_signal(barrier, inc=1, device_id=peer, device_id_type=MESH)
    pl.semaphore_wait(barrier, len(peers))


def _all_gather_layer(shards, collective_id):
    names = list(shards)
    nt = len(names)
    axes = [SHARD_AXIS[k] for k in names]
    widths = [shards[k].shape[SHARD_AXIS[k]] for k in names]

    def full_shape(k):
        shp = list(shards[k].shape)
        shp[SHARD_AXIS[k]] *= N_DEV
        return tuple(shp)

    def body(*refs):
        ins, outs = refs[:nt], refs[nt:2 * nt]
        send_sems, recv_sems, local_sem = refs[2 * nt:]
        me = _my_place()
        x, y, c = me
        sibling = (x, y, 1 - c)
        chips = [(1 - x, y), (x, 1 - y), (1 - x, 1 - y)]
        _handshake([sibling] + [(*chip, c) for chip in chips])

        def block(t, place):
            return _shard_of(outs[t], axes[t], _position(place), widths[t])

        def copy(t, k, place, to, src=None):
            return pltpu.make_async_remote_copy(
                src_ref=block(t, place) if src is None else src, dst_ref=block(t, place),
                send_sem=send_sems.at[k], recv_sem=recv_sems.at[k], device_id=to, device_id_type=MESH)

        mine = [pltpu.make_async_copy(ins[t], block(t, me), local_sem) for t in range(nt)]
        for cp in mine:
            cp.start()
        first = []
        for j, chip in enumerate(chips):
            first += [copy(t, 1 + j, me, (*chip, c), src=ins[t]) for t in range(nt)]
        first += [copy(t, 0, me, sibling, src=ins[t]) for t in range(nt)]
        for cp in first:
            cp.start()
        passed = []
        for j, chip in enumerate(chips):
            for t in range(nt):
                copy(t, 1 + j, (*chip, c), me).wait_recv()
            for t in range(nt):
                fwd = copy(t, 4 + j, (*chip, c), sibling)
                fwd.start()
                passed.append(fwd)
        for t in range(nt):
            copy(t, 0, sibling, me).wait_recv()
        for j, chip in enumerate(chips):
            for t in range(nt):
                copy(t, 4 + j, (*chip, 1 - c), me).wait_recv()
        for cp in first + passed:
            cp.wait_send()
        for cp in mine:
            cp.wait()

    outs = pl.kernel(
        body, name=f"all_gather_layer_{collective_id}",
        out_type=[jax.ShapeDtypeStruct(full_shape(k), shards[k].dtype) for k in names],
        mesh=_sequencer_mesh(),
        scratch_types=[pltpu.SemaphoreType.DMA((7,)), pltpu.SemaphoreType.DMA((7,)), pltpu.SemaphoreType.DMA],
        compiler_params=pltpu.CompilerParams(collective_id=collective_id),
    )(*[shards[k] for k in names])
    return dict(zip(names, outs))


def _exchange_layer(grads, collective_id):
    names = list(grads)
    nt = len(names)
    axes = [SHARD_AXIS[k] for k in names]
    widths = [grads[k].shape[SHARD_AXIS[k]] // N_DEV for k in names]

    def slot_shape(t):
        shp = list(grads[names[t]].shape)
        shp[axes[t]] = widths[t]
        return (N_DEV, *shp)

    def body(*refs):
        ins, outs = refs[:nt], refs[nt:2 * nt]
        send_sems, recv_sems, local_sem = refs[2 * nt:]
        me = _my_place()
        mine = _position(me)
        _handshake([_flip(me, flips) for flips in FLIPS])
        local = [pltpu.make_async_copy(_shard_of(ins[t], axes[t], mine, widths[t]), outs[t].at[mine], local_sem)
                 for t in range(nt)]
        for cp in local:
            cp.start()
        copies = []
        for k, flips in enumerate(FLIPS):
            peer = _flip(me, flips)
            for t in range(nt):
                cp = pltpu.make_async_remote_copy(
                    src_ref=_shard_of(ins[t], axes[t], _position(peer), widths[t]), dst_ref=outs[t].at[mine],
                    send_sem=send_sems.at[k], recv_sem=recv_sems.at[k], device_id=peer, device_id_type=MESH)
                cp.start()
                copies.append(cp)
        for cp in copies:
            cp.wait()
        for cp in local:
            cp.wait()

    outs = pl.kernel(
        body, name=f"exchange_layer_{collective_id}",
        out_type=[jax.ShapeDtypeStruct(slot_shape(t), BF16) for t in range(nt)],
        mesh=_sequencer_mesh(),
        scratch_types=[pltpu.SemaphoreType.DMA((7,)), pltpu.SemaphoreType.DMA((7,)), pltpu.SemaphoreType.DMA],
        compiler_params=pltpu.CompilerParams(collective_id=collective_id),
    )(*[grads[k] for k in names])
    return dict(zip(names, outs))


def _gather_rows(buf, reduce):
    r, c = buf.shape

    def body(in_ref, out_ref, *scratch):
        if reduce:
            all_ref, send_sems, recv_sems = scratch
        else:
            all_ref = out_ref
            send_sems, recv_sems = scratch
        me = _my_place()
        all_ref[_position(me)] = in_ref[...]
        copies = []
        for k, flips in enumerate(FLIPS):
            cp = pltpu.make_async_remote_copy(
                src_ref=in_ref, dst_ref=all_ref.at[_position(me)],
                send_sem=send_sems.at[k], recv_sem=recv_sems.at[k], device_id=_flip(me, flips), device_id_type=MESH)
            cp.start()
            copies.append(cp)
        for cp in copies:
            cp.wait()
        if reduce:
            total = all_ref[0]
            for j in range(1, N_DEV):
                total = total + all_ref[j]
            out_ref[...] = total

    return pl.pallas_call(
        body, name="sum_rows" if reduce else "gather_rows",
        in_specs=[VMEM_SPEC], out_specs=VMEM_SPEC,
        out_shape=jax.ShapeDtypeStruct((r, c) if reduce else (N_DEV, r, c), F32),
        scratch_shapes=([pltpu.VMEM((N_DEV, r, c), F32)] if reduce else [])
        + [pltpu.SemaphoreType.DMA((7,)), pltpu.SemaphoreType.DMA((7,))],
    )(buf)


def _adamw_math(w, g, m, v):
    m = ADAM_B1 * m + (1.0 - ADAM_B1) * g
    v = ADAM_B2 * v + (1.0 - ADAM_B2) * (g * g)
    m_hat = m / (1.0 - ADAM_B1 ** ADAM_STEP)
    v_hat = v / (1.0 - ADAM_B2 ** ADAM_STEP)
    delta = -ADAM_LR * (m_hat / (jnp.sqrt(v_hat) + ADAM_EPS) + ADAM_WD * w)
    return delta, m, v


def _run_behind(x, token):
    def body(x_ref, token_ref, out_ref):
        out_ref[...] = jnp.zeros_like(out_ref)

    any_spec = pl.BlockSpec(memory_space=pl.ANY)
    return pl.pallas_call(
        body, name="run_behind",
        in_specs=[any_spec, any_spec], out_specs=VMEM_SPEC,
        out_shape=jax.ShapeDtypeStruct((8, 128), F32),
    )(x, token)


def _adamw_pieces(pieces, w, m, v, after):
    shape = w.shape
    nl = shape[0]
    cols = shape[-1]
    rows = w.size // (nl * cols)
    tr = min(256 // nl, rows)
    flat3 = lambda a: a.reshape(nl, rows, cols)

    def body(*refs):
        p_refs = refs[:nl]
        w_ref, m_ref, v_ref, after_ref, g_ref, d_ref, nm_ref, nv_ref = refs[nl:]
        for l in range(nl):
            g = p_refs[l][0].astype(F32)
            for j in range(1, N_DEV):
                g = g + p_refs[l][j].astype(F32)
            g_ref[l] = g
            d_ref[l], nm_ref[l], nv_ref[l] = _adamw_math(w_ref[l], g, m_ref[l], v_ref[l])

    blk = pl.BlockSpec((nl, tr, cols), lambda i: (0, i, 0))
    outs = pl.pallas_call(
        body, name="adamw_pieces",
        grid=(rows // tr,),
        in_specs=[pl.BlockSpec((N_DEV, tr, cols), lambda i: (0, i, 0))] * nl
        + [blk, blk, blk, pl.BlockSpec(memory_space=pl.ANY)],
        out_specs=[blk] * 4,
        out_shape=[jax.ShapeDtypeStruct((nl, rows, cols), F32)] * 4,
        compiler_params=_cparams("parallel"),
    )(*[a.reshape(N_DEV, rows, cols) for a in pieces], flat3(w), flat3(m), flat3(v), after)
    return [a.reshape(shape) for a in outs]


def _adamw_small(g, w, m, v):
    shape = w.shape
    two = lambda a: a.reshape(-1, shape[-1])

    def body(g_ref, w_ref, m_ref, v_ref, d_ref, nm_ref, nv_ref):
        d_ref[...], nm_ref[...], nv_ref[...] = _adamw_math(w_ref[...], g_ref[...], m_ref[...], v_ref[...])

    outs = pl.pallas_call(
        body, name="adamw_small",
        in_specs=[VMEM_SPEC] * 4, out_specs=[VMEM_SPEC] * 3,
        out_shape=[jax.ShapeDtypeStruct(two(w).shape, F32)] * 3,
    )(two(g), two(w), two(m), two(v))
    return [a.reshape(shape) for a in outs]


WEIGHTS = ("norm_mix", "a_w_in", "a_w_conv", "a_w_out", "b_w_in", "b_w_grp", "b_scale", "b_w_out",
           "ple_norm", "ple_w_gate", "ple_w_proj", "final_norm")
SMALL_ROWS = 24
GATHER_ID = 0
EXCHANGE_ID = 4
LAST_EXCHANGE_ID = 8


def kernel(x, p, norm_mix, a_w_in, a_w_conv, a_w_out, b_w_in, b_w_grp, b_scale, b_w_out, ple_norm, ple_w_gate, ple_w_proj, final_norm, loss_target, m_norm_mix, m_a_w_in, m_a_w_conv, m_a_w_out, m_b_w_in, m_b_w_grp, m_b_scale, m_b_w_out, m_ple_norm, m_ple_w_gate, m_ple_w_proj, m_final_norm, v_norm_mix, v_a_w_in, v_a_w_conv, v_a_w_out, v_b_w_in, v_b_w_grp, v_b_scale, v_b_w_out, v_ple_norm, v_ple_w_gate, v_ple_w_proj, v_final_norm):
    wts = dict(norm_mix=norm_mix, a_w_in=a_w_in, a_w_conv=a_w_conv, a_w_out=a_w_out, b_w_in=b_w_in, b_w_grp=b_w_grp,
               b_scale=b_scale, b_w_out=b_w_out, ple_norm=ple_norm, ple_w_gate=ple_w_gate, ple_w_proj=ple_w_proj,
               final_norm=final_norm)
    mom = dict(norm_mix=m_norm_mix, a_w_in=m_a_w_in, a_w_conv=m_a_w_conv, a_w_out=m_a_w_out, b_w_in=m_b_w_in,
               b_w_grp=m_b_w_grp, b_scale=m_b_scale, b_w_out=m_b_w_out, ple_norm=m_ple_norm, ple_w_gate=m_ple_w_gate,
               ple_w_proj=m_ple_w_proj, final_norm=m_final_norm)
    var = dict(norm_mix=v_norm_mix, a_w_in=v_a_w_in, a_w_conv=v_a_w_conv, a_w_out=v_a_w_out, b_w_in=v_b_w_in,
               b_w_grp=v_b_w_grp, b_scale=v_b_scale, b_w_out=v_b_w_out, ple_norm=v_ple_norm, ple_w_gate=v_ple_w_gate,
               ple_w_proj=v_ple_w_proj, final_norm=v_final_norm)
    d = x.shape[2]
    depth = norm_mix.shape[0]
    n_a, n_b = a_w_conv.shape[0], b_scale.shape[0]
    cw = a_w_conv.shape[2]
    pos = _position(_my_place())

    def layer_matrices(i):
        j = i // 2
        mixer = {"w_in": ("a_w_in", j), "w_out": ("a_w_out", j)} if i % 2 == 0 else \
                {"w_in": ("b_w_in", j), "w_grp": ("b_w_grp", j), "w_out": ("b_w_out", j)}
        return {**mixer, "gate": ("ple_w_gate", i), "proj": ("ple_w_proj", i)}

    full = [_all_gather_layer({k: wts[name][idx].astype(BF16) for k, (name, idx) in layer_matrices(i).items()},
                              GATHER_ID + i) for i in range(depth)]
    vec_rows = jnp.concatenate([a_w_conv.reshape(-1, cw), b_scale], axis=0)
    vecs = _gather_rows(vec_rows, reduce=False)
    n_conv = 3 * n_a
    conv_w = vecs[:, :n_conv].transpose(1, 0, 2).reshape(n_a, 3, N_DEV * cw)
    scale_w = vecs[:, n_conv:].transpose(1, 0, 2).reshape(n_b, N_DEV * cw)

    def exchange(i, g):
        if i > 0:
            return _exchange_layer(g, EXCHANGE_ID + i)
        early = {k: a for k, a in g.items() if k != "w_in"}
        return {**_exchange_layer(early, EXCHANGE_ID), **_exchange_layer({"w_in": g["w_in"]}, LAST_EXCHANGE_ID)}

    loss_row, dx, sent, (d_norm, d_ple_norm, d_final, d_conv, d_scale) = _forward_backward(
        x[0], p[:, 0], loss_target[0], full, conv_w, scale_w, norm_mix, ple_norm, final_norm, exchange)
    pieces = {name: [None] * wts[name].shape[0] for name in WEIGHTS if wts[name].ndim >= 3 and name != "a_w_conv"}
    for i in range(depth):
        for k, (name, idx) in layer_matrices(i).items():
            pieces[name][idx] = sent[i][k]

    pad = lambda a: jnp.pad(a, ((0, 0), (0, d - a.shape[1])))
    small = jnp.concatenate(d_norm + d_ple_norm + [d_final] + d_conv + d_scale + [pad(loss_row)], axis=0)
    small = jnp.pad(small, ((0, SMALL_ROWS - small.shape[0]), (0, 0)))
    total = _gather_rows(small, reduce=True)
    o = 0
    gsum = {}
    gsum["norm_mix"] = total[o:o + depth]; o += depth
    gsum["ple_norm"] = total[o:o + depth]; o += depth
    gsum["final_norm"] = total[o]; o += 1
    conv_full = total[o:o + n_conv].reshape(n_a, 3, d); o += n_conv
    scale_full = total[o:o + n_b]; o += n_b
    loss = total[o, 0]
    gsum["a_w_conv"] = lax.dynamic_slice_in_dim(conv_full, pos * cw, cw, axis=2)
    gsum["b_scale"] = lax.dynamic_slice_in_dim(scale_full, pos * cw, cw, axis=1)

    token = total
    for i in reversed(range(depth)):
        token = _run_behind(sent[i]["w_out"], token)
    last_token = _run_behind(sent[0]["w_in"], token)
    grad, delta, new_m, new_v = {}, {}, {}, {}
    for k in sorted(WEIGHTS, key=lambda name: name == "a_w_in"):
        if k in pieces:
            grad[k], delta[k], new_m[k], new_v[k] = _adamw_pieces(
                pieces[k], wts[k], mom[k], var[k], last_token if k == "a_w_in" else token)
        else:
            grad[k] = gsum[k]
            delta[k], new_m[k], new_v[k] = _adamw_small(gsum[k], wts[k], mom[k], var[k])
    return (loss, dx[None], *[grad[k] for k in WEIGHTS], *[delta[k] for k in WEIGHTS],
            *[new_m[k] for k in WEIGHTS], *[new_v[k] for k in WEIGHTS])
```

```python
import jax
import jax.numpy as jnp
from jax import lax
from jax.experimental import pallas as pl
from jax.experimental.pallas import tpu as pltpu
from jax.experimental.pallas import tpu_sc as plsc

F32 = jnp.float32
BF16 = jnp.bfloat16
MESH = pl.DeviceIdType.MESH

RMS_EPS = 1e-6
POOL_WINDOWS = (2, 4, 8, 16)
N_POOL_GROUPS = len(POOL_WINDOWS)
ADAM_LR = 0.001
ADAM_B1 = 0.9
ADAM_B2 = 0.999
ADAM_EPS = 1e-08
ADAM_WD = 0.01
ADAM_STEP = 10
N_DEV = 8

HALO = 16
LAYER_ROW_TILE = 256
BWD_ROW_TILE = 512
WGRAD_ROW_TILE = 1024
VMEM_LIMIT = 56 * 1024 * 1024


def _cparams(*sem):
    return pltpu.CompilerParams(dimension_semantics=sem, vmem_limit_bytes=VMEM_LIMIT)


def _dot(a, b):
    return jnp.dot(a, b, preferred_element_type=F32)


def _dot_nt(a, b):
    return lax.dot_general(a, b, (((1,), (1,)), ((), ())), preferred_element_type=F32)


def _dot_tn(a, b):
    return lax.dot_general(a, b, (((0,), (0,)), ((), ())), preferred_element_type=F32)


def _rms_stats(x):
    r = lax.rsqrt(jnp.mean(x * x, axis=-1, keepdims=True) + RMS_EPS)
    return x * r, r


def _rms_bwd(dy, xh, r, g):
    a = dy * g
    return r * (a - xh * jnp.mean(a * xh, axis=-1, keepdims=True))


def _sigmoid(x):
    return 1.0 / (1.0 + jnp.exp(-x))


def _shift_down(x, k):
    return pltpu.roll(x, k, 0)


def _shift_up(x, k):
    return pltpu.roll(x, x.shape[0] - k, 0)


def _conv_taps(u, u_prev):
    uu = jnp.concatenate([u_prev, u], axis=0)
    return _shift_down(uu, 1)[HALO:], _shift_down(uu, 2)[HALO:]


def _window_mean_minus(uu, row0, window):
    acc = uu
    span = 1
    while span < window:
        acc = acc + _shift_down(acc, span)
        span *= 2
    return acc[HALO:] * _inv_count(uu.shape[0] - HALO, row0, window) - uu[HALO:]


def _inv_count(rows, row0, window):
    t = row0 + lax.broadcasted_iota(jnp.int32, (rows, 1), 0)
    return 1.0 / jnp.minimum(t + 1, window).astype(F32)


def _whole(*shape):
    return pl.BlockSpec(shape, lambda i: (0,) * len(shape), pipeline_mode=pl.Buffered(1))


def _layer_fwd(h, g, w_in, mixer, w_out, pn, w_gate, p_all, w_proj, layer, head=None):
    s, d = h.shape
    n = w_in.shape[1]
    e = w_out.shape[0]
    nsplit = n // e
    gdim = e // N_POOL_GROUPS
    pdim = p_all.shape[2]
    ts = min(LAYER_ROW_TILE, s)
    is_conv = mixer[0] == "conv"
    params = mixer[1:]
    head = tuple(head or ())

    def body(*refs):
        h_ref, g_ref, win_ref = refs[:3]
        mix_refs = refs[3:3 + len(params)]
        wo_ref, pn_ref, wg_ref, p_ref, wp_ref = refs[3 + len(params):8 + len(params)]
        head_refs = refs[8 + len(params):8 + len(params) + len(head)]
        proj_ref, hn_ref, o_ref, h1_ref, h2_ref, gl_ref, pp_ref = refs[8 + len(params) + len(head):][:7]
        carry_ref = refs[-1]
        i = pl.program_id(0)

        @pl.when(i == 0)
        def _():
            carry_ref[...] = jnp.zeros_like(carry_ref)

        x = h_ref[...]
        xh, _ = _rms_stats(x)
        hn = (xh * g_ref[...]).astype(BF16)
        hn_ref[...] = hn
        parts = []
        for k in range(nsplit):
            part = _dot(hn, win_ref[:, k * e:(k + 1) * e])
            proj_ref[k] = part.astype(BF16)
            parts.append(part)
        prev = carry_ref[...]
        if is_conv:
            b, c, v, z = parts
            w_ref, = mix_refs
            u = c * v
            u1, u2 = _conv_taps(u, prev)
            mixed = b * (w_ref[0:1, :] * u2 + w_ref[1:2, :] * u1 + w_ref[2:3, :] * u)
        else:
            u, z = parts
            wgrp_ref, sc_ref = mix_refs
            uu = jnp.concatenate([prev, u], axis=0)
            cols = []
            for gi, window in enumerate(POOL_WINDOWS):
                dg = _window_mean_minus(uu[:, gi * gdim:(gi + 1) * gdim], i * ts, window)
                cols.append(_dot(dg.astype(BF16), wgrp_ref[gi]))
            mixed = jnp.concatenate(cols, axis=1) * sc_ref[...]
        carry_ref[...] = u[ts - HALO:]
        o = ((z * _sigmoid(z)) * mixed).astype(BF16)
        o_ref[...] = o
        h1 = x + _dot(o, wo_ref[...])
        h1_ref[...] = h1
        xh1, _ = _rms_stats(h1)
        gl = _dot((xh1 * pn_ref[...]).astype(BF16), wg_ref[...])
        pp = _dot(p_ref[...].astype(BF16), wp_ref[...])
        gl_ref[...] = gl.astype(BF16)
        pp_ref[...] = pp.astype(BF16)
        h2 = h1 + _sigmoid(gl) * pp
        if not head:
            h2_ref[...] = h2
            return
        t_ref, gain_ref = head_refs
        loss_ref, dgain_ref = refs[-3], refs[-2]

        @pl.when(i == 0)
        def _():
            loss_ref[...] = jnp.zeros_like(loss_ref)
            dgain_ref[...] = jnp.zeros_like(dgain_ref)

        gain = gain_ref[...]
        yh, r = _rms_stats(h2)
        err = yh * gain - t_ref[...]
        loss_ref[...] += jnp.full(loss_ref.shape, (0.5 / d) * jnp.sum(err * err), F32)
        dy = err * (1.0 / d)
        dgain_ref[...] += jnp.sum(dy * yh, axis=0, keepdims=True)
        h2_ref[...] = _rms_bwd(dy, yh, r, gain)

    row = lambda width: pl.BlockSpec((ts, width), lambda i: (i, 0))
    mix_specs = [_whole(*a.shape) for a in params]
    head_specs = [row(d), _whole(1, d)] if head else []
    head_out_specs = [_whole(1, 128), _whole(1, d)] if head else []
    head_out_shape = [jax.ShapeDtypeStruct((1, 128), F32), jax.ShapeDtypeStruct((1, d), F32)] if head else []
    outs = pl.pallas_call(
        body, name="layer_fwd",
        grid=(s // ts,),
        in_specs=[row(d), _whole(1, d), _whole(d, n)] + mix_specs
        + [_whole(e, d), _whole(1, d), _whole(d, d),
           pl.BlockSpec((None, ts, pdim), lambda i: (layer, i, 0)), _whole(pdim, d)] + head_specs,
        out_specs=[pl.BlockSpec((nsplit, ts, e), lambda i: (0, i, 0)),
                   row(d), row(e), row(d), row(d), row(d), row(d)] + head_out_specs,
        out_shape=[jax.ShapeDtypeStruct((nsplit, s, e), BF16), jax.ShapeDtypeStruct((s, d), BF16),
                   jax.ShapeDtypeStruct((s, e), BF16), jax.ShapeDtypeStruct((s, d), F32),
                   jax.ShapeDtypeStruct((s, d), F32), jax.ShapeDtypeStruct((s, d), BF16),
                   jax.ShapeDtypeStruct((s, d), BF16)] + head_out_shape,
        scratch_shapes=[pltpu.VMEM((HALO, e), F32)],
        compiler_params=_cparams("arbitrary"),
    )(h, g, w_in, *params, w_out, pn, w_gate, p_all, w_proj, *head)
    proj, hn, o, h1, h2, gl, pp = outs[:7]
    return (h2, *outs[7:]) if head else h2, (proj, hn, o, h1, gl, pp)


def _out_ple_bwd(dh2, gl, pp, h1, p_all, o, pn, wgate, wout, layer, after):
    s, d = dh2.shape
    e = o.shape[1]
    pdim = p_all.shape[2]
    ts = min(BWD_ROW_TILE, s)
    last = s // ts - 1

    def body(dh2_ref, gl_ref, pp_ref, h1_ref, p_ref, o_ref, pn_ref, wg_ref, wo_ref, after_ref,
             dh1_ref, do_ref, dwp_ref, dwg_ref, dwo_ref, dpn_ref, awp, awg, awo):
        i = pl.program_id(0)

        @pl.when(i == 0)
        def _():
            awp[...] = jnp.zeros_like(awp)
            awg[...] = jnp.zeros_like(awg)
            awo[...] = jnp.zeros_like(awo)
            dpn_ref[...] = jnp.zeros_like(dpn_ref)

        dh2 = dh2_ref[...]
        gate = _sigmoid(gl_ref[...].astype(F32))
        dpp = (dh2 * gate).astype(BF16)
        dgl = (dh2 * pp_ref[...].astype(F32) * gate * (1.0 - gate)).astype(BF16)
        xh, r = _rms_stats(h1_ref[...])
        pn = pn_ref[...]
        awp[...] += _dot_tn(p_ref[...].astype(BF16), dpp)
        awg[...] += _dot_tn((xh * pn).astype(BF16), dgl)
        dr = _dot_nt(dgl, wg_ref[...])
        dpn_ref[...] += jnp.sum(dr * xh, axis=0, keepdims=True)
        dh1 = dh2 + _rms_bwd(dr, xh, r, pn)
        dh1_ref[...] = dh1
        dh1b = dh1.astype(BF16)
        do_ref[...] = _dot_nt(dh1b, wo_ref[...]).astype(BF16)
        awo[...] += _dot_tn(o_ref[...], dh1b)

        @pl.when(i == last)
        def _():
            dwp_ref[...] = awp[...].astype(BF16)
            dwg_ref[...] = awg[...].astype(BF16)
            dwo_ref[...] = awo[...].astype(BF16)

    row = lambda width: pl.BlockSpec((ts, width), lambda i: (i, 0))
    return pl.pallas_call(
        body, name="out_ple_bwd",
        grid=(s // ts,),
        in_specs=[row(d), row(d), row(d), row(d),
                  pl.BlockSpec((None, ts, pdim), lambda i: (layer, i, 0)),
                  row(e), _whole(1, d), _whole(d, d), _whole(e, d), pl.BlockSpec(memory_space=pl.ANY)],
        out_specs=[row(d), row(e), _whole(pdim, d), _whole(d, d), _whole(e, d), _whole(1, d)],
        out_shape=[jax.ShapeDtypeStruct((s, d), F32), jax.ShapeDtypeStruct((s, e), BF16),
                   jax.ShapeDtypeStruct((pdim, d), BF16), jax.ShapeDtypeStruct((d, d), BF16),
                   jax.ShapeDtypeStruct((e, d), BF16), jax.ShapeDtypeStruct((1, d), F32)],
        scratch_shapes=[pltpu.VMEM((pdim, d), F32), pltpu.VMEM((d, d), F32), pltpu.VMEM((e, d), F32)],
        compiler_params=_cparams("arbitrary"),
    )(dh2, gl, pp, h1, p_all, o, pn, wgate, wout, after)


def _mixer_bwd(do, proj, mixer, w_in, h, g, dh1, hn=None):
    s, d = h.shape
    nsplit, _, e = proj.shape
    gdim = e // N_POOL_GROUPS
    ts = min(LAYER_ROW_TILE, s)
    nt = s // ts
    is_conv = mixer[0] == "conv"
    params = mixer[1:]
    n_mix_out = 1 if is_conv else 2
    with_wgrad = hn is not None

    def body(*refs):
        refs = list(refs)
        take = lambda n: [refs.pop(0) for _ in range(n)]
        do_ref, p_ref, ph_ref = take(3)
        mix_refs = take(len(params))
        win_ref, h_ref, g_ref, dh1_ref = take(4)
        hn_ref, = take(1) if with_wgrad else [None]
        first_out, dh_ref, dg_ref = take(3)
        mix_out = take(n_mix_out)
        carry_ref, = take(1)
        dp_ref, acc_ref = take(2) if with_wgrad else (first_out, None)
        accumulators = refs + ([acc_ref] if with_wgrad else [])
        i = pl.program_id(0)
        tile = nt - 1 - i

        @pl.when(i == 0)
        def _():
            carry_ref[...] = jnp.zeros_like(carry_ref)
            dg_ref[...] = jnp.zeros_like(dg_ref)
            for ref in mix_out[-1:] + accumulators:
                ref[...] = jnp.zeros_like(ref)

        dof = do_ref[...].astype(F32)
        nxt = carry_ref[...]
        if is_conv:
            w_ref, = mix_refs
            dw_ref, = mix_out
            w0, w1, w2 = w_ref[0:1, :], w_ref[1:2, :], w_ref[2:3, :]
            b, c, v, z = [p_ref[k].astype(F32) for k in range(4)]
            u = c * v
            u_prev = jnp.where(tile == 0, 0.0, ph_ref[1].astype(F32) * ph_ref[2].astype(F32))
            u1, u2 = _conv_taps(u, u_prev)
            conv = w0 * u2 + w1 * u1 + w2 * u
            sig = _sigmoid(z)
            sz = z * sig
            dy = dof * sz
            dp_ref[3] = (dof * (b * conv) * (sig + sz * (1.0 - sig))).astype(BF16)
            dp_ref[0] = (dy * conv).astype(BF16)
            dconv = dy * b
            dw_ref[0:1, :] += jnp.sum(dconv * u2, axis=0, keepdims=True)
            dw_ref[1:2, :] += jnp.sum(dconv * u1, axis=0, keepdims=True)
            dw_ref[2:3, :] += jnp.sum(dconv * u, axis=0, keepdims=True)
            dcc = jnp.concatenate([dconv, nxt], axis=0)
            du = w2 * dconv + w1 * _shift_up(dcc, 1)[:ts] + w0 * _shift_up(dcc, 2)[:ts]
            carry_ref[...] = dconv[:HALO]
            dp_ref[1] = (du * v).astype(BF16)
            dp_ref[2] = (du * c).astype(BF16)
        else:
            wgrp_ref, sc_ref = mix_refs
            dwg_ref, dsc_ref = mix_out
            agrp, = refs
            u = p_ref[0].astype(F32)
            z = p_ref[1].astype(F32)
            u_prev = jnp.where(tile == 0, 0.0, ph_ref[0].astype(F32))
            uu = jnp.concatenate([u_prev, u], axis=0)
            sig = _sigmoid(z)
            sz = z * sig
            dm = dof * sz
            dsilu = dof * (sig + sz * (1.0 - sig))
            for gi, window in enumerate(POOL_WINDOWS):
                cols = slice(gi * gdim, (gi + 1) * gdim)
                w = wgrp_ref[gi]
                scale = sc_ref[:, cols]
                db = _window_mean_minus(uu[:, cols], tile * ts, window).astype(BF16)
                mr = _dot(db, w)
                dp_ref[1, :, cols] = (dsilu[:, cols] * (mr * scale)).astype(BF16)
                dmg = dm[:, cols]
                dmr = (dmg * scale).astype(BF16)
                agrp[gi] += _dot_tn(db, dmr)
                dsc_ref[:, cols] += jnp.sum(dmg * mr, axis=0, keepdims=True)
                dd = _dot_nt(dmr, w)
                ddq = dd * _inv_count(ts, tile * ts, window)
                acc = jnp.concatenate([ddq, nxt[:, cols]], axis=0)
                span = 1
                while span < window:
                    acc = acc + _shift_up(acc, span)
                    span *= 2
                carry_ref[:, cols] = ddq[:HALO]
                dp_ref[0, :, cols] = (acc[:ts] - dd).astype(BF16)

            @pl.when(i == nt - 1)
            def _():
                dwg_ref[...] = agrp[...].astype(BF16)

        dhn = _dot_nt(dp_ref[0], win_ref[:, 0:e])
        for k in range(1, nsplit):
            dhn += _dot_nt(dp_ref[k], win_ref[:, k * e:(k + 1) * e])
        xh, r = _rms_stats(h_ref[...])
        dg_ref[...] += jnp.sum(dhn * xh, axis=0, keepdims=True)
        dh_ref[...] = dh1_ref[...] + _rms_bwd(dhn, xh, r, g_ref[...])
        if with_wgrad:
            hn_tile = hn_ref[...]
            for k in range(nsplit):
                acc_ref[:, k * e:(k + 1) * e] += _dot_tn(hn_tile, dp_ref[k])

            @pl.when(i == nt - 1)
            def _():
                first_out[...] = acc_ref[...].astype(BF16)

    rev = lambda width: pl.BlockSpec((ts, width), lambda i: (nt - 1 - i, 0))
    halo_blocks = ts // HALO
    in_specs = [rev(e),
                pl.BlockSpec((nsplit, ts, e), lambda i: (0, nt - 1 - i, 0)),
                pl.BlockSpec((nsplit, HALO, e), lambda i: (0, jnp.maximum((nt - 1 - i) * halo_blocks - 1, 0), 0))]
    in_specs += [_whole(*a.shape) for a in params]
    in_specs += [_whole(d, nsplit * e), rev(d), _whole(1, d), rev(d)]
    scratch = [pltpu.VMEM((HALO, e), F32)]
    if with_wgrad:
        in_specs += [rev(d)]
        out_specs = [_whole(d, nsplit * e)]
        out_shape = [jax.ShapeDtypeStruct((d, nsplit * e), BF16)]
        scratch += [pltpu.VMEM((nsplit, ts, e), BF16), pltpu.VMEM((d, nsplit * e), F32)]
    else:
        out_specs = [pl.BlockSpec((nsplit, ts, e), lambda i: (0, nt - 1 - i, 0))]
        out_shape = [jax.ShapeDtypeStruct((nsplit, s, e), BF16)]
    out_specs += [rev(d), _whole(1, d)]
    out_shape += [jax.ShapeDtypeStruct((s, d), F32), jax.ShapeDtypeStruct((1, d), F32)]
    if is_conv:
        out_specs += [_whole(3, e)]
        out_shape += [jax.ShapeDtypeStruct((3, e), F32)]
    else:
        out_specs += [_whole(N_POOL_GROUPS, gdim, gdim), _whole(1, e)]
        out_shape += [jax.ShapeDtypeStruct((N_POOL_GROUPS, gdim, gdim), BF16), jax.ShapeDtypeStruct((1, e), F32)]
        scratch += [pltpu.VMEM((N_POOL_GROUPS, gdim, gdim), F32)]
    return pl.pallas_call(
        body, name="mixer_bwd",
        grid=(nt,),
        in_specs=in_specs, out_specs=out_specs, out_shape=out_shape, scratch_shapes=scratch,
        compiler_params=_cparams("arbitrary"),
    )(do, proj, proj, *params, w_in, h, g, dh1, *([hn] if with_wgrad else []))


def _proj_wgrad(hn, dproj):
    s, d = hn.shape
    nsplit, _, e = dproj.shape
    ts = min(WGRAD_ROW_TILE, s)
    last = s // ts - 1

    def body(hn_ref, dp_ref, dw_ref, acc):
        i = pl.program_id(1)

        @pl.when(i == 0)
        def _():
            acc[...] = jnp.zeros_like(acc)

        acc[...] += _dot_tn(hn_ref[...], dp_ref[...])

        @pl.when(i == last)
        def _():
            dw_ref[...] = acc[...].astype(BF16)

    return pl.pallas_call(
        body, name="proj_wgrad",
        grid=(nsplit, s // ts),
        in_specs=[pl.BlockSpec((ts, d), lambda k, i: (i, 0)),
                  pl.BlockSpec((None, ts, e), lambda k, i: (k, i, 0))],
        out_specs=pl.BlockSpec((d, e), lambda k, i: (0, k)),
        out_shape=jax.ShapeDtypeStruct((d, nsplit * e), BF16),
        scratch_shapes=[pltpu.VMEM((d, e), F32)],
        compiler_params=_cparams("parallel", "arbitrary"),
    )(hn, dproj)


def _forward_backward(xs, ps, target, full, conv_w, scale_w, norm_mix, ple_norm, final_norm, exchange):
    depth = len(full)
    row = lambda a, i: a[i][None, :]
    mixer_of = lambda i: ("conv", conv_w[i // 2]) if i % 2 == 0 else ("pool", full[i]["w_grp"], row(scale_w, i // 2))

    saved = []
    h = xs
    for i in range(depth):
        w = full[i]
        head = (target, final_norm[None, :]) if i == depth - 1 else None
        h_next, acts = _layer_fwd(h, row(norm_mix, i), w["w_in"], mixer_of(i), w["w_out"], row(ple_norm, i),
                                  w["gate"], ps, w["proj"], i, head)
        saved.append((h, *acts))
        h = h_next
    dh, loss_row, d_final = h

    d_norm, d_ple_norm, d_conv, d_scale, sent = [None] * depth, [None] * depth, [], [], [None] * depth
    after = d_final
    for i in reversed(range(depth)):
        w = full[i]
        h_in, proj, hn, o, h1, gl, pp = saved[i]
        g = {}
        dh1, do, g["proj"], g["gate"], g["w_out"], d_ple_norm[i] = _out_ple_bwd(
            dh, gl, pp, h1, ps, o, row(ple_norm, i), w["gate"], w["w_out"], i, after)
        if i % 2 == 0:
            dproj, dh, d_norm[i], dw_conv = _mixer_bwd(do, proj, mixer_of(i), w["w_in"], h_in, row(norm_mix, i), dh1)
            d_conv.insert(0, dw_conv)
            g["w_in"] = _proj_wgrad(hn, dproj)
        else:
            g["w_in"], dh, d_norm[i], g["w_grp"], dw_scale = _mixer_bwd(
                do, proj, mixer_of(i), w["w_in"], h_in, row(norm_mix, i), dh1, hn=hn)
            d_scale.insert(0, dw_scale)
        after = g["w_in"]
        sent[i] = exchange(i, g)
    return loss_row, dh, sent, (d_norm, d_ple_norm, d_final, d_conv, d_scale)


VMEM_SPEC = pl.BlockSpec(memory_space=pltpu.VMEM)

FLIPS = [(fx, fy, fc) for fx in (0, 1) for fy in (0, 1) for fc in (0, 1)][1:]
SHARD_AXIS = {"w_in": 1, "w_out": 0, "w_grp": 1, "gate": 0, "proj": 1}


def _my_place():
    return lax.axis_index("x"), lax.axis_index("y"), lax.axis_index("c")


def _position(place):
    x, y, c = place
    return 4 * x + 2 * y + c


def _flip(place, flips):
    return tuple(1 - v if f else v for v, f in zip(place, flips))


def _shard_of(ref, axis, pos, n):
    idx = [slice(None)] * len(ref.shape)
    idx[axis] = pl.ds(pl.multiple_of(pos * n, n), n)
    return ref.at[tuple(idx)]


def _sequencer_mesh():
    return plsc.ScalarSubcoreMesh(axis_name="sequencer", num_cores=1)


def _handshake(peers):
    barrier = pltpu.get_barrier_semaphore()
    for peer in peers:
        pl.semaphore_signal(barrier, inc=1, device_id=peer, device_id_type=MESH)
    pl.semaphore_wait(barrier, len(peers))


def _all_gather_layer(shards, collective_id):
    names = list(shards)
    nt = len(names)
    axes = [SHARD_AXIS[k] for k in names]
    widths = [shards[k].shape[SHARD_AXIS[k]] for k in names]

    def full_shape(k):
        shp = list(shards[k].shape)
        shp[SHARD_AXIS[k]] *= N_DEV
        return tuple(shp)

    def body(*refs):
        ins, outs = refs[:nt], refs[nt:2 * nt]
        send_sems, recv_sems, local_sem = refs[2 * nt:]
        me = _my_place()
        x, y, c = me
        sibling = (x, y, 1 - c)
        chips = [(1 - x, y), (x, 1 - y), (1 - x, 1 - y)]
        _handshake([sibling] + [(*chip, c) for chip in chips])

        def block(t, place):
            return _shard_of(outs[t], axes[t], _position(place), widths[t])

        def copy(t, k, place, to, src=None):
            return pltpu.make_async_remote_copy(
                src_ref=block(t, place) if src is None else src, dst_ref=block(t, place),
                send_sem=send_sems.at[k], recv_sem=recv_sems.at[k], device_id=to, device_id_type=MESH)

        mine = [pltpu.make_async_copy(ins[t], block(t, me), local_sem) for t in range(nt)]
        for cp in mine:
            cp.start()
        first = []
        for j, chip in enumerate(chips):
            first += [copy(t, 1 + j, me, (*chip, c), src=ins[t]) for t in range(nt)]
        first += [copy(t, 0, me, sibling, src=ins[t]) for t in range(nt)]
        for cp in first:
            cp.start()
        passed = []
        for j, chip in enumerate(chips):
            for t in range(nt):
                copy(t, 1 + j, (*chip, c), me).wait_recv()
            for t in range(nt):
                fwd = copy(t, 4 + j, (*chip, c), sibling)
                fwd.start()
                passed.append(fwd)
        for t in range(nt):
            copy(t, 0, sibling, me).wait_recv()
        for j, chip in enumerate(chips):
            for t in range(nt):
                copy(t, 4 + j, (*chip, 1 - c), me).wait_recv()
        for cp in first + passed:
            cp.wait_send()
        for cp in mine:
            cp.wait()

    outs = pl.kernel(
        body, name=f"all_gather_layer_{collective_id}",
        out_type=[jax.ShapeDtypeStruct(full_shape(k), shards[k].dtype) for k in names],
        mesh=_sequencer_mesh(),
        scratch_types=[pltpu.SemaphoreType.DMA((7,)), pltpu.SemaphoreType.DMA((7,)), pltpu.SemaphoreType.DMA],
        compiler_params=pltpu.CompilerParams(collective_id=collective_id),
    )(*[shards[k] for k in names])
    return dict(zip(names, outs))


def _exchange_layer(grads, collective_id):
    names = list(grads)
    nt = len(names)
    axes = [SHARD_AXIS[k] for k in names]
    widths = [grads[k].shape[SHARD_AXIS[k]] // N_DEV for k in names]

    def slot_shape(t):
        shp = list(grads[names[t]].shape)
        shp[axes[t]] = widths[t]
        return (N_DEV, *shp)

    def body(*refs):
        ins, outs = refs[:nt], refs[nt:2 * nt]
        send_sems, recv_sems, local_sem = refs[2 * nt:]
        me = _my_place()
        mine = _position(me)
        _handshake([_flip(me, flips) for flips in FLIPS])
        local = [pltpu.make_async_copy(_shard_of(ins[t], axes[t], mine, widths[t]), outs[t].at[mine], local_sem)
                 for t in range(nt)]
        for cp in local:
            cp.start()
        copies = []
        for k, flips in enumerate(FLIPS):
            peer = _flip(me, flips)
            for t in range(nt):
                cp = pltpu.make_async_remote_copy(
                    src_ref=_shard_of(ins[t], axes[t], _position(peer), widths[t]), dst_ref=outs[t].at[mine],
                    send_sem=send_sems.at[k], recv_sem=recv_sems.at[k], device_id=peer, device_id_type=MESH)
                cp.start()
                copies.append(cp)
        for cp in copies:
            cp.wait()
        for cp in local:
            cp.wait()

    outs = pl.kernel(
        body, name=f"exchange_layer_{collective_id}",
        out_type=[jax.ShapeDtypeStruct(slot_shape(t), BF16) for t in range(nt)],
        mesh=_sequencer_mesh(),
        scratch_types=[pltpu.SemaphoreType.DMA((7,)), pltpu.SemaphoreType.DMA((7,)), pltpu.SemaphoreType.DMA],
        compiler_params=pltpu.CompilerParams(collective_id=collective_id),
    )(*[grads[k] for k in names])
    return dict(zip(names, outs))


def _gather_rows(buf, reduce):
    r, c = buf.shape

    def body(in_ref, out_ref, *scratch):
        if reduce:
            all_ref, send_sems, recv_sems = scratch
        else:
            all_ref = out_ref
            send_sems, recv_sems = scratch
        me = _my_place()
        all_ref[_position(me)] = in_ref[...]
        copies = []
        for k, flips in enumerate(FLIPS):
            cp = pltpu.make_async_remote_copy(
                src_ref=in_ref, dst_ref=all_ref.at[_position(me)],
                send_sem=send_sems.at[k], recv_sem=recv_sems.at[k], device_id=_flip(me, flips), device_id_type=MESH)
            cp.start()
            copies.append(cp)
        for cp in copies:
            cp.wait()
        if reduce:
            total = all_ref[0]
            for j in range(1, N_DEV):
                total = total + all_ref[j]
            out_ref[...] = total

    return pl.pallas_call(
        body, name="sum_rows" if reduce else "gather_rows",
        in_specs=[VMEM_SPEC], out_specs=VMEM_SPEC,
        out_shape=jax.ShapeDtypeStruct((r, c) if reduce else (N_DEV, r, c), F32),
        scratch_shapes=([pltpu.VMEM((N_DEV, r, c), F32)] if reduce else [])
        + [pltpu.SemaphoreType.DMA((7,)), pltpu.SemaphoreType.DMA((7,))],
    )(buf)


def _adamw_math(w, g, m, v):
    m = ADAM_B1 * m + (1.0 - ADAM_B1) * g
    v = ADAM_B2 * v + (1.0 - ADAM_B2) * (g * g)
    m_hat = m / (1.0 - ADAM_B1 ** ADAM_STEP)
    v_hat = v / (1.0 - ADAM_B2 ** ADAM_STEP)
    delta = -ADAM_LR * (m_hat / (jnp.sqrt(v_hat) + ADAM_EPS) + ADAM_WD * w)
    return delta, m, v


def _run_behind(x, token):
    def body(x_ref, token_ref, out_ref):
        out_ref[...] = jnp.zeros_like(out_ref)

    any_spec = pl.BlockSpec(memory_space=pl.ANY)
    return pl.pallas_call(
        body, name="run_behind",
        in_specs=[any_spec, any_spec], out_specs=VMEM_SPEC,
        out_shape=jax.ShapeDtypeStruct((8, 128), F32),
    )(x, token)


def _adamw_pieces(pieces, w, m, v, after):
    shape = w.shape
    nl = shape[0]
    cols = shape[-1]
    rows = w.size // (nl * cols)
    tr = min(256 // nl, rows)
    flat3 = lambda a: a.reshape(nl, rows, cols)

    def body(*refs):
        p_refs = refs[:nl]
        w_ref, m_ref, v_ref, after_ref, g_ref, d_ref, nm_ref, nv_ref = refs[nl:]
        for l in range(nl):
            g = p_refs[l][0].astype(F32)
            for j in range(1, N_DEV):
                g = g + p_refs[l][j].astype(F32)
            g_ref[l] = g
            d_ref[l], nm_ref[l], nv_ref[l] = _adamw_math(w_ref[l], g, m_ref[l], v_ref[l])

    blk = pl.BlockSpec((nl, tr, cols), lambda i: (0, i, 0))
    outs = pl.pallas_call(
        body, name="adamw_pieces",
        grid=(rows // tr,),
        in_specs=[pl.BlockSpec((N_DEV, tr, cols), lambda i: (0, i, 0))] * nl
        + [blk, blk, blk, pl.BlockSpec(memory_space=pl.ANY)],
        out_specs=[blk] * 4,
        out_shape=[jax.ShapeDtypeStruct((nl, rows, cols), F32)] * 4,
        compiler_params=_cparams("parallel"),
    )(*[a.reshape(N_DEV, rows, cols) for a in pieces], flat3(w), flat3(m), flat3(v), after)
    return [a.reshape(shape) for a in outs]


def _adamw_small(g, w, m, v):
    shape = w.shape
    two = lambda a: a.reshape(-1, shape[-1])

    def body(g_ref, w_ref, m_ref, v_ref, d_ref, nm_ref, nv_ref):
        d_ref[...], nm_ref[...], nv_ref[...] = _adamw_math(w_ref[...], g_ref[...], m_ref[...], v_ref[...])

    outs = pl.pallas_call(
        body, name="adamw_small",
        in_specs=[VMEM_SPEC] * 4, out_specs=[VMEM_SPEC] * 3,
        out_shape=[jax.ShapeDtypeStruct(two(w).shape, F32)] * 3,
    )(two(g), two(w), two(m), two(v))
    return [a.reshape(shape) for a in outs]


WEIGHTS = ("norm_mix", "a_w_in", "a_w_conv", "a_w_out", "b_w_in", "b_w_grp", "b_scale", "b_w_out",
           "ple_norm", "ple_w_gate", "ple_w_proj", "final_norm")
SMALL_ROWS = 24
GATHER_ID = 0
EXCHANGE_ID = 4
LAST_EXCHANGE_ID = 8


def kernel(x, p, norm_mix, a_w_in, a_w_conv, a_w_out, b_w_in, b_w_grp, b_scale, b_w_out, ple_norm, ple_w_gate, ple_w_proj, final_norm, loss_target, m_norm_mix, m_a_w_in, m_a_w_conv, m_a_w_out, m_b_w_in, m_b_w_grp, m_b_scale, m_b_w_out, m_ple_norm, m_ple_w_gate, m_ple_w_proj, m_final_norm, v_norm_mix, v_a_w_in, v_a_w_conv, v_a_w_out, v_b_w_in, v_b_w_grp, v_b_scale, v_b_w_out, v_ple_norm, v_ple_w_gate, v_ple_w_proj, v_final_norm):
    wts = dict(norm_mix=norm_mix, a_w_in=a_w_in, a_w_conv=a_w_conv, a_w_out=a_w_out, b_w_in=b_w_in, b_w_grp=b_w_grp,
               b_scale=b_scale, b_w_out=b_w_out, ple_norm=ple_norm, ple_w_gate=ple_w_gate, ple_w_proj=ple_w_proj,
               final_norm=final_norm)
    mom = dict(norm_mix=m_norm_mix, a_w_in=m_a_w_in, a_w_conv=m_a_w_conv, a_w_out=m_a_w_out, b_w_in=m_b_w_in,
               b_w_grp=m_b_w_grp, b_scale=m_b_scale, b_w_out=m_b_w_out, ple_norm=m_ple_norm, ple_w_gate=m_ple_w_gate,
               ple_w_proj=m_ple_w_proj, final_norm=m_final_norm)
    var = dict(norm_mix=v_norm_mix, a_w_in=v_a_w_in, a_w_conv=v_a_w_conv, a_w_out=v_a_w_out, b_w_in=v_b_w_in,
               b_w_grp=v_b_w_grp, b_scale=v_b_scale, b_w_out=v_b_w_out, ple_norm=v_ple_norm, ple_w_gate=v_ple_w_gate,
               ple_w_proj=v_ple_w_proj, final_norm=v_final_norm)
    d = x.shape[2]
    depth = norm_mix.shape[0]
    n_a, n_b = a_w_conv.shape[0], b_scale.shape[0]
    cw = a_w_conv.shape[2]
    pos = _position(_my_place())

    def layer_matrices(i):
        j = i // 2
        mixer = {"w_in": ("a_w_in", j), "w_out": ("a_w_out", j)} if i % 2 == 0 else \
                {"w_in": ("b_w_in", j), "w_grp": ("b_w_grp", j), "w_out": ("b_w_out", j)}
        return {**mixer, "gate": ("ple_w_gate", i), "proj": ("ple_w_proj", i)}

    full = [_all_gather_layer({k: wts[name][idx].astype(BF16) for k, (name, idx) in layer_matrices(i).items()},
                              GATHER_ID + i) for i in range(depth)]
    vec_rows = jnp.concatenate([a_w_conv.reshape(-1, cw), b_scale], axis=0)
    vecs = _gather_rows(vec_rows, reduce=False)
    n_conv = 3 * n_a
    conv_w = vecs[:, :n_conv].transpose(1, 0, 2).reshape(n_a, 3, N_DEV * cw)
    scale_w = vecs[:, n_conv:].transpose(1, 0, 2).reshape(n_b, N_DEV * cw)

    def exchange(i, g):
        if i > 0:
            return _exchange_layer(g, EXCHANGE_ID + i)
        early = {k: a for k, a in g.items() if k != "w_in"}
        return {**_exchange_layer(early, EXCHANGE_ID), **_exchange_layer({"w_in": g["w_in"]}, LAST_EXCHANGE_ID)}

    loss_row, dx, sent, (d_norm, d_ple_norm, d_final, d_conv, d_scale) = _forward_backward(
        x[0], p[:, 0], loss_target[0], full, conv_w, scale_w, norm_mix, ple_norm, final_norm, exchange)
    pieces = {name: [None] * wts[name].shape[0] for name in WEIGHTS if wts[name].ndim >= 3 and name != "a_w_conv"}
    for i in range(depth):
        for k, (name, idx) in layer_matrices(i).items():
            pieces[name][idx] = sent[i][k]

    pad = lambda a: jnp.pad(a, ((0, 0), (0, d - a.shape[1])))
    small = jnp.concatenate(d_norm + d_ple_norm + [d_final] + d_conv + d_scale + [pad(loss_row)], axis=0)
    small = jnp.pad(small, ((0, SMALL_ROWS - small.shape[0]), (0, 0)))
    total = _gather_rows(small, reduce=True)
    o = 0
    gsum = {}
    gsum["norm_mix"] = total[o:o + depth]; o += depth
    gsum["ple_norm"] = total[o:o + depth]; o += depth
    gsum["final_norm"] = total[o]; o += 1
    conv_full = total[o:o + n_conv].reshape(n_a, 3, d); o += n_conv
    scale_full = total[o:o + n_b]; o += n_b
    loss = total[o, 0]
    gsum["a_w_conv"] = lax.dynamic_slice_in_dim(conv_full, pos * cw, cw, axis=2)
    gsum["b_scale"] = lax.dynamic_slice_in_dim(scale_full, pos * cw, cw, axis=1)

    token = total
    for i in reversed(range(depth)):
        token = _run_behind(sent[i]["w_out"], token)
    last_token = _run_behind(sent[0]["w_in"], token)
    grad, delta, new_m, new_v = {}, {}, {}, {}
    for k in sorted(WEIGHTS, key=lambda name: name == "a_w_in"):
        if k in pieces:
            grad[k], delta[k], new_m[k], new_v[k] = _adamw_pieces(
                pieces[k], wts[k], mom[k], var[k], last_token if k == "a_w_in" else token)
        else:
            grad[k] = gsum[k]
            delta[k], new_m[k], new_v[k] = _adamw_small(gsum[k], wts[k], mom[k], var[k])
    return (loss, dx[None], *[grad[k] for k in WEIGHTS], *[delta[k] for k in WEIGHTS],
            *[new_m[k] for k in WEIGHTS], *[new_v[k] for k in WEIGHTS])
```

```python
import jax
import jax.numpy as jnp
from jax import lax
from jax.experimental import pallas as pl
from jax.experimental.pallas import tpu as pltpu
from jax.experimental.pallas import tpu_sc as plsc

F32 = jnp.float32
BF16 = jnp.bfloat16
MESH = pl.DeviceIdType.MESH

RMS_EPS = 1e-6
POOL_WINDOWS = (2, 4, 8, 16)
N_POOL_GROUPS = len(POOL_WINDOWS)
ADAM_LR = 0.001
ADAM_B1 = 0.9
ADAM_B2 = 0.999
ADAM_EPS = 1e-08
ADAM_WD = 0.01
ADAM_STEP = 10
N_DEV = 8

HALO = 16
LAYER_ROW_TILE = 256
BWD_ROW_TILE = 512
WGRAD_ROW_TILE = 1024
VMEM_LIMIT = 56 * 1024 * 1024


def _cparams(*sem):
    return pltpu.CompilerParams(dimension_semantics=sem, vmem_limit_bytes=VMEM_LIMIT)


def _dot(a, b):
    return jnp.dot(a, b, preferred_element_type=F32)


def _dot_nt(a, b):
    return lax.dot_general(a, b, (((1,), (1,)), ((), ())), preferred_element_type=F32)


def _dot_tn(a, b):
    return lax.dot_general(a, b, (((0,), (0,)), ((), ())), preferred_element_type=F32)


def _rms_stats(x):
    r = lax.rsqrt(jnp.mean(x * x, axis=-1, keepdims=True) + RMS_EPS)
    return x * r, r


def _rms_bwd(dy, xh, r, g):
    a = dy * g
    return r * (a - xh * jnp.mean(a * xh, axis=-1, keepdims=True))


def _sigmoid(x):
    return 1.0 / (1.0 + jnp.exp(-x))


def _shift_down(x, k):
    return pltpu.roll(x, k, 0)


def _shift_up(x, k):
    return pltpu.roll(x, x.shape[0] - k, 0)


def _conv_taps(u, u_prev):
    uu = jnp.concatenate([u_prev, u], axis=0)
    return _shift_down(uu, 1)[HALO:], _shift_down(uu, 2)[HALO:]


def _window_mean_minus(uu, row0, window):
    acc = uu
    span = 1
    while span < window:
        acc = acc + _shift_down(acc, span)
        span *= 2
    return acc[HALO:] * _inv_count(uu.shape[0] - HALO, row0, window) - uu[HALO:]


def _inv_count(rows, row0, window):
    t = row0 + lax.broadcasted_iota(jnp.int32, (rows, 1), 0)
    return 1.0 / jnp.minimum(t + 1, window).astype(F32)


def _whole(*shape):
    return pl.BlockSpec(shape, lambda i: (0,) * len(shape), pipeline_mode=pl.Buffered(1))


def _layer_fwd(h, g, w_in, mixer, w_out, pn, w_gate, p_all, w_proj, layer, head=None):
    s, d = h.shape
    n = w_in.shape[1]
    e = w_out.shape[0]
    nsplit = n // e
    gdim = e // N_POOL_GROUPS
    pdim = p_all.shape[2]
    ts = min(LAYER_ROW_TILE, s)
    is_conv = mixer[0] == "conv"
    params = mixer[1:]
    head = tuple(head or ())

    def body(*refs):
        h_ref, g_ref, win_ref = refs[:3]
        mix_refs = refs[3:3 + len(params)]
        wo_ref, pn_ref, wg_ref, p_ref, wp_ref = refs[3 + len(params):8 + len(params)]
        head_refs = refs[8 + len(params):8 + len(params) + len(head)]
        proj_ref, hn_ref, o_ref, h1_ref, h2_ref, gl_ref, pp_ref = refs[8 + len(params) + len(head):][:7]
        carry_ref = refs[-1]
        i = pl.program_id(0)

        @pl.when(i == 0)
        def _():
            carry_ref[...] = jnp.zeros_like(carry_ref)

        x = h_ref[...]
        xh, _ = _rms_stats(x)
        hn = (xh * g_ref[...]).astype(BF16)
        hn_ref[...] = hn
        parts = []
        for k in range(nsplit):
            part = _dot(hn, win_ref[:, k * e:(k + 1) * e])
            proj_ref[k] = part.astype(BF16)
            parts.append(part)
        prev = carry_ref[...]
        if is_conv:
            b, c, v, z = parts
            w_ref, = mix_refs
            u = c * v
            u1, u2 = _conv_taps(u, prev)
            mixed = b * (w_ref[0:1, :] * u2 + w_ref[1:2, :] * u1 + w_ref[2:3, :] * u)
        else:
            u, z = parts
            wgrp_ref, sc_ref = mix_refs
            uu = jnp.concatenate([prev, u], axis=0)
            cols = []
            for gi, window in enumerate(POOL_WINDOWS):
                dg = _window_mean_minus(uu[:, gi * gdim:(gi + 1) * gdim], i * ts, window)
                cols.append(_dot(dg.astype(BF16), wgrp_ref[gi]))
            mixed = jnp.concatenate(cols, axis=1) * sc_ref[...]
        carry_ref[...] = u[ts - HALO:]
        o = ((z * _sigmoid(z)) * mixed).astype(BF16)
        o_ref[...] = o
        h1 = x + _dot(o, wo_ref[...])
        h1_ref[...] = h1
        xh1, _ = _rms_stats(h1)
        gl = _dot((xh1 * pn_ref[...]).astype(BF16), wg_ref[...])
        pp = _dot(p_ref[...].astype(BF16), wp_ref[...])
        gl_ref[...] = gl.astype(BF16)
        pp_ref[...] = pp.astype(BF16)
        h2 = h1 + _sigmoid(gl) * pp
        if not head:
            h2_ref[...] = h2
            return
        t_ref, gain_ref = head_refs
        loss_ref, dgain_ref = refs[-3], refs[-2]

        @pl.when(i == 0)
        def _():
            loss_ref[...] = jnp.zeros_like(loss_ref)
            dgain_ref[...] = jnp.zeros_like(dgain_ref)

        gain = gain_ref[...]
        yh, r = _rms_stats(h2)
        err = yh * gain - t_ref[...]
        loss_ref[...] += jnp.full(loss_ref.shape, (0.5 / d) * jnp.sum(err * err), F32)
        dy = err * (1.0 / d)
        dgain_ref[...] += jnp.sum(dy * yh, axis=0, keepdims=True)
        h2_ref[...] = _rms_bwd(dy, yh, r, gain)

    row = lambda width: pl.BlockSpec((ts, width), lambda i: (i, 0))
    mix_specs = [_whole(*a.shape) for a in params]
    head_specs = [row(d), _whole(1, d)] if head else []
    head_out_specs = [_whole(1, 128), _whole(1, d)] if head else []
    head_out_shape = [jax.ShapeDtypeStruct((1, 128), F32), jax.ShapeDtypeStruct((1, d), F32)] if head else []
    outs = pl.pallas_call(
        body, name="layer_fwd",
        grid=(s // ts,),
        in_specs=[row(d), _whole(1, d), _whole(d, n)] + mix_specs
        + [_whole(e, d), _whole(1, d), _whole(d, d),
           pl.BlockSpec((None, ts, pdim), lambda i: (layer, i, 0)), _whole(pdim, d)] + head_specs,
        out_specs=[pl.BlockSpec((nsplit, ts, e), lambda i: (0, i, 0)),
                   row(d), row(e), row(d), row(d), row(d), row(d)] + head_out_specs,
        out_shape=[jax.ShapeDtypeStruct((nsplit, s, e), BF16), jax.ShapeDtypeStruct((s, d), BF16),
                   jax.ShapeDtypeStruct((s, e), BF16), jax.ShapeDtypeStruct((s, d), F32),
                   jax.ShapeDtypeStruct((s, d), F32), jax.ShapeDtypeStruct((s, d), BF16),
                   jax.ShapeDtypeStruct((s, d), BF16)] + head_out_shape,
        scratch_shapes=[pltpu.VMEM((HALO, e), F32)],
        compiler_params=_cparams("arbitrary"),
    )(h, g, w_in, *params, w_out, pn, w_gate, p_all, w_proj, *head)
    proj, hn, o, h1, h2, gl, pp = outs[:7]
    return (h2, *outs[7:]) if head else h2, (proj, hn, o, h1, gl, pp)


def _out_ple_bwd(dh2, gl, pp, h1, p_all, o, pn, wgate, wout, layer, after):
    s, d = dh2.shape
    e = o.shape[1]
    pdim = p_all.shape[2]
    ts = min(BWD_ROW_TILE, s)
    last = s // ts - 1

    def body(dh2_ref, gl_ref, pp_ref, h1_ref, p_ref, o_ref, pn_ref, wg_ref, wo_ref, after_ref,
             dh1_ref, do_ref, dwp_ref, dwg_ref, dwo_ref, dpn_ref, awp, awg, awo):
        i = pl.program_id(0)

        @pl.when(i == 0)
        def _():
            awp[...] = jnp.zeros_like(awp)
            awg[...] = jnp.zeros_like(awg)
            awo[...] = jnp.zeros_like(awo)
            dpn_ref[...] = jnp.zeros_like(dpn_ref)

        dh2 = dh2_ref[...]
        gate = _sigmoid(gl_ref[...].astype(F32))
        dpp = (dh2 * gate).astype(BF16)
        dgl = (dh2 * pp_ref[...].astype(F32) * gate * (1.0 - gate)).astype(BF16)
        xh, r = _rms_stats(h1_ref[...])
        pn = pn_ref[...]
        awp[...] += _dot_tn(p_ref[...].astype(BF16), dpp)
        awg[...] += _dot_tn((xh * pn).astype(BF16), dgl)
        dr = _dot_nt(dgl, wg_ref[...])
        dpn_ref[...] += jnp.sum(dr * xh, axis=0, keepdims=True)
        dh1 = dh2 + _rms_bwd(dr, xh, r, pn)
        dh1_ref[...] = dh1
        dh1b = dh1.astype(BF16)
        do_ref[...] = _dot_nt(dh1b, wo_ref[...]).astype(BF16)
        awo[...] += _dot_tn(o_ref[...], dh1b)

        @pl.when(i == last)
        def _():
            dwp_ref[...] = awp[...].astype(BF16)
            dwg_ref[...] = awg[...].astype(BF16)
            dwo_ref[...] = awo[...].astype(BF16)

    row = lambda width: pl.BlockSpec((ts, width), lambda i: (i, 0))
    return pl.pallas_call(
        body, name="out_ple_bwd",
        grid=(s // ts,),
        in_specs=[row(d), row(d), row(d), row(d),
                  pl.BlockSpec((None, ts, pdim), lambda i: (layer, i, 0)),
                  row(e), _whole(1, d), _whole(d, d), _whole(e, d), pl.BlockSpec(memory_space=pl.ANY)],
        out_specs=[row(d), row(e), _whole(pdim, d), _whole(d, d), _whole(e, d), _whole(1, d)],
        out_shape=[jax.ShapeDtypeStruct((s, d), F32), jax.ShapeDtypeStruct((s, e), BF16),
                   jax.ShapeDtypeStruct((pdim, d), BF16), jax.ShapeDtypeStruct((d, d), BF16),
                   jax.ShapeDtypeStruct((e, d), BF16), jax.ShapeDtypeStruct((1, d), F32)],
        scratch_shapes=[pltpu.VMEM((pdim, d), F32), pltpu.VMEM((d, d), F32), pltpu.VMEM((e, d), F32)],
        compiler_params=_cparams("arbitrary"),
    )(dh2, gl, pp, h1, p_all, o, pn, wgate, wout, after)


def _mixer_bwd(do, proj, mixer, w_in, h, g, dh1, hn=None):
    s, d = h.shape
    nsplit, _, e = proj.shape
    gdim = e // N_POOL_GROUPS
    ts = min(LAYER_ROW_TILE, s)
    nt = s // ts
    is_conv = mixer[0] == "conv"
    params = mixer[1:]
    n_mix_out = 1 if is_conv else 2
    with_wgrad = hn is not None

    def body(*refs):
        refs = list(refs)
        take = lambda n: [refs.pop(0) for _ in range(n)]
        do_ref, p_ref, ph_ref = take(3)
        mix_refs = take(len(params))
        win_ref, h_ref, g_ref, dh1_ref = take(4)
        hn_ref, = take(1) if with_wgrad else [None]
        first_out, dh_ref, dg_ref = take(3)
        mix_out = take(n_mix_out)
        carry_ref, = take(1)
        dp_ref, acc_ref = take(2) if with_wgrad else (first_out, None)
        accumulators = refs + ([acc_ref] if with_wgrad else [])
        i = pl.program_id(0)
        tile = nt - 1 - i

        @pl.when(i == 0)
        def _():
            carry_ref[...] = jnp.zeros_like(carry_ref)
            dg_ref[...] = jnp.zeros_like(dg_ref)
            for ref in mix_out[-1:] + accumulators:
                ref[...] = jnp.zeros_like(ref)

        dof = do_ref[...].astype(F32)
        nxt = carry_ref[...]
        if is_conv:
            w_ref, = mix_refs
            dw_ref, = mix_out
            w0, w1, w2 = w_ref[0:1, :], w_ref[1:2, :], w_ref[2:3, :]
            b, c, v, z = [p_ref[k].astype(F32) for k in range(4)]
            u = c * v
            u_prev = jnp.where(tile == 0, 0.0, ph_ref[1].astype(F32) * ph_ref[2].astype(F32))
            u1, u2 = _conv_taps(u, u_prev)
            conv = w0 * u2 + w1 * u1 + w2 * u
            sig = _sigmoid(z)
            sz = z * sig
            dy = dof * sz
            dp_ref[3] = (dof * (b * conv) * (sig + sz * (1.0 - sig))).astype(BF16)
            dp_ref[0] = (dy * conv).astype(BF16)
            dconv = dy * b
            dw_ref[0:1, :] += jnp.sum(dconv * u2, axis=0, keepdims=True)
            dw_ref[1:2, :] += jnp.sum(dconv * u1, axis=0, keepdims=True)
            dw_ref[2:3, :] += jnp.sum(dconv * u, axis=0, keepdims=True)
            dcc = jnp.concatenate([dconv, nxt], axis=0)
            du = w2 * dconv + w1 * _shift_up(dcc, 1)[:ts] + w0 * _shift_up(dcc, 2)[:ts]
            carry_ref[...] = dconv[:HALO]
            dp_ref[1] = (du * v).astype(BF16)
            dp_ref[2] = (du * c).astype(BF16)
        else:
            wgrp_ref, sc_ref = mix_refs
            dwg_ref, dsc_ref = mix_out
            agrp, = refs
            u = p_ref[0].astype(F32)
            z = p_ref[1].astype(F32)
            u_prev = jnp.where(tile == 0, 0.0, ph_ref[0].astype(F32))
            uu = jnp.concatenate([u_prev, u], axis=0)
            sig = _sigmoid(z)
            sz = z * sig
            dm = dof * sz
            dsilu = dof * (sig + sz * (1.0 - sig))
            for gi, window in enumerate(POOL_WINDOWS):
                cols = slice(gi * gdim, (gi + 1) * gdim)
                w = wgrp_ref[gi]
                scale = sc_ref[:, cols]
                db = _window_mean_minus(uu[:, cols], tile * ts, window).astype(BF16)
                mr = _dot(db, w)
                dp_ref[1, :, cols] = (dsilu[:, cols] * (mr * scale)).astype(BF16)
                dmg = dm[:, cols]
                dmr = (dmg * scale).astype(BF16)
                agrp[gi] += _dot_tn(db, dmr)
                dsc_ref[:, cols] += jnp.sum(dmg * mr, axis=0, keepdims=True)
                dd = _dot_nt(dmr, w)
                ddq = dd * _inv_count(ts, tile * ts, window)
                acc = jnp.concatenate([ddq, nxt[:, cols]], axis=0)
                span = 1
                while span < window:
                    acc = acc + _shift_up(acc, span)
                    span *= 2
                carry_ref[:, cols] = ddq[:HALO]
                dp_ref[0, :, cols] = (acc[:ts] - dd).astype(BF16)

            @pl.when(i == nt - 1)
            def _():
                dwg_ref[...] = agrp[...].astype(BF16)

        dhn = _dot_nt(dp_ref[0], win_ref[:, 0:e])
        for k in range(1, nsplit):
            dhn += _dot_nt(dp_ref[k], win_ref[:, k * e:(k + 1) * e])
        xh, r = _rms_stats(h_ref[...])
        dg_ref[...] += jnp.sum(dhn * xh, axis=0, keepdims=True)
        dh_ref[...] = dh1_ref[...] + _rms_bwd(dhn, xh, r, g_ref[...])
        if with_wgrad:
            hn_tile = hn_ref[...]
            for k in range(nsplit):
                acc_ref[:, k * e:(k + 1) * e] += _dot_tn(hn_tile, dp_ref[k])

            @pl.when(i == nt - 1)
            def _():
                first_out[...] = acc_ref[...].astype(BF16)

    rev = lambda width: pl.BlockSpec((ts, width), lambda i: (nt - 1 - i, 0))
    halo_blocks = ts // HALO
    in_specs = [rev(e),
                pl.BlockSpec((nsplit, ts, e), lambda i: (0, nt - 1 - i, 0)),
                pl.BlockSpec((nsplit, HALO, e), lambda i: (0, jnp.maximum((nt - 1 - i) * halo_blocks - 1, 0), 0))]
    in_specs += [_whole(*a.shape) for a in params]
    in_specs += [_whole(d, nsplit * e), rev(d), _whole(1, d), rev(d)]
    scratch = [pltpu.VMEM((HALO, e), F32)]
    if with_wgrad:
        in_specs += [rev(d)]
        out_specs = [_whole(d, nsplit * e)]
        out_shape = [jax.ShapeDtypeStruct((d, nsplit * e), BF16)]
        scratch += [pltpu.VMEM((nsplit, ts, e), BF16), pltpu.VMEM((d, nsplit * e), F32)]
    else:
        out_specs = [pl.BlockSpec((nsplit, ts, e), lambda i: (0, nt - 1 - i, 0))]
        out_shape = [jax.ShapeDtypeStruct((nsplit, s, e), BF16)]
    out_specs += [rev(d), _whole(1, d)]
    out_shape += [jax.ShapeDtypeStruct((s, d), F32), jax.ShapeDtypeStruct((1, d), F32)]
    if is_conv:
        out_specs += [_whole(3, e)]
        out_shape += [jax.ShapeDtypeStruct((3, e), F32)]
    else:
        out_specs += [_whole(N_POOL_GROUPS, gdim, gdim), _whole(1, e)]
        out_shape += [jax.ShapeDtypeStruct((N_POOL_GROUPS, gdim, gdim), BF16), jax.ShapeDtypeStruct((1, e), F32)]
        scratch += [pltpu.VMEM((N_POOL_GROUPS, gdim, gdim), F32)]
    return pl.pallas_call(
        body, name="mixer_bwd",
        grid=(nt,),
        in_specs=in_specs, out_specs=out_specs, out_shape=out_shape, scratch_shapes=scratch,
        compiler_params=_cparams("arbitrary"),
    )(do, proj, proj, *params, w_in, h, g, dh1, *([hn] if with_wgrad else []))


def _proj_wgrad(hn, dproj):
    s, d = hn.shape
    nsplit, _, e = dproj.shape
    ts = min(WGRAD_ROW_TILE, s)
    last = s // ts - 1

    def body(hn_ref, dp_ref, dw_ref, acc):
        i = pl.program_id(1)

        @pl.when(i == 0)
        def _():
            acc[...] = jnp.zeros_like(acc)

        acc[...] += _dot_tn(hn_ref[...], dp_ref[...])

        @pl.when(i == last)
        def _():
            dw_ref[...] = acc[...].astype(BF16)

    return pl.pallas_call(
        body, name="proj_wgrad",
        grid=(nsplit, s // ts),
        in_specs=[pl.BlockSpec((ts, d), lambda k, i: (i, 0)),
                  pl.BlockSpec((None, ts, e), lambda k, i: (k, i, 0))],
        out_specs=pl.BlockSpec((d, e), lambda k, i: (0, k)),
        out_shape=jax.ShapeDtypeStruct((d, nsplit * e), BF16),
        scratch_shapes=[pltpu.VMEM((d, e), F32)],
        compiler_params=_cparams("parallel", "arbitrary"),
    )(hn, dproj)


def _forward_backward(xs, ps, target, full, conv_w, scale_w, norm_mix, ple_norm, final_norm, exchange):
    depth = len(full)
    row = lambda a, i: a[i][None, :]
    mixer_of = lambda i: ("conv", conv_w[i // 2]) if i % 2 == 0 else ("pool", full[i]["w_grp"], row(scale_w, i // 2))

    saved = []
    h = xs
    for i in range(depth):
        w = full[i]
        head = (target, final_norm[None, :]) if i == depth - 1 else None
        h_next, acts = _layer_fwd(h, row(norm_mix, i), w["w_in"], mixer_of(i), w["w_out"], row(ple_norm, i),
                                  w["gate"], ps, w["proj"], i, head)
        saved.append((h, *acts))
        h = h_next
    dh, loss_row, d_final = h

    d_norm, d_ple_norm, d_conv, d_scale, sent = [None] * depth, [None] * depth, [], [], [None] * depth
    after = d_final
    for i in reversed(range(depth)):
        w = full[i]
        h_in, proj, hn, o, h1, gl, pp = saved[i]
        g = {}
        dh1, do, g["proj"], g["gate"], g["w_out"], d_ple_norm[i] = _out_ple_bwd(
            dh, gl, pp, h1, ps, o, row(ple_norm, i), w["gate"], w["w_out"], i, after)
        if i % 2 == 0:
            g["w_in"], dh, d_norm[i], dw_conv = _mixer_bwd(
                do, proj, mixer_of(i), w["w_in"], h_in, row(norm_mix, i), dh1, hn=hn)
            d_conv.insert(0, dw_conv)
        else:
            g["w_in"], dh, d_norm[i], g["w_grp"], dw_scale = _mixer_bwd(
                do, proj, mixer_of(i), w["w_in"], h_in, row(norm_mix, i), dh1, hn=hn)
            d_scale.insert(0, dw_scale)
        after = g["w_in"]
        sent[i] = exchange(i, g)
    return loss_row, dh, sent, (d_norm, d_ple_norm, d_final, d_conv, d_scale)


VMEM_SPEC = pl.BlockSpec(memory_space=pltpu.VMEM)

FLIPS = [(fx, fy, fc) for fx in (0, 1) for fy in (0, 1) for fc in (0, 1)][1:]
SHARD_AXIS = {"w_in": 1, "w_out": 0, "w_grp": 1, "gate": 0, "proj": 1}


def _my_place():
    return lax.axis_index("x"), lax.axis_index("y"), lax.axis_index("c")


def _position(place):
    x, y, c = place
    return 4 * x + 2 * y + c


def _flip(place, flips):
    return tuple(1 - v if f else v for v, f in zip(place, flips))


def _shard_of(ref, axis, pos, n):
    idx = [slice(None)] * len(ref.shape)
    idx[axis] = pl.ds(pl.multiple_of(pos * n, n), n)
    return ref.at[tuple(idx)]


def _sequencer_mesh():
    return plsc.ScalarSubcoreMesh(axis_name="sequencer", num_cores=1)


def _handshake(peers):
    barrier = pltpu.get_barrier_semaphore()
    for peer in peers:
        pl.semaphore_signal(barrier, inc=1, device_id=peer, device_id_type=MESH)
    pl.semaphore_wait(barrier, len(peers))


def _all_gather_layer(shards, collective_id):
    names = list(shards)
    nt = len(names)
    axes = [SHARD_AXIS[k] for k in names]
    widths = [shards[k].shape[SHARD_AXIS[k]] for k in names]

    def full_shape(k):
        shp = list(shards[k].shape)
        shp[SHARD_AXIS[k]] *= N_DEV
        return tuple(shp)

    def body(*refs):
        ins, outs = refs[:nt], refs[nt:2 * nt]
        send_sems, recv_sems, local_sem = refs[2 * nt:]
        me = _my_place()
        x, y, c = me
        sibling = (x, y, 1 - c)
        chips = [(1 - x, y), (x, 1 - y), (1 - x, 1 - y)]
        _handshake([sibling] + [(*chip, c) for chip in chips])

        def block(t, place):
            return _shard_of(outs[t], axes[t], _position(place), widths[t])

        def copy(t, k, place, to, src=None):
            return pltpu.make_async_remote_copy(
                src_ref=block(t, place) if src is None else src, dst_ref=block(t, place),
                send_sem=send_sems.at[k], recv_sem=recv_sems.at[k], device_id=to, device_id_type=MESH)

        mine = [pltpu.make_async_copy(ins[t], block(t, me), local_sem) for t in range(nt)]
        for cp in mine:
            cp.start()
        first = []
        for j, chip in enumerate(chips):
            first += [copy(t, 1 + j, me, (*chip, c), src=ins[t]) for t in range(nt)]
        first += [copy(t, 0, me, sibling, src=ins[t]) for t in range(nt)]
        for cp in first:
            cp.start()
        passed = []
        for j, chip in enumerate(chips):
            for t in range(nt):
                copy(t, 1 + j, (*chip, c), me).wait_recv()
            for t in range(nt):
                fwd = copy(t, 4 + j, (*chip, c), sibling)
                fwd.start()
                passed.append(fwd)
        for t in range(nt):
            copy(t, 0, sibling, me).wait_recv()
        for j, chip in enumerate(chips):
            for t in range(nt):
                copy(t, 4 + j, (*chip, 1 - c), me).wait_recv()
        for cp in first + passed:
            cp.wait_send()
        for cp in mine:
            cp.wait()

    outs = pl.kernel(
        body, name=f"all_gather_layer_{collective_id}",
        out_type=[jax.ShapeDtypeStruct(full_shape(k), shards[k].dtype) for k in names],
        mesh=_sequencer_mesh(),
        scratch_types=[pltpu.SemaphoreType.DMA((7,)), pltpu.SemaphoreType.DMA((7,)), pltpu.SemaphoreType.DMA],
        compiler_params=pltpu.CompilerParams(collective_id=collective_id),
    )(*[shards[k] for k in names])
    return dict(zip(names, outs))


def _exchange_layer(grads, collective_id):
    names = list(grads)
    nt = len(names)
    axes = [SHARD_AXIS[k] for k in names]
    widths = [grads[k].shape[SHARD_AXIS[k]] // N_DEV for k in names]

    def slot_shape(t):
        shp = list(grads[names[t]].shape)
        shp[axes[t]] = widths[t]
        return (N_DEV, *shp)

    def body(*refs):
        ins, outs = refs[:nt], refs[nt:2 * nt]
        send_sems, recv_sems, local_sem = refs[2 * nt:]
        me = _my_place()
        mine = _position(me)
        _handshake([_flip(me, flips) for flips in FLIPS])
        local = [pltpu.make_async_copy(_shard_of(ins[t], axes[t], mine, widths[t]), outs[t].at[mine], local_sem)
                 for t in range(nt)]
        for cp in local:
            cp.start()
        copies = []
        for k, flips in enumerate(FLIPS):
            peer = _flip(me, flips)
            for t in range(nt):
                cp = pltpu.make_async_remote_copy(
                    src_ref=_shard_of(ins[t], axes[t], _position(peer), widths[t]), dst_ref=outs[t].at[mine],
                    send_sem=send_sems.at[k], recv_sem=recv_sems.at[k], device_id=peer, device_id_type=MESH)
                cp.start()
                copies.append(cp)
        for cp in copies:
            cp.wait()
        for cp in local:
            cp.wait()

    outs = pl.kernel(
        body, name=f"exchange_layer_{collective_id}",
        out_type=[jax.ShapeDtypeStruct(slot_shape(t), BF16) for t in range(nt)],
        mesh=_sequencer_mesh(),
        scratch_types=[pltpu.SemaphoreType.DMA((7,)), pltpu.SemaphoreType.DMA((7,)), pltpu.SemaphoreType.DMA],
        compiler_params=pltpu.CompilerParams(collective_id=collective_id),
    )(*[grads[k] for k in names])
    return dict(zip(names, outs))


def _gather_rows(buf, reduce):
    r, c = buf.shape

    def body(in_ref, out_ref, *scratch):
        if reduce:
            all_ref, send_sems, recv_sems = scratch
        else:
            all_ref = out_ref
            send_sems, recv_sems = scratch
        me = _my_place()
        all_ref[_position(me)] = in_ref[...]
        copies = []
        for k, flips in enumerate(FLIPS):
            cp = pltpu.make_async_remote_copy(
                src_ref=in_ref, dst_ref=all_ref.at[_position(me)],
                send_sem=send_sems.at[k], recv_sem=recv_sems.at[k], device_id=_flip(me, flips), device_id_type=MESH)
            cp.start()
            copies.append(cp)
        for cp in copies:
            cp.wait()
        if reduce:
            total = all_ref[0]
            for j in range(1, N_DEV):
                total = total + all_ref[j]
            out_ref[...] = total

    return pl.pallas_call(
        body, name="sum_rows" if reduce else "gather_rows",
        in_specs=[VMEM_SPEC], out_specs=VMEM_SPEC,
        out_shape=jax.ShapeDtypeStruct((r, c) if reduce else (N_DEV, r, c), F32),
        scratch_shapes=([pltpu.VMEM((N_DEV, r, c), F32)] if reduce else [])
        + [pltpu.SemaphoreType.DMA((7,)), pltpu.SemaphoreType.DMA((7,))],
    )(buf)


def _adamw_math(w, g, m, v):
    m = ADAM_B1 * m + (1.0 - ADAM_B1) * g
    v = ADAM_B2 * v + (1.0 - ADAM_B2) * (g * g)
    m_hat = m / (1.0 - ADAM_B1 ** ADAM_STEP)
    v_hat = v / (1.0 - ADAM_B2 ** ADAM_STEP)
    delta = -ADAM_LR * (m_hat / (jnp.sqrt(v_hat) + ADAM_EPS) + ADAM_WD * w)
    return delta, m, v


def _run_behind(x, token):
    def body(x_ref, token_ref, out_ref):
        out_ref[...] = jnp.zeros_like(out_ref)

    any_spec = pl.BlockSpec(memory_space=pl.ANY)
    return pl.pallas_call(
        body, name="run_behind",
        in_specs=[any_spec, any_spec], out_specs=VMEM_SPEC,
        out_shape=jax.ShapeDtypeStruct((8, 128), F32),
    )(x, token)


def _adamw_pieces(pieces, w, m, v, after):
    shape = w.shape
    nl = shape[0]
    cols = shape[-1]
    rows = w.size // (nl * cols)
    tr = min(256 // nl, rows)
    flat3 = lambda a: a.reshape(nl, rows, cols)

    def body(*refs):
        p_refs = refs[:nl]
        w_ref, m_ref, v_ref, after_ref, g_ref, d_ref, nm_ref, nv_ref = refs[nl:]
        for l in range(nl):
            g = p_refs[l][0].astype(F32)
            for j in range(1, N_DEV):
                g = g + p_refs[l][j].astype(F32)
            g_ref[l] = g
            d_ref[l], nm_ref[l], nv_ref[l] = _adamw_math(w_ref[l], g, m_ref[l], v_ref[l])

    blk = pl.BlockSpec((nl, tr, cols), lambda i: (0, i, 0))
    outs = pl.pallas_call(
        body, name="adamw_pieces",
        grid=(rows // tr,),
        in_specs=[pl.BlockSpec((N_DEV, tr, cols), lambda i: (0, i, 0))] * nl
        + [blk, blk, blk, pl.BlockSpec(memory_space=pl.ANY)],
        out_specs=[blk] * 4,
        out_shape=[jax.ShapeDtypeStruct((nl, rows, cols), F32)] * 4,
        compiler_params=_cparams("parallel"),
    )(*[a.reshape(N_DEV, rows, cols) for a in pieces], flat3(w), flat3(m), flat3(v), after)
    return [a.reshape(shape) for a in outs]


def _adamw_small(g, w, m, v):
    shape = w.shape
    two = lambda a: a.reshape(-1, shape[-1])

    def body(g_ref, w_ref, m_ref, v_ref, d_ref, nm_ref, nv_ref):
        d_ref[...], nm_ref[...], nv_ref[...] = _adamw_math(w_ref[...], g_ref[...], m_ref[...], v_ref[...])

    outs = pl.pallas_call(
        body, name="adamw_small",
        in_specs=[VMEM_SPEC] * 4, out_specs=[VMEM_SPEC] * 3,
        out_shape=[jax.ShapeDtypeStruct(two(w).shape, F32)] * 3,
    )(two(g), two(w), two(m), two(v))
    return [a.reshape(shape) for a in outs]


WEIGHTS = ("norm_mix", "a_w_in", "a_w_conv", "a_w_out", "b_w_in", "b_w_grp", "b_scale", "b_w_out",
           "ple_norm", "ple_w_gate", "ple_w_proj", "final_norm")
SMALL_ROWS = 24
GATHER_ID = 0
EXCHANGE_ID = 4
LAST_EXCHANGE_ID = 8


def kernel(x, p, norm_mix, a_w_in, a_w_conv, a_w_out, b_w_in, b_w_grp, b_scale, b_w_out, ple_norm, ple_w_gate, ple_w_proj, final_norm, loss_target, m_norm_mix, m_a_w_in, m_a_w_conv, m_a_w_out, m_b_w_in, m_b_w_grp, m_b_scale, m_b_w_out, m_ple_norm, m_ple_w_gate, m_ple_w_proj, m_final_norm, v_norm_mix, v_a_w_in, v_a_w_conv, v_a_w_out, v_b_w_in, v_b_w_grp, v_b_scale, v_b_w_out, v_ple_norm, v_ple_w_gate, v_ple_w_proj, v_final_norm):
    wts = dict(norm_mix=norm_mix, a_w_in=a_w_in, a_w_conv=a_w_conv, a_w_out=a_w_out, b_w_in=b_w_in, b_w_grp=b_w_grp,
               b_scale=b_scale, b_w_out=b_w_out, ple_norm=ple_norm, ple_w_gate=ple_w_gate, ple_w_proj=ple_w_proj,
               final_norm=final_norm)
    mom = dict(norm_mix=m_norm_mix, a_w_in=m_a_w_in, a_w_conv=m_a_w_conv, a_w_out=m_a_w_out, b_w_in=m_b_w_in,
               b_w_grp=m_b_w_grp, b_scale=m_b_scale, b_w_out=m_b_w_out, ple_norm=m_ple_norm, ple_w_gate=m_ple_w_gate,
               ple_w_proj=m_ple_w_proj, final_norm=m_final_norm)
    var = dict(norm_mix=v_norm_mix, a_w_in=v_a_w_in, a_w_conv=v_a_w_conv, a_w_out=v_a_w_out, b_w_in=v_b_w_in,
               b_w_grp=v_b_w_grp, b_scale=v_b_scale, b_w_out=v_b_w_out, ple_norm=v_ple_norm, ple_w_gate=v_ple_w_gate,
               ple_w_proj=v_ple_w_proj, final_norm=v_final_norm)
    d = x.shape[2]
    depth = norm_mix.shape[0]
    n_a, n_b = a_w_conv.shape[0], b_scale.shape[0]
    cw = a_w_conv.shape[2]
    pos = _position(_my_place())

    def layer_matrices(i):
        j = i // 2
        mixer = {"w_in": ("a_w_in", j), "w_out": ("a_w_out", j)} if i % 2 == 0 else \
                {"w_in": ("b_w_in", j), "w_grp": ("b_w_grp", j), "w_out": ("b_w_out", j)}
        return {**mixer, "gate": ("ple_w_gate", i), "proj": ("ple_w_proj", i)}

    full = [_all_gather_layer({k: wts[name][idx].astype(BF16) for k, (name, idx) in layer_matrices(i).items()},
                              GATHER_ID + i) for i in range(depth)]
    vec_rows = jnp.concatenate([a_w_conv.reshape(-1, cw), b_scale], axis=0)
    vecs = _gather_rows(vec_rows, reduce=False)
    n_conv = 3 * n_a
    conv_w = vecs[:, :n_conv].transpose(1, 0, 2).reshape(n_a, 3, N_DEV * cw)
    scale_w = vecs[:, n_conv:].transpose(1, 0, 2).reshape(n_b, N_DEV * cw)

    def exchange(i, g):
        if i > 0:
            return _exchange_layer(g, EXCHANGE_ID + i)
        early = {k: a for k, a in g.items() if k != "w_in"}
        return {**_exchange_layer(early, EXCHANGE_ID), **_exchange_layer({"w_in": g["w_in"]}, LAST_EXCHANGE_ID)}

    loss_row, dx, sent, (d_norm, d_ple_norm, d_final, d_conv, d_scale) = _forward_backward(
        x[0], p[:, 0], loss_target[0], full, conv_w, scale_w, norm_mix, ple_norm, final_norm, exchange)
    pieces = {name: [None] * wts[name].shape[0] for name in WEIGHTS if wts[name].ndim >= 3 and name != "a_w_conv"}
    for i in range(depth):
        for k, (name, idx) in layer_matrices(i).items():
            pieces[name][idx] = sent[i][k]

    pad = lambda a: jnp.pad(a, ((0, 0), (0, d - a.shape[1])))
    small = jnp.concatenate(d_norm + d_ple_norm + [d_final] + d_conv + d_scale + [pad(loss_row)], axis=0)
    small = jnp.pad(small, ((0, SMALL_ROWS - small.shape[0]), (0, 0)))
    total = _gather_rows(small, reduce=True)
    o = 0
    gsum = {}
    gsum["norm_mix"] = total[o:o + depth]; o += depth
    gsum["ple_norm"] = total[o:o + depth]; o += depth
    gsum["final_norm"] = total[o]; o += 1
    conv_full = total[o:o + n_conv].reshape(n_a, 3, d); o += n_conv
    scale_full = total[o:o + n_b]; o += n_b
    loss = total[o, 0]
    gsum["a_w_conv"] = lax.dynamic_slice_in_dim(conv_full, pos * cw, cw, axis=2)
    gsum["b_scale"] = lax.dynamic_slice_in_dim(scale_full, pos * cw, cw, axis=1)

    token = total
    for i in reversed(range(depth)):
        token = _run_behind(sent[i]["w_out"], token)
    last_token = _run_behind(sent[0]["w_in"], token)
    grad, delta, new_m, new_v = {}, {}, {}, {}
    for k in sorted(WEIGHTS, key=lambda name: name == "a_w_in"):
        if k in pieces:
            grad[k], delta[k], new_m[k], new_v[k] = _adamw_pieces(
                pieces[k], wts[k], mom[k], var[k], last_token if k == "a_w_in" else token)
        else:
            grad[k] = gsum[k]
            delta[k], new_m[k], new_v[k] = _adamw_small(gsum[k], wts[k], mom[k], var[k])
    return (loss, dx[None], *[grad[k] for k in WEIGHTS], *[delta[k] for k in WEIGHTS],
            *[new_m[k] for k in WEIGHTS], *[new_v[k] for k in WEIGHTS])
```

```python
import jax
import jax.numpy as jnp
from jax import lax
from jax.experimental import pallas as pl
from jax.experimental.pallas import tpu as pltpu
from jax.experimental.pallas import tpu_sc as plsc

F32 = jnp.float32
BF16 = jnp.bfloat16
MESH = pl.DeviceIdType.MESH

RMS_EPS = 1e-6
POOL_WINDOWS = (2, 4, 8, 16)
N_POOL_GROUPS = len(POOL_WINDOWS)
ADAM_LR = 0.001
ADAM_B1 = 0.9
ADAM_B2 = 0.999
ADAM_EPS = 1e-08
ADAM_WD = 0.01
ADAM_STEP = 10
N_DEV = 8

HALO = 16
LAYER_ROW_TILE = 256
POOL_FWD_ROW_TILE = 512
BWD_ROW_TILE = 512
VMEM_LIMIT = 56 * 1024 * 1024


def _cparams(*sem):
    return pltpu.CompilerParams(dimension_semantics=sem, vmem_limit_bytes=VMEM_LIMIT)


def _dot(a, b):
    return jnp.dot(a, b, preferred_element_type=F32)


def _dot_nt(a, b):
    return lax.dot_general(a, b, (((1,), (1,)), ((), ())), preferred_element_type=F32)


def _dot_tn(a, b):
    return lax.dot_general(a, b, (((0,), (0,)), ((), ())), preferred_element_type=F32)


def _rms_stats(x):
    r = lax.rsqrt(jnp.mean(x * x, axis=-1, keepdims=True) + RMS_EPS)
    return x * r, r


def _rms_bwd(dy, xh, r, g):
    a = dy * g
    return r * (a - xh * jnp.mean(a * xh, axis=-1, keepdims=True))


def _sigmoid(x):
    return 1.0 / (1.0 + jnp.exp(-x))


def _shift_down(x, k):
    return pltpu.roll(x, k, 0)


def _shift_up(x, k):
    return pltpu.roll(x, x.shape[0] - k, 0)


def _conv_taps(u, u_prev):
    uu = jnp.concatenate([u_prev, u], axis=0)
    return _shift_down(uu, 1)[HALO:], _shift_down(uu, 2)[HALO:]


def _window_mean_minus(uu, row0, window):
    acc = uu
    span = 1
    while span < window:
        acc = acc + _shift_down(acc, span)
        span *= 2
    return acc[HALO:] * _inv_count(uu.shape[0] - HALO, row0, window) - uu[HALO:]


def _inv_count(rows, row0, window):
    t = row0 + lax.broadcasted_iota(jnp.int32, (rows, 1), 0)
    return 1.0 / jnp.minimum(t + 1, window).astype(F32)


def _whole(*shape):
    return pl.BlockSpec(shape, lambda i: (0,) * len(shape), pipeline_mode=pl.Buffered(1))


def _layer_fwd(h, g, w_in, mixer, w_out, pn, w_gate, p_all, w_proj, layer, head=None):
    s, d = h.shape
    n = w_in.shape[1]
    e = w_out.shape[0]
    nsplit = n // e
    gdim = e // N_POOL_GROUPS
    pdim = p_all.shape[2]
    is_conv = mixer[0] == "conv"
    ts = min(LAYER_ROW_TILE if is_conv else POOL_FWD_ROW_TILE, s)
    params = mixer[1:]
    head = tuple(head or ())

    def body(*refs):
        h_ref, g_ref, win_ref = refs[:3]
        mix_refs = refs[3:3 + len(params)]
        wo_ref, pn_ref, wg_ref, p_ref, wp_ref = refs[3 + len(params):8 + len(params)]
        head_refs = refs[8 + len(params):8 + len(params) + len(head)]
        proj_ref, hn_ref, o_ref, h1_ref, h2_ref, gl_ref, pp_ref = refs[8 + len(params) + len(head):][:7]
        carry_ref = refs[-1]
        i = pl.program_id(0)

        @pl.when(i == 0)
        def _():
            carry_ref[...] = jnp.zeros_like(carry_ref)

        x = h_ref[...]
        xh, _ = _rms_stats(x)
        hn = (xh * g_ref[...]).astype(BF16)
        hn_ref[...] = hn
        parts = []
        for k in range(nsplit):
            part = _dot(hn, win_ref[:, k * e:(k + 1) * e])
            proj_ref[k] = part.astype(BF16)
            parts.append(part)
        prev = carry_ref[...]
        if is_conv:
            b, c, v, z = parts
            w_ref, = mix_refs
            u = c * v
            u1, u2 = _conv_taps(u, prev)
            mixed = b * (w_ref[0:1, :] * u2 + w_ref[1:2, :] * u1 + w_ref[2:3, :] * u)
        else:
            u, z = parts
            wgrp_ref, sc_ref = mix_refs
            uu = jnp.concatenate([prev, u], axis=0)
            cols = []
            for gi, window in enumerate(POOL_WINDOWS):
                dg = _window_mean_minus(uu[:, gi * gdim:(gi + 1) * gdim], i * ts, window)
                cols.append(_dot(dg.astype(BF16), wgrp_ref[gi]))
            mixed = jnp.concatenate(cols, axis=1) * sc_ref[...]
        carry_ref[...] = u[ts - HALO:]
        o = ((z * _sigmoid(z)) * mixed).astype(BF16)
        o_ref[...] = o
        h1 = x + _dot(o, wo_ref[...])
        h1_ref[...] = h1
        xh1, _ = _rms_stats(h1)
        gl = _dot((xh1 * pn_ref[...]).astype(BF16), wg_ref[...])
        pp = _dot(p_ref[...].astype(BF16), wp_ref[...])
        gl_ref[...] = gl.astype(BF16)
        pp_ref[...] = pp.astype(BF16)
        h2 = h1 + _sigmoid(gl) * pp
        if not head:
            h2_ref[...] = h2
            return
        t_ref, gain_ref = head_refs
        loss_ref, dgain_ref = refs[-3], refs[-2]

        @pl.when(i == 0)
        def _():
            loss_ref[...] = jnp.zeros_like(loss_ref)
            dgain_ref[...] = jnp.zeros_like(dgain_ref)

        gain = gain_ref[...]
        yh, r = _rms_stats(h2)
        err = yh * gain - t_ref[...]
        loss_ref[...] += jnp.full(loss_ref.shape, (0.5 / d) * jnp.sum(err * err), F32)
        dy = err * (1.0 / d)
        dgain_ref[...] += jnp.sum(dy * yh, axis=0, keepdims=True)
        h2_ref[...] = _rms_bwd(dy, yh, r, gain)

    row = lambda width: pl.BlockSpec((ts, width), lambda i: (i, 0))
    mix_specs = [_whole(*a.shape) for a in params]
    head_specs = [row(d), _whole(1, d)] if head else []
    head_out_specs = [_whole(1, 128), _whole(1, d)] if head else []
    head_out_shape = [jax.ShapeDtypeStruct((1, 128), F32), jax.ShapeDtypeStruct((1, d), F32)] if head else []
    outs = pl.pallas_call(
        body, name="layer_fwd",
        grid=(s // ts,),
        in_specs=[row(d), _whole(1, d), _whole(d, n)] + mix_specs
        + [_whole(e, d), _whole(1, d), _whole(d, d),
           pl.BlockSpec((None, ts, pdim), lambda i: (layer, i, 0)), _whole(pdim, d)] + head_specs,
        out_specs=[pl.BlockSpec((nsplit, ts, e), lambda i: (0, i, 0)),
                   row(d), row(e), row(d), row(d), row(d), row(d)] + head_out_specs,
        out_shape=[jax.ShapeDtypeStruct((nsplit, s, e), BF16), jax.ShapeDtypeStruct((s, d), BF16),
                   jax.ShapeDtypeStruct((s, e), BF16), jax.ShapeDtypeStruct((s, d), F32),
                   jax.ShapeDtypeStruct((s, d), F32), jax.ShapeDtypeStruct((s, d), BF16),
                   jax.ShapeDtypeStruct((s, d), BF16)] + head_out_shape,
        scratch_shapes=[pltpu.VMEM((HALO, e), F32)],
        compiler_params=_cparams("arbitrary"),
    )(h, g, w_in, *params, w_out, pn, w_gate, p_all, w_proj, *head)
    proj, hn, o, h1, h2, gl, pp = outs[:7]
    return (h2, *outs[7:]) if head else h2, (proj, hn, o, h1, gl, pp)


def _out_ple_bwd(dh2, gl, pp, h1, p_all, o, pn, wgate, wout, layer):
    s, d = dh2.shape
    e = o.shape[1]
    pdim = p_all.shape[2]
    ts = min(BWD_ROW_TILE, s)
    last = s // ts - 1

    def body(dh2_ref, gl_ref, pp_ref, h1_ref, p_ref, o_ref, pn_ref, wg_ref, wo_ref,
             dh1_ref, do_ref, dwp_ref, dwg_ref, dwo_ref, dpn_ref, awp, awg, awo):
        i = pl.program_id(0)

        @pl.when(i == 0)
        def _():
            awp[...] = jnp.zeros_like(awp)
            awg[...] = jnp.zeros_like(awg)
            awo[...] = jnp.zeros_like(awo)
            dpn_ref[...] = jnp.zeros_like(dpn_ref)

        dh2 = dh2_ref[...]
        gate = _sigmoid(gl_ref[...].astype(F32))
        dpp = (dh2 * gate).astype(BF16)
        dgl = (dh2 * pp_ref[...].astype(F32) * gate * (1.0 - gate)).astype(BF16)
        xh, r = _rms_stats(h1_ref[...])
        pn = pn_ref[...]
        awp[...] += _dot_tn(p_ref[...].astype(BF16), dpp)
        awg[...] += _dot_tn((xh * pn).astype(BF16), dgl)
        dr = _dot_nt(dgl, wg_ref[...])
        dpn_ref[...] += jnp.sum(dr * xh, axis=0, keepdims=True)
        dh1 = dh2 + _rms_bwd(dr, xh, r, pn)
        dh1_ref[...] = dh1
        dh1b = dh1.astype(BF16)
        do_ref[...] = _dot_nt(dh1b, wo_ref[...]).astype(BF16)
        awo[...] += _dot_tn(o_ref[...], dh1b)

        @pl.when(i == last)
        def _():
            dwp_ref[...] = awp[...].astype(BF16)
            dwg_ref[...] = awg[...].astype(BF16)
            dwo_ref[...] = awo[...].astype(BF16)

    row = lambda width: pl.BlockSpec((ts, width), lambda i: (i, 0))
    return pl.pallas_call(
        body, name="out_ple_bwd",
        grid=(s // ts,),
        in_specs=[row(d), row(d), row(d), row(d),
                  pl.BlockSpec((None, ts, pdim), lambda i: (layer, i, 0)),
                  row(e), _whole(1, d), _whole(d, d), _whole(e, d)],
        out_specs=[row(d), row(e), _whole(pdim, d), _whole(d, d), _whole(e, d), _whole(1, d)],
        out_shape=[jax.ShapeDtypeStruct((s, d), F32), jax.ShapeDtypeStruct((s, e), BF16),
                   jax.ShapeDtypeStruct((pdim, d), BF16), jax.ShapeDtypeStruct((d, d), BF16),
                   jax.ShapeDtypeStruct((e, d), BF16), jax.ShapeDtypeStruct((1, d), F32)],
        scratch_shapes=[pltpu.VMEM((pdim, d), F32), pltpu.VMEM((d, d), F32), pltpu.VMEM((e, d), F32)],
        compiler_params=_cparams("arbitrary"),
    )(dh2, gl, pp, h1, p_all, o, pn, wgate, wout)


def _mixer_bwd(do, proj, hn, mixer, w_in, h, g, dh1):
    s, d = h.shape
    nsplit, _, e = proj.shape
    gdim = e // N_POOL_GROUPS
    ts = min(LAYER_ROW_TILE, s)
    nt = s // ts
    is_conv = mixer[0] == "conv"
    params = mixer[1:]
    n_mix_out = 1 if is_conv else 2

    def body(*refs):
        refs = list(refs)
        take = lambda n: [refs.pop(0) for _ in range(n)]
        do_ref, p_ref, ph_ref = take(3)
        mix_refs = take(len(params))
        win_ref, h_ref, g_ref, dh1_ref, hn_ref = take(5)
        dwin_ref, dh_ref, dg_ref = take(3)
        mix_out = take(n_mix_out)
        carry_ref, dp_ref, acc_ref = take(3)
        i = pl.program_id(0)
        tile = nt - 1 - i

        @pl.when(i == 0)
        def _():
            for ref in [carry_ref, dg_ref, acc_ref] + mix_out[-1:] + refs:
                ref[...] = jnp.zeros_like(ref)

        dof = do_ref[...].astype(F32)
        nxt = carry_ref[...]
        if is_conv:
            w_ref, = mix_refs
            dw_ref, = mix_out
            w0, w1, w2 = w_ref[0:1, :], w_ref[1:2, :], w_ref[2:3, :]
            b, c, v, z = [p_ref[k].astype(F32) for k in range(4)]
            u = c * v
            u_prev = jnp.where(tile == 0, 0.0, ph_ref[1].astype(F32) * ph_ref[2].astype(F32))
            u1, u2 = _conv_taps(u, u_prev)
            conv = w0 * u2 + w1 * u1 + w2 * u
            sig = _sigmoid(z)
            sz = z * sig
            dy = dof * sz
            dp_ref[3] = (dof * (b * conv) * (sig + sz * (1.0 - sig))).astype(BF16)
            dp_ref[0] = (dy * conv).astype(BF16)
            dconv = dy * b
            dw_ref[0:1, :] += jnp.sum(dconv * u2, axis=0, keepdims=True)
            dw_ref[1:2, :] += jnp.sum(dconv * u1, axis=0, keepdims=True)
            dw_ref[2:3, :] += jnp.sum(dconv * u, axis=0, keepdims=True)
            dcc = jnp.concatenate([dconv, nxt], axis=0)
            du = w2 * dconv + w1 * _shift_up(dcc, 1)[:ts] + w0 * _shift_up(dcc, 2)[:ts]
            carry_ref[...] = dconv[:HALO]
            dp_ref[1] = (du * v).astype(BF16)
            dp_ref[2] = (du * c).astype(BF16)
        else:
            wgrp_ref, sc_ref = mix_refs
            dsc_ref = mix_out[1]
            agrp, = refs
            u = p_ref[0].astype(F32)
            z = p_ref[1].astype(F32)
            u_prev = jnp.where(tile == 0, 0.0, ph_ref[0].astype(F32))
            uu = jnp.concatenate([u_prev, u], axis=0)
            sig = _sigmoid(z)
            sz = z * sig
            dm = dof * sz
            dsilu = dof * (sig + sz * (1.0 - sig))
            for gi, window in enumerate(POOL_WINDOWS):
                cols = slice(gi * gdim, (gi + 1) * gdim)
                w = wgrp_ref[gi]
                scale = sc_ref[:, cols]
                db = _window_mean_minus(uu[:, cols], tile * ts, window).astype(BF16)
                mr = _dot(db, w)
                dp_ref[1, :, cols] = (dsilu[:, cols] * (mr * scale)).astype(BF16)
                dmg = dm[:, cols]
                dmr = (dmg * scale).astype(BF16)
                agrp[gi] += _dot_tn(db, dmr)
                dsc_ref[:, cols] += jnp.sum(dmg * mr, axis=0, keepdims=True)
                dd = _dot_nt(dmr, w)
                ddq = dd * _inv_count(ts, tile * ts, window)
                acc = jnp.concatenate([ddq, nxt[:, cols]], axis=0)
                span = 1
                while span < window:
                    acc = acc + _shift_up(acc, span)
                    span *= 2
                carry_ref[:, cols] = ddq[:HALO]
                dp_ref[0, :, cols] = (acc[:ts] - dd).astype(BF16)

        dhn = _dot_nt(dp_ref[0], win_ref[:, 0:e])
        for k in range(1, nsplit):
            dhn += _dot_nt(dp_ref[k], win_ref[:, k * e:(k + 1) * e])
        xh, r = _rms_stats(h_ref[...])
        dg_ref[...] += jnp.sum(dhn * xh, axis=0, keepdims=True)
        dh_ref[...] = dh1_ref[...] + _rms_bwd(dhn, xh, r, g_ref[...])
        hn_tile = hn_ref[...]
        for k in range(nsplit):
            acc_ref[:, k * e:(k + 1) * e] += _dot_tn(hn_tile, dp_ref[k])

        @pl.when(i == nt - 1)
        def _():
            dwin_ref[...] = acc_ref[...].astype(BF16)
            if not is_conv:
                mix_out[0][...] = refs[0][...].astype(BF16)

    rev = lambda width: pl.BlockSpec((ts, width), lambda i: (nt - 1 - i, 0))
    halo_blocks = ts // HALO
    in_specs = [rev(e),
                pl.BlockSpec((nsplit, ts, e), lambda i: (0, nt - 1 - i, 0)),
                pl.BlockSpec((nsplit, HALO, e), lambda i: (0, jnp.maximum((nt - 1 - i) * halo_blocks - 1, 0), 0))]
    in_specs += [_whole(*a.shape) for a in params]
    in_specs += [_whole(d, nsplit * e), rev(d), _whole(1, d), rev(d), rev(d)]
    out_specs = [_whole(d, nsplit * e), rev(d), _whole(1, d)]
    out_shape = [jax.ShapeDtypeStruct((d, nsplit * e), BF16), jax.ShapeDtypeStruct((s, d), F32),
                 jax.ShapeDtypeStruct((1, d), F32)]
    scratch = [pltpu.VMEM((HALO, e), F32), pltpu.VMEM((nsplit, ts, e), BF16), pltpu.VMEM((d, nsplit * e), F32)]
    if is_conv:
        out_specs += [_whole(3, e)]
        out_shape += [jax.ShapeDtypeStruct((3, e), F32)]
    else:
        out_specs += [_whole(N_POOL_GROUPS, gdim, gdim), _whole(1, e)]
        out_shape += [jax.ShapeDtypeStruct((N_POOL_GROUPS, gdim, gdim), BF16), jax.ShapeDtypeStruct((1, e), F32)]
        scratch += [pltpu.VMEM((N_POOL_GROUPS, gdim, gdim), F32)]
    return pl.pallas_call(
        body, name="mixer_bwd",
        grid=(nt,),
        in_specs=in_specs, out_specs=out_specs, out_shape=out_shape, scratch_shapes=scratch,
        compiler_params=_cparams("arbitrary"),
    )(do, proj, proj, *params, w_in, h, g, dh1, hn)


def _forward_backward(xs, ps, target, full, conv_w, scale_w, norm_mix, ple_norm, final_norm, exchange):
    depth = len(full)
    row = lambda a, i: a[i][None, :]
    mixer_of = lambda i: ("conv", conv_w[i // 2]) if i % 2 == 0 else ("pool", full[i]["w_grp"], row(scale_w, i // 2))

    saved = []
    h = xs
    for i in range(depth):
        w = full[i]
        head = (target, final_norm[None, :]) if i == depth - 1 else None
        h_next, acts = _layer_fwd(h, row(norm_mix, i), w["w_in"], mixer_of(i), w["w_out"], row(ple_norm, i),
                                  w["gate"], ps, w["proj"], i, head)
        saved.append((h, *acts))
        h = h_next
    dh, loss_row, d_final = h

    d_norm, d_ple_norm, d_conv, d_scale, sent = [None] * depth, [None] * depth, [], [], [None] * depth
    for i in reversed(range(depth)):
        w = full[i]
        h_in, proj, hn, o, h1, gl, pp = saved[i]
        g = {}
        dh1, do, g["proj"], g["gate"], g["w_out"], d_ple_norm[i] = _out_ple_bwd(
            dh, gl, pp, h1, ps, o, row(ple_norm, i), w["gate"], w["w_out"], i)
        g["w_in"], dh, d_norm[i], *mixer_grads = _mixer_bwd(
            do, proj, hn, mixer_of(i), w["w_in"], h_in, row(norm_mix, i), dh1)
        if i % 2 == 0:
            d_conv.insert(0, mixer_grads[0])
        else:
            g["w_grp"] = mixer_grads[0]
            d_scale.insert(0, mixer_grads[1])
        sent[i] = exchange(i, g)
    return loss_row, dh, sent, (d_norm, d_ple_norm, d_final, d_conv, d_scale)


VMEM_SPEC = pl.BlockSpec(memory_space=pltpu.VMEM)

FLIPS = [(fx, fy, fc) for fx in (0, 1) for fy in (0, 1) for fc in (0, 1)][1:]
SHARD_AXIS = {"w_in": 1, "w_out": 0, "w_grp": 1, "gate": 0, "proj": 1}


def _my_place():
    return lax.axis_index("x"), lax.axis_index("y"), lax.axis_index("c")


def _position(place):
    x, y, c = place
    return 4 * x + 2 * y + c


def _flip(place, flips):
    return tuple(1 - v if f else v for v, f in zip(place, flips))


def _shard_of(ref, axis, pos, n):
    idx = [slice(None)] * len(ref.shape)
    idx[axis] = pl.ds(pl.multiple_of(pos * n, n), n)
    return ref.at[tuple(idx)]


def _sequencer_mesh():
    return plsc.ScalarSubcoreMesh(axis_name="sequencer", num_cores=1)


def _handshake(peers):
    barrier = pltpu.get_barrier_semaphore()
    for peer in peers:
        pl.semaphore_signal(barrier, inc=1, device_id=peer, device_id_type=MESH)
    pl.semaphore_wait(barrier, len(peers))


def _all_gather_layer(shards, collective_id):
    names = list(shards)
    nt = len(names)
    axes = [SHARD_AXIS[k] for k in names]
    widths = [shards[k].shape[SHARD_AXIS[k]] for k in names]

    def full_shape(k):
        shp = list(shards[k].shape)
        shp[SHARD_AXIS[k]] *= N_DEV
        return tuple(shp)

    def body(*refs):
        ins, outs = refs[:nt], refs[nt:2 * nt]
        send_sems, recv_sems, local_sem = refs[2 * nt:]
        me = _my_place()
        x, y, c = me
        sibling = (x, y, 1 - c)
        chips = [(1 - x, y), (x, 1 - y), (1 - x, 1 - y)]
        _handshake([sibling] + [(*chip, c) for chip in chips])

        def block(t, place):
            return _shard_of(outs[t], axes[t], _position(place), widths[t])

        def copy(t, k, place, to, src=None):
            return pltpu.make_async_remote_copy(
                src_ref=block(t, place) if src is None else src, dst_ref=block(t, place),
                send_sem=send_sems.at[k], recv_sem=recv_sems.at[k], device_id=to, device_id_type=MESH)

        mine = [pltpu.make_async_copy(ins[t], block(t, me), local_sem) for t in range(nt)]
        for cp in mine:
            cp.start()
        first = []
        for j, chip in enumerate(chips):
            first += [copy(t, 1 + j, me, (*chip, c), src=ins[t]) for t in range(nt)]
        first += [copy(t, 0, me, sibling, src=ins[t]) for t in range(nt)]
        for cp in first:
            cp.start()
        passed = []
        for j, chip in enumerate(chips):
            for t in range(nt):
                copy(t, 1 + j, (*chip, c), me).wait_recv()
            for t in range(nt):
                fwd = copy(t, 4 + j, (*chip, c), sibling)
                fwd.start()
                passed.append(fwd)
        for t in range(nt):
            copy(t, 0, sibling, me).wait_recv()
        for j, chip in enumerate(chips):
            for t in range(nt):
                copy(t, 4 + j, (*chip, 1 - c), me).wait_recv()
        for cp in first + passed:
            cp.wait_send()
        for cp in mine:
            cp.wait()

    outs = pl.kernel(
        body, name=f"all_gather_layer_{collective_id}",
        out_type=[jax.ShapeDtypeStruct(full_shape(k), shards[k].dtype) for k in names],
        mesh=_sequencer_mesh(),
        scratch_types=[pltpu.SemaphoreType.DMA((7,)), pltpu.SemaphoreType.DMA((7,)), pltpu.SemaphoreType.DMA],
        compiler_params=pltpu.CompilerParams(collective_id=collective_id),
    )(*[shards[k] for k in names])
    return dict(zip(names, outs))


def _exchange_layer(grads, collective_id):
    names = list(grads)
    nt = len(names)
    axes = [SHARD_AXIS[k] for k in names]
    widths = [grads[k].shape[SHARD_AXIS[k]] // N_DEV for k in names]

    def slot_shape(t):
        shp = list(grads[names[t]].shape)
        shp[axes[t]] = widths[t]
        return (N_DEV, *shp)

    def body(*refs):
        ins, outs = refs[:nt], refs[nt:2 * nt]
        send_sems, recv_sems, local_sem = refs[2 * nt:]
        me = _my_place()
        mine = _position(me)
        _handshake([_flip(me, flips) for flips in FLIPS])
        local = [pltpu.make_async_copy(_shard_of(ins[t], axes[t], mine, widths[t]), outs[t].at[mine], local_sem)
                 for t in range(nt)]
        for cp in local:
            cp.start()
        copies = []
        for k, flips in enumerate(FLIPS):
            peer = _flip(me, flips)
            for t in range(nt):
                cp = pltpu.make_async_remote_copy(
                    src_ref=_shard_of(ins[t], axes[t], _position(peer), widths[t]), dst_ref=outs[t].at[mine],
                    send_sem=send_sems.at[k], recv_sem=recv_sems.at[k], device_id=peer, device_id_type=MESH)
                cp.start()
                copies.append(cp)
        for cp in copies:
            cp.wait()
        for cp in local:
            cp.wait()

    outs = pl.kernel(
        body, name=f"exchange_layer_{collective_id}",
        out_type=[jax.ShapeDtypeStruct(slot_shape(t), BF16) for t in range(nt)],
        mesh=_sequencer_mesh(),
        scratch_types=[pltpu.SemaphoreType.DMA((7,)), pltpu.SemaphoreType.DMA((7,)), pltpu.SemaphoreType.DMA],
        compiler_params=pltpu.CompilerParams(collective_id=collective_id),
    )(*[grads[k] for k in names])
    return dict(zip(names, outs))


def _gather_rows(buf, reduce):
    r, c = buf.shape

    def body(in_ref, out_ref, *scratch):
        if reduce:
            all_ref, send_sems, recv_sems = scratch
        else:
            all_ref = out_ref
            send_sems, recv_sems = scratch
        me = _my_place()
        all_ref[_position(me)] = in_ref[...]
        copies = []
        for k, flips in enumerate(FLIPS):
            cp = pltpu.make_async_remote_copy(
                src_ref=in_ref, dst_ref=all_ref.at[_position(me)],
                send_sem=send_sems.at[k], recv_sem=recv_sems.at[k], device_id=_flip(me, flips), device_id_type=MESH)
            cp.start()
            copies.append(cp)
        for cp in copies:
            cp.wait()
        if reduce:
            total = all_ref[0]
            for j in range(1, N_DEV):
                total = total + all_ref[j]
            out_ref[...] = total

    return pl.pallas_call(
        body, name="sum_rows" if reduce else "gather_rows",
        in_specs=[VMEM_SPEC], out_specs=VMEM_SPEC,
        out_shape=jax.ShapeDtypeStruct((r, c) if reduce else (N_DEV, r, c), F32),
        scratch_shapes=([pltpu.VMEM((N_DEV, r, c), F32)] if reduce else [])
        + [pltpu.SemaphoreType.DMA((7,)), pltpu.SemaphoreType.DMA((7,))],
    )(buf)


def _adamw_math(w, g, m, v):
    m = ADAM_B1 * m + (1.0 - ADAM_B1) * g
    v = ADAM_B2 * v + (1.0 - ADAM_B2) * (g * g)
    m_hat = m / (1.0 - ADAM_B1 ** ADAM_STEP)
    v_hat = v / (1.0 - ADAM_B2 ** ADAM_STEP)
    delta = -ADAM_LR * (m_hat / (jnp.sqrt(v_hat) + ADAM_EPS) + ADAM_WD * w)
    return delta, m, v


def _run_behind(x, token):
    def body(x_ref, token_ref, out_ref):
        out_ref[...] = jnp.zeros_like(out_ref)

    any_spec = pl.BlockSpec(memory_space=pl.ANY)
    return pl.pallas_call(
        body, name="run_behind",
        in_specs=[any_spec, any_spec], out_specs=VMEM_SPEC,
        out_shape=jax.ShapeDtypeStruct((8, 128), F32),
    )(x, token)


def _adamw_pieces(pieces, w, m, v, after):
    shape = w.shape
    nl = shape[0]
    cols = shape[-1]
    rows = w.size // (nl * cols)
    tr = min(256 // nl, rows)
    flat3 = lambda a: a.reshape(nl, rows, cols)

    def body(*refs):
        p_refs = refs[:nl]
        w_ref, m_ref, v_ref, after_ref, g_ref, d_ref, nm_ref, nv_ref = refs[nl:]
        for l in range(nl):
            g = p_refs[l][0].astype(F32)
            for j in range(1, N_DEV):
                g = g + p_refs[l][j].astype(F32)
            g_ref[l] = g
            d_ref[l], nm_ref[l], nv_ref[l] = _adamw_math(w_ref[l], g, m_ref[l], v_ref[l])

    blk = pl.BlockSpec((nl, tr, cols), lambda i: (0, i, 0))
    outs = pl.pallas_call(
        body, name="adamw_pieces",
        grid=(rows // tr,),
        in_specs=[pl.BlockSpec((N_DEV, tr, cols), lambda i: (0, i, 0))] * nl
        + [blk, blk, blk, pl.BlockSpec(memory_space=pl.ANY)],
        out_specs=[blk] * 4,
        out_shape=[jax.ShapeDtypeStruct((nl, rows, cols), F32)] * 4,
        compiler_params=_cparams("parallel"),
    )(*[a.reshape(N_DEV, rows, cols) for a in pieces], flat3(w), flat3(m), flat3(v), after)
    return [a.reshape(shape) for a in outs]


def _adamw_small(g, w, m, v):
    shape = w.shape
    two = lambda a: a.reshape(-1, shape[-1])

    def body(g_ref, w_ref, m_ref, v_ref, d_ref, nm_ref, nv_ref):
        d_ref[...], nm_ref[...], nv_ref[...] = _adamw_math(w_ref[...], g_ref[...], m_ref[...], v_ref[...])

    outs = pl.pallas_call(
        body, name="adamw_small",
        in_specs=[VMEM_SPEC] * 4, out_specs=[VMEM_SPEC] * 3,
        out_shape=[jax.ShapeDtypeStruct(two(w).shape, F32)] * 3,
    )(two(g), two(w), two(m), two(v))
    return [a.reshape(shape) for a in outs]


WEIGHTS = ("norm_mix", "a_w_in", "a_w_conv", "a_w_out", "b_w_in", "b_w_grp", "b_scale", "b_w_out",
           "ple_norm", "ple_w_gate", "ple_w_proj", "final_norm")
SMALL_ROWS = 24
GATHER_ID = 0
EXCHANGE_ID = 4
LAST_EXCHANGE_ID = 8


def kernel(x, p, norm_mix, a_w_in, a_w_conv, a_w_out, b_w_in, b_w_grp, b_scale, b_w_out, ple_norm, ple_w_gate, ple_w_proj, final_norm, loss_target, m_norm_mix, m_a_w_in, m_a_w_conv, m_a_w_out, m_b_w_in, m_b_w_grp, m_b_scale, m_b_w_out, m_ple_norm, m_ple_w_gate, m_ple_w_proj, m_final_norm, v_norm_mix, v_a_w_in, v_a_w_conv, v_a_w_out, v_b_w_in, v_b_w_grp, v_b_scale, v_b_w_out, v_ple_norm, v_ple_w_gate, v_ple_w_proj, v_final_norm):
    wts = dict(norm_mix=norm_mix, a_w_in=a_w_in, a_w_conv=a_w_conv, a_w_out=a_w_out, b_w_in=b_w_in, b_w_grp=b_w_grp,
               b_scale=b_scale, b_w_out=b_w_out, ple_norm=ple_norm, ple_w_gate=ple_w_gate, ple_w_proj=ple_w_proj,
               final_norm=final_norm)
    mom = dict(norm_mix=m_norm_mix, a_w_in=m_a_w_in, a_w_conv=m_a_w_conv, a_w_out=m_a_w_out, b_w_in=m_b_w_in,
               b_w_grp=m_b_w_grp, b_scale=m_b_scale, b_w_out=m_b_w_out, ple_norm=m_ple_norm, ple_w_gate=m_ple_w_gate,
               ple_w_proj=m_ple_w_proj, final_norm=m_final_norm)
    var = dict(norm_mix=v_norm_mix, a_w_in=v_a_w_in, a_w_conv=v_a_w_conv, a_w_out=v_a_w_out, b_w_in=v_b_w_in,
               b_w_grp=v_b_w_grp, b_scale=v_b_scale, b_w_out=v_b_w_out, ple_norm=v_ple_norm, ple_w_gate=v_ple_w_gate,
               ple_w_proj=v_ple_w_proj, final_norm=v_final_norm)
    d = x.shape[2]
    depth = norm_mix.shape[0]
    n_a, n_b = a_w_conv.shape[0], b_scale.shape[0]
    cw = a_w_conv.shape[2]
    pos = _position(_my_place())

    def layer_matrices(i):
        j = i // 2
        mixer = {"w_in": ("a_w_in", j), "w_out": ("a_w_out", j)} if i % 2 == 0 else \
                {"w_in": ("b_w_in", j), "w_grp": ("b_w_grp", j), "w_out": ("b_w_out", j)}
        return {**mixer, "gate": ("ple_w_gate", i), "proj": ("ple_w_proj", i)}

    full = [_all_gather_layer({k: wts[name][idx].astype(BF16) for k, (name, idx) in layer_matrices(i).items()},
                              GATHER_ID + i) for i in range(depth)]
    vec_rows = jnp.concatenate([a_w_conv.reshape(-1, cw), b_scale], axis=0)
    vecs = _gather_rows(vec_rows, reduce=False)
    n_conv = 3 * n_a
    conv_w = vecs[:, :n_conv].transpose(1, 0, 2).reshape(n_a, 3, N_DEV * cw)
    scale_w = vecs[:, n_conv:].transpose(1, 0, 2).reshape(n_b, N_DEV * cw)

    def exchange(i, g):
        if i > 0:
            return _exchange_layer(g, EXCHANGE_ID + i)
        early = {k: a for k, a in g.items() if k != "w_in"}
        return {**_exchange_layer(early, EXCHANGE_ID), **_exchange_layer({"w_in": g["w_in"]}, LAST_EXCHANGE_ID)}

    loss_row, dx, sent, (d_norm, d_ple_norm, d_final, d_conv, d_scale) = _forward_backward(
        x[0], p[:, 0], loss_target[0], full, conv_w, scale_w, norm_mix, ple_norm, final_norm, exchange)
    pieces = {name: [None] * wts[name].shape[0] for name in WEIGHTS if wts[name].ndim >= 3 and name != "a_w_conv"}
    for i in range(depth):
        for k, (name, idx) in layer_matrices(i).items():
            pieces[name][idx] = sent[i][k]

    pad = lambda a: jnp.pad(a, ((0, 0), (0, d - a.shape[1])))
    small = jnp.concatenate(d_norm + d_ple_norm + [d_final] + d_conv + d_scale + [pad(loss_row)], axis=0)
    small = jnp.pad(small, ((0, SMALL_ROWS - small.shape[0]), (0, 0)))
    total = _gather_rows(small, reduce=True)
    o = 0
    gsum = {}
    gsum["norm_mix"] = total[o:o + depth]; o += depth
    gsum["ple_norm"] = total[o:o + depth]; o += depth
    gsum["final_norm"] = total[o]; o += 1
    conv_full = total[o:o + n_conv].reshape(n_a, 3, d); o += n_conv
    scale_full = total[o:o + n_b]; o += n_b
    loss = total[o, 0]
    gsum["a_w_conv"] = lax.dynamic_slice_in_dim(conv_full, pos * cw, cw, axis=2)
    gsum["b_scale"] = lax.dynamic_slice_in_dim(scale_full, pos * cw, cw, axis=1)

    token = total
    for i in reversed(range(depth)):
        token = _run_behind(sent[i]["w_out"], token)
    last_token = _run_behind(sent[0]["w_in"], token)
    grad, delta, new_m, new_v = {}, {}, {}, {}
    for k in sorted(WEIGHTS, key=lambda name: name == "a_w_in"):
        if k in pieces:
            grad[k], delta[k], new_m[k], new_v[k] = _adamw_pieces(
                pieces[k], wts[k], mom[k], var[k], last_token if k == "a_w_in" else token)
        else:
            grad[k] = gsum[k]
            delta[k], new_m[k], new_v[k] = _adamw_small(gsum[k], wts[k], mom[k], var[k])
    return (loss, dx[None], *[grad[k] for k in WEIGHTS], *[delta[k] for k in WEIGHTS],
            *[new_m[k] for k in WEIGHTS], *[new_v[k] for k in WEIGHTS])
```

```python
import jax
import jax.numpy as jnp
from jax import lax
from jax.experimental import pallas as pl
from jax.experimental.pallas import tpu as pltpu
from jax.experimental.pallas import tpu_sc as plsc

F32 = jnp.float32
BF16 = jnp.bfloat16
MESH = pl.DeviceIdType.MESH

RMS_EPS = 1e-6
POOL_WINDOWS = (2, 4, 8, 16)
N_POOL_GROUPS = len(POOL_WINDOWS)
ADAM_LR = 0.001
ADAM_B1 = 0.9
ADAM_B2 = 0.999
ADAM_EPS = 1e-08
ADAM_WD = 0.01
ADAM_STEP = 10
N_DEV = 8

HALO = 16
LAYER_ROW_TILE = 256
POOL_FWD_ROW_TILE = 512
POOL_BWD_ROW_TILE = 512
BWD_ROW_TILE = 512
VMEM_LIMIT = 56 * 1024 * 1024


def _cparams(*sem):
    return pltpu.CompilerParams(dimension_semantics=sem, vmem_limit_bytes=VMEM_LIMIT)


def _dot(a, b):
    return jnp.dot(a, b, preferred_element_type=F32)


def _dot_nt(a, b):
    return lax.dot_general(a, b, (((1,), (1,)), ((), ())), preferred_element_type=F32)


def _dot_tn(a, b):
    return lax.dot_general(a, b, (((0,), (0,)), ((), ())), preferred_element_type=F32)


def _rms_stats(x):
    r = lax.rsqrt(jnp.mean(x * x, axis=-1, keepdims=True) + RMS_EPS)
    return x * r, r


def _rms_bwd(dy, xh, r, g):
    a = dy * g
    return r * (a - xh * jnp.mean(a * xh, axis=-1, keepdims=True))


def _sigmoid(x):
    return 1.0 / (1.0 + jnp.exp(-x))


def _shift_down(x, k):
    return pltpu.roll(x, k, 0)


def _shift_up(x, k):
    return pltpu.roll(x, x.shape[0] - k, 0)


def _conv_taps(u, u_prev):
    uu = jnp.concatenate([u_prev, u], axis=0)
    return _shift_down(uu, 1)[HALO:], _shift_down(uu, 2)[HALO:]


def _window_mean_minus(uu, row0, window):
    acc = uu
    span = 1
    while span < window:
        acc = acc + _shift_down(acc, span)
        span *= 2
    return acc[HALO:] * _inv_count(uu.shape[0] - HALO, row0, window) - uu[HALO:]


def _inv_count(rows, row0, window):
    t = row0 + lax.broadcasted_iota(jnp.int32, (rows, 1), 0)
    return 1.0 / jnp.minimum(t + 1, window).astype(F32)


def _whole(*shape):
    return pl.BlockSpec(shape, lambda i: (0,) * len(shape), pipeline_mode=pl.Buffered(1))


def _layer_fwd(h, g, w_in, mixer, w_out, pn, w_gate, p_all, w_proj, layer, head=None):
    s, d = h.shape
    n = w_in.shape[1]
    e = w_out.shape[0]
    nsplit = n // e
    gdim = e // N_POOL_GROUPS
    pdim = p_all.shape[2]
    is_conv = mixer[0] == "conv"
    ts = min(LAYER_ROW_TILE if is_conv else POOL_FWD_ROW_TILE, s)
    params = mixer[1:]
    head = tuple(head or ())

    def body(*refs):
        h_ref, g_ref, win_ref = refs[:3]
        mix_refs = refs[3:3 + len(params)]
        wo_ref, pn_ref, wg_ref, p_ref, wp_ref = refs[3 + len(params):8 + len(params)]
        head_refs = refs[8 + len(params):8 + len(params) + len(head)]
        proj_ref, hn_ref, o_ref, h1_ref, h2_ref, gl_ref, pp_ref = refs[8 + len(params) + len(head):][:7]
        carry_ref = refs[-1]
        i = pl.program_id(0)

        @pl.when(i == 0)
        def _():
            carry_ref[...] = jnp.zeros_like(carry_ref)

        x = h_ref[...]
        xh, _ = _rms_stats(x)
        hn = (xh * g_ref[...]).astype(BF16)
        hn_ref[...] = hn
        parts = []
        for k in range(nsplit):
            part = _dot(hn, win_ref[:, k * e:(k + 1) * e])
            proj_ref[k] = part.astype(BF16)
            parts.append(part)
        prev = carry_ref[...]
        if is_conv:
            b, c, v, z = parts
            w_ref, = mix_refs
            u = c * v
            u1, u2 = _conv_taps(u, prev)
            mixed = b * (w_ref[0:1, :] * u2 + w_ref[1:2, :] * u1 + w_ref[2:3, :] * u)
        else:
            u, z = parts
            wgrp_ref, sc_ref = mix_refs
            uu = jnp.concatenate([prev, u], axis=0)
            cols = []
            for gi, window in enumerate(POOL_WINDOWS):
                dg = _window_mean_minus(uu[:, gi * gdim:(gi + 1) * gdim], i * ts, window)
                cols.append(_dot(dg.astype(BF16), wgrp_ref[gi]))
            mixed = jnp.concatenate(cols, axis=1) * sc_ref[...]
        carry_ref[...] = u[ts - HALO:]
        o = ((z * _sigmoid(z)) * mixed).astype(BF16)
        o_ref[...] = o
        h1 = x + _dot(o, wo_ref[...])
        h1_ref[...] = h1
        xh1, _ = _rms_stats(h1)
        gl = _dot((xh1 * pn_ref[...]).astype(BF16), wg_ref[...])
        pp = _dot(p_ref[...].astype(BF16), wp_ref[...])
        gl_ref[...] = gl.astype(BF16)
        pp_ref[...] = pp.astype(BF16)
        h2 = h1 + _sigmoid(gl) * pp
        if not head:
            h2_ref[...] = h2
            return
        t_ref, gain_ref = head_refs
        loss_ref, dgain_ref = refs[-3], refs[-2]

        @pl.when(i == 0)
        def _():
            loss_ref[...] = jnp.zeros_like(loss_ref)
            dgain_ref[...] = jnp.zeros_like(dgain_ref)

        gain = gain_ref[...]
        yh, r = _rms_stats(h2)
        err = yh * gain - t_ref[...]
        loss_ref[...] += jnp.full(loss_ref.shape, (0.5 / d) * jnp.sum(err * err), F32)
        dy = err * (1.0 / d)
        dgain_ref[...] += jnp.sum(dy * yh, axis=0, keepdims=True)
        h2_ref[...] = _rms_bwd(dy, yh, r, gain)

    row = lambda width: pl.BlockSpec((ts, width), lambda i: (i, 0))
    mix_specs = [_whole(*a.shape) for a in params]
    head_specs = [row(d), _whole(1, d)] if head else []
    head_out_specs = [_whole(1, 128), _whole(1, d)] if head else []
    head_out_shape = [jax.ShapeDtypeStruct((1, 128), F32), jax.ShapeDtypeStruct((1, d), F32)] if head else []
    outs = pl.pallas_call(
        body, name="layer_fwd",
        grid=(s // ts,),
        in_specs=[row(d), _whole(1, d), _whole(d, n)] + mix_specs
        + [_whole(e, d), _whole(1, d), _whole(d, d),
           pl.BlockSpec((None, ts, pdim), lambda i: (layer, i, 0)), _whole(pdim, d)] + head_specs,
        out_specs=[pl.BlockSpec((nsplit, ts, e), lambda i: (0, i, 0)),
                   row(d), row(e), row(d), row(d), row(d), row(d)] + head_out_specs,
        out_shape=[jax.ShapeDtypeStruct((nsplit, s, e), BF16), jax.ShapeDtypeStruct((s, d), BF16),
                   jax.ShapeDtypeStruct((s, e), BF16), jax.ShapeDtypeStruct((s, d), F32),
                   jax.ShapeDtypeStruct((s, d), F32), jax.ShapeDtypeStruct((s, d), BF16),
                   jax.ShapeDtypeStruct((s, d), BF16)] + head_out_shape,
        scratch_shapes=[pltpu.VMEM((HALO, e), F32)],
        compiler_params=_cparams("arbitrary"),
    )(h, g, w_in, *params, w_out, pn, w_gate, p_all, w_proj, *head)
    proj, hn, o, h1, h2, gl, pp = outs[:7]
    return (h2, *outs[7:]) if head else h2, (proj, hn, o, h1, gl, pp)


def _out_ple_bwd(dh2, gl, pp, h1, p_all, o, pn, wgate, wout, layer):
    s, d = dh2.shape
    e = o.shape[1]
    pdim = p_all.shape[2]
    ts = min(BWD_ROW_TILE, s)
    last = s // ts - 1

    def body(dh2_ref, gl_ref, pp_ref, h1_ref, p_ref, o_ref, pn_ref, wg_ref, wo_ref,
             dh1_ref, do_ref, dwp_ref, dwg_ref, dwo_ref, dpn_ref, awp, awg, awo):
        i = pl.program_id(0)

        @pl.when(i == 0)
        def _():
            awp[...] = jnp.zeros_like(awp)
            awg[...] = jnp.zeros_like(awg)
            awo[...] = jnp.zeros_like(awo)
            dpn_ref[...] = jnp.zeros_like(dpn_ref)

        dh2 = dh2_ref[...]
        gate = _sigmoid(gl_ref[...].astype(F32))
        dpp = (dh2 * gate).astype(BF16)
        dgl = (dh2 * pp_ref[...].astype(F32) * gate * (1.0 - gate)).astype(BF16)
        xh, r = _rms_stats(h1_ref[...])
        pn = pn_ref[...]
        awp[...] += _dot_tn(p_ref[...].astype(BF16), dpp)
        awg[...] += _dot_tn((xh * pn).astype(BF16), dgl)
        dr = _dot_nt(dgl, wg_ref[...])
        dpn_ref[...] += jnp.sum(dr * xh, axis=0, keepdims=True)
        dh1 = dh2 + _rms_bwd(dr, xh, r, pn)
        dh1_ref[...] = dh1
        dh1b = dh1.astype(BF16)
        do_ref[...] = _dot_nt(dh1b, wo_ref[...]).astype(BF16)
        awo[...] += _dot_tn(o_ref[...], dh1b)

        @pl.when(i == last)
        def _():
            dwp_ref[...] = awp[...].astype(BF16)
            dwg_ref[...] = awg[...].astype(BF16)
            dwo_ref[...] = awo[...].astype(BF16)

    row = lambda width: pl.BlockSpec((ts, width), lambda i: (i, 0))
    return pl.pallas_call(
        body, name="out_ple_bwd",
        grid=(s // ts,),
        in_specs=[row(d), row(d), row(d), row(d),
                  pl.BlockSpec((None, ts, pdim), lambda i: (layer, i, 0)),
                  row(e), _whole(1, d), _whole(d, d), _whole(e, d)],
        out_specs=[row(d), row(e), _whole(pdim, d), _whole(d, d), _whole(e, d), _whole(1, d)],
        out_shape=[jax.ShapeDtypeStruct((s, d), F32), jax.ShapeDtypeStruct((s, e), BF16),
                   jax.ShapeDtypeStruct((pdim, d), BF16), jax.ShapeDtypeStruct((d, d), BF16),
                   jax.ShapeDtypeStruct((e, d), BF16), jax.ShapeDtypeStruct((1, d), F32)],
        scratch_shapes=[pltpu.VMEM((pdim, d), F32), pltpu.VMEM((d, d), F32), pltpu.VMEM((e, d), F32)],
        compiler_params=_cparams("arbitrary"),
    )(dh2, gl, pp, h1, p_all, o, pn, wgate, wout)


def _mixer_bwd(do, proj, hn, mixer, w_in, h, g, dh1):
    s, d = h.shape
    nsplit, _, e = proj.shape
    gdim = e // N_POOL_GROUPS
    is_conv = mixer[0] == "conv"
    ts = min(LAYER_ROW_TILE if is_conv else POOL_BWD_ROW_TILE, s)
    nt = s // ts
    params = mixer[1:]
    n_mix_out = 1 if is_conv else 2

    def body(*refs):
        refs = list(refs)
        take = lambda n: [refs.pop(0) for _ in range(n)]
        do_ref, p_ref, ph_ref = take(3)
        mix_refs = take(len(params))
        win_ref, h_ref, g_ref, dh1_ref, hn_ref = take(5)
        dwin_ref, dh_ref, dg_ref = take(3)
        mix_out = take(n_mix_out)
        carry_ref, dp_ref, acc_ref = take(3)
        i = pl.program_id(0)
        tile = nt - 1 - i

        @pl.when(i == 0)
        def _():
            for ref in [carry_ref, dg_ref, acc_ref] + mix_out[-1:] + refs:
                ref[...] = jnp.zeros_like(ref)

        dof = do_ref[...].astype(F32)
        nxt = carry_ref[...]
        if is_conv:
            w_ref, = mix_refs
            dw_ref, = mix_out
            w0, w1, w2 = w_ref[0:1, :], w_ref[1:2, :], w_ref[2:3, :]
            b, c, v, z = [p_ref[k].astype(F32) for k in range(4)]
            u = c * v
            u_prev = jnp.where(tile == 0, 0.0, ph_ref[1].astype(F32) * ph_ref[2].astype(F32))
            u1, u2 = _conv_taps(u, u_prev)
            conv = w0 * u2 + w1 * u1 + w2 * u
            sig = _sigmoid(z)
            sz = z * sig
            dy = dof * sz
            dp_ref[3] = (dof * (b * conv) * (sig + sz * (1.0 - sig))).astype(BF16)
            dp_ref[0] = (dy * conv).astype(BF16)
            dconv = dy * b
            dw_ref[0:1, :] += jnp.sum(dconv * u2, axis=0, keepdims=True)
            dw_ref[1:2, :] += jnp.sum(dconv * u1, axis=0, keepdims=True)
            dw_ref[2:3, :] += jnp.sum(dconv * u, axis=0, keepdims=True)
            dcc = jnp.concatenate([dconv, nxt], axis=0)
            du = w2 * dconv + w1 * _shift_up(dcc, 1)[:ts] + w0 * _shift_up(dcc, 2)[:ts]
            carry_ref[...] = dconv[:HALO]
            dp_ref[1] = (du * v).astype(BF16)
            dp_ref[2] = (du * c).astype(BF16)
        else:
            wgrp_ref, sc_ref = mix_refs
            dsc_ref = mix_out[1]
            agrp, = refs
            u = p_ref[0].astype(F32)
            z = p_ref[1].astype(F32)
            u_prev = jnp.where(tile == 0, 0.0, ph_ref[0].astype(F32))
            uu = jnp.concatenate([u_prev, u], axis=0)
            sig = _sigmoid(z)
            sz = z * sig
            dm = dof * sz
            dsilu = dof * (sig + sz * (1.0 - sig))
            for gi, window in enumerate(POOL_WINDOWS):
                cols = slice(gi * gdim, (gi + 1) * gdim)
                w = wgrp_ref[gi]
                scale = sc_ref[:, cols]
                db = _window_mean_minus(uu[:, cols], tile * ts, window).astype(BF16)
                mr = _dot(db, w)
                dp_ref[1, :, cols] = (dsilu[:, cols] * (mr * scale)).astype(BF16)
                dmg = dm[:, cols]
                dmr = (dmg * scale).astype(BF16)
                agrp[gi] += _dot_tn(db, dmr)
                dsc_ref[:, cols] += jnp.sum(dmg * mr, axis=0, keepdims=True)
                dd = _dot_nt(dmr, w)
                ddq = dd * _inv_count(ts, tile * ts, window)
                acc = jnp.concatenate([ddq, nxt[:, cols]], axis=0)
                span = 1
                while span < window:
                    acc = acc + _shift_up(acc, span)
                    span *= 2
                carry_ref[:, cols] = ddq[:HALO]
                dp_ref[0, :, cols] = (acc[:ts] - dd).astype(BF16)

        dhn = _dot_nt(dp_ref[0], win_ref[:, 0:e])
        for k in range(1, nsplit):
            dhn += _dot_nt(dp_ref[k], win_ref[:, k * e:(k + 1) * e])
        xh, r = _rms_stats(h_ref[...])
        dg_ref[...] += jnp.sum(dhn * xh, axis=0, keepdims=True)
        dh_ref[...] = dh1_ref[...] + _rms_bwd(dhn, xh, r, g_ref[...])
        hn_tile = hn_ref[...]
        for k in range(nsplit):
            acc_ref[:, k * e:(k + 1) * e] += _dot_tn(hn_tile, dp_ref[k])

        @pl.when(i == nt - 1)
        def _():
            dwin_ref[...] = acc_ref[...].astype(BF16)
            if not is_conv:
                mix_out[0][...] = refs[0][...].astype(BF16)

    rev = lambda width: pl.BlockSpec((ts, width), lambda i: (nt - 1 - i, 0))
    halo_blocks = ts // HALO
    in_specs = [rev(e),
                pl.BlockSpec((nsplit, ts, e), lambda i: (0, nt - 1 - i, 0)),
                pl.BlockSpec((nsplit, HALO, e), lambda i: (0, jnp.maximum((nt - 1 - i) * halo_blocks - 1, 0), 0))]
    in_specs += [_whole(*a.shape) for a in params]
    in_specs += [_whole(d, nsplit * e), rev(d), _whole(1, d), rev(d), rev(d)]
    out_specs = [_whole(d, nsplit * e), rev(d), _whole(1, d)]
    out_shape = [jax.ShapeDtypeStruct((d, nsplit * e), BF16), jax.ShapeDtypeStruct((s, d), F32),
                 jax.ShapeDtypeStruct((1, d), F32)]
    scratch = [pltpu.VMEM((HALO, e), F32), pltpu.VMEM((nsplit, ts, e), BF16), pltpu.VMEM((d, nsplit * e), F32)]
    if is_conv:
        out_specs += [_whole(3, e)]
        out_shape += [jax.ShapeDtypeStruct((3, e), F32)]
    else:
        out_specs += [_whole(N_POOL_GROUPS, gdim, gdim), _whole(1, e)]
        out_shape += [jax.ShapeDtypeStruct((N_POOL_GROUPS, gdim, gdim), BF16), jax.ShapeDtypeStruct((1, e), F32)]
        scratch += [pltpu.VMEM((N_POOL_GROUPS, gdim, gdim), F32)]
    return pl.pallas_call(
        body, name="mixer_bwd",
        grid=(nt,),
        in_specs=in_specs, out_specs=out_specs, out_shape=out_shape, scratch_shapes=scratch,
        compiler_params=_cparams("arbitrary"),
    )(do, proj, proj, *params, w_in, h, g, dh1, hn)


def _forward_backward(xs, ps, target, full, conv_w, scale_w, norm_mix, ple_norm, final_norm, exchange):
    depth = len(full)
    row = lambda a, i: a[i][None, :]
    mixer_of = lambda i: ("conv", conv_w[i // 2]) if i % 2 == 0 else ("pool", full[i]["w_grp"], row(scale_w, i // 2))

    saved = []
    h = xs
    for i in range(depth):
        w = full[i]
        head = (target, final_norm[None, :]) if i == depth - 1 else None
        h_next, acts = _layer_fwd(h, row(norm_mix, i), w["w_in"], mixer_of(i), w["w_out"], row(ple_norm, i),
                                  w["gate"], ps, w["proj"], i, head)
        saved.append((h, *acts))
        h = h_next
    dh, loss_row, d_final = h

    d_norm, d_ple_norm, d_conv, d_scale, sent = [None] * depth, [None] * depth, [], [], [None] * depth
    for i in reversed(range(depth)):
        w = full[i]
        h_in, proj, hn, o, h1, gl, pp = saved[i]
        g = {}
        dh1, do, g["proj"], g["gate"], g["w_out"], d_ple_norm[i] = _out_ple_bwd(
            dh, gl, pp, h1, ps, o, row(ple_norm, i), w["gate"], w["w_out"], i)
        g["w_in"], dh, d_norm[i], *mixer_grads = _mixer_bwd(
            do, proj, hn, mixer_of(i), w["w_in"], h_in, row(norm_mix, i), dh1)
        if i % 2 == 0:
            d_conv.insert(0, mixer_grads[0])
        else:
            g["w_grp"] = mixer_grads[0]
            d_scale.insert(0, mixer_grads[1])
        sent[i] = exchange(i, g)
    return loss_row, dh, sent, (d_norm, d_ple_norm, d_final, d_conv, d_scale)


VMEM_SPEC = pl.BlockSpec(memory_space=pltpu.VMEM)

FLIPS = [(fx, fy, fc) for fx in (0, 1) for fy in (0, 1) for fc in (0, 1)][1:]
SHARD_AXIS = {"w_in": 1, "w_out": 0, "w_grp": 1, "gate": 0, "proj": 1}


def _my_place():
    return lax.axis_index("x"), lax.axis_index("y"), lax.axis_index("c")


def _position(place):
    x, y, c = place
    return 4 * x + 2 * y + c


def _flip(place, flips):
    return tuple(1 - v if f else v for v, f in zip(place, flips))


def _shard_of(ref, axis, pos, n):
    idx = [slice(None)] * len(ref.shape)
    idx[axis] = pl.ds(pl.multiple_of(pos * n, n), n)
    return ref.at[tuple(idx)]


def _sequencer_mesh():
    return plsc.ScalarSubcoreMesh(axis_name="sequencer", num_cores=1)


def _handshake(peers):
    barrier = pltpu.get_barrier_semaphore()
    for peer in peers:
        pl.semaphore_signal(barrier, inc=1, device_id=peer, device_id_type=MESH)
    pl.semaphore_wait(barrier, len(peers))


def _all_gather_layer(shards, collective_id):
    names = list(shards)
    nt = len(names)
    axes = [SHARD_AXIS[k] for k in names]
    widths = [shards[k].shape[SHARD_AXIS[k]] for k in names]

    def full_shape(k):
        shp = list(shards[k].shape)
        shp[SHARD_AXIS[k]] *= N_DEV
        return tuple(shp)

    def body(*refs):
        ins, outs = refs[:nt], refs[nt:2 * nt]
        send_sems, recv_sems, local_sem = refs[2 * nt:]
        me = _my_place()
        x, y, c = me
        sibling = (x, y, 1 - c)
        chips = [(1 - x, y), (x, 1 - y), (1 - x, 1 - y)]
        _handshake([sibling] + [(*chip, c) for chip in chips])

        def block(t, place):
            return _shard_of(outs[t], axes[t], _position(place), widths[t])

        def copy(t, k, place, to, src=None):
            return pltpu.make_async_remote_copy(
                src_ref=block(t, place) if src is None else src, dst_ref=block(t, place),
                send_sem=send_sems.at[k], recv_sem=recv_sems.at[k], device_id=to, device_id_type=MESH)

        mine = [pltpu.make_async_copy(ins[t], block(t, me), local_sem) for t in range(nt)]
        for cp in mine:
            cp.start()
        first = []
        for j, chip in enumerate(chips):
            first += [copy(t, 1 + j, me, (*chip, c), src=ins[t]) for t in range(nt)]
        first += [copy(t, 0, me, sibling, src=ins[t]) for t in range(nt)]
        for cp in first:
            cp.start()
        passed = []
        for j, chip in enumerate(chips):
            for t in range(nt):
                copy(t, 1 + j, (*chip, c), me).wait_recv()
            for t in range(nt):
                fwd = copy(t, 4 + j, (*chip, c), sibling)
                fwd.start()
                passed.append(fwd)
        for t in range(nt):
            copy(t, 0, sibling, me).wait_recv()
        for j, chip in enumerate(chips):
            for t in range(nt):
                copy(t, 4 + j, (*chip, 1 - c), me).wait_recv()
        for cp in first + passed:
            cp.wait_send()
        for cp in mine:
            cp.wait()

    outs = pl.kernel(
        body, name=f"all_gather_layer_{collective_id}",
        out_type=[jax.ShapeDtypeStruct(full_shape(k), shards[k].dtype) for k in names],
        mesh=_sequencer_mesh(),
        scratch_types=[pltpu.SemaphoreType.DMA((7,)), pltpu.SemaphoreType.DMA((7,)), pltpu.SemaphoreType.DMA],
        compiler_params=pltpu.CompilerParams(collective_id=collective_id),
    )(*[shards[k] for k in names])
    return dict(zip(names, outs))


def _exchange_layer(grads, collective_id):
    names = list(grads)
    nt = len(names)
    axes = [SHARD_AXIS[k] for k in names]
    widths = [grads[k].shape[SHARD_AXIS[k]] // N_DEV for k in names]

    def slot_shape(t):
        shp = list(grads[names[t]].shape)
        shp[axes[t]] = widths[t]
        return (N_DEV, *shp)

    def body(*refs):
        ins, outs = refs[:nt], refs[nt:2 * nt]
        send_sems, recv_sems, local_sem = refs[2 * nt:]
        me = _my_place()
        mine = _position(me)
        _handshake([_flip(me, flips) for flips in FLIPS])
        local = [pltpu.make_async_copy(_shard_of(ins[t], axes[t], mine, widths[t]), outs[t].at[mine], local_sem)
                 for t in range(nt)]
        for cp in local:
            cp.start()
        copies = []
        for k, flips in enumerate(FLIPS):
            peer = _flip(me, flips)
            for t in range(nt):
                cp = pltpu.make_async_remote_copy(
                    src_ref=_shard_of(ins[t], axes[t], _position(peer), widths[t]), dst_ref=outs[t].at[mine],
                    send_sem=send_sems.at[k], recv_sem=recv_sems.at[k], device_id=peer, device_id_type=MESH)
                cp.start()
                copies.append(cp)
        for cp in copies:
            cp.wait()
        for cp in local:
            cp.wait()

    outs = pl.kernel(
        body, name=f"exchange_layer_{collective_id}",
        out_type=[jax.ShapeDtypeStruct(slot_shape(t), BF16) for t in range(nt)],
        mesh=_sequencer_mesh(),
        scratch_types=[pltpu.SemaphoreType.DMA((7,)), pltpu.SemaphoreType.DMA((7,)), pltpu.SemaphoreType.DMA],
        compiler_params=pltpu.CompilerParams(collective_id=collective_id),
    )(*[grads[k] for k in names])
    return dict(zip(names, outs))


def _gather_rows(buf, reduce):
    r, c = buf.shape

    def body(in_ref, out_ref, *scratch):
        if reduce:
            all_ref, send_sems, recv_sems = scratch
        else:
            all_ref = out_ref
            send_sems, recv_sems = scratch
        me = _my_place()
        all_ref[_position(me)] = in_ref[...]
        copies = []
        for k, flips in enumerate(FLIPS):
            cp = pltpu.make_async_remote_copy(
                src_ref=in_ref, dst_ref=all_ref.at[_position(me)],
                send_sem=send_sems.at[k], recv_sem=recv_sems.at[k], device_id=_flip(me, flips), device_id_type=MESH)
            cp.start()
            copies.append(cp)
        for cp in copies:
            cp.wait()
        if reduce:
            total = all_ref[0]
            for j in range(1, N_DEV):
                total = total + all_ref[j]
            out_ref[...] = total

    return pl.pallas_call(
        body, name="sum_rows" if reduce else "gather_rows",
        in_specs=[VMEM_SPEC], out_specs=VMEM_SPEC,
        out_shape=jax.ShapeDtypeStruct((r, c) if reduce else (N_DEV, r, c), F32),
        scratch_shapes=([pltpu.VMEM((N_DEV, r, c), F32)] if reduce else [])
        + [pltpu.SemaphoreType.DMA((7,)), pltpu.SemaphoreType.DMA((7,))],
    )(buf)


def _adamw_math(w, g, m, v):
    m = ADAM_B1 * m + (1.0 - ADAM_B1) * g
    v = ADAM_B2 * v + (1.0 - ADAM_B2) * (g * g)
    m_hat = m / (1.0 - ADAM_B1 ** ADAM_STEP)
    v_hat = v / (1.0 - ADAM_B2 ** ADAM_STEP)
    delta = -ADAM_LR * (m_hat / (jnp.sqrt(v_hat) + ADAM_EPS) + ADAM_WD * w)
    return delta, m, v


def _run_behind(x, token):
    def body(x_ref, token_ref, out_ref):
        out_ref[...] = jnp.zeros_like(out_ref)

    any_spec = pl.BlockSpec(memory_space=pl.ANY)
    return pl.pallas_call(
        body, name="run_behind",
        in_specs=[any_spec, any_spec], out_specs=VMEM_SPEC,
        out_shape=jax.ShapeDtypeStruct((8, 128), F32),
    )(x, token)


def _adamw_pieces(pieces, w, m, v, after):
    shape = w.shape
    nl = shape[0]
    cols = shape[-1]
    rows = w.size // (nl * cols)
    tr = min(256 // nl, rows)
    flat3 = lambda a: a.reshape(nl, rows, cols)

    def body(*refs):
        p_refs = refs[:nl]
        w_ref, m_ref, v_ref, after_ref, g_ref, d_ref, nm_ref, nv_ref = refs[nl:]
        for l in range(nl):
            g = p_refs[l][0].astype(F32)
            for j in range(1, N_DEV):
                g = g + p_refs[l][j].astype(F32)
            g_ref[l] = g
            d_ref[l], nm_ref[l], nv_ref[l] = _adamw_math(w_ref[l], g, m_ref[l], v_ref[l])

    blk = pl.BlockSpec((nl, tr, cols), lambda i: (0, i, 0))
    outs = pl.pallas_call(
        body, name="adamw_pieces",
        grid=(rows // tr,),
        in_specs=[pl.BlockSpec((N_DEV, tr, cols), lambda i: (0, i, 0))] * nl
        + [blk, blk, blk, pl.BlockSpec(memory_space=pl.ANY)],
        out_specs=[blk] * 4,
        out_shape=[jax.ShapeDtypeStruct((nl, rows, cols), F32)] * 4,
        compiler_params=_cparams("parallel"),
    )(*[a.reshape(N_DEV, rows, cols) for a in pieces], flat3(w), flat3(m), flat3(v), after)
    return [a.reshape(shape) for a in outs]


def _adamw_small(g, w, m, v):
    shape = w.shape
    two = lambda a: a.reshape(-1, shape[-1])

    def body(g_ref, w_ref, m_ref, v_ref, d_ref, nm_ref, nv_ref):
        d_ref[...], nm_ref[...], nv_ref[...] = _adamw_math(w_ref[...], g_ref[...], m_ref[...], v_ref[...])

    outs = pl.pallas_call(
        body, name="adamw_small",
        in_specs=[VMEM_SPEC] * 4, out_specs=[VMEM_SPEC] * 3,
        out_shape=[jax.ShapeDtypeStruct(two(w).shape, F32)] * 3,
    )(two(g), two(w), two(m), two(v))
    return [a.reshape(shape) for a in outs]


WEIGHTS = ("norm_mix", "a_w_in", "a_w_conv", "a_w_out", "b_w_in", "b_w_grp", "b_scale", "b_w_out",
           "ple_norm", "ple_w_gate", "ple_w_proj", "final_norm")
SMALL_ROWS = 24
GATHER_ID = 0
EXCHANGE_ID = 4
LAST_EXCHANGE_ID = 8


def kernel(x, p, norm_mix, a_w_in, a_w_conv, a_w_out, b_w_in, b_w_grp, b_scale, b_w_out, ple_norm, ple_w_gate, ple_w_proj, final_norm, loss_target, m_norm_mix, m_a_w_in, m_a_w_conv, m_a_w_out, m_b_w_in, m_b_w_grp, m_b_scale, m_b_w_out, m_ple_norm, m_ple_w_gate, m_ple_w_proj, m_final_norm, v_norm_mix, v_a_w_in, v_a_w_conv, v_a_w_out, v_b_w_in, v_b_w_grp, v_b_scale, v_b_w_out, v_ple_norm, v_ple_w_gate, v_ple_w_proj, v_final_norm):
    wts = dict(norm_mix=norm_mix, a_w_in=a_w_in, a_w_conv=a_w_conv, a_w_out=a_w_out, b_w_in=b_w_in, b_w_grp=b_w_grp,
               b_scale=b_scale, b_w_out=b_w_out, ple_norm=ple_norm, ple_w_gate=ple_w_gate, ple_w_proj=ple_w_proj,
               final_norm=final_norm)
    mom = dict(norm_mix=m_norm_mix, a_w_in=m_a_w_in, a_w_conv=m_a_w_conv, a_w_out=m_a_w_out, b_w_in=m_b_w_in,
               b_w_grp=m_b_w_grp, b_scale=m_b_scale, b_w_out=m_b_w_out, ple_norm=m_ple_norm, ple_w_gate=m_ple_w_gate,
               ple_w_proj=m_ple_w_proj, final_norm=m_final_norm)
    var = dict(norm_mix=v_norm_mix, a_w_in=v_a_w_in, a_w_conv=v_a_w_conv, a_w_out=v_a_w_out, b_w_in=v_b_w_in,
               b_w_grp=v_b_w_grp, b_scale=v_b_scale, b_w_out=v_b_w_out, ple_norm=v_ple_norm, ple_w_gate=v_ple_w_gate,
               ple_w_proj=v_ple_w_proj, final_norm=v_final_norm)
    d = x.shape[2]
    depth = norm_mix.shape[0]
    n_a, n_b = a_w_conv.shape[0], b_scale.shape[0]
    cw = a_w_conv.shape[2]
    pos = _position(_my_place())

    def layer_matrices(i):
        j = i // 2
        mixer = {"w_in": ("a_w_in", j), "w_out": ("a_w_out", j)} if i % 2 == 0 else \
                {"w_in": ("b_w_in", j), "w_grp": ("b_w_grp", j), "w_out": ("b_w_out", j)}
        return {**mixer, "gate": ("ple_w_gate", i), "proj": ("ple_w_proj", i)}

    full = [_all_gather_layer({k: wts[name][idx].astype(BF16) for k, (name, idx) in layer_matrices(i).items()},
                              GATHER_ID + i) for i in range(depth)]
    vec_rows = jnp.concatenate([a_w_conv.reshape(-1, cw), b_scale], axis=0)
    vecs = _gather_rows(vec_rows, reduce=False)
    n_conv = 3 * n_a
    conv_w = vecs[:, :n_conv].transpose(1, 0, 2).reshape(n_a, 3, N_DEV * cw)
    scale_w = vecs[:, n_conv:].transpose(1, 0, 2).reshape(n_b, N_DEV * cw)

    def exchange(i, g):
        if i > 0:
            return _exchange_layer(g, EXCHANGE_ID + i)
        early = {k: a for k, a in g.items() if k != "w_in"}
        return {**_exchange_layer(early, EXCHANGE_ID), **_exchange_layer({"w_in": g["w_in"]}, LAST_EXCHANGE_ID)}

    loss_row, dx, sent, (d_norm, d_ple_norm, d_final, d_conv, d_scale) = _forward_backward(
        x[0], p[:, 0], loss_target[0], full, conv_w, scale_w, norm_mix, ple_norm, final_norm, exchange)
    pieces = {name: [None] * wts[name].shape[0] for name in WEIGHTS if wts[name].ndim >= 3 and name != "a_w_conv"}
    for i in range(depth):
        for k, (name, idx) in layer_matrices(i).items():
            pieces[name][idx] = sent[i][k]

    pad = lambda a: jnp.pad(a, ((0, 0), (0, d - a.shape[1])))
    small = jnp.concatenate(d_norm + d_ple_norm + [d_final] + d_conv + d_scale + [pad(loss_row)], axis=0)
    small = jnp.pad(small, ((0, SMALL_ROWS - small.shape[0]), (0, 0)))
    total = _gather_rows(small, reduce=True)
    o = 0
    gsum = {}
    gsum["norm_mix"] = total[o:o + depth]; o += depth
    gsum["ple_norm"] = total[o:o + depth]; o += depth
    gsum["final_norm"] = total[o]; o += 1
    conv_full = total[o:o + n_conv].reshape(n_a, 3, d); o += n_conv
    scale_full = total[o:o + n_b]; o += n_b
    loss = total[o, 0]
    gsum["a_w_conv"] = lax.dynamic_slice_in_dim(conv_full, pos * cw, cw, axis=2)
    gsum["b_scale"] = lax.dynamic_slice_in_dim(scale_full, pos * cw, cw, axis=1)

    token = total
    for i in reversed(range(depth)):
        token = _run_behind(sent[i]["w_out"], token)
    last_token = _run_behind(sent[0]["w_in"], token)
    grad, delta, new_m, new_v = {}, {}, {}, {}
    for k in sorted(WEIGHTS, key=lambda name: name == "a_w_in"):
        if k in pieces:
            grad[k], delta[k], new_m[k], new_v[k] = _adamw_pieces(
                pieces[k], wts[k], mom[k], var[k], last_token if k == "a_w_in" else token)
        else:
            grad[k] = gsum[k]
            delta[k], new_m[k], new_v[k] = _adamw_small(gsum[k], wts[k], mom[k], var[k])
    return (loss, dx[None], *[grad[k] for k in WEIGHTS], *[delta[k] for k in WEIGHTS],
            *[new_m[k] for k in WEIGHTS], *[new_v[k] for k in WEIGHTS])
```

```python
import jax
import jax.numpy as jnp
from jax import lax
from jax.experimental import pallas as pl
from jax.experimental.pallas import tpu as pltpu
from jax.experimental.pallas import tpu_sc as plsc

F32 = jnp.float32
BF16 = jnp.bfloat16
MESH = pl.DeviceIdType.MESH

RMS_EPS = 1e-6
POOL_WINDOWS = (2, 4, 8, 16)
N_POOL_GROUPS = len(POOL_WINDOWS)
ADAM_LR = 0.001
ADAM_B1 = 0.9
ADAM_B2 = 0.999
ADAM_EPS = 1e-08
ADAM_WD = 0.01
ADAM_STEP = 10
N_DEV = 8

HALO = 16
LAYER_ROW_TILE = 256
POOL_FWD_ROW_TILE = 512
POOL_BWD_ROW_TILE = 512
BWD_ROW_TILE = 512
VMEM_LIMIT = 56 * 1024 * 1024


def _cparams(*sem):
    return pltpu.CompilerParams(dimension_semantics=sem, vmem_limit_bytes=VMEM_LIMIT)


def _dot(a, b):
    return jnp.dot(a, b, preferred_element_type=F32)


def _dot_nt(a, b):
    return lax.dot_general(a, b, (((1,), (1,)), ((), ())), preferred_element_type=F32)


def _dot_tn(a, b):
    return lax.dot_general(a, b, (((0,), (0,)), ((), ())), preferred_element_type=F32)


def _rms_stats(x):
    r = lax.rsqrt(jnp.mean(x * x, axis=-1, keepdims=True) + RMS_EPS)
    return x * r, r


def _rms_bwd(dy, xh, r, g):
    a = dy * g
    return r * (a - xh * jnp.mean(a * xh, axis=-1, keepdims=True))


def _sigmoid(x):
    return 1.0 / (1.0 + jnp.exp(-x))


def _shift_down(x, k):
    return pltpu.roll(x, k, 0)


def _shift_up(x, k):
    return pltpu.roll(x, x.shape[0] - k, 0)


def _conv_taps(u, u_prev):
    uu = jnp.concatenate([u_prev, u], axis=0)
    return _shift_down(uu, 1)[HALO:], _shift_down(uu, 2)[HALO:]


def _window_mean_minus(uu, row0, window):
    acc = uu
    span = 1
    while span < window:
        acc = acc + _shift_down(acc, span)
        span *= 2
    return acc[HALO:] * _inv_count(uu.shape[0] - HALO, row0, window) - uu[HALO:]


def _inv_count(rows, row0, window):
    t = row0 + lax.broadcasted_iota(jnp.int32, (rows, 1), 0)
    return 1.0 / jnp.minimum(t + 1, window).astype(F32)


def _whole(*shape):
    return pl.BlockSpec(shape, lambda i: (0,) * len(shape), pipeline_mode=pl.Buffered(1))


def _layer_fwd(h, g, w_in, mixer, w_out, pn, w_gate, p_all, w_proj, layer, head=None):
    s, d = h.shape
    n = w_in.shape[1]
    e = w_out.shape[0]
    nsplit = n // e
    gdim = e // N_POOL_GROUPS
    pdim = p_all.shape[2]
    is_conv = mixer[0] == "conv"
    ts = min(LAYER_ROW_TILE if is_conv else POOL_FWD_ROW_TILE, s)
    params = mixer[1:]
    head = tuple(head or ())

    def body(*refs):
        h_ref, g_ref, win_ref = refs[:3]
        mix_refs = refs[3:3 + len(params)]
        wo_ref, pn_ref, wg_ref, p_ref, wp_ref = refs[3 + len(params):8 + len(params)]
        head_refs = refs[8 + len(params):8 + len(params) + len(head)]
        proj_ref, hn_ref, o_ref, h1_ref, h2_ref, gl_ref, pp_ref = refs[8 + len(params) + len(head):][:7]
        carry_ref = refs[-1]
        i = pl.program_id(0)

        @pl.when(i == 0)
        def _():
            carry_ref[...] = jnp.zeros_like(carry_ref)

        x = h_ref[...]
        xh, _ = _rms_stats(x)
        hn = (xh * g_ref[...]).astype(BF16)
        hn_ref[...] = hn
        parts = []
        for k in range(nsplit):
            part = _dot(hn, win_ref[:, k * e:(k + 1) * e])
            proj_ref[k] = part.astype(BF16)
            parts.append(part)
        prev = carry_ref[...]
        if is_conv:
            b, c, v, z = parts
            w_ref, = mix_refs
            u = c * v
            u1, u2 = _conv_taps(u, prev)
            mixed = b * (w_ref[0:1, :] * u2 + w_ref[1:2, :] * u1 + w_ref[2:3, :] * u)
        else:
            u, z = parts
            wgrp_ref, sc_ref = mix_refs
            uu = jnp.concatenate([prev, u], axis=0)
            cols = []
            for gi, window in enumerate(POOL_WINDOWS):
                dg = _window_mean_minus(uu[:, gi * gdim:(gi + 1) * gdim], i * ts, window)
                cols.append(_dot(dg.astype(BF16), wgrp_ref[gi]))
            mixed = jnp.concatenate(cols, axis=1) * sc_ref[...]
        carry_ref[...] = u[ts - HALO:]
        o = ((z * _sigmoid(z)) * mixed).astype(BF16)
        o_ref[...] = o
        h1 = x + _dot(o, wo_ref[...])
        h1_ref[...] = h1
        xh1, _ = _rms_stats(h1)
        gl = _dot((xh1 * pn_ref[...]).astype(BF16), wg_ref[...])
        pp = _dot(p_ref[...].astype(BF16), wp_ref[...])
        gl_ref[...] = gl.astype(BF16)
        pp_ref[...] = pp.astype(BF16)
        h2 = h1 + _sigmoid(gl) * pp
        if not head:
            h2_ref[...] = h2
            return
        t_ref, gain_ref = head_refs
        loss_ref, dgain_ref = refs[-3], refs[-2]

        @pl.when(i == 0)
        def _():
            loss_ref[...] = jnp.zeros_like(loss_ref)
            dgain_ref[...] = jnp.zeros_like(dgain_ref)

        gain = gain_ref[...]
        yh, r = _rms_stats(h2)
        err = yh * gain - t_ref[...]
        loss_ref[...] += jnp.full(loss_ref.shape, (0.5 / d) * jnp.sum(err * err), F32)
        dy = err * (1.0 / d)
        dgain_ref[...] += jnp.sum(dy * yh, axis=0, keepdims=True)
        h2_ref[...] = _rms_bwd(dy, yh, r, gain)

    row = lambda width: pl.BlockSpec((ts, width), lambda i: (i, 0))
    mix_specs = [_whole(*a.shape) for a in params]
    head_specs = [row(d), _whole(1, d)] if head else []
    head_out_specs = [_whole(1, 128), _whole(1, d)] if head else []
    head_out_shape = [jax.ShapeDtypeStruct((1, 128), F32), jax.ShapeDtypeStruct((1, d), F32)] if head else []
    outs = pl.pallas_call(
        body, name="layer_fwd",
        grid=(s // ts,),
        in_specs=[row(d), _whole(1, d), _whole(d, n)] + mix_specs
        + [_whole(e, d), _whole(1, d), _whole(d, d),
           pl.BlockSpec((None, ts, pdim), lambda i: (layer, i, 0)), _whole(pdim, d)] + head_specs,
        out_specs=[pl.BlockSpec((nsplit, ts, e), lambda i: (0, i, 0)),
                   row(d), row(e), row(d), row(d), row(d), row(d)] + head_out_specs,
        out_shape=[jax.ShapeDtypeStruct((nsplit, s, e), BF16), jax.ShapeDtypeStruct((s, d), BF16),
                   jax.ShapeDtypeStruct((s, e), BF16), jax.ShapeDtypeStruct((s, d), F32),
                   jax.ShapeDtypeStruct((s, d), F32), jax.ShapeDtypeStruct((s, d), BF16),
                   jax.ShapeDtypeStruct((s, d), BF16)] + head_out_shape,
        scratch_shapes=[pltpu.VMEM((HALO, e), F32)],
        compiler_params=_cparams("arbitrary"),
    )(h, g, w_in, *params, w_out, pn, w_gate, p_all, w_proj, *head)
    proj, hn, o, h1, h2, gl, pp = outs[:7]
    return (h2, *outs[7:]) if head else h2, (proj, hn, o, h1, gl, pp)


def _out_ple_bwd(dh2, gl, pp, h1, p_all, o, pn, wgate, wout, layer):
    s, d = dh2.shape
    e = o.shape[1]
    pdim = p_all.shape[2]
    ts = min(BWD_ROW_TILE, s)
    last = s // ts - 1

    def body(dh2_ref, gl_ref, pp_ref, h1_ref, p_ref, o_ref, pn_ref, wg_ref, wo_ref,
             dh1_ref, do_ref, dwp_ref, dwg_ref, dwo_ref, dpn_ref, awp, awg, awo):
        i = pl.program_id(0)

        @pl.when(i == 0)
        def _():
            awp[...] = jnp.zeros_like(awp)
            awg[...] = jnp.zeros_like(awg)
            awo[...] = jnp.zeros_like(awo)
            dpn_ref[...] = jnp.zeros_like(dpn_ref)

        dh2 = dh2_ref[...]
        gate = _sigmoid(gl_ref[...].astype(F32))
        dpp = (dh2 * gate).astype(BF16)
        dgl = (dh2 * pp_ref[...].astype(F32) * gate * (1.0 - gate)).astype(BF16)
        xh, r = _rms_stats(h1_ref[...])
        pn = pn_ref[...]
        awp[...] += _dot_tn(p_ref[...].astype(BF16), dpp)
        awg[...] += _dot_tn((xh * pn).astype(BF16), dgl)
        dr = _dot_nt(dgl, wg_ref[...])
        dpn_ref[...] += jnp.sum(dr * xh, axis=0, keepdims=True)
        dh1 = dh2 + _rms_bwd(dr, xh, r, pn)
        dh1_ref[...] = dh1
        dh1b = dh1.astype(BF16)
        do_ref[...] = _dot_nt(dh1b, wo_ref[...]).astype(BF16)
        awo[...] += _dot_tn(o_ref[...], dh1b)

        @pl.when(i == last)
        def _():
            dwp_ref[...] = awp[...].astype(BF16)
            dwg_ref[...] = awg[...].astype(BF16)
            dwo_ref[...] = awo[...].astype(BF16)

    row = lambda width: pl.BlockSpec((ts, width), lambda i: (i, 0))
    return pl.pallas_call(
        body, name="out_ple_bwd",
        grid=(s // ts,),
        in_specs=[row(d), row(d), row(d), row(d),
                  pl.BlockSpec((None, ts, pdim), lambda i: (layer, i, 0)),
                  row(e), _whole(1, d), _whole(d, d), _whole(e, d)],
        out_specs=[row(d), row(e), _whole(pdim, d), _whole(d, d), _whole(e, d), _whole(1, d)],
        out_shape=[jax.ShapeDtypeStruct((s, d), F32), jax.ShapeDtypeStruct((s, e), BF16),
                   jax.ShapeDtypeStruct((pdim, d), BF16), jax.ShapeDtypeStruct((d, d), BF16),
                   jax.ShapeDtypeStruct((e, d), BF16), jax.ShapeDtypeStruct((1, d), F32)],
        scratch_shapes=[pltpu.VMEM((pdim, d), F32), pltpu.VMEM((d, d), F32), pltpu.VMEM((e, d), F32)],
        compiler_params=_cparams("arbitrary"),
    )(dh2, gl, pp, h1, p_all, o, pn, wgate, wout)


def _mixer_bwd(do, proj, hn, mixer, w_in, h, g, dh1):
    s, d = h.shape
    nsplit, _, e = proj.shape
    gdim = e // N_POOL_GROUPS
    is_conv = mixer[0] == "conv"
    ts = min(LAYER_ROW_TILE if is_conv else POOL_BWD_ROW_TILE, s)
    nt = s // ts
    params = mixer[1:]
    n_mix_out = 1 if is_conv else 2

    def body(*refs):
        refs = list(refs)
        take = lambda n: [refs.pop(0) for _ in range(n)]
        do_ref, p_ref, ph_ref = take(3)
        mix_refs = take(len(params))
        win_ref, h_ref, g_ref, dh1_ref, hn_ref = take(5)
        dwin_ref, dh_ref, dg_ref = take(3)
        mix_out = take(n_mix_out)
        carry_ref, dp_ref, acc_ref = take(3)
        i = pl.program_id(0)
        tile = nt - 1 - i

        @pl.when(i == 0)
        def _():
            for ref in [carry_ref, dg_ref, acc_ref] + mix_out[-1:] + refs:
                ref[...] = jnp.zeros_like(ref)

        dof = do_ref[...].astype(F32)
        nxt = carry_ref[...]
        if is_conv:
            w_ref, = mix_refs
            dw_ref, = mix_out
            w0, w1, w2 = w_ref[0:1, :], w_ref[1:2, :], w_ref[2:3, :]
            b, c, v, z = [p_ref[k].astype(F32) for k in range(4)]
            u = c * v
            u_prev = jnp.where(tile == 0, 0.0, ph_ref[1].astype(F32) * ph_ref[2].astype(F32))
            u1, u2 = _conv_taps(u, u_prev)
            conv = w0 * u2 + w1 * u1 + w2 * u
            sig = _sigmoid(z)
            sz = z * sig
            dy = dof * sz
            dp_ref[3] = (dof * (b * conv) * (sig + sz * (1.0 - sig))).astype(BF16)
            dp_ref[0] = (dy * conv).astype(BF16)
            dconv = dy * b
            dw_ref[0:1, :] += jnp.sum(dconv * u2, axis=0, keepdims=True)
            dw_ref[1:2, :] += jnp.sum(dconv * u1, axis=0, keepdims=True)
            dw_ref[2:3, :] += jnp.sum(dconv * u, axis=0, keepdims=True)
            dcc = jnp.concatenate([dconv, nxt], axis=0)
            du = w2 * dconv + w1 * _shift_up(dcc, 1)[:ts] + w0 * _shift_up(dcc, 2)[:ts]
            carry_ref[...] = dconv[:HALO]
            dp_ref[1] = (du * v).astype(BF16)
            dp_ref[2] = (du * c).astype(BF16)
        else:
            wgrp_ref, sc_ref = mix_refs
            dsc_ref = mix_out[1]
            agrp, = refs
            u = p_ref[0].astype(F32)
            z = p_ref[1].astype(F32)
            u_prev = jnp.where(tile == 0, 0.0, ph_ref[0].astype(F32))
            uu = jnp.concatenate([u_prev, u], axis=0)
            sig = _sigmoid(z)
            sz = z * sig
            dm = dof * sz
            dsilu = dof * (sig + sz * (1.0 - sig))
            for gi, window in enumerate(POOL_WINDOWS):
                cols = slice(gi * gdim, (gi + 1) * gdim)
                w = wgrp_ref[gi]
                scale = sc_ref[:, cols]
                db = _window_mean_minus(uu[:, cols], tile * ts, window).astype(BF16)
                mr = _dot(db, w)
                dp_ref[1, :, cols] = (dsilu[:, cols] * (mr * scale)).astype(BF16)
                dmg = dm[:, cols]
                dmr = (dmg * scale).astype(BF16)
                agrp[gi] += _dot_tn(db, dmr)
                dsc_ref[:, cols] += jnp.sum(dmg * mr, axis=0, keepdims=True)
                dd = _dot_nt(dmr, w)
                ddq = dd * _inv_count(ts, tile * ts, window)
                acc = jnp.concatenate([ddq, nxt[:, cols]], axis=0)
                span = 1
                while span < window:
                    acc = acc + _shift_up(acc, span)
                    span *= 2
                carry_ref[:, cols] = ddq[:HALO]
                dp_ref[0, :, cols] = (acc[:ts] - dd).astype(BF16)

        dhn = _dot_nt(dp_ref[0], win_ref[:, 0:e])
        for k in range(1, nsplit):
            dhn += _dot_nt(dp_ref[k], win_ref[:, k * e:(k + 1) * e])
        xh, r = _rms_stats(h_ref[...])
        dg_ref[...] += jnp.sum(dhn * xh, axis=0, keepdims=True)
        dh_ref[...] = dh1_ref[...] + _rms_bwd(dhn, xh, r, g_ref[...])
        hn_tile = hn_ref[...]
        for k in range(nsplit):
            acc_ref[:, k * e:(k + 1) * e] += _dot_tn(hn_tile, dp_ref[k])

        @pl.when(i == nt - 1)
        def _():
            dwin_ref[...] = acc_ref[...].astype(BF16)
            if not is_conv:
                mix_out[0][...] = refs[0][...].astype(BF16)

    rev = lambda width: pl.BlockSpec((ts, width), lambda i: (nt - 1 - i, 0))
    halo_blocks = ts // HALO
    in_specs = [rev(e),
                pl.BlockSpec((nsplit, ts, e), lambda i: (0, nt - 1 - i, 0)),
                pl.BlockSpec((nsplit, HALO, e), lambda i: (0, jnp.maximum((nt - 1 - i) * halo_blocks - 1, 0), 0))]
    in_specs += [_whole(*a.shape) for a in params]
    in_specs += [_whole(d, nsplit * e), rev(d), _whole(1, d), rev(d), rev(d)]
    out_specs = [_whole(d, nsplit * e), rev(d), _whole(1, d)]
    out_shape = [jax.ShapeDtypeStruct((d, nsplit * e), BF16), jax.ShapeDtypeStruct((s, d), F32),
                 jax.ShapeDtypeStruct((1, d), F32)]
    scratch = [pltpu.VMEM((HALO, e), F32), pltpu.VMEM((nsplit, ts, e), BF16), pltpu.VMEM((d, nsplit * e), F32)]
    if is_conv:
        out_specs += [_whole(3, e)]
        out_shape += [jax.ShapeDtypeStruct((3, e), F32)]
    else:
        out_specs += [_whole(N_POOL_GROUPS, gdim, gdim), _whole(1, e)]
        out_shape += [jax.ShapeDtypeStruct((N_POOL_GROUPS, gdim, gdim), BF16), jax.ShapeDtypeStruct((1, e), F32)]
        scratch += [pltpu.VMEM((N_POOL_GROUPS, gdim, gdim), F32)]
    return pl.pallas_call(
        body, name="mixer_bwd",
        grid=(nt,),
        in_specs=in_specs, out_specs=out_specs, out_shape=out_shape, scratch_shapes=scratch,
        compiler_params=_cparams("arbitrary"),
    )(do, proj, proj, *params, w_in, h, g, dh1, hn)


def _forward_backward(xs, ps, target, full, conv_w, scale_w, norm_mix, ple_norm, final_norm, exchange):
    depth = len(full)
    row = lambda a, i: a[i][None, :]
    mixer_of = lambda i: ("conv", conv_w[i // 2]) if i % 2 == 0 else ("pool", full[i]["w_grp"], row(scale_w, i // 2))

    saved = []
    h = xs
    for i in range(depth):
        w = full[i]
        head = (target, final_norm[None, :]) if i == depth - 1 else None
        h_next, acts = _layer_fwd(h, row(norm_mix, i), w["w_in"], mixer_of(i), w["w_out"], row(ple_norm, i),
                                  w["gate"], ps, w["proj"], i, head)
        saved.append((h, *acts))
        h = h_next
    dh, loss_row, d_final = h

    d_norm, d_ple_norm, d_conv, d_scale, sent = [None] * depth, [None] * depth, [], [], [None] * depth
    for i in reversed(range(depth)):
        w = full[i]
        h_in, proj, hn, o, h1, gl, pp = saved[i]
        g = {}
        dh1, do, g["proj"], g["gate"], g["w_out"], d_ple_norm[i] = _out_ple_bwd(
            dh, gl, pp, h1, ps, o, row(ple_norm, i), w["gate"], w["w_out"], i)
        g["w_in"], dh, d_norm[i], *mixer_grads = _mixer_bwd(
            do, proj, hn, mixer_of(i), w["w_in"], h_in, row(norm_mix, i), dh1)
        if i % 2 == 0:
            d_conv.insert(0, mixer_grads[0])
        else:
            g["w_grp"] = mixer_grads[0]
            d_scale.insert(0, mixer_grads[1])
        sent[i] = exchange(i, g)
    return loss_row, dh, sent, (d_norm, d_ple_norm, d_final, d_conv, d_scale)


VMEM_SPEC = pl.BlockSpec(memory_space=pltpu.VMEM)

FLIPS = [(fx, fy, fc) for fx in (0, 1) for fy in (0, 1) for fc in (0, 1)][1:]
SHARD_AXIS = {"w_in": 1, "w_out": 0, "w_grp": 1, "gate": 0, "proj": 1}


def _my_place():
    return lax.axis_index("x"), lax.axis_index("y"), lax.axis_index("c")


def _position(place):
    x, y, c = place
    return 4 * x + 2 * y + c


def _flip(place, flips):
    return tuple(1 - v if f else v for v, f in zip(place, flips))


def _shard_of(ref, axis, pos, n):
    idx = [slice(None)] * len(ref.shape)
    idx[axis] = pl.ds(pl.multiple_of(pos * n, n), n)
    return ref.at[tuple(idx)]


def _sequencer_mesh():
    return plsc.ScalarSubcoreMesh(axis_name="sequencer", num_cores=1)


def _handshake(peers):
    barrier = pltpu.get_barrier_semaphore()
    for peer in peers:
        pl.semaphore_signal(barrier, inc=1, device_id=peer, device_id_type=MESH)
    pl.semaphore_wait(barrier, len(peers))


def _all_gather_layer(shards, collective_id):
    names = list(shards)
    nt = len(names)
    axes = [SHARD_AXIS[k] for k in names]
    widths = [shards[k].shape[SHARD_AXIS[k]] for k in names]

    def full_shape(k):
        shp = list(shards[k].shape)
        shp[SHARD_AXIS[k]] *= N_DEV
        return tuple(shp)

    def body(*refs):
        ins, outs = refs[:nt], refs[nt:2 * nt]
        send_sems, recv_sems, local_sem = refs[2 * nt:]
        me = _my_place()
        x, y, c = me
        sibling = (x, y, 1 - c)
        chips = [(1 - x, y), (x, 1 - y), (1 - x, 1 - y)]
        _handshake([sibling] + [(*chip, c) for chip in chips])

        def block(t, place):
            return _shard_of(outs[t], axes[t], _position(place), widths[t])

        def copy(t, k, place, to, src=None):
            return pltpu.make_async_remote_copy(
                src_ref=block(t, place) if src is None else src, dst_ref=block(t, place),
                send_sem=send_sems.at[k], recv_sem=recv_sems.at[k], device_id=to, device_id_type=MESH)

        mine = [pltpu.make_async_copy(ins[t], block(t, me), local_sem) for t in range(nt)]
        for cp in mine:
            cp.start()
        first = []
        for j, chip in enumerate(chips):
            first += [copy(t, 1 + j, me, (*chip, c), src=ins[t]) for t in range(nt)]
        first += [copy(t, 0, me, sibling, src=ins[t]) for t in range(nt)]
        for cp in first:
            cp.start()
        passed = []
        for j, chip in enumerate(chips):
            for t in range(nt):
                copy(t, 1 + j, (*chip, c), me).wait_recv()
            for t in range(nt):
                fwd = copy(t, 4 + j, (*chip, c), sibling)
                fwd.start()
                passed.append(fwd)
        for t in range(nt):
            copy(t, 0, sibling, me).wait_recv()
        for j, chip in enumerate(chips):
            for t in range(nt):
                copy(t, 4 + j, (*chip, 1 - c), me).wait_recv()
        for cp in first + passed:
            cp.wait_send()
        for cp in mine:
            cp.wait()

    outs = pl.kernel(
        body, name=f"all_gather_layer_{collective_id}",
        out_type=[jax.ShapeDtypeStruct(full_shape(k), shards[k].dtype) for k in names],
        mesh=_sequencer_mesh(),
        scratch_types=[pltpu.SemaphoreType.DMA((7,)), pltpu.SemaphoreType.DMA((7,)), pltpu.SemaphoreType.DMA],
        compiler_params=pltpu.CompilerParams(collective_id=collective_id),
    )(*[shards[k] for k in names])
    return dict(zip(names, outs))


def _exchange_layer(grads, collective_id):
    names = list(grads)
    nt = len(names)
    axes = [SHARD_AXIS[k] for k in names]
    widths = [grads[k].shape[SHARD_AXIS[k]] // N_DEV for k in names]

    def slot_shape(t):
        shp = list(grads[names[t]].shape)
        shp[axes[t]] = widths[t]
        return (N_DEV, *shp)

    def body(*refs):
        ins, outs = refs[:nt], refs[nt:2 * nt]
        send_sems, recv_sems, local_sem = refs[2 * nt:]
        me = _my_place()
        mine = _position(me)
        _handshake([_flip(me, flips) for flips in FLIPS])
        local = [pltpu.make_async_copy(_shard_of(ins[t], axes[t], mine, widths[t]), outs[t].at[mine], local_sem)
                 for t in range(nt)]
        for cp in local:
            cp.start()
        copies = []
        for k, flips in enumerate(FLIPS):
            peer = _flip(me, flips)
            for t in range(nt):
                cp = pltpu.make_async_remote_copy(
                    src_ref=_shard_of(ins[t], axes[t], _position(peer), widths[t]), dst_ref=outs[t].at[mine],
                    send_sem=send_sems.at[k], recv_sem=recv_sems.at[k], device_id=peer, device_id_type=MESH)
                cp.start()
                copies.append(cp)
        for cp in copies:
            cp.wait()
        for cp in local:
            cp.wait()

    outs = pl.kernel(
        body, name=f"exchange_layer_{collective_id}",
        out_type=[jax.ShapeDtypeStruct(slot_shape(t), BF16) for t in range(nt)],
        mesh=_sequencer_mesh(),
        scratch_types=[pltpu.SemaphoreType.DMA((7,)), pltpu.SemaphoreType.DMA((7,)), pltpu.SemaphoreType.DMA],
        compiler_params=pltpu.CompilerParams(collective_id=collective_id),
    )(*[grads[k] for k in names])
    return dict(zip(names, outs))


def _gather_rows(buf, reduce):
    r, c = buf.shape

    def body(in_ref, out_ref, *scratch):
        if reduce:
            all_ref, send_sems, recv_sems = scratch
        else:
            all_ref = out_ref
            send_sems, recv_sems = scratch
        me = _my_place()
        all_ref[_position(me)] = in_ref[...]
        copies = []
        for k, flips in enumerate(FLIPS):
            cp = pltpu.make_async_remote_copy(
                src_ref=in_ref, dst_ref=all_ref.at[_position(me)],
                send_sem=send_sems.at[k], recv_sem=recv_sems.at[k], device_id=_flip(me, flips), device_id_type=MESH)
            cp.start()
            copies.append(cp)
        for cp in copies:
            cp.wait()
        if reduce:
            total = all_ref[0]
            for j in range(1, N_DEV):
                total = total + all_ref[j]
            out_ref[...] = total

    return pl.pallas_call(
        body, name="sum_rows" if reduce else "gather_rows",
        in_specs=[VMEM_SPEC], out_specs=VMEM_SPEC,
        out_shape=jax.ShapeDtypeStruct((r, c) if reduce else (N_DEV, r, c), F32),
        scratch_shapes=([pltpu.VMEM((N_DEV, r, c), F32)] if reduce else [])
        + [pltpu.SemaphoreType.DMA((7,)), pltpu.SemaphoreType.DMA((7,))],
    )(buf)


def _adamw_math(w, g, m, v):
    m = ADAM_B1 * m + (1.0 - ADAM_B1) * g
    v = ADAM_B2 * v + (1.0 - ADAM_B2) * (g * g)
    m_hat = m / (1.0 - ADAM_B1 ** ADAM_STEP)
    v_hat = v / (1.0 - ADAM_B2 ** ADAM_STEP)
    delta = -ADAM_LR * (m_hat / (jnp.sqrt(v_hat) + ADAM_EPS) + ADAM_WD * w)
    return delta, m, v


def _run_behind(x, token):
    def body(x_ref, token_ref, out_ref):
        out_ref[...] = jnp.zeros_like(out_ref)

    any_spec = pl.BlockSpec(memory_space=pl.ANY)
    return pl.pallas_call(
        body, name="run_behind",
        in_specs=[any_spec, any_spec], out_specs=VMEM_SPEC,
        out_shape=jax.ShapeDtypeStruct((8, 128), F32),
    )(x, token)


def _adamw_pieces(pieces, w, m, v, after, layer=None, into=()):
    shape = w.shape
    nl = shape[0]
    cols = shape[-1]
    rows = w.size // (nl * cols)
    first, nh = (0, nl) if layer is None else (layer, 1)
    tr = min(256 // nh, rows)
    flat3 = lambda a: a.reshape(nl, rows, cols)
    into = [flat3(a) for a in into]

    def body(*refs):
        p_refs = refs[:nh]
        w_ref, m_ref, v_ref = refs[nh:nh + 3]
        g_ref, d_ref, nm_ref, nv_ref = refs[-4:]
        for l in range(nh):
            g = p_refs[l][0].astype(F32)
            for j in range(1, N_DEV):
                g = g + p_refs[l][j].astype(F32)
            g_ref[l] = g
            d_ref[l], nm_ref[l], nv_ref[l] = _adamw_math(w_ref[l], g, m_ref[l], v_ref[l])

    blk = pl.BlockSpec((nh, tr, cols), lambda i: (first, i, 0))
    any_spec = pl.BlockSpec(memory_space=pl.ANY)
    outs = pl.pallas_call(
        body, name="adamw_pieces",
        grid=(rows // tr,),
        in_specs=[pl.BlockSpec((N_DEV, tr, cols), lambda i: (0, i, 0))] * nh
        + [blk, blk, blk, any_spec] + [any_spec] * len(into),
        out_specs=[blk] * 4,
        out_shape=[jax.ShapeDtypeStruct((nl, rows, cols), F32)] * 4,
        input_output_aliases={nh + 4 + k: k for k in range(len(into))},
        compiler_params=_cparams("parallel"),
    )(*[a.reshape(N_DEV, rows, cols) for a in pieces], flat3(w), flat3(m), flat3(v), after, *into)
    return [a.reshape(shape) for a in outs]


def _adamw_small(g, w, m, v):
    shape = w.shape
    two = lambda a: a.reshape(-1, shape[-1])

    def body(g_ref, w_ref, m_ref, v_ref, d_ref, nm_ref, nv_ref):
        d_ref[...], nm_ref[...], nv_ref[...] = _adamw_math(w_ref[...], g_ref[...], m_ref[...], v_ref[...])

    outs = pl.pallas_call(
        body, name="adamw_small",
        in_specs=[VMEM_SPEC] * 4, out_specs=[VMEM_SPEC] * 3,
        out_shape=[jax.ShapeDtypeStruct(two(w).shape, F32)] * 3,
    )(two(g), two(w), two(m), two(v))
    return [a.reshape(shape) for a in outs]


WEIGHTS = ("norm_mix", "a_w_in", "a_w_conv", "a_w_out", "b_w_in", "b_w_grp", "b_scale", "b_w_out",
           "ple_norm", "ple_w_gate", "ple_w_proj", "final_norm")
SMALL_ROWS = 24
GATHER_ID = 0
EXCHANGE_ID = 4
LAST_EXCHANGE_ID = 8


def kernel(x, p, norm_mix, a_w_in, a_w_conv, a_w_out, b_w_in, b_w_grp, b_scale, b_w_out, ple_norm, ple_w_gate, ple_w_proj, final_norm, loss_target, m_norm_mix, m_a_w_in, m_a_w_conv, m_a_w_out, m_b_w_in, m_b_w_grp, m_b_scale, m_b_w_out, m_ple_norm, m_ple_w_gate, m_ple_w_proj, m_final_norm, v_norm_mix, v_a_w_in, v_a_w_conv, v_a_w_out, v_b_w_in, v_b_w_grp, v_b_scale, v_b_w_out, v_ple_norm, v_ple_w_gate, v_ple_w_proj, v_final_norm):
    wts = dict(norm_mix=norm_mix, a_w_in=a_w_in, a_w_conv=a_w_conv, a_w_out=a_w_out, b_w_in=b_w_in, b_w_grp=b_w_grp,
               b_scale=b_scale, b_w_out=b_w_out, ple_norm=ple_norm, ple_w_gate=ple_w_gate, ple_w_proj=ple_w_proj,
               final_norm=final_norm)
    mom = dict(norm_mix=m_norm_mix, a_w_in=m_a_w_in, a_w_conv=m_a_w_conv, a_w_out=m_a_w_out, b_w_in=m_b_w_in,
               b_w_grp=m_b_w_grp, b_scale=m_b_scale, b_w_out=m_b_w_out, ple_norm=m_ple_norm, ple_w_gate=m_ple_w_gate,
               ple_w_proj=m_ple_w_proj, final_norm=m_final_norm)
    var = dict(norm_mix=v_norm_mix, a_w_in=v_a_w_in, a_w_conv=v_a_w_conv, a_w_out=v_a_w_out, b_w_in=v_b_w_in,
               b_w_grp=v_b_w_grp, b_scale=v_b_scale, b_w_out=v_b_w_out, ple_norm=v_ple_norm, ple_w_gate=v_ple_w_gate,
               ple_w_proj=v_ple_w_proj, final_norm=v_final_norm)
    d = x.shape[2]
    depth = norm_mix.shape[0]
    n_a, n_b = a_w_conv.shape[0], b_scale.shape[0]
    cw = a_w_conv.shape[2]
    pos = _position(_my_place())

    def layer_matrices(i):
        j = i // 2
        mixer = {"w_in": ("a_w_in", j), "w_out": ("a_w_out", j)} if i % 2 == 0 else \
                {"w_in": ("b_w_in", j), "w_grp": ("b_w_grp", j), "w_out": ("b_w_out", j)}
        return {**mixer, "gate": ("ple_w_gate", i), "proj": ("ple_w_proj", i)}

    full = [_all_gather_layer({k: wts[name][idx].astype(BF16) for k, (name, idx) in layer_matrices(i).items()},
                              GATHER_ID + i) for i in range(depth)]
    vec_rows = jnp.concatenate([a_w_conv.reshape(-1, cw), b_scale], axis=0)
    vecs = _gather_rows(vec_rows, reduce=False)
    n_conv = 3 * n_a
    conv_w = vecs[:, :n_conv].transpose(1, 0, 2).reshape(n_a, 3, N_DEV * cw)
    scale_w = vecs[:, n_conv:].transpose(1, 0, 2).reshape(n_b, N_DEV * cw)

    def exchange(i, g):
        if i > 0:
            return _exchange_layer(g, EXCHANGE_ID + i)
        early = {k: a for k, a in g.items() if k != "w_in"}
        return {**_exchange_layer(early, EXCHANGE_ID), **_exchange_layer({"w_in": g["w_in"]}, LAST_EXCHANGE_ID)}

    loss_row, dx, sent, (d_norm, d_ple_norm, d_final, d_conv, d_scale) = _forward_backward(
        x[0], p[:, 0], loss_target[0], full, conv_w, scale_w, norm_mix, ple_norm, final_norm, exchange)
    pieces = {name: [None] * wts[name].shape[0] for name in WEIGHTS if wts[name].ndim >= 3 and name != "a_w_conv"}
    for i in range(depth):
        for k, (name, idx) in layer_matrices(i).items():
            pieces[name][idx] = sent[i][k]

    pad = lambda a: jnp.pad(a, ((0, 0), (0, d - a.shape[1])))
    small = jnp.concatenate(d_norm + d_ple_norm + [d_final] + d_conv + d_scale + [pad(loss_row)], axis=0)
    small = jnp.pad(small, ((0, SMALL_ROWS - small.shape[0]), (0, 0)))
    total = _gather_rows(small, reduce=True)
    o = 0
    gsum = {}
    gsum["norm_mix"] = total[o:o + depth]; o += depth
    gsum["ple_norm"] = total[o:o + depth]; o += depth
    gsum["final_norm"] = total[o]; o += 1
    conv_full = total[o:o + n_conv].reshape(n_a, 3, d); o += n_conv
    scale_full = total[o:o + n_b]; o += n_b
    loss = total[o, 0]
    gsum["a_w_conv"] = lax.dynamic_slice_in_dim(conv_full, pos * cw, cw, axis=2)
    gsum["b_scale"] = lax.dynamic_slice_in_dim(scale_full, pos * cw, cw, axis=1)

    token = total
    for i in reversed(range(depth)):
        token = _run_behind(sent[i]["w_out"], token)
    last_token = _run_behind(sent[0]["w_in"], token)
    grad, delta, new_m, new_v = {}, {}, {}, {}
    for k in sorted(WEIGHTS, key=lambda name: name == "a_w_in"):
        if k == "a_w_in":
            upper = [None] * 4
            for j in reversed(range(1, n_a)):
                upper = _adamw_pieces([pieces[k][j]], wts[k], mom[k], var[k], token, j, [a for a in upper if a is not None])
            grad[k], delta[k], new_m[k], new_v[k] = _adamw_pieces(
                [pieces[k][0]], wts[k], mom[k], var[k], last_token, 0, upper)
        elif k in pieces:
            grad[k], delta[k], new_m[k], new_v[k] = _adamw_pieces(pieces[k], wts[k], mom[k], var[k], token)
        else:
            grad[k] = gsum[k]
            delta[k], new_m[k], new_v[k] = _adamw_small(gsum[k], wts[k], mom[k], var[k])
    return (loss, dx[None], *[grad[k] for k in WEIGHTS], *[delta[k] for k in WEIGHTS],
            *[new_m[k] for k in WEIGHTS], *[new_v[k] for k in WEIGHTS])
```

```python
import jax
import jax.numpy as jnp
from jax import lax
from jax.experimental import pallas as pl
from jax.experimental.pallas import tpu as pltpu
from jax.experimental.pallas import tpu_sc as plsc

F32 = jnp.float32
BF16 = jnp.bfloat16
MESH = pl.DeviceIdType.MESH

RMS_EPS = 1e-6
POOL_WINDOWS = (2, 4, 8, 16)
N_POOL_GROUPS = len(POOL_WINDOWS)
ADAM_LR = 0.001
ADAM_B1 = 0.9
ADAM_B2 = 0.999
ADAM_EPS = 1e-08
ADAM_WD = 0.01
ADAM_STEP = 10
N_DEV = 8

HALO = 16
FWD_ROW_TILE = 512
POOL_BWD_ROW_TILE = 512
CONV_BWD_ROW_TILE = 256
BWD_ROW_TILE = 512
VMEM_LIMIT = 56 * 1024 * 1024


def _cparams(*sem):
    return pltpu.CompilerParams(dimension_semantics=sem, vmem_limit_bytes=VMEM_LIMIT)


def _dot(a, b):
    return jnp.dot(a, b, preferred_element_type=F32)


def _dot_nt(a, b):
    return lax.dot_general(a, b, (((1,), (1,)), ((), ())), preferred_element_type=F32)


def _dot_tn(a, b):
    return lax.dot_general(a, b, (((0,), (0,)), ((), ())), preferred_element_type=F32)


def _rms_stats(x):
    r = lax.rsqrt(jnp.mean(x * x, axis=-1, keepdims=True) + RMS_EPS)
    return x * r, r


def _rms_bwd(dy, xh, r, g):
    a = dy * g
    return r * (a - xh * jnp.mean(a * xh, axis=-1, keepdims=True))


def _sigmoid(x):
    return 1.0 / (1.0 + jnp.exp(-x))


def _shift_down(x, k):
    return pltpu.roll(x, k, 0)


def _shift_up(x, k):
    return pltpu.roll(x, x.shape[0] - k, 0)


def _conv_taps(u, u_prev):
    uu = jnp.concatenate([u_prev, u], axis=0)
    return _shift_down(uu, 1)[HALO:], _shift_down(uu, 2)[HALO:]


def _window_mean_minus(uu, row0, window):
    acc = uu
    span = 1
    while span < window:
        acc = acc + _shift_down(acc, span)
        span *= 2
    return acc[HALO:] * _inv_count(uu.shape[0] - HALO, row0, window) - uu[HALO:]


def _inv_count(rows, row0, window):
    t = row0 + lax.broadcasted_iota(jnp.int32, (rows, 1), 0)
    return 1.0 / jnp.minimum(t + 1, window).astype(F32)


def _whole(*shape):
    return pl.BlockSpec(shape, lambda i: (0,) * len(shape), pipeline_mode=pl.Buffered(1))


def _layer_fwd(h, g, w_in, mixer, w_out, pn, w_gate, p_all, w_proj, layer, head=None):
    s, d = h.shape
    n = w_in.shape[1]
    e = w_out.shape[0]
    nsplit = n // e
    gdim = e // N_POOL_GROUPS
    pdim = p_all.shape[2]
    is_conv = mixer[0] == "conv"
    ts = min(FWD_ROW_TILE, s)
    params = mixer[1:]
    head = tuple(head or ())

    def body(*refs):
        h_ref, g_ref, win_ref = refs[:3]
        mix_refs = refs[3:3 + len(params)]
        wo_ref, pn_ref, wg_ref, p_ref, wp_ref = refs[3 + len(params):8 + len(params)]
        head_refs = refs[8 + len(params):8 + len(params) + len(head)]
        proj_ref, hn_ref, o_ref, h1_ref, h2_ref, gl_ref, pp_ref = refs[8 + len(params) + len(head):][:7]
        carry_ref = refs[-1]
        i = pl.program_id(0)

        @pl.when(i == 0)
        def _():
            carry_ref[...] = jnp.zeros_like(carry_ref)

        x = h_ref[...]
        xh, _ = _rms_stats(x)
        hn = (xh * g_ref[...]).astype(BF16)
        hn_ref[...] = hn
        parts = []
        for k in range(nsplit):
            part = _dot(hn, win_ref[:, k * e:(k + 1) * e])
            proj_ref[k] = part.astype(BF16)
            parts.append(part)
        prev = carry_ref[...]
        if is_conv:
            b, c, v, z = parts
            w_ref, = mix_refs
            u = c * v
            u1, u2 = _conv_taps(u, prev)
            mixed = b * (w_ref[0:1, :] * u2 + w_ref[1:2, :] * u1 + w_ref[2:3, :] * u)
        else:
            u, z = parts
            wgrp_ref, sc_ref = mix_refs
            uu = jnp.concatenate([prev, u], axis=0)
            cols = []
            for gi, window in enumerate(POOL_WINDOWS):
                dg = _window_mean_minus(uu[:, gi * gdim:(gi + 1) * gdim], i * ts, window)
                cols.append(_dot(dg.astype(BF16), wgrp_ref[gi]))
            mixed = jnp.concatenate(cols, axis=1) * sc_ref[...]
        carry_ref[...] = u[ts - HALO:]
        o = ((z * _sigmoid(z)) * mixed).astype(BF16)
        o_ref[...] = o
        h1 = x + _dot(o, wo_ref[...])
        h1_ref[...] = h1
        xh1, _ = _rms_stats(h1)
        gl = _dot((xh1 * pn_ref[...]).astype(BF16), wg_ref[...])
        pp = _dot(p_ref[...].astype(BF16), wp_ref[...])
        gl_ref[...] = gl.astype(BF16)
        pp_ref[...] = pp.astype(BF16)
        h2 = h1 + _sigmoid(gl) * pp
        if not head:
            h2_ref[...] = h2
            return
        t_ref, gain_ref = head_refs
        loss_ref, dgain_ref = refs[-3], refs[-2]

        @pl.when(i == 0)
        def _():
            loss_ref[...] = jnp.zeros_like(loss_ref)
            dgain_ref[...] = jnp.zeros_like(dgain_ref)

        gain = gain_ref[...]
        yh, r = _rms_stats(h2)
        err = yh * gain - t_ref[...]
        loss_ref[...] += jnp.full(loss_ref.shape, (0.5 / d) * jnp.sum(err * err), F32)
        dy = err * (1.0 / d)
        dgain_ref[...] += jnp.sum(dy * yh, axis=0, keepdims=True)
        h2_ref[...] = _rms_bwd(dy, yh, r, gain)

    row = lambda width: pl.BlockSpec((ts, width), lambda i: (i, 0))
    mix_specs = [_whole(*a.shape) for a in params]
    head_specs = [row(d), _whole(1, d)] if head else []
    head_out_specs = [_whole(1, 128), _whole(1, d)] if head else []
    head_out_shape = [jax.ShapeDtypeStruct((1, 128), F32), jax.ShapeDtypeStruct((1, d), F32)] if head else []
    outs = pl.pallas_call(
        body, name="layer_fwd",
        grid=(s // ts,),
        in_specs=[row(d), _whole(1, d), _whole(d, n)] + mix_specs
        + [_whole(e, d), _whole(1, d), _whole(d, d),
           pl.BlockSpec((None, ts, pdim), lambda i: (layer, i, 0)), _whole(pdim, d)] + head_specs,
        out_specs=[pl.BlockSpec((nsplit, ts, e), lambda i: (0, i, 0)),
                   row(d), row(e), row(d), row(d), row(d), row(d)] + head_out_specs,
        out_shape=[jax.ShapeDtypeStruct((nsplit, s, e), BF16), jax.ShapeDtypeStruct((s, d), BF16),
                   jax.ShapeDtypeStruct((s, e), BF16), jax.ShapeDtypeStruct((s, d), F32),
                   jax.ShapeDtypeStruct((s, d), F32), jax.ShapeDtypeStruct((s, d), BF16),
                   jax.ShapeDtypeStruct((s, d), BF16)] + head_out_shape,
        scratch_shapes=[pltpu.VMEM((HALO, e), F32)],
        compiler_params=_cparams("arbitrary"),
    )(h, g, w_in, *params, w_out, pn, w_gate, p_all, w_proj, *head)
    proj, hn, o, h1, h2, gl, pp = outs[:7]
    return (h2, *outs[7:]) if head else h2, (proj, hn, o, h1, gl, pp)


def _out_ple_bwd(dh2, gl, pp, h1, p_all, o, pn, wgate, wout, layer):
    s, d = dh2.shape
    e = o.shape[1]
    pdim = p_all.shape[2]
    ts = min(BWD_ROW_TILE, s)
    last = s // ts - 1

    def body(dh2_ref, gl_ref, pp_ref, h1_ref, p_ref, o_ref, pn_ref, wg_ref, wo_ref,
             dh1_ref, do_ref, dwp_ref, dwg_ref, dwo_ref, dpn_ref, awp, awg, awo):
        i = pl.program_id(0)

        @pl.when(i == 0)
        def _():
            awp[...] = jnp.zeros_like(awp)
            awg[...] = jnp.zeros_like(awg)
            awo[...] = jnp.zeros_like(awo)
            dpn_ref[...] = jnp.zeros_like(dpn_ref)

        dh2 = dh2_ref[...]
        gate = _sigmoid(gl_ref[...].astype(F32))
        dpp = (dh2 * gate).astype(BF16)
        dgl = (dh2 * pp_ref[...].astype(F32) * gate * (1.0 - gate)).astype(BF16)
        xh, r = _rms_stats(h1_ref[...])
        pn = pn_ref[...]
        awp[...] += _dot_tn(p_ref[...].astype(BF16), dpp)
        awg[...] += _dot_tn((xh * pn).astype(BF16), dgl)
        dr = _dot_nt(dgl, wg_ref[...])
        dpn_ref[...] += jnp.sum(dr * xh, axis=0, keepdims=True)
        dh1 = dh2 + _rms_bwd(dr, xh, r, pn)
        dh1_ref[...] = dh1
        dh1b = dh1.astype(BF16)
        do_ref[...] = _dot_nt(dh1b, wo_ref[...]).astype(BF16)
        awo[...] += _dot_tn(o_ref[...], dh1b)

        @pl.when(i == last)
        def _():
            dwp_ref[...] = awp[...].astype(BF16)
            dwg_ref[...] = awg[...].astype(BF16)
            dwo_ref[...] = awo[...].astype(BF16)

    row = lambda width: pl.BlockSpec((ts, width), lambda i: (i, 0))
    return pl.pallas_call(
        body, name="out_ple_bwd",
        grid=(s // ts,),
        in_specs=[row(d), row(d), row(d), row(d),
                  pl.BlockSpec((None, ts, pdim), lambda i: (layer, i, 0)),
                  row(e), _whole(1, d), _whole(d, d), _whole(e, d)],
        out_specs=[row(d), row(e), _whole(pdim, d), _whole(d, d), _whole(e, d), _whole(1, d)],
        out_shape=[jax.ShapeDtypeStruct((s, d), F32), jax.ShapeDtypeStruct((s, e), BF16),
                   jax.ShapeDtypeStruct((pdim, d), BF16), jax.ShapeDtypeStruct((d, d), BF16),
                   jax.ShapeDtypeStruct((e, d), BF16), jax.ShapeDtypeStruct((1, d), F32)],
        scratch_shapes=[pltpu.VMEM((pdim, d), F32), pltpu.VMEM((d, d), F32), pltpu.VMEM((e, d), F32)],
        compiler_params=_cparams("arbitrary"),
    )(dh2, gl, pp, h1, p_all, o, pn, wgate, wout)


def _mixer_bwd(do, proj, hn, mixer, w_in, h, g, dh1):
    s, d = h.shape
    nsplit, _, e = proj.shape
    gdim = e // N_POOL_GROUPS
    is_conv = mixer[0] == "conv"
    ts = min(CONV_BWD_ROW_TILE if is_conv else POOL_BWD_ROW_TILE, s)
    nt = s // ts
    params = mixer[1:]
    n_mix_out = 1 if is_conv else 2

    def body(*refs):
        refs = list(refs)
        take = lambda n: [refs.pop(0) for _ in range(n)]
        do_ref, p_ref, ph_ref = take(3)
        mix_refs = take(len(params))
        win_ref, h_ref, g_ref, dh1_ref, hn_ref = take(5)
        dwin_ref, dh_ref, dg_ref = take(3)
        mix_out = take(n_mix_out)
        carry_ref, dp_ref, acc_ref = take(3)
        i = pl.program_id(0)
        tile = nt - 1 - i

        @pl.when(i == 0)
        def _():
            for ref in [carry_ref, dg_ref, acc_ref] + mix_out[-1:] + refs:
                ref[...] = jnp.zeros_like(ref)

        dof = do_ref[...].astype(F32)
        nxt = carry_ref[...]
        if is_conv:
            w_ref, = mix_refs
            dw_ref, = mix_out
            w0, w1, w2 = w_ref[0:1, :], w_ref[1:2, :], w_ref[2:3, :]
            b, c, v, z = [p_ref[k].astype(F32) for k in range(4)]
            u = c * v
            u_prev = jnp.where(tile == 0, 0.0, ph_ref[1].astype(F32) * ph_ref[2].astype(F32))
            u1, u2 = _conv_taps(u, u_prev)
            conv = w0 * u2 + w1 * u1 + w2 * u
            sig = _sigmoid(z)
            sz = z * sig
            dy = dof * sz
            dp_ref[3] = (dof * (b * conv) * (sig + sz * (1.0 - sig))).astype(BF16)
            dp_ref[0] = (dy * conv).astype(BF16)
            dconv = dy * b
            dw_ref[0:1, :] += jnp.sum(dconv * u2, axis=0, keepdims=True)
            dw_ref[1:2, :] += jnp.sum(dconv * u1, axis=0, keepdims=True)
            dw_ref[2:3, :] += jnp.sum(dconv * u, axis=0, keepdims=True)
            dcc = jnp.concatenate([dconv, nxt], axis=0)
            du = w2 * dconv + w1 * _shift_up(dcc, 1)[:ts] + w0 * _shift_up(dcc, 2)[:ts]
            carry_ref[...] = dconv[:HALO]
            dp_ref[1] = (du * v).astype(BF16)
            dp_ref[2] = (du * c).astype(BF16)
        else:
            wgrp_ref, sc_ref = mix_refs
            dsc_ref = mix_out[1]
            agrp, = refs
            u = p_ref[0].astype(F32)
            z = p_ref[1].astype(F32)
            u_prev = jnp.where(tile == 0, 0.0, ph_ref[0].astype(F32))
            uu = jnp.concatenate([u_prev, u], axis=0)
            sig = _sigmoid(z)
            sz = z * sig
            dm = dof * sz
            dsilu = dof * (sig + sz * (1.0 - sig))
            for gi, window in enumerate(POOL_WINDOWS):
                cols = slice(gi * gdim, (gi + 1) * gdim)
                w = wgrp_ref[gi]
                scale = sc_ref[:, cols]
                db = _window_mean_minus(uu[:, cols], tile * ts, window).astype(BF16)
                mr = _dot(db, w)
                dp_ref[1, :, cols] = (dsilu[:, cols] * (mr * scale)).astype(BF16)
                dmg = dm[:, cols]
                dmr = (dmg * scale).astype(BF16)
                agrp[gi] += _dot_tn(db, dmr)
                dsc_ref[:, cols] += jnp.sum(dmg * mr, axis=0, keepdims=True)
                dd = _dot_nt(dmr, w)
                ddq = dd * _inv_count(ts, tile * ts, window)
                acc = jnp.concatenate([ddq, nxt[:, cols]], axis=0)
                span = 1
                while span < window:
                    acc = acc + _shift_up(acc, span)
                    span *= 2
                carry_ref[:, cols] = ddq[:HALO]
                dp_ref[0, :, cols] = (acc[:ts] - dd).astype(BF16)

        dhn = _dot_nt(dp_ref[0], win_ref[:, 0:e])
        for k in range(1, nsplit):
            dhn += _dot_nt(dp_ref[k], win_ref[:, k * e:(k + 1) * e])
        xh, r = _rms_stats(h_ref[...])
        dg_ref[...] += jnp.sum(dhn * xh, axis=0, keepdims=True)
        dh_ref[...] = dh1_ref[...] + _rms_bwd(dhn, xh, r, g_ref[...])
        hn_tile = hn_ref[...]
        for k in range(nsplit):
            acc_ref[:, k * e:(k + 1) * e] += _dot_tn(hn_tile, dp_ref[k])

        @pl.when(i == nt - 1)
        def _():
            dwin_ref[...] = acc_ref[...].astype(BF16)
            if not is_conv:
                mix_out[0][...] = refs[0][...].astype(BF16)

    rev = lambda width: pl.BlockSpec((ts, width), lambda i: (nt - 1 - i, 0))
    halo_blocks = ts // HALO
    in_specs = [rev(e),
                pl.BlockSpec((nsplit, ts, e), lambda i: (0, nt - 1 - i, 0)),
                pl.BlockSpec((nsplit, HALO, e), lambda i: (0, jnp.maximum((nt - 1 - i) * halo_blocks - 1, 0), 0))]
    in_specs += [_whole(*a.shape) for a in params]
    in_specs += [_whole(d, nsplit * e), rev(d), _whole(1, d), rev(d), rev(d)]
    out_specs = [_whole(d, nsplit * e), rev(d), _whole(1, d)]
    out_shape = [jax.ShapeDtypeStruct((d, nsplit * e), BF16), jax.ShapeDtypeStruct((s, d), F32),
                 jax.ShapeDtypeStruct((1, d), F32)]
    scratch = [pltpu.VMEM((HALO, e), F32), pltpu.VMEM((nsplit, ts, e), BF16), pltpu.VMEM((d, nsplit * e), F32)]
    if is_conv:
        out_specs += [_whole(3, e)]
        out_shape += [jax.ShapeDtypeStruct((3, e), F32)]
    else:
        out_specs += [_whole(N_POOL_GROUPS, gdim, gdim), _whole(1, e)]
        out_shape += [jax.ShapeDtypeStruct((N_POOL_GROUPS, gdim, gdim), BF16), jax.ShapeDtypeStruct((1, e), F32)]
        scratch += [pltpu.VMEM((N_POOL_GROUPS, gdim, gdim), F32)]
    return pl.pallas_call(
        body, name="mixer_bwd",
        grid=(nt,),
        in_specs=in_specs, out_specs=out_specs, out_shape=out_shape, scratch_shapes=scratch,
        compiler_params=_cparams("arbitrary"),
    )(do, proj, proj, *params, w_in, h, g, dh1, hn)


def _forward_backward(xs, ps, target, full, conv_w, scale_w, norm_mix, ple_norm, final_norm, exchange):
    depth = len(full)
    row = lambda a, i: a[i][None, :]
    mixer_of = lambda i: ("conv", conv_w[i // 2]) if i % 2 == 0 else ("pool", full[i]["w_grp"], row(scale_w, i // 2))

    saved = []
    h = xs
    for i in range(depth):
        w = full[i]
        head = (target, final_norm[None, :]) if i == depth - 1 else None
        h_next, acts = _layer_fwd(h, row(norm_mix, i), w["w_in"], mixer_of(i), w["w_out"], row(ple_norm, i),
                                  w["gate"], ps, w["proj"], i, head)
        saved.append((h, *acts))
        h = h_next
    dh, loss_row, d_final = h

    d_norm, d_ple_norm, d_conv, d_scale, sent = [None] * depth, [None] * depth, [], [], [None] * depth
    for i in reversed(range(depth)):
        w = full[i]
        h_in, proj, hn, o, h1, gl, pp = saved[i]
        g = {}
        dh1, do, g["proj"], g["gate"], g["w_out"], d_ple_norm[i] = _out_ple_bwd(
            dh, gl, pp, h1, ps, o, row(ple_norm, i), w["gate"], w["w_out"], i)
        g["w_in"], dh, d_norm[i], *mixer_grads = _mixer_bwd(
            do, proj, hn, mixer_of(i), w["w_in"], h_in, row(norm_mix, i), dh1)
        if i % 2 == 0:
            d_conv.insert(0, mixer_grads[0])
        else:
            g["w_grp"] = mixer_grads[0]
            d_scale.insert(0, mixer_grads[1])
        sent[i] = exchange(i, g)
    return loss_row, dh, sent, (d_norm, d_ple_norm, d_final, d_conv, d_scale)


VMEM_SPEC = pl.BlockSpec(memory_space=pltpu.VMEM)

FLIPS = [(fx, fy, fc) for fx in (0, 1) for fy in (0, 1) for fc in (0, 1)][1:]
SHARD_AXIS = {"w_in": 1, "w_out": 0, "w_grp": 1, "gate": 0, "proj": 1}


def _my_place():
    return lax.axis_index("x"), lax.axis_index("y"), lax.axis_index("c")


def _position(place):
    x, y, c = place
    return 4 * x + 2 * y + c


def _flip(place, flips):
    return tuple(1 - v if f else v for v, f in zip(place, flips))


def _shard_of(ref, axis, pos, n):
    idx = [slice(None)] * len(ref.shape)
    idx[axis] = pl.ds(pl.multiple_of(pos * n, n), n)
    return ref.at[tuple(idx)]


def _sequencer_mesh():
    return plsc.ScalarSubcoreMesh(axis_name="sequencer", num_cores=1)


def _handshake(peers):
    barrier = pltpu.get_barrier_semaphore()
    for peer in peers:
        pl.semaphore_signal(barrier, inc=1, device_id=peer, device_id_type=MESH)
    pl.semaphore_wait(barrier, len(peers))


def _all_gather_layer(shards, collective_id):
    names = list(shards)
    nt = len(names)
    axes = [SHARD_AXIS[k] for k in names]
    widths = [shards[k].shape[SHARD_AXIS[k]] for k in names]

    def full_shape(k):
        shp = list(shards[k].shape)
        shp[SHARD_AXIS[k]] *= N_DEV
        return tuple(shp)

    def body(*refs):
        ins, outs = refs[:nt], refs[nt:2 * nt]
        send_sems, recv_sems, local_sem = refs[2 * nt:]
        me = _my_place()
        x, y, c = me
        sibling = (x, y, 1 - c)
        chips = [(1 - x, y), (x, 1 - y), (1 - x, 1 - y)]
        _handshake([sibling] + [(*chip, c) for chip in chips])

        def block(t, place):
            return _shard_of(outs[t], axes[t], _position(place), widths[t])

        def copy(t, k, place, to, src=None):
            return pltpu.make_async_remote_copy(
                src_ref=block(t, place) if src is None else src, dst_ref=block(t, place),
                send_sem=send_sems.at[k], recv_sem=recv_sems.at[k], device_id=to, device_id_type=MESH)

        mine = [pltpu.make_async_copy(ins[t], block(t, me), local_sem) for t in range(nt)]
        for cp in mine:
            cp.start()
        first = []
        for j, chip in enumerate(chips):
            first += [copy(t, 1 + j, me, (*chip, c), src=ins[t]) for t in range(nt)]
        first += [copy(t, 0, me, sibling, src=ins[t]) for t in range(nt)]
        for cp in first:
            cp.start()
        passed = []
        for j, chip in enumerate(chips):
            for t in range(nt):
                copy(t, 1 + j, (*chip, c), me).wait_recv()
            for t in range(nt):
                fwd = copy(t, 4 + j, (*chip, c), sibling)
                fwd.start()
                passed.append(fwd)
        for t in range(nt):
            copy(t, 0, sibling, me).wait_recv()
        for j, chip in enumerate(chips):
            for t in range(nt):
                copy(t, 4 + j, (*chip, 1 - c), me).wait_recv()
        for cp in first + passed:
            cp.wait_send()
        for cp in mine:
            cp.wait()

    outs = pl.kernel(
        body, name=f"all_gather_layer_{collective_id}",
        out_type=[jax.ShapeDtypeStruct(full_shape(k), shards[k].dtype) for k in names],
        mesh=_sequencer_mesh(),
        scratch_types=[pltpu.SemaphoreType.DMA((7,)), pltpu.SemaphoreType.DMA((7,)), pltpu.SemaphoreType.DMA],
        compiler_params=pltpu.CompilerParams(collective_id=collective_id),
    )(*[shards[k] for k in names])
    return dict(zip(names, outs))


def _exchange_layer(grads, collective_id):
    names = list(grads)
    nt = len(names)
    axes = [SHARD_AXIS[k] for k in names]
    widths = [grads[k].shape[SHARD_AXIS[k]] // N_DEV for k in names]

    def slot_shape(t):
        shp = list(grads[names[t]].shape)
        shp[axes[t]] = widths[t]
        return (N_DEV, *shp)

    def body(*refs):
        ins, outs = refs[:nt], refs[nt:2 * nt]
        send_sems, recv_sems, local_sem = refs[2 * nt:]
        me = _my_place()
        mine = _position(me)
        _handshake([_flip(me, flips) for flips in FLIPS])
        local = [pltpu.make_async_copy(_shard_of(ins[t], axes[t], mine, widths[t]), outs[t].at[mine], local_sem)
                 for t in range(nt)]
        for cp in local:
            cp.start()
        copies = []
        for k, flips in enumerate(FLIPS):
            peer = _flip(me, flips)
            for t in range(nt):
                cp = pltpu.make_async_remote_copy(
                    src_ref=_shard_of(ins[t], axes[t], _position(peer), widths[t]), dst_ref=outs[t].at[mine],
                    send_sem=send_sems.at[k], recv_sem=recv_sems.at[k], device_id=peer, device_id_type=MESH)
                cp.start()
                copies.append(cp)
        for cp in copies:
            cp.wait()
        for cp in local:
            cp.wait()

    outs = pl.kernel(
        body, name=f"exchange_layer_{collective_id}",
        out_type=[jax.ShapeDtypeStruct(slot_shape(t), BF16) for t in range(nt)],
        mesh=_sequencer_mesh(),
        scratch_types=[pltpu.SemaphoreType.DMA((7,)), pltpu.SemaphoreType.DMA((7,)), pltpu.SemaphoreType.DMA],
        compiler_params=pltpu.CompilerParams(collective_id=collective_id),
    )(*[grads[k] for k in names])
    return dict(zip(names, outs))


def _gather_rows(buf, reduce):
    r, c = buf.shape

    def body(in_ref, out_ref, *scratch):
        if reduce:
            all_ref, send_sems, recv_sems = scratch
        else:
            all_ref = out_ref
            send_sems, recv_sems = scratch
        me = _my_place()
        all_ref[_position(me)] = in_ref[...]
        copies = []
        for k, flips in enumerate(FLIPS):
            cp = pltpu.make_async_remote_copy(
                src_ref=in_ref, dst_ref=all_ref.at[_position(me)],
                send_sem=send_sems.at[k], recv_sem=recv_sems.at[k], device_id=_flip(me, flips), device_id_type=MESH)
            cp.start()
            copies.append(cp)
        for cp in copies:
            cp.wait()
        if reduce:
            total = all_ref[0]
            for j in range(1, N_DEV):
                total = total + all_ref[j]
            out_ref[...] = total

    return pl.pallas_call(
        body, name="sum_rows" if reduce else "gather_rows",
        in_specs=[VMEM_SPEC], out_specs=VMEM_SPEC,
        out_shape=jax.ShapeDtypeStruct((r, c) if reduce else (N_DEV, r, c), F32),
        scratch_shapes=([pltpu.VMEM((N_DEV, r, c), F32)] if reduce else [])
        + [pltpu.SemaphoreType.DMA((7,)), pltpu.SemaphoreType.DMA((7,))],
    )(buf)


def _adamw_math(w, g, m, v):
    m = ADAM_B1 * m + (1.0 - ADAM_B1) * g
    v = ADAM_B2 * v + (1.0 - ADAM_B2) * (g * g)
    m_hat = m / (1.0 - ADAM_B1 ** ADAM_STEP)
    v_hat = v / (1.0 - ADAM_B2 ** ADAM_STEP)
    delta = -ADAM_LR * (m_hat / (jnp.sqrt(v_hat) + ADAM_EPS) + ADAM_WD * w)
    return delta, m, v


def _run_behind(x, token):
    def body(x_ref, token_ref, out_ref):
        out_ref[...] = jnp.zeros_like(out_ref)

    any_spec = pl.BlockSpec(memory_space=pl.ANY)
    return pl.pallas_call(
        body, name="run_behind",
        in_specs=[any_spec, any_spec], out_specs=VMEM_SPEC,
        out_shape=jax.ShapeDtypeStruct((8, 128), F32),
    )(x, token)


def _adamw_pieces(pieces, w, m, v, after, layer=None, into=()):
    shape = w.shape
    nl = shape[0]
    cols = shape[-1]
    rows = w.size // (nl * cols)
    first, nh = (0, nl) if layer is None else (layer, 1)
    tr = min(256 // nh, rows)
    flat3 = lambda a: a.reshape(nl, rows, cols)
    into = [flat3(a) for a in into]

    def body(*refs):
        p_refs = refs[:nh]
        w_ref, m_ref, v_ref = refs[nh:nh + 3]
        g_ref, d_ref, nm_ref, nv_ref = refs[-4:]
        for l in range(nh):
            g = p_refs[l][0].astype(F32)
            for j in range(1, N_DEV):
                g = g + p_refs[l][j].astype(F32)
            g_ref[l] = g
            d_ref[l], nm_ref[l], nv_ref[l] = _adamw_math(w_ref[l], g, m_ref[l], v_ref[l])

    blk = pl.BlockSpec((nh, tr, cols), lambda i: (first, i, 0))
    any_spec = pl.BlockSpec(memory_space=pl.ANY)
    outs = pl.pallas_call(
        body, name="adamw_pieces",
        grid=(rows // tr,),
        in_specs=[pl.BlockSpec((N_DEV, tr, cols), lambda i: (0, i, 0))] * nh
        + [blk, blk, blk, any_spec] + [any_spec] * len(into),
        out_specs=[blk] * 4,
        out_shape=[jax.ShapeDtypeStruct((nl, rows, cols), F32)] * 4,
        input_output_aliases={nh + 4 + k: k for k in range(len(into))},
        compiler_params=_cparams("parallel"),
    )(*[a.reshape(N_DEV, rows, cols) for a in pieces], flat3(w), flat3(m), flat3(v), after, *into)
    return [a.reshape(shape) for a in outs]


def _adamw_small(g, w, m, v):
    shape = w.shape
    two = lambda a: a.reshape(-1, shape[-1])

    def body(g_ref, w_ref, m_ref, v_ref, d_ref, nm_ref, nv_ref):
        d_ref[...], nm_ref[...], nv_ref[...] = _adamw_math(w_ref[...], g_ref[...], m_ref[...], v_ref[...])

    outs = pl.pallas_call(
        body, name="adamw_small",
        in_specs=[VMEM_SPEC] * 4, out_specs=[VMEM_SPEC] * 3,
        out_shape=[jax.ShapeDtypeStruct(two(w).shape, F32)] * 3,
    )(two(g), two(w), two(m), two(v))
    return [a.reshape(shape) for a in outs]


WEIGHTS = ("norm_mix", "a_w_in", "a_w_conv", "a_w_out", "b_w_in", "b_w_grp", "b_scale", "b_w_out",
           "ple_norm", "ple_w_gate", "ple_w_proj", "final_norm")
SMALL_ROWS = 24
GATHER_ID = 0
EXCHANGE_ID = 4
LAST_EXCHANGE_ID = 8


def kernel(x, p, norm_mix, a_w_in, a_w_conv, a_w_out, b_w_in, b_w_grp, b_scale, b_w_out, ple_norm, ple_w_gate, ple_w_proj, final_norm, loss_target, m_norm_mix, m_a_w_in, m_a_w_conv, m_a_w_out, m_b_w_in, m_b_w_grp, m_b_scale, m_b_w_out, m_ple_norm, m_ple_w_gate, m_ple_w_proj, m_final_norm, v_norm_mix, v_a_w_in, v_a_w_conv, v_a_w_out, v_b_w_in, v_b_w_grp, v_b_scale, v_b_w_out, v_ple_norm, v_ple_w_gate, v_ple_w_proj, v_final_norm):
    wts = dict(norm_mix=norm_mix, a_w_in=a_w_in, a_w_conv=a_w_conv, a_w_out=a_w_out, b_w_in=b_w_in, b_w_grp=b_w_grp,
               b_scale=b_scale, b_w_out=b_w_out, ple_norm=ple_norm, ple_w_gate=ple_w_gate, ple_w_proj=ple_w_proj,
               final_norm=final_norm)
    mom = dict(norm_mix=m_norm_mix, a_w_in=m_a_w_in, a_w_conv=m_a_w_conv, a_w_out=m_a_w_out, b_w_in=m_b_w_in,
               b_w_grp=m_b_w_grp, b_scale=m_b_scale, b_w_out=m_b_w_out, ple_norm=m_ple_norm, ple_w_gate=m_ple_w_gate,
               ple_w_proj=m_ple_w_proj, final_norm=m_final_norm)
    var = dict(norm_mix=v_norm_mix, a_w_in=v_a_w_in, a_w_conv=v_a_w_conv, a_w_out=v_a_w_out, b_w_in=v_b_w_in,
               b_w_grp=v_b_w_grp, b_scale=v_b_scale, b_w_out=v_b_w_out, ple_norm=v_ple_norm, ple_w_gate=v_ple_w_gate,
               ple_w_proj=v_ple_w_proj, final_norm=v_final_norm)
    d = x.shape[2]
    depth = norm_mix.shape[0]
    n_a, n_b = a_w_conv.shape[0], b_scale.shape[0]
    cw = a_w_conv.shape[2]
    pos = _position(_my_place())

    def layer_matrices(i):
        j = i // 2
        mixer = {"w_in": ("a_w_in", j), "w_out": ("a_w_out", j)} if i % 2 == 0 else \
                {"w_in": ("b_w_in", j), "w_grp": ("b_w_grp", j), "w_out": ("b_w_out", j)}
        return {**mixer, "gate": ("ple_w_gate", i), "proj": ("ple_w_proj", i)}

    full = [_all_gather_layer({k: wts[name][idx].astype(BF16) for k, (name, idx) in layer_matrices(i).items()},
                              GATHER_ID + i) for i in range(depth)]
    vec_rows = jnp.concatenate([a_w_conv.reshape(-1, cw), b_scale], axis=0)
    vecs = _gather_rows(vec_rows, reduce=False)
    n_conv = 3 * n_a
    conv_w = vecs[:, :n_conv].transpose(1, 0, 2).reshape(n_a, 3, N_DEV * cw)
    scale_w = vecs[:, n_conv:].transpose(1, 0, 2).reshape(n_b, N_DEV * cw)

    def exchange(i, g):
        if i > 0:
            return _exchange_layer(g, EXCHANGE_ID + i)
        early = {k: a for k, a in g.items() if k != "w_in"}
        return {**_exchange_layer(early, EXCHANGE_ID), **_exchange_layer({"w_in": g["w_in"]}, LAST_EXCHANGE_ID)}

    loss_row, dx, sent, (d_norm, d_ple_norm, d_final, d_conv, d_scale) = _forward_backward(
        x[0], p[:, 0], loss_target[0], full, conv_w, scale_w, norm_mix, ple_norm, final_norm, exchange)
    pieces = {name: [None] * wts[name].shape[0] for name in WEIGHTS if wts[name].ndim >= 3 and name != "a_w_conv"}
    for i in range(depth):
        for k, (name, idx) in layer_matrices(i).items():
            pieces[name][idx] = sent[i][k]

    pad = lambda a: jnp.pad(a, ((0, 0), (0, d - a.shape[1])))
    small = jnp.concatenate(d_norm + d_ple_norm + [d_final] + d_conv + d_scale + [pad(loss_row)], axis=0)
    small = jnp.pad(small, ((0, SMALL_ROWS - small.shape[0]), (0, 0)))
    total = _gather_rows(small, reduce=True)
    o = 0
    gsum = {}
    gsum["norm_mix"] = total[o:o + depth]; o += depth
    gsum["ple_norm"] = total[o:o + depth]; o += depth
    gsum["final_norm"] = total[o]; o += 1
    conv_full = total[o:o + n_conv].reshape(n_a, 3, d); o += n_conv
    scale_full = total[o:o + n_b]; o += n_b
    loss = total[o, 0]
    gsum["a_w_conv"] = lax.dynamic_slice_in_dim(conv_full, pos * cw, cw, axis=2)
    gsum["b_scale"] = lax.dynamic_slice_in_dim(scale_full, pos * cw, cw, axis=1)

    token = total
    for i in reversed(range(depth)):
        token = _run_behind(sent[i]["w_out"], token)
    last_token = _run_behind(sent[0]["w_in"], token)
    grad, delta, new_m, new_v = {}, {}, {}, {}
    for k in sorted(WEIGHTS, key=lambda name: name == "a_w_in"):
        if k == "a_w_in":
            upper = [None] * 4
            for j in reversed(range(1, n_a)):
                upper = _adamw_pieces([pieces[k][j]], wts[k], mom[k], var[k], token, j, [a for a in upper if a is not None])
            grad[k], delta[k], new_m[k], new_v[k] = _adamw_pieces(
                [pieces[k][0]], wts[k], mom[k], var[k], last_token, 0, upper)
        elif k in pieces:
            grad[k], delta[k], new_m[k], new_v[k] = _adamw_pieces(pieces[k], wts[k], mom[k], var[k], token)
        else:
            grad[k] = gsum[k]
            delta[k], new_m[k], new_v[k] = _adamw_small(gsum[k], wts[k], mom[k], var[k])
    return (loss, dx[None], *[grad[k] for k in WEIGHTS], *[delta[k] for k in WEIGHTS],
            *[new_m[k] for k in WEIGHTS], *[new_v[k] for k in WEIGHTS])
```

```python
import jax
import jax.numpy as jnp
from jax import lax
from jax.experimental import pallas as pl
from jax.experimental.pallas import tpu as pltpu
from jax.experimental.pallas import tpu_sc as plsc

F32 = jnp.float32
BF16 = jnp.bfloat16
MESH = pl.DeviceIdType.MESH

RMS_EPS = 1e-6
POOL_WINDOWS = (2, 4, 8, 16)
N_POOL_GROUPS = len(POOL_WINDOWS)
ADAM_LR = 0.001
ADAM_B1 = 0.9
ADAM_B2 = 0.999
ADAM_EPS = 1e-08
ADAM_WD = 0.01
ADAM_STEP = 10
N_DEV = 8

HALO = 16
FWD_ROW_TILE = 512
POOL_BWD_ROW_TILE = 512
CONV_BWD_ROW_TILE = 256
ADAMW_BLOCK_ROWS = 512
BWD_ROW_TILE = 512
VMEM_LIMIT = 56 * 1024 * 1024


def _cparams(*sem):
    return pltpu.CompilerParams(dimension_semantics=sem, vmem_limit_bytes=VMEM_LIMIT)


def _dot(a, b):
    return jnp.dot(a, b, preferred_element_type=F32)


def _dot_nt(a, b):
    return lax.dot_general(a, b, (((1,), (1,)), ((), ())), preferred_element_type=F32)


def _dot_tn(a, b):
    return lax.dot_general(a, b, (((0,), (0,)), ((), ())), preferred_element_type=F32)


def _rms_stats(x):
    r = lax.rsqrt(jnp.mean(x * x, axis=-1, keepdims=True) + RMS_EPS)
    return x * r, r


def _rms_bwd(dy, xh, r, g):
    a = dy * g
    return r * (a - xh * jnp.mean(a * xh, axis=-1, keepdims=True))


def _sigmoid(x):
    return 1.0 / (1.0 + jnp.exp(-x))


def _shift_down(x, k):
    return pltpu.roll(x, k, 0)


def _shift_up(x, k):
    return pltpu.roll(x, x.shape[0] - k, 0)


def _conv_taps(u, u_prev):
    uu = jnp.concatenate([u_prev, u], axis=0)
    return _shift_down(uu, 1)[HALO:], _shift_down(uu, 2)[HALO:]


def _window_mean_minus(uu, row0, window):
    acc = uu
    span = 1
    while span < window:
        acc = acc + _shift_down(acc, span)
        span *= 2
    return acc[HALO:] * _inv_count(uu.shape[0] - HALO, row0, window) - uu[HALO:]


def _inv_count(rows, row0, window):
    t = row0 + lax.broadcasted_iota(jnp.int32, (rows, 1), 0)
    return 1.0 / jnp.minimum(t + 1, window).astype(F32)


def _whole(*shape):
    return pl.BlockSpec(shape, lambda i: (0,) * len(shape), pipeline_mode=pl.Buffered(1))


def _layer_fwd(h, g, w_in, mixer, w_out, pn, w_gate, p_all, w_proj, layer, head=None):
    s, d = h.shape
    n = w_in.shape[1]
    e = w_out.shape[0]
    nsplit = n // e
    gdim = e // N_POOL_GROUPS
    pdim = p_all.shape[2]
    is_conv = mixer[0] == "conv"
    ts = min(FWD_ROW_TILE, s)
    params = mixer[1:]
    head = tuple(head or ())

    def body(*refs):
        h_ref, g_ref, win_ref = refs[:3]
        mix_refs = refs[3:3 + len(params)]
        wo_ref, pn_ref, wg_ref, p_ref, wp_ref = refs[3 + len(params):8 + len(params)]
        head_refs = refs[8 + len(params):8 + len(params) + len(head)]
        proj_ref, hn_ref, o_ref, h1_ref, h2_ref, gl_ref, pp_ref = refs[8 + len(params) + len(head):][:7]
        carry_ref = refs[-1]
        i = pl.program_id(0)

        @pl.when(i == 0)
        def _():
            carry_ref[...] = jnp.zeros_like(carry_ref)

        x = h_ref[...]
        xh, _ = _rms_stats(x)
        hn = (xh * g_ref[...]).astype(BF16)
        hn_ref[...] = hn
        parts = []
        for k in range(nsplit):
            part = _dot(hn, win_ref[:, k * e:(k + 1) * e])
            proj_ref[k] = part.astype(BF16)
            parts.append(part)
        prev = carry_ref[...]
        if is_conv:
            b, c, v, z = parts
            w_ref, = mix_refs
            u = c * v
            u1, u2 = _conv_taps(u, prev)
            mixed = b * (w_ref[0:1, :] * u2 + w_ref[1:2, :] * u1 + w_ref[2:3, :] * u)
        else:
            u, z = parts
            wgrp_ref, sc_ref = mix_refs
            uu = jnp.concatenate([prev, u], axis=0)
            cols = []
            for gi, window in enumerate(POOL_WINDOWS):
                dg = _window_mean_minus(uu[:, gi * gdim:(gi + 1) * gdim], i * ts, window)
                cols.append(_dot(dg.astype(BF16), wgrp_ref[gi]))
            mixed = jnp.concatenate(cols, axis=1) * sc_ref[...]
        carry_ref[...] = u[ts - HALO:]
        o = ((z * _sigmoid(z)) * mixed).astype(BF16)
        o_ref[...] = o
        h1 = x + _dot(o, wo_ref[...])
        h1_ref[...] = h1
        xh1, _ = _rms_stats(h1)
        gl = _dot((xh1 * pn_ref[...]).astype(BF16), wg_ref[...])
        pp = _dot(p_ref[...].astype(BF16), wp_ref[...])
        gl_ref[...] = gl.astype(BF16)
        pp_ref[...] = pp.astype(BF16)
        h2 = h1 + _sigmoid(gl) * pp
        if not head:
            h2_ref[...] = h2
            return
        t_ref, gain_ref = head_refs
        loss_ref, dgain_ref = refs[-3], refs[-2]

        @pl.when(i == 0)
        def _():
            loss_ref[...] = jnp.zeros_like(loss_ref)
            dgain_ref[...] = jnp.zeros_like(dgain_ref)

        gain = gain_ref[...]
        yh, r = _rms_stats(h2)
        err = yh * gain - t_ref[...]
        loss_ref[...] += jnp.full(loss_ref.shape, (0.5 / d) * jnp.sum(err * err), F32)
        dy = err * (1.0 / d)
        dgain_ref[...] += jnp.sum(dy * yh, axis=0, keepdims=True)
        h2_ref[...] = _rms_bwd(dy, yh, r, gain)

    row = lambda width: pl.BlockSpec((ts, width), lambda i: (i, 0))
    mix_specs = [_whole(*a.shape) for a in params]
    head_specs = [row(d), _whole(1, d)] if head else []
    head_out_specs = [_whole(1, 128), _whole(1, d)] if head else []
    head_out_shape = [jax.ShapeDtypeStruct((1, 128), F32), jax.ShapeDtypeStruct((1, d), F32)] if head else []
    outs = pl.pallas_call(
        body, name="layer_fwd",
        grid=(s // ts,),
        in_specs=[row(d), _whole(1, d), _whole(d, n)] + mix_specs
        + [_whole(e, d), _whole(1, d), _whole(d, d),
           pl.BlockSpec((None, ts, pdim), lambda i: (layer, i, 0)), _whole(pdim, d)] + head_specs,
        out_specs=[pl.BlockSpec((nsplit, ts, e), lambda i: (0, i, 0)),
                   row(d), row(e), row(d), row(d), row(d), row(d)] + head_out_specs,
        out_shape=[jax.ShapeDtypeStruct((nsplit, s, e), BF16), jax.ShapeDtypeStruct((s, d), BF16),
                   jax.ShapeDtypeStruct((s, e), BF16), jax.ShapeDtypeStruct((s, d), F32),
                   jax.ShapeDtypeStruct((s, d), F32), jax.ShapeDtypeStruct((s, d), BF16),
                   jax.ShapeDtypeStruct((s, d), BF16)] + head_out_shape,
        scratch_shapes=[pltpu.VMEM((HALO, e), F32)],
        compiler_params=_cparams("arbitrary"),
    )(h, g, w_in, *params, w_out, pn, w_gate, p_all, w_proj, *head)
    proj, hn, o, h1, h2, gl, pp = outs[:7]
    return (h2, *outs[7:]) if head else h2, (proj, hn, o, h1, gl, pp)


def _out_ple_bwd(dh2, gl, pp, h1, p_all, o, pn, wgate, wout, layer):
    s, d = dh2.shape
    e = o.shape[1]
    pdim = p_all.shape[2]
    ts = min(BWD_ROW_TILE, s)
    last = s // ts - 1

    def body(dh2_ref, gl_ref, pp_ref, h1_ref, p_ref, o_ref, pn_ref, wg_ref, wo_ref,
             dh1_ref, do_ref, dwp_ref, dwg_ref, dwo_ref, dpn_ref, awp, awg, awo):
        i = pl.program_id(0)

        @pl.when(i == 0)
        def _():
            awp[...] = jnp.zeros_like(awp)
            awg[...] = jnp.zeros_like(awg)
            awo[...] = jnp.zeros_like(awo)
            dpn_ref[...] = jnp.zeros_like(dpn_ref)

        dh2 = dh2_ref[...]
        gate = _sigmoid(gl_ref[...].astype(F32))
        dpp = (dh2 * gate).astype(BF16)
        dgl = (dh2 * pp_ref[...].astype(F32) * gate * (1.0 - gate)).astype(BF16)
        xh, r = _rms_stats(h1_ref[...])
        pn = pn_ref[...]
        awp[...] += _dot_tn(p_ref[...].astype(BF16), dpp)
        awg[...] += _dot_tn((xh * pn).astype(BF16), dgl)
        dr = _dot_nt(dgl, wg_ref[...])
        dpn_ref[...] += jnp.sum(dr * xh, axis=0, keepdims=True)
        dh1 = dh2 + _rms_bwd(dr, xh, r, pn)
        dh1_ref[...] = dh1
        dh1b = dh1.astype(BF16)
        do_ref[...] = _dot_nt(dh1b, wo_ref[...]).astype(BF16)
        awo[...] += _dot_tn(o_ref[...], dh1b)

        @pl.when(i == last)
        def _():
            dwp_ref[...] = awp[...].astype(BF16)
            dwg_ref[...] = awg[...].astype(BF16)
            dwo_ref[...] = awo[...].astype(BF16)

    row = lambda width: pl.BlockSpec((ts, width), lambda i: (i, 0))
    return pl.pallas_call(
        body, name="out_ple_bwd",
        grid=(s // ts,),
        in_specs=[row(d), row(d), row(d), row(d),
                  pl.BlockSpec((None, ts, pdim), lambda i: (layer, i, 0)),
                  row(e), _whole(1, d), _whole(d, d), _whole(e, d)],
        out_specs=[row(d), row(e), _whole(pdim, d), _whole(d, d), _whole(e, d), _whole(1, d)],
        out_shape=[jax.ShapeDtypeStruct((s, d), F32), jax.ShapeDtypeStruct((s, e), BF16),
                   jax.ShapeDtypeStruct((pdim, d), BF16), jax.ShapeDtypeStruct((d, d), BF16),
                   jax.ShapeDtypeStruct((e, d), BF16), jax.ShapeDtypeStruct((1, d), F32)],
        scratch_shapes=[pltpu.VMEM((pdim, d), F32), pltpu.VMEM((d, d), F32), pltpu.VMEM((e, d), F32)],
        compiler_params=_cparams("arbitrary"),
    )(dh2, gl, pp, h1, p_all, o, pn, wgate, wout)


def _mixer_bwd(do, proj, hn, mixer, w_in, h, g, dh1):
    s, d = h.shape
    nsplit, _, e = proj.shape
    gdim = e // N_POOL_GROUPS
    is_conv = mixer[0] == "conv"
    ts = min(CONV_BWD_ROW_TILE if is_conv else POOL_BWD_ROW_TILE, s)
    nt = s // ts
    params = mixer[1:]
    n_mix_out = 1 if is_conv else 2

    def body(*refs):
        refs = list(refs)
        take = lambda n: [refs.pop(0) for _ in range(n)]
        do_ref, p_ref, ph_ref = take(3)
        mix_refs = take(len(params))
        win_ref, h_ref, g_ref, dh1_ref, hn_ref = take(5)
        dwin_ref, dh_ref, dg_ref = take(3)
        mix_out = take(n_mix_out)
        carry_ref, dp_ref, acc_ref = take(3)
        i = pl.program_id(0)
        tile = nt - 1 - i

        @pl.when(i == 0)
        def _():
            for ref in [carry_ref, dg_ref, acc_ref] + mix_out[-1:] + refs:
                ref[...] = jnp.zeros_like(ref)

        dof = do_ref[...].astype(F32)
        nxt = carry_ref[...]
        if is_conv:
            w_ref, = mix_refs
            dw_ref, = mix_out
            w0, w1, w2 = w_ref[0:1, :], w_ref[1:2, :], w_ref[2:3, :]
            b, c, v, z = [p_ref[k].astype(F32) for k in range(4)]
            u = c * v
            u_prev = jnp.where(tile == 0, 0.0, ph_ref[1].astype(F32) * ph_ref[2].astype(F32))
            u1, u2 = _conv_taps(u, u_prev)
            conv = w0 * u2 + w1 * u1 + w2 * u
            sig = _sigmoid(z)
            sz = z * sig
            dy = dof * sz
            dp_ref[3] = (dof * (b * conv) * (sig + sz * (1.0 - sig))).astype(BF16)
            dp_ref[0] = (dy * conv).astype(BF16)
            dconv = dy * b
            dw_ref[0:1, :] += jnp.sum(dconv * u2, axis=0, keepdims=True)
            dw_ref[1:2, :] += jnp.sum(dconv * u1, axis=0, keepdims=True)
            dw_ref[2:3, :] += jnp.sum(dconv * u, axis=0, keepdims=True)
            dcc = jnp.concatenate([dconv, nxt], axis=0)
            du = w2 * dconv + w1 * _shift_up(dcc, 1)[:ts] + w0 * _shift_up(dcc, 2)[:ts]
            carry_ref[...] = dconv[:HALO]
            dp_ref[1] = (du * v).astype(BF16)
            dp_ref[2] = (du * c).astype(BF16)
        else:
            wgrp_ref, sc_ref = mix_refs
            dsc_ref = mix_out[1]
            agrp, = refs
            u = p_ref[0].astype(F32)
            z = p_ref[1].astype(F32)
            u_prev = jnp.where(tile == 0, 0.0, ph_ref[0].astype(F32))
            uu = jnp.concatenate([u_prev, u], axis=0)
            sig = _sigmoid(z)
            sz = z * sig
            dm = dof * sz
            dsilu = dof * (sig + sz * (1.0 - sig))
            for gi, window in enumerate(POOL_WINDOWS):
                cols = slice(gi * gdim, (gi + 1) * gdim)
                w = wgrp_ref[gi]
                scale = sc_ref[:, cols]
                db = _window_mean_minus(uu[:, cols], tile * ts, window).astype(BF16)
                mr = _dot(db, w)
                dp_ref[1, :, cols] = (dsilu[:, cols] * (mr * scale)).astype(BF16)
                dmg = dm[:, cols]
                dmr = (dmg * scale).astype(BF16)
                agrp[gi] += _dot_tn(db, dmr)
                dsc_ref[:, cols] += jnp.sum(dmg * mr, axis=0, keepdims=True)
                dd = _dot_nt(dmr, w)
                ddq = dd * _inv_count(ts, tile * ts, window)
                acc = jnp.concatenate([ddq, nxt[:, cols]], axis=0)
                span = 1
                while span < window:
                    acc = acc + _shift_up(acc, span)
                    span *= 2
                carry_ref[:, cols] = ddq[:HALO]
                dp_ref[0, :, cols] = (acc[:ts] - dd).astype(BF16)

        dhn = _dot_nt(dp_ref[0], win_ref[:, 0:e])
        for k in range(1, nsplit):
            dhn += _dot_nt(dp_ref[k], win_ref[:, k * e:(k + 1) * e])
        xh, r = _rms_stats(h_ref[...])
        dg_ref[...] += jnp.sum(dhn * xh, axis=0, keepdims=True)
        dh_ref[...] = dh1_ref[...] + _rms_bwd(dhn, xh, r, g_ref[...])
        hn_tile = hn_ref[...]
        for k in range(nsplit):
            acc_ref[:, k * e:(k + 1) * e] += _dot_tn(hn_tile, dp_ref[k])

        @pl.when(i == nt - 1)
        def _():
            dwin_ref[...] = acc_ref[...].astype(BF16)
            if not is_conv:
                mix_out[0][...] = refs[0][...].astype(BF16)

    rev = lambda width: pl.BlockSpec((ts, width), lambda i: (nt - 1 - i, 0))
    halo_blocks = ts // HALO
    in_specs = [rev(e),
                pl.BlockSpec((nsplit, ts, e), lambda i: (0, nt - 1 - i, 0)),
                pl.BlockSpec((nsplit, HALO, e), lambda i: (0, jnp.maximum((nt - 1 - i) * halo_blocks - 1, 0), 0))]
    in_specs += [_whole(*a.shape) for a in params]
    in_specs += [_whole(d, nsplit * e), rev(d), _whole(1, d), rev(d), rev(d)]
    out_specs = [_whole(d, nsplit * e), rev(d), _whole(1, d)]
    out_shape = [jax.ShapeDtypeStruct((d, nsplit * e), BF16), jax.ShapeDtypeStruct((s, d), F32),
                 jax.ShapeDtypeStruct((1, d), F32)]
    scratch = [pltpu.VMEM((HALO, e), F32), pltpu.VMEM((nsplit, ts, e), BF16), pltpu.VMEM((d, nsplit * e), F32)]
    if is_conv:
        out_specs += [_whole(3, e)]
        out_shape += [jax.ShapeDtypeStruct((3, e), F32)]
    else:
        out_specs += [_whole(N_POOL_GROUPS, gdim, gdim), _whole(1, e)]
        out_shape += [jax.ShapeDtypeStruct((N_POOL_GROUPS, gdim, gdim), BF16), jax.ShapeDtypeStruct((1, e), F32)]
        scratch += [pltpu.VMEM((N_POOL_GROUPS, gdim, gdim), F32)]
    return pl.pallas_call(
        body, name="mixer_bwd",
        grid=(nt,),
        in_specs=in_specs, out_specs=out_specs, out_shape=out_shape, scratch_shapes=scratch,
        compiler_params=_cparams("arbitrary"),
    )(do, proj, proj, *params, w_in, h, g, dh1, hn)


def _forward_backward(xs, ps, target, full, conv_w, scale_w, norm_mix, ple_norm, final_norm, exchange):
    depth = len(full)
    row = lambda a, i: a[i][None, :]
    mixer_of = lambda i: ("conv", conv_w[i // 2]) if i % 2 == 0 else ("pool", full[i]["w_grp"], row(scale_w, i // 2))

    saved = []
    h = xs
    for i in range(depth):
        w = full[i]
        head = (target, final_norm[None, :]) if i == depth - 1 else None
        h_next, acts = _layer_fwd(h, row(norm_mix, i), w["w_in"], mixer_of(i), w["w_out"], row(ple_norm, i),
                                  w["gate"], ps, w["proj"], i, head)
        saved.append((h, *acts))
        h = h_next
    dh, loss_row, d_final = h

    d_norm, d_ple_norm, d_conv, d_scale, sent = [None] * depth, [None] * depth, [], [], [None] * depth
    for i in reversed(range(depth)):
        w = full[i]
        h_in, proj, hn, o, h1, gl, pp = saved[i]
        g = {}
        dh1, do, g["proj"], g["gate"], g["w_out"], d_ple_norm[i] = _out_ple_bwd(
            dh, gl, pp, h1, ps, o, row(ple_norm, i), w["gate"], w["w_out"], i)
        g["w_in"], dh, d_norm[i], *mixer_grads = _mixer_bwd(
            do, proj, hn, mixer_of(i), w["w_in"], h_in, row(norm_mix, i), dh1)
        if i % 2 == 0:
            d_conv.insert(0, mixer_grads[0])
        else:
            g["w_grp"] = mixer_grads[0]
            d_scale.insert(0, mixer_grads[1])
        sent[i] = exchange(i, g)
    return loss_row, dh, sent, (d_norm, d_ple_norm, d_final, d_conv, d_scale)


VMEM_SPEC = pl.BlockSpec(memory_space=pltpu.VMEM)

FLIPS = [(fx, fy, fc) for fx in (0, 1) for fy in (0, 1) for fc in (0, 1)][1:]
SHARD_AXIS = {"w_in": 1, "w_out": 0, "w_grp": 1, "gate": 0, "proj": 1}


def _my_place():
    return lax.axis_index("x"), lax.axis_index("y"), lax.axis_index("c")


def _position(place):
    x, y, c = place
    return 4 * x + 2 * y + c


def _flip(place, flips):
    return tuple(1 - v if f else v for v, f in zip(place, flips))


def _shard_of(ref, axis, pos, n):
    idx = [slice(None)] * len(ref.shape)
    idx[axis] = pl.ds(pl.multiple_of(pos * n, n), n)
    return ref.at[tuple(idx)]


def _sequencer_mesh():
    return plsc.ScalarSubcoreMesh(axis_name="sequencer", num_cores=1)


def _handshake(peers):
    barrier = pltpu.get_barrier_semaphore()
    for peer in peers:
        pl.semaphore_signal(barrier, inc=1, device_id=peer, device_id_type=MESH)
    pl.semaphore_wait(barrier, len(peers))


def _all_gather_layer(shards, collective_id):
    names = list(shards)
    nt = len(names)
    axes = [SHARD_AXIS[k] for k in names]
    widths = [shards[k].shape[SHARD_AXIS[k]] for k in names]

    def full_shape(k):
        shp = list(shards[k].shape)
        shp[SHARD_AXIS[k]] *= N_DEV
        return tuple(shp)

    def body(*refs):
        ins, outs = refs[:nt], refs[nt:2 * nt]
        send_sems, recv_sems, local_sem = refs[2 * nt:]
        me = _my_place()
        x, y, c = me
        sibling = (x, y, 1 - c)
        chips = [(1 - x, y), (x, 1 - y), (1 - x, 1 - y)]
        _handshake([sibling] + [(*chip, c) for chip in chips])

        def block(t, place):
            return _shard_of(outs[t], axes[t], _position(place), widths[t])

        def copy(t, k, place, to, src=None):
            return pltpu.make_async_remote_copy(
                src_ref=block(t, place) if src is None else src, dst_ref=block(t, place),
                send_sem=send_sems.at[k], recv_sem=recv_sems.at[k], device_id=to, device_id_type=MESH)

        mine = [pltpu.make_async_copy(ins[t], block(t, me), local_sem) for t in range(nt)]
        for cp in mine:
            cp.start()
        first = []
        for j, chip in enumerate(chips):
            first += [copy(t, 1 + j, me, (*chip, c), src=ins[t]) for t in range(nt)]
        first += [copy(t, 0, me, sibling, src=ins[t]) for t in range(nt)]
        for cp in first:
            cp.start()
        passed = []
        for j, chip in enumerate(chips):
            for t in range(nt):
                copy(t, 1 + j, (*chip, c), me).wait_recv()
            for t in range(nt):
                fwd = copy(t, 4 + j, (*chip, c), sibling)
                fwd.start()
                passed.append(fwd)
        for t in range(nt):
            copy(t, 0, sibling, me).wait_recv()
        for j, chip in enumerate(chips):
            for t in range(nt):
                copy(t, 4 + j, (*chip, 1 - c), me).wait_recv()
        for cp in first + passed:
            cp.wait_send()
        for cp in mine:
            cp.wait()

    outs = pl.kernel(
        body, name=f"all_gather_layer_{collective_id}",
        out_type=[jax.ShapeDtypeStruct(full_shape(k), shards[k].dtype) for k in names],
        mesh=_sequencer_mesh(),
        scratch_types=[pltpu.SemaphoreType.DMA((7,)), pltpu.SemaphoreType.DMA((7,)), pltpu.SemaphoreType.DMA],
        compiler_params=pltpu.CompilerParams(collective_id=collective_id),
    )(*[shards[k] for k in names])
    return dict(zip(names, outs))


def _exchange_layer(grads, collective_id):
    names = list(grads)
    nt = len(names)
    axes = [SHARD_AXIS[k] for k in names]
    widths = [grads[k].shape[SHARD_AXIS[k]] // N_DEV for k in names]

    def slot_shape(t):
        shp = list(grads[names[t]].shape)
        shp[axes[t]] = widths[t]
        return (N_DEV, *shp)

    def body(*refs):
        ins, outs = refs[:nt], refs[nt:2 * nt]
        send_sems, recv_sems, local_sem = refs[2 * nt:]
        me = _my_place()
        mine = _position(me)
        _handshake([_flip(me, flips) for flips in FLIPS])
        local = [pltpu.make_async_copy(_shard_of(ins[t], axes[t], mine, widths[t]), outs[t].at[mine], local_sem)
                 for t in range(nt)]
        for cp in local:
            cp.start()
        copies = []
        for k, flips in enumerate(FLIPS):
            peer = _flip(me, flips)
            for t in range(nt):
                cp = pltpu.make_async_remote_copy(
                    src_ref=_shard_of(ins[t], axes[t], _position(peer), widths[t]), dst_ref=outs[t].at[mine],
                    send_sem=send_sems.at[k], recv_sem=recv_sems.at[k], device_id=peer, device_id_type=MESH)
                cp.start()
                copies.append(cp)
        for cp in copies:
            cp.wait()
        for cp in local:
            cp.wait()

    outs = pl.kernel(
        body, name=f"exchange_layer_{collective_id}",
        out_type=[jax.ShapeDtypeStruct(slot_shape(t), BF16) for t in range(nt)],
        mesh=_sequencer_mesh(),
        scratch_types=[pltpu.SemaphoreType.DMA((7,)), pltpu.SemaphoreType.DMA((7,)), pltpu.SemaphoreType.DMA],
        compiler_params=pltpu.CompilerParams(collective_id=collective_id),
    )(*[grads[k] for k in names])
    return dict(zip(names, outs))


def _gather_rows(buf, reduce):
    r, c = buf.shape

    def body(in_ref, out_ref, *scratch):
        if reduce:
            all_ref, send_sems, recv_sems = scratch
        else:
            all_ref = out_ref
            send_sems, recv_sems = scratch
        me = _my_place()
        all_ref[_position(me)] = in_ref[...]
        copies = []
        for k, flips in enumerate(FLIPS):
            cp = pltpu.make_async_remote_copy(
                src_ref=in_ref, dst_ref=all_ref.at[_position(me)],
                send_sem=send_sems.at[k], recv_sem=recv_sems.at[k], device_id=_flip(me, flips), device_id_type=MESH)
            cp.start()
            copies.append(cp)
        for cp in copies:
            cp.wait()
        if reduce:
            total = all_ref[0]
            for j in range(1, N_DEV):
                total = total + all_ref[j]
            out_ref[...] = total

    return pl.pallas_call(
        body, name="sum_rows" if reduce else "gather_rows",
        in_specs=[VMEM_SPEC], out_specs=VMEM_SPEC,
        out_shape=jax.ShapeDtypeStruct((r, c) if reduce else (N_DEV, r, c), F32),
        scratch_shapes=([pltpu.VMEM((N_DEV, r, c), F32)] if reduce else [])
        + [pltpu.SemaphoreType.DMA((7,)), pltpu.SemaphoreType.DMA((7,))],
    )(buf)


def _adamw_math(w, g, m, v):
    m = ADAM_B1 * m + (1.0 - ADAM_B1) * g
    v = ADAM_B2 * v + (1.0 - ADAM_B2) * (g * g)
    m_hat = m / (1.0 - ADAM_B1 ** ADAM_STEP)
    v_hat = v / (1.0 - ADAM_B2 ** ADAM_STEP)
    delta = -ADAM_LR * (m_hat / (jnp.sqrt(v_hat) + ADAM_EPS) + ADAM_WD * w)
    return delta, m, v


def _run_behind(x, token):
    def body(x_ref, token_ref, out_ref):
        out_ref[...] = jnp.zeros_like(out_ref)

    any_spec = pl.BlockSpec(memory_space=pl.ANY)
    return pl.pallas_call(
        body, name="run_behind",
        in_specs=[any_spec, any_spec], out_specs=VMEM_SPEC,
        out_shape=jax.ShapeDtypeStruct((8, 128), F32),
    )(x, token)


def _adamw_pieces(pieces, w, m, v, after, layer=None, into=()):
    shape = w.shape
    nl = shape[0]
    cols = shape[-1]
    rows = w.size // (nl * cols)
    first, nh = (0, nl) if layer is None else (layer, 1)
    tr = min(ADAMW_BLOCK_ROWS // nh, rows // 2)
    flat3 = lambda a: a.reshape(nl, rows, cols)
    into = [flat3(a) for a in into]

    def body(*refs):
        p_refs = refs[:nh]
        w_ref, m_ref, v_ref = refs[nh:nh + 3]
        g_ref, d_ref, nm_ref, nv_ref = refs[-4:]
        for l in range(nh):
            g = p_refs[l][0].astype(F32)
            for j in range(1, N_DEV):
                g = g + p_refs[l][j].astype(F32)
            g_ref[l] = g
            d_ref[l], nm_ref[l], nv_ref[l] = _adamw_math(w_ref[l], g, m_ref[l], v_ref[l])

    blk = pl.BlockSpec((nh, tr, cols), lambda i: (first, i, 0))
    any_spec = pl.BlockSpec(memory_space=pl.ANY)
    outs = pl.pallas_call(
        body, name="adamw_pieces",
        grid=(rows // tr,),
        in_specs=[pl.BlockSpec((N_DEV, tr, cols), lambda i: (0, i, 0))] * nh
        + [blk, blk, blk, any_spec] + [any_spec] * len(into),
        out_specs=[blk] * 4,
        out_shape=[jax.ShapeDtypeStruct((nl, rows, cols), F32)] * 4,
        input_output_aliases={nh + 4 + k: k for k in range(len(into))},
        compiler_params=_cparams("parallel"),
    )(*[a.reshape(N_DEV, rows, cols) for a in pieces], flat3(w), flat3(m), flat3(v), after, *into)
    return [a.reshape(shape) for a in outs]


def _adamw_small(g, w, m, v):
    shape = w.shape
    two = lambda a: a.reshape(-1, shape[-1])

    def body(g_ref, w_ref, m_ref, v_ref, d_ref, nm_ref, nv_ref):
        d_ref[...], nm_ref[...], nv_ref[...] = _adamw_math(w_ref[...], g_ref[...], m_ref[...], v_ref[...])

    outs = pl.pallas_call(
        body, name="adamw_small",
        in_specs=[VMEM_SPEC] * 4, out_specs=[VMEM_SPEC] * 3,
        out_shape=[jax.ShapeDtypeStruct(two(w).shape, F32)] * 3,
    )(two(g), two(w), two(m), two(v))
    return [a.reshape(shape) for a in outs]


WEIGHTS = ("norm_mix", "a_w_in", "a_w_conv", "a_w_out", "b_w_in", "b_w_grp", "b_scale", "b_w_out",
           "ple_norm", "ple_w_gate", "ple_w_proj", "final_norm")
SMALL_ROWS = 24
GATHER_ID = 0
EXCHANGE_ID = 4
LAST_EXCHANGE_ID = 8


def kernel(x, p, norm_mix, a_w_in, a_w_conv, a_w_out, b_w_in, b_w_grp, b_scale, b_w_out, ple_norm, ple_w_gate, ple_w_proj, final_norm, loss_target, m_norm_mix, m_a_w_in, m_a_w_conv, m_a_w_out, m_b_w_in, m_b_w_grp, m_b_scale, m_b_w_out, m_ple_norm, m_ple_w_gate, m_ple_w_proj, m_final_norm, v_norm_mix, v_a_w_in, v_a_w_conv, v_a_w_out, v_b_w_in, v_b_w_grp, v_b_scale, v_b_w_out, v_ple_norm, v_ple_w_gate, v_ple_w_proj, v_final_norm):
    wts = dict(norm_mix=norm_mix, a_w_in=a_w_in, a_w_conv=a_w_conv, a_w_out=a_w_out, b_w_in=b_w_in, b_w_grp=b_w_grp,
               b_scale=b_scale, b_w_out=b_w_out, ple_norm=ple_norm, ple_w_gate=ple_w_gate, ple_w_proj=ple_w_proj,
               final_norm=final_norm)
    mom = dict(norm_mix=m_norm_mix, a_w_in=m_a_w_in, a_w_conv=m_a_w_conv, a_w_out=m_a_w_out, b_w_in=m_b_w_in,
               b_w_grp=m_b_w_grp, b_scale=m_b_scale, b_w_out=m_b_w_out, ple_norm=m_ple_norm, ple_w_gate=m_ple_w_gate,
               ple_w_proj=m_ple_w_proj, final_norm=m_final_norm)
    var = dict(norm_mix=v_norm_mix, a_w_in=v_a_w_in, a_w_conv=v_a_w_conv, a_w_out=v_a_w_out, b_w_in=v_b_w_in,
               b_w_grp=v_b_w_grp, b_scale=v_b_scale, b_w_out=v_b_w_out, ple_norm=v_ple_norm, ple_w_gate=v_ple_w_gate,
               ple_w_proj=v_ple_w_proj, final_norm=v_final_norm)
    d = x.shape[2]
    depth = norm_mix.shape[0]
    n_a, n_b = a_w_conv.shape[0], b_scale.shape[0]
    cw = a_w_conv.shape[2]
    pos = _position(_my_place())

    def layer_matrices(i):
        j = i // 2
        mixer = {"w_in": ("a_w_in", j), "w_out": ("a_w_out", j)} if i % 2 == 0 else \
                {"w_in": ("b_w_in", j), "w_grp": ("b_w_grp", j), "w_out": ("b_w_out", j)}
        return {**mixer, "gate": ("ple_w_gate", i), "proj": ("ple_w_proj", i)}

    full = [_all_gather_layer({k: wts[name][idx].astype(BF16) for k, (name, idx) in layer_matrices(i).items()},
                              GATHER_ID + i) for i in range(depth)]
    vec_rows = jnp.concatenate([a_w_conv.reshape(-1, cw), b_scale], axis=0)
    vecs = _gather_rows(vec_rows, reduce=False)
    n_conv = 3 * n_a
    conv_w = vecs[:, :n_conv].transpose(1, 0, 2).reshape(n_a, 3, N_DEV * cw)
    scale_w = vecs[:, n_conv:].transpose(1, 0, 2).reshape(n_b, N_DEV * cw)

    def exchange(i, g):
        if i > 0:
            return _exchange_layer(g, EXCHANGE_ID + i)
        early = {k: a for k, a in g.items() if k != "w_in"}
        return {**_exchange_layer(early, EXCHANGE_ID), **_exchange_layer({"w_in": g["w_in"]}, LAST_EXCHANGE_ID)}

    loss_row, dx, sent, (d_norm, d_ple_norm, d_final, d_conv, d_scale) = _forward_backward(
        x[0], p[:, 0], loss_target[0], full, conv_w, scale_w, norm_mix, ple_norm, final_norm, exchange)
    pieces = {name: [None] * wts[name].shape[0] for name in WEIGHTS if wts[name].ndim >= 3 and name != "a_w_conv"}
    for i in range(depth):
        for k, (name, idx) in layer_matrices(i).items():
            pieces[name][idx] = sent[i][k]

    pad = lambda a: jnp.pad(a, ((0, 0), (0, d - a.shape[1])))
    small = jnp.concatenate(d_norm + d_ple_norm + [d_final] + d_conv + d_scale + [pad(loss_row)], axis=0)
    small = jnp.pad(small, ((0, SMALL_ROWS - small.shape[0]), (0, 0)))
    total = _gather_rows(small, reduce=True)
    o = 0
    gsum = {}
    gsum["norm_mix"] = total[o:o + depth]; o += depth
    gsum["ple_norm"] = total[o:o + depth]; o += depth
    gsum["final_norm"] = total[o]; o += 1
    conv_full = total[o:o + n_conv].reshape(n_a, 3, d); o += n_conv
    scale_full = total[o:o + n_b]; o += n_b
    loss = total[o, 0]
    gsum["a_w_conv"] = lax.dynamic_slice_in_dim(conv_full, pos * cw, cw, axis=2)
    gsum["b_scale"] = lax.dynamic_slice_in_dim(scale_full, pos * cw, cw, axis=1)

    token = total
    for i in reversed(range(depth)):
        token = _run_behind(sent[i]["w_out"], token)
    last_token = _run_behind(sent[0]["w_in"], token)
    grad, delta, new_m, new_v = {}, {}, {}, {}
    for k in sorted(WEIGHTS, key=lambda name: name == "a_w_in"):
        if k == "a_w_in":
            upper = [None] * 4
            for j in reversed(range(1, n_a)):
                upper = _adamw_pieces([pieces[k][j]], wts[k], mom[k], var[k], token, j, [a for a in upper if a is not None])
            grad[k], delta[k], new_m[k], new_v[k] = _adamw_pieces(
                [pieces[k][0]], wts[k], mom[k], var[k], last_token, 0, upper)
        elif k in pieces:
            grad[k], delta[k], new_m[k], new_v[k] = _adamw_pieces(pieces[k], wts[k], mom[k], var[k], token)
        else:
            grad[k] = gsum[k]
            delta[k], new_m[k], new_v[k] = _adamw_small(gsum[k], wts[k], mom[k], var[k])
    return (loss, dx[None], *[grad[k] for k in WEIGHTS], *[delta[k] for k in WEIGHTS],
            *[new_m[k] for k in WEIGHTS], *[new_v[k] for k in WEIGHTS])
```

```python
import jax
import jax.numpy as jnp
from jax import lax
from jax.experimental import pallas as pl
from jax.experimental.pallas import tpu as pltpu
from jax.experimental.pallas import tpu_sc as plsc

F32 = jnp.float32
BF16 = jnp.bfloat16
MESH = pl.DeviceIdType.MESH

RMS_EPS = 1e-6
POOL_WINDOWS = (2, 4, 8, 16)
N_POOL_GROUPS = len(POOL_WINDOWS)
ADAM_LR = 0.001
ADAM_B1 = 0.9
ADAM_B2 = 0.999
ADAM_EPS = 1e-08
ADAM_WD = 0.01
ADAM_STEP = 10
N_DEV = 8

HALO = 16
FWD_ROW_TILE = 512
POOL_BWD_ROW_TILE = 512
CONV_BWD_ROW_TILE = 256
ADAMW_BLOCK_ROWS = 512
BWD_ROW_TILE = 512
VMEM_LIMIT = 56 * 1024 * 1024


def _cparams(*sem):
    return pltpu.CompilerParams(dimension_semantics=sem, vmem_limit_bytes=VMEM_LIMIT)


def _dot(a, b):
    return jnp.dot(a, b, preferred_element_type=F32)


def _dot_nt(a, b):
    return lax.dot_general(a, b, (((1,), (1,)), ((), ())), preferred_element_type=F32)


def _dot_tn(a, b):
    return lax.dot_general(a, b, (((0,), (0,)), ((), ())), preferred_element_type=F32)


def _rms_stats(x):
    r = lax.rsqrt(jnp.mean(x * x, axis=-1, keepdims=True) + RMS_EPS)
    return x * r, r


def _rms_bwd(dy, xh, r, g):
    a = dy * g
    return r * (a - xh * jnp.mean(a * xh, axis=-1, keepdims=True))


def _sigmoid(x):
    return 1.0 / (1.0 + jnp.exp(-x))


def _shift_down(x, k):
    return pltpu.roll(x, k, 0)


def _shift_up(x, k):
    return pltpu.roll(x, x.shape[0] - k, 0)


def _conv_taps(u, u_prev):
    uu = jnp.concatenate([u_prev, u], axis=0)
    return _shift_down(uu, 1)[HALO:], _shift_down(uu, 2)[HALO:]


def _window_mean_minus(uu, row0, window):
    acc = uu
    span = 1
    while span < window:
        acc = acc + _shift_down(acc, span)
        span *= 2
    return acc[HALO:] * _inv_count(uu.shape[0] - HALO, row0, window) - uu[HALO:]


def _inv_count(rows, row0, window):
    t = row0 + lax.broadcasted_iota(jnp.int32, (rows, 1), 0)
    return 1.0 / jnp.minimum(t + 1, window).astype(F32)


def _whole(*shape):
    return pl.BlockSpec(shape, lambda i: (0,) * len(shape), pipeline_mode=pl.Buffered(1))


def _layer_fwd(h, g, w_in, mixer, w_out, pn, w_gate, p_all, w_proj, layer, head=None):
    s, d = h.shape
    n = w_in.shape[1]
    e = w_out.shape[0]
    nsplit = n // e
    gdim = e // N_POOL_GROUPS
    pdim = p_all.shape[2]
    is_conv = mixer[0] == "conv"
    ts = min(FWD_ROW_TILE, s)
    params = mixer[1:]
    head = tuple(head or ())

    def body(*refs):
        h_ref, g_ref, win_ref = refs[:3]
        mix_refs = refs[3:3 + len(params)]
        wo_ref, pn_ref, wg_ref, p_ref, wp_ref = refs[3 + len(params):8 + len(params)]
        head_refs = refs[8 + len(params):8 + len(params) + len(head)]
        proj_ref, hn_ref, o_ref, h1_ref, h2_ref, gl_ref, pp_ref = refs[8 + len(params) + len(head):][:7]
        carry_ref = refs[-1]
        i = pl.program_id(0)

        @pl.when(i == 0)
        def _():
            carry_ref[...] = jnp.zeros_like(carry_ref)

        x = h_ref[...]
        xh, _ = _rms_stats(x)
        hn = (xh * g_ref[...]).astype(BF16)
        hn_ref[...] = hn
        parts = []
        for k in range(nsplit):
            part = _dot(hn, win_ref[:, k * e:(k + 1) * e])
            proj_ref[k] = part.astype(BF16)
            parts.append(part)
        prev = carry_ref[...]
        if is_conv:
            b, c, v, z = parts
            w_ref, = mix_refs
            u = c * v
            u1, u2 = _conv_taps(u, prev)
            mixed = b * (w_ref[0:1, :] * u2 + w_ref[1:2, :] * u1 + w_ref[2:3, :] * u)
        else:
            u, z = parts
            wgrp_ref, sc_ref = mix_refs
            uu = jnp.concatenate([prev, u], axis=0)
            cols = []
            for gi, window in enumerate(POOL_WINDOWS):
                dg = _window_mean_minus(uu[:, gi * gdim:(gi + 1) * gdim], i * ts, window)
                cols.append(_dot(dg.astype(BF16), wgrp_ref[gi]))
            mixed = jnp.concatenate(cols, axis=1) * sc_ref[...]
        carry_ref[...] = u[ts - HALO:]
        o = ((z * _sigmoid(z)) * mixed).astype(BF16)
        o_ref[...] = o
        h1 = x + _dot(o, wo_ref[...])
        h1_ref[...] = h1
        xh1, _ = _rms_stats(h1)
        gl = _dot((xh1 * pn_ref[...]).astype(BF16), wg_ref[...])
        pp = _dot(p_ref[...].astype(BF16), wp_ref[...])
        gl_ref[...] = gl.astype(BF16)
        pp_ref[...] = pp.astype(BF16)
        h2 = h1 + _sigmoid(gl) * pp
        if not head:
            h2_ref[...] = h2
            return
        t_ref, gain_ref = head_refs
        loss_ref, dgain_ref = refs[-3], refs[-2]

        @pl.when(i == 0)
        def _():
            loss_ref[...] = jnp.zeros_like(loss_ref)
            dgain_ref[...] = jnp.zeros_like(dgain_ref)

        gain = gain_ref[...]
        yh, r = _rms_stats(h2)
        err = yh * gain - t_ref[...]
        loss_ref[...] += jnp.full(loss_ref.shape, (0.5 / d) * jnp.sum(err * err), F32)
        dy = err * (1.0 / d)
        dgain_ref[...] += jnp.sum(dy * yh, axis=0, keepdims=True)
        h2_ref[...] = _rms_bwd(dy, yh, r, gain)

    row = lambda width: pl.BlockSpec((ts, width), lambda i: (i, 0))
    mix_specs = [_whole(*a.shape) for a in params]
    head_specs = [row(d), _whole(1, d)] if head else []
    head_out_specs = [_whole(1, 128), _whole(1, d)] if head else []
    head_out_shape = [jax.ShapeDtypeStruct((1, 128), F32), jax.ShapeDtypeStruct((1, d), F32)] if head else []
    outs = pl.pallas_call(
        body, name="layer_fwd",
        grid=(s // ts,),
        in_specs=[row(d), _whole(1, d), _whole(d, n)] + mix_specs
        + [_whole(e, d), _whole(1, d), _whole(d, d),
           pl.BlockSpec((None, ts, pdim), lambda i: (layer, i, 0)), _whole(pdim, d)] + head_specs,
        out_specs=[pl.BlockSpec((nsplit, ts, e), lambda i: (0, i, 0)),
                   row(d), row(e), row(d), row(d), row(d), row(d)] + head_out_specs,
        out_shape=[jax.ShapeDtypeStruct((nsplit, s, e), BF16), jax.ShapeDtypeStruct((s, d), BF16),
                   jax.ShapeDtypeStruct((s, e), BF16), jax.ShapeDtypeStruct((s, d), F32),
                   jax.ShapeDtypeStruct((s, d), F32), jax.ShapeDtypeStruct((s, d), BF16),
                   jax.ShapeDtypeStruct((s, d), BF16)] + head_out_shape,
        scratch_shapes=[pltpu.VMEM((HALO, e), F32)],
        compiler_params=_cparams("arbitrary"),
    )(h, g, w_in, *params, w_out, pn, w_gate, p_all, w_proj, *head)
    proj, hn, o, h1, h2, gl, pp = outs[:7]
    return (h2, *outs[7:]) if head else h2, (proj, hn, o, h1, gl, pp)


def _out_ple_bwd(dh2, gl, pp, h1, p_all, o, pn, wgate, wout, layer):
    s, d = dh2.shape
    e = o.shape[1]
    pdim = p_all.shape[2]
    ts = min(BWD_ROW_TILE, s)
    last = s // ts - 1

    def body(dh2_ref, gl_ref, pp_ref, h1_ref, p_ref, o_ref, pn_ref, wg_ref, wo_ref,
             dh1_ref, do_ref, dwp_ref, dwg_ref, dwo_ref, dpn_ref, awp, awg, awo):
        i = pl.program_id(0)

        @pl.when(i == 0)
        def _():
            awp[...] = jnp.zeros_like(awp)
            awg[...] = jnp.zeros_like(awg)
            awo[...] = jnp.zeros_like(awo)
            dpn_ref[...] = jnp.zeros_like(dpn_ref)

        dh2 = dh2_ref[...]
        gate = _sigmoid(gl_ref[...].astype(F32))
        dpp = (dh2 * gate).astype(BF16)
        dgl = (dh2 * pp_ref[...].astype(F32) * gate * (1.0 - gate)).astype(BF16)
        xh, r = _rms_stats(h1_ref[...])
        pn = pn_ref[...]
        awp[...] += _dot_tn(p_ref[...].astype(BF16), dpp)
        awg[...] += _dot_tn((xh * pn).astype(BF16), dgl)
        dr = _dot_nt(dgl, wg_ref[...])
        dpn_ref[...] += jnp.sum(dr * xh, axis=0, keepdims=True)
        dh1 = dh2 + _rms_bwd(dr, xh, r, pn)
        dh1_ref[...] = dh1
        dh1b = dh1.astype(BF16)
        do_ref[...] = _dot_nt(dh1b, wo_ref[...]).astype(BF16)
        awo[...] += _dot_tn(o_ref[...], dh1b)

        @pl.when(i == last)
        def _():
            dwp_ref[...] = awp[...].astype(BF16)
            dwg_ref[...] = awg[...].astype(BF16)
            dwo_ref[...] = awo[...].astype(BF16)

    row = lambda width: pl.BlockSpec((ts, width), lambda i: (i, 0))
    return pl.pallas_call(
        body, name="out_ple_bwd",
        grid=(s // ts,),
        in_specs=[row(d), row(d), row(d), row(d),
                  pl.BlockSpec((None, ts, pdim), lambda i: (layer, i, 0)),
                  row(e), _whole(1, d), _whole(d, d), _whole(e, d)],
        out_specs=[row(d), row(e), _whole(pdim, d), _whole(d, d), _whole(e, d), _whole(1, d)],
        out_shape=[jax.ShapeDtypeStruct((s, d), F32), jax.ShapeDtypeStruct((s, e), BF16),
                   jax.ShapeDtypeStruct((pdim, d), BF16), jax.ShapeDtypeStruct((d, d), BF16),
                   jax.ShapeDtypeStruct((e, d), BF16), jax.ShapeDtypeStruct((1, d), F32)],
        scratch_shapes=[pltpu.VMEM((pdim, d), F32), pltpu.VMEM((d, d), F32), pltpu.VMEM((e, d), F32)],
        compiler_params=_cparams("arbitrary"),
    )(dh2, gl, pp, h1, p_all, o, pn, wgate, wout)


def _mixer_bwd(do, proj, hn, mixer, w_in, h, g, dh1):
    s, d = h.shape
    nsplit, _, e = proj.shape
    gdim = e // N_POOL_GROUPS
    is_conv = mixer[0] == "conv"
    ts = min(CONV_BWD_ROW_TILE if is_conv else POOL_BWD_ROW_TILE, s)
    nt = s // ts
    params = mixer[1:]
    n_mix_out = 1 if is_conv else 2

    def body(*refs):
        refs = list(refs)
        take = lambda n: [refs.pop(0) for _ in range(n)]
        do_ref, p_ref, ph_ref = take(3)
        mix_refs = take(len(params))
        win_ref, h_ref, g_ref, dh1_ref, hn_ref = take(5)
        dwin_ref, dh_ref, dg_ref = take(3)
        mix_out = take(n_mix_out)
        carry_ref, dp_ref, acc_ref = take(3)
        i = pl.program_id(0)
        tile = nt - 1 - i

        @pl.when(i == 0)
        def _():
            for ref in [carry_ref, dg_ref, acc_ref] + mix_out[-1:] + refs:
                ref[...] = jnp.zeros_like(ref)

        dof = do_ref[...].astype(F32)
        nxt = carry_ref[...]
        if is_conv:
            w_ref, = mix_refs
            dw_ref, = mix_out
            w0, w1, w2 = w_ref[0:1, :], w_ref[1:2, :], w_ref[2:3, :]
            b, c, v, z = [p_ref[k].astype(F32) for k in range(4)]
            u = c * v
            u_prev = jnp.where(tile == 0, 0.0, ph_ref[1].astype(F32) * ph_ref[2].astype(F32))
            u1, u2 = _conv_taps(u, u_prev)
            conv = w0 * u2 + w1 * u1 + w2 * u
            sig = _sigmoid(z)
            sz = z * sig
            dy = dof * sz
            dp_ref[3] = (dof * (b * conv) * (sig + sz * (1.0 - sig))).astype(BF16)
            dp_ref[0] = (dy * conv).astype(BF16)
            dconv = dy * b
            dw_ref[0:1, :] += jnp.sum(dconv * u2, axis=0, keepdims=True)
            dw_ref[1:2, :] += jnp.sum(dconv * u1, axis=0, keepdims=True)
            dw_ref[2:3, :] += jnp.sum(dconv * u, axis=0, keepdims=True)
            dcc = jnp.concatenate([dconv, nxt], axis=0)
            du = w2 * dconv + w1 * _shift_up(dcc, 1)[:ts] + w0 * _shift_up(dcc, 2)[:ts]
            carry_ref[...] = dconv[:HALO]
            dp_ref[1] = (du * v).astype(BF16)
            dp_ref[2] = (du * c).astype(BF16)
        else:
            wgrp_ref, sc_ref = mix_refs
            dsc_ref = mix_out[1]
            agrp, = refs
            u = p_ref[0].astype(F32)
            z = p_ref[1].astype(F32)
            u_prev = jnp.where(tile == 0, 0.0, ph_ref[0].astype(F32))
            uu = jnp.concatenate([u_prev, u], axis=0)
            sig = _sigmoid(z)
            sz = z * sig
            dm = dof * sz
            dsilu = dof * (sig + sz * (1.0 - sig))
            for gi, window in enumerate(POOL_WINDOWS):
                cols = slice(gi * gdim, (gi + 1) * gdim)
                w = wgrp_ref[gi]
                scale = sc_ref[:, cols]
                db = _window_mean_minus(uu[:, cols], tile * ts, window).astype(BF16)
                mr = _dot(db, w)
                dp_ref[1, :, cols] = (dsilu[:, cols] * (mr * scale)).astype(BF16)
                dmg = dm[:, cols]
                dmr = (dmg * scale).astype(BF16)
                agrp[gi] += _dot_tn(db, dmr)
                dsc_ref[:, cols] += jnp.sum(dmg * mr, axis=0, keepdims=True)
                dd = _dot_nt(dmr, w)
                ddq = dd * _inv_count(ts, tile * ts, window)
                acc = jnp.concatenate([ddq, nxt[:, cols]], axis=0)
                span = 1
                while span < window:
                    acc = acc + _shift_up(acc, span)
                    span *= 2
                carry_ref[:, cols] = ddq[:HALO]
                dp_ref[0, :, cols] = (acc[:ts] - dd).astype(BF16)

        dhn = _dot_nt(dp_ref[0], win_ref[:, 0:e])
        for k in range(1, nsplit):
            dhn += _dot_nt(dp_ref[k], win_ref[:, k * e:(k + 1) * e])
        xh, r = _rms_stats(h_ref[...])
        dg_ref[...] += jnp.sum(dhn * xh, axis=0, keepdims=True)
        dh_ref[...] = dh1_ref[...] + _rms_bwd(dhn, xh, r, g_ref[...])
        hn_tile = hn_ref[...]
        for k in range(nsplit):
            acc_ref[:, k * e:(k + 1) * e] += _dot_tn(hn_tile, dp_ref[k])

        @pl.when(i == nt - 1)
        def _():
            dwin_ref[...] = acc_ref[...].astype(BF16)
            if not is_conv:
                mix_out[0][...] = refs[0][...].astype(BF16)

    rev = lambda width: pl.BlockSpec((ts, width), lambda i: (nt - 1 - i, 0))
    halo_blocks = ts // HALO
    in_specs = [rev(e),
                pl.BlockSpec((nsplit, ts, e), lambda i: (0, nt - 1 - i, 0)),
                pl.BlockSpec((nsplit, HALO, e), lambda i: (0, jnp.maximum((nt - 1 - i) * halo_blocks - 1, 0), 0))]
    in_specs += [_whole(*a.shape) for a in params]
    in_specs += [_whole(d, nsplit * e), rev(d), _whole(1, d), rev(d), rev(d)]
    out_specs = [_whole(d, nsplit * e), rev(d), _whole(1, d)]
    out_shape = [jax.ShapeDtypeStruct((d, nsplit * e), BF16), jax.ShapeDtypeStruct((s, d), F32),
                 jax.ShapeDtypeStruct((1, d), F32)]
    scratch = [pltpu.VMEM((HALO, e), F32), pltpu.VMEM((nsplit, ts, e), BF16), pltpu.VMEM((d, nsplit * e), F32)]
    if is_conv:
        out_specs += [_whole(3, e)]
        out_shape += [jax.ShapeDtypeStruct((3, e), F32)]
    else:
        out_specs += [_whole(N_POOL_GROUPS, gdim, gdim), _whole(1, e)]
        out_shape += [jax.ShapeDtypeStruct((N_POOL_GROUPS, gdim, gdim), BF16), jax.ShapeDtypeStruct((1, e), F32)]
        scratch += [pltpu.VMEM((N_POOL_GROUPS, gdim, gdim), F32)]
    return pl.pallas_call(
        body, name="mixer_bwd",
        grid=(nt,),
        in_specs=in_specs, out_specs=out_specs, out_shape=out_shape, scratch_shapes=scratch,
        compiler_params=_cparams("arbitrary"),
    )(do, proj, proj, *params, w_in, h, g, dh1, hn)


def _forward_backward(xs, ps, target, full, conv_w, scale_w, norm_mix, ple_norm, final_norm, exchange):
    depth = len(full)
    row = lambda a, i: a[i][None, :]
    mixer_of = lambda i: ("conv", conv_w[i // 2]) if i % 2 == 0 else ("pool", full[i]["w_grp"], row(scale_w, i // 2))

    saved = []
    h = xs
    for i in range(depth):
        w = full[i]
        head = (target, final_norm[None, :]) if i == depth - 1 else None
        h_next, acts = _layer_fwd(h, row(norm_mix, i), w["w_in"], mixer_of(i), w["w_out"], row(ple_norm, i),
                                  w["gate"], ps, w["proj"], i, head)
        saved.append((h, *acts))
        h = h_next
    dh, loss_row, d_final = h

    d_norm, d_ple_norm, d_conv, d_scale, sent = [None] * depth, [None] * depth, [], [], [None] * depth
    for i in reversed(range(depth)):
        w = full[i]
        h_in, proj, hn, o, h1, gl, pp = saved[i]
        g = {}
        dh1, do, g["proj"], g["gate"], g["w_out"], d_ple_norm[i] = _out_ple_bwd(
            dh, gl, pp, h1, ps, o, row(ple_norm, i), w["gate"], w["w_out"], i)
        g["w_in"], dh, d_norm[i], *mixer_grads = _mixer_bwd(
            do, proj, hn, mixer_of(i), w["w_in"], h_in, row(norm_mix, i), dh1)
        if i % 2 == 0:
            d_conv.insert(0, mixer_grads[0])
        else:
            g["w_grp"] = mixer_grads[0]
            d_scale.insert(0, mixer_grads[1])
        sent[i] = exchange(i, g)
    return loss_row, dh, sent, (d_norm, d_ple_norm, d_final, d_conv, d_scale)


VMEM_SPEC = pl.BlockSpec(memory_space=pltpu.VMEM)

FLIPS = [(fx, fy, fc) for fx in (0, 1) for fy in (0, 1) for fc in (0, 1)][1:]
SHARD_AXIS = {"w_in": 1, "w_out": 0, "w_grp": 1, "gate": 0, "proj": 1}


def _my_place():
    return lax.axis_index("x"), lax.axis_index("y"), lax.axis_index("c")


def _position(place):
    x, y, c = place
    return 4 * x + 2 * y + c


def _flip(place, flips):
    return tuple(1 - v if f else v for v, f in zip(place, flips))


def _shard_of(ref, axis, pos, n):
    idx = [slice(None)] * len(ref.shape)
    idx[axis] = pl.ds(pl.multiple_of(pos * n, n), n)
    return ref.at[tuple(idx)]


def _sequencer_mesh():
    return plsc.ScalarSubcoreMesh(axis_name="sequencer", num_cores=1)


def _handshake(peers):
    barrier = pltpu.get_barrier_semaphore()
    for peer in peers:
        pl.semaphore_signal(barrier, inc=1, device_id=peer, device_id_type=MESH)
    pl.semaphore_wait(barrier, len(peers))


def _all_gather_layer(shards, collective_id):
    names = list(shards)
    nt = len(names)
    axes = [SHARD_AXIS[k] for k in names]
    widths = [shards[k].shape[SHARD_AXIS[k]] for k in names]

    def full_shape(k):
        shp = list(shards[k].shape)
        shp[SHARD_AXIS[k]] *= N_DEV
        return tuple(shp)

    def body(*refs):
        ins, outs = refs[:nt], refs[nt:2 * nt]
        send_sems, recv_sems, local_sem = refs[2 * nt:]
        me = _my_place()
        x, y, c = me
        sibling = (x, y, 1 - c)
        flip = lambda v, f: v + f - 2 * v * f
        neighbour = lambda core, fx: (flip(x, fx), flip(y, 1 - fx), core)
        first, second = neighbour(c, c), neighbour(c, 1 - c)
        diagonal = (1 - x, 1 - y, c)
        _handshake([sibling, first, second])

        def block(t, place):
            return _shard_of(outs[t], axes[t], _position(place), widths[t])

        def copy(t, k, place, to, src=None):
            return pltpu.make_async_remote_copy(
                src_ref=block(t, place) if src is None else src, dst_ref=block(t, place),
                send_sem=send_sems.at[k], recv_sem=recv_sems.at[k], device_id=to, device_id_type=MESH)

        everything = lambda make: [make(t) for t in range(nt)]
        mine = everything(lambda t: pltpu.make_async_copy(ins[t], block(t, me), local_sem))
        sent = (everything(lambda t: copy(t, 1, me, first, src=ins[t]))
                + everything(lambda t: copy(t, 2, me, second, src=ins[t]))
                + everything(lambda t: copy(t, 0, me, sibling, src=ins[t])))
        for cp in mine + sent:
            cp.start()
        for k, arrived, onward in ((1, first, second), (2, second, None), (3, diagonal, None)):
            for t in range(nt):
                copy(t, k, arrived, me).wait_recv()
            if onward is not None:
                sent += everything(lambda t: copy(t, 3, arrived, onward))
            sent += everything(lambda t: copy(t, 3 + k, arrived, sibling))
            for cp in sent[-nt * (1 + (onward is not None)):]:
                cp.start()
        for k, place in ((0, sibling), (4, neighbour(1 - c, 1 - c)), (5, neighbour(1 - c, c)), (6, (1 - x, 1 - y, 1 - c))):
            for t in range(nt):
                copy(t, k, place, me).wait_recv()
        for cp in sent:
            cp.wait_send()
        for cp in mine:
            cp.wait()

    outs = pl.kernel(
        body, name=f"all_gather_layer_{collective_id}",
        out_type=[jax.ShapeDtypeStruct(full_shape(k), shards[k].dtype) for k in names],
        mesh=_sequencer_mesh(),
        scratch_types=[pltpu.SemaphoreType.DMA((7,)), pltpu.SemaphoreType.DMA((7,)), pltpu.SemaphoreType.DMA],
        compiler_params=pltpu.CompilerParams(collective_id=collective_id),
    )(*[shards[k] for k in names])
    return dict(zip(names, outs))


def _exchange_layer(grads, collective_id):
    names = list(grads)
    nt = len(names)
    axes = [SHARD_AXIS[k] for k in names]
    widths = [grads[k].shape[SHARD_AXIS[k]] // N_DEV for k in names]

    def slot_shape(t):
        shp = list(grads[names[t]].shape)
        shp[axes[t]] = widths[t]
        return (N_DEV, *shp)

    def body(*refs):
        ins, outs = refs[:nt], refs[nt:2 * nt]
        send_sems, recv_sems, local_sem = refs[2 * nt:]
        me = _my_place()
        mine = _position(me)
        _handshake([_flip(me, flips) for flips in FLIPS])
        local = [pltpu.make_async_copy(_shard_of(ins[t], axes[t], mine, widths[t]), outs[t].at[mine], local_sem)
                 for t in range(nt)]
        for cp in local:
            cp.start()
        copies = []
        for k, flips in enumerate(FLIPS):
            peer = _flip(me, flips)
            for t in range(nt):
                cp = pltpu.make_async_remote_copy(
                    src_ref=_shard_of(ins[t], axes[t], _position(peer), widths[t]), dst_ref=outs[t].at[mine],
                    send_sem=send_sems.at[k], recv_sem=recv_sems.at[k], device_id=peer, device_id_type=MESH)
                cp.start()
                copies.append(cp)
        for cp in copies:
            cp.wait()
        for cp in local:
            cp.wait()

    outs = pl.kernel(
        body, name=f"exchange_layer_{collective_id}",
        out_type=[jax.ShapeDtypeStruct(slot_shape(t), BF16) for t in range(nt)],
        mesh=_sequencer_mesh(),
        scratch_types=[pltpu.SemaphoreType.DMA((7,)), pltpu.SemaphoreType.DMA((7,)), pltpu.SemaphoreType.DMA],
        compiler_params=pltpu.CompilerParams(collective_id=collective_id),
    )(*[grads[k] for k in names])
    return dict(zip(names, outs))


def _gather_rows(buf, reduce):
    r, c = buf.shape

    def body(in_ref, out_ref, *scratch):
        if reduce:
            all_ref, send_sems, recv_sems = scratch
        else:
            all_ref = out_ref
            send_sems, recv_sems = scratch
        me = _my_place()
        all_ref[_position(me)] = in_ref[...]
        copies = []
        for k, flips in enumerate(FLIPS):
            cp = pltpu.make_async_remote_copy(
                src_ref=in_ref, dst_ref=all_ref.at[_position(me)],
                send_sem=send_sems.at[k], recv_sem=recv_sems.at[k], device_id=_flip(me, flips), device_id_type=MESH)
            cp.start()
            copies.append(cp)
        for cp in copies:
            cp.wait()
        if reduce:
            total = all_ref[0]
            for j in range(1, N_DEV):
                total = total + all_ref[j]
            out_ref[...] = total

    return pl.pallas_call(
        body, name="sum_rows" if reduce else "gather_rows",
        in_specs=[VMEM_SPEC], out_specs=VMEM_SPEC,
        out_shape=jax.ShapeDtypeStruct((r, c) if reduce else (N_DEV, r, c), F32),
        scratch_shapes=([pltpu.VMEM((N_DEV, r, c), F32)] if reduce else [])
        + [pltpu.SemaphoreType.DMA((7,)), pltpu.SemaphoreType.DMA((7,))],
    )(buf)


def _adamw_math(w, g, m, v):
    m = ADAM_B1 * m + (1.0 - ADAM_B1) * g
    v = ADAM_B2 * v + (1.0 - ADAM_B2) * (g * g)
    m_hat = m / (1.0 - ADAM_B1 ** ADAM_STEP)
    v_hat = v / (1.0 - ADAM_B2 ** ADAM_STEP)
    delta = -ADAM_LR * (m_hat / (jnp.sqrt(v_hat) + ADAM_EPS) + ADAM_WD * w)
    return delta, m, v


def _run_behind(x, token):
    def body(x_ref, token_ref, out_ref):
        out_ref[...] = jnp.zeros_like(out_ref)

    any_spec = pl.BlockSpec(memory_space=pl.ANY)
    return pl.pallas_call(
        body, name="run_behind",
        in_specs=[any_spec, any_spec], out_specs=VMEM_SPEC,
        out_shape=jax.ShapeDtypeStruct((8, 128), F32),
    )(x, token)


def _adamw_pieces(pieces, w, m, v, after, layer=None, into=()):
    shape = w.shape
    nl = shape[0]
    cols = shape[-1]
    rows = w.size // (nl * cols)
    first, nh = (0, nl) if layer is None else (layer, 1)
    tr = min(ADAMW_BLOCK_ROWS // nh, rows // 2)
    flat3 = lambda a: a.reshape(nl, rows, cols)
    into = [flat3(a) for a in into]

    def body(*refs):
        p_refs = refs[:nh]
        w_ref, m_ref, v_ref = refs[nh:nh + 3]
        g_ref, d_ref, nm_ref, nv_ref = refs[-4:]
        for l in range(nh):
            g = p_refs[l][0].astype(F32)
            for j in range(1, N_DEV):
                g = g + p_refs[l][j].astype(F32)
            g_ref[l] = g
            d_ref[l], nm_ref[l], nv_ref[l] = _adamw_math(w_ref[l], g, m_ref[l], v_ref[l])

    blk = pl.BlockSpec((nh, tr, cols), lambda i: (first, i, 0))
    any_spec = pl.BlockSpec(memory_space=pl.ANY)
    outs = pl.pallas_call(
        body, name="adamw_pieces",
        grid=(rows // tr,),
        in_specs=[pl.BlockSpec((N_DEV, tr, cols), lambda i: (0, i, 0))] * nh
        + [blk, blk, blk, any_spec] + [any_spec] * len(into),
        out_specs=[blk] * 4,
        out_shape=[jax.ShapeDtypeStruct((nl, rows, cols), F32)] * 4,
        input_output_aliases={nh + 4 + k: k for k in range(len(into))},
        compiler_params=_cparams("parallel"),
    )(*[a.reshape(N_DEV, rows, cols) for a in pieces], flat3(w), flat3(m), flat3(v), after, *into)
    return [a.reshape(shape) for a in outs]


def _adamw_small(g, w, m, v):
    shape = w.shape
    two = lambda a: a.reshape(-1, shape[-1])

    def body(g_ref, w_ref, m_ref, v_ref, d_ref, nm_ref, nv_ref):
        d_ref[...], nm_ref[...], nv_ref[...] = _adamw_math(w_ref[...], g_ref[...], m_ref[...], v_ref[...])

    outs = pl.pallas_call(
        body, name="adamw_small",
        in_specs=[VMEM_SPEC] * 4, out_specs=[VMEM_SPEC] * 3,
        out_shape=[jax.ShapeDtypeStruct(two(w).shape, F32)] * 3,
    )(two(g), two(w), two(m), two(v))
    return [a.reshape(shape) for a in outs]


WEIGHTS = ("norm_mix", "a_w_in", "a_w_conv", "a_w_out", "b_w_in", "b_w_grp", "b_scale", "b_w_out",
           "ple_norm", "ple_w_gate", "ple_w_proj", "final_norm")
SMALL_ROWS = 24
GATHER_ID = 0
EXCHANGE_ID = 4
LAST_EXCHANGE_ID = 8


def kernel(x, p, norm_mix, a_w_in, a_w_conv, a_w_out, b_w_in, b_w_grp, b_scale, b_w_out, ple_norm, ple_w_gate, ple_w_proj, final_norm, loss_target, m_norm_mix, m_a_w_in, m_a_w_conv, m_a_w_out, m_b_w_in, m_b_w_grp, m_b_scale, m_b_w_out, m_ple_norm, m_ple_w_gate, m_ple_w_proj, m_final_norm, v_norm_mix, v_a_w_in, v_a_w_conv, v_a_w_out, v_b_w_in, v_b_w_grp, v_b_scale, v_b_w_out, v_ple_norm, v_ple_w_gate, v_ple_w_proj, v_final_norm):
    wts = dict(norm_mix=norm_mix, a_w_in=a_w_in, a_w_conv=a_w_conv, a_w_out=a_w_out, b_w_in=b_w_in, b_w_grp=b_w_grp,
               b_scale=b_scale, b_w_out=b_w_out, ple_norm=ple_norm, ple_w_gate=ple_w_gate, ple_w_proj=ple_w_proj,
               final_norm=final_norm)
    mom = dict(norm_mix=m_norm_mix, a_w_in=m_a_w_in, a_w_conv=m_a_w_conv, a_w_out=m_a_w_out, b_w_in=m_b_w_in,
               b_w_grp=m_b_w_grp, b_scale=m_b_scale, b_w_out=m_b_w_out, ple_norm=m_ple_norm, ple_w_gate=m_ple_w_gate,
               ple_w_proj=m_ple_w_proj, final_norm=m_final_norm)
    var = dict(norm_mix=v_norm_mix, a_w_in=v_a_w_in, a_w_conv=v_a_w_conv, a_w_out=v_a_w_out, b_w_in=v_b_w_in,
               b_w_grp=v_b_w_grp, b_scale=v_b_scale, b_w_out=v_b_w_out, ple_norm=v_ple_norm, ple_w_gate=v_ple_w_gate,
               ple_w_proj=v_ple_w_proj, final_norm=v_final_norm)
    d = x.shape[2]
    depth = norm_mix.shape[0]
    n_a, n_b = a_w_conv.shape[0], b_scale.shape[0]
    cw = a_w_conv.shape[2]
    pos = _position(_my_place())

    def layer_matrices(i):
        j = i // 2
        mixer = {"w_in": ("a_w_in", j), "w_out": ("a_w_out", j)} if i % 2 == 0 else \
                {"w_in": ("b_w_in", j), "w_grp": ("b_w_grp", j), "w_out": ("b_w_out", j)}
        return {**mixer, "gate": ("ple_w_gate", i), "proj": ("ple_w_proj", i)}

    full = [_all_gather_layer({k: wts[name][idx].astype(BF16) for k, (name, idx) in layer_matrices(i).items()},
                              GATHER_ID + i) for i in range(depth)]
    vec_rows = jnp.concatenate([a_w_conv.reshape(-1, cw), b_scale], axis=0)
    vecs = _gather_rows(vec_rows, reduce=False)
    n_conv = 3 * n_a
    conv_w = vecs[:, :n_conv].transpose(1, 0, 2).reshape(n_a, 3, N_DEV * cw)
    scale_w = vecs[:, n_conv:].transpose(1, 0, 2).reshape(n_b, N_DEV * cw)

    def exchange(i, g):
        if i > 0:
            return _exchange_layer(g, EXCHANGE_ID + i)
        early = {k: a for k, a in g.items() if k != "w_in"}
        return {**_exchange_layer(early, EXCHANGE_ID), **_exchange_layer({"w_in": g["w_in"]}, LAST_EXCHANGE_ID)}

    loss_row, dx, sent, (d_norm, d_ple_norm, d_final, d_conv, d_scale) = _forward_backward(
        x[0], p[:, 0], loss_target[0], full, conv_w, scale_w, norm_mix, ple_norm, final_norm, exchange)
    pieces = {name: [None] * wts[name].shape[0] for name in WEIGHTS if wts[name].ndim >= 3 and name != "a_w_conv"}
    for i in range(depth):
        for k, (name, idx) in layer_matrices(i).items():
            pieces[name][idx] = sent[i][k]

    pad = lambda a: jnp.pad(a, ((0, 0), (0, d - a.shape[1])))
    small = jnp.concatenate(d_norm + d_ple_norm + [d_final] + d_conv + d_scale + [pad(loss_row)], axis=0)
    small = jnp.pad(small, ((0, SMALL_ROWS - small.shape[0]), (0, 0)))
    total = _gather_rows(small, reduce=True)
    o = 0
    gsum = {}
    gsum["norm_mix"] = total[o:o + depth]; o += depth
    gsum["ple_norm"] = total[o:o + depth]; o += depth
    gsum["final_norm"] = total[o]; o += 1
    conv_full = total[o:o + n_conv].reshape(n_a, 3, d); o += n_conv
    scale_full = total[o:o + n_b]; o += n_b
    loss = total[o, 0]
    gsum["a_w_conv"] = lax.dynamic_slice_in_dim(conv_full, pos * cw, cw, axis=2)
    gsum["b_scale"] = lax.dynamic_slice_in_dim(scale_full, pos * cw, cw, axis=1)

    token = total
    for i in reversed(range(depth)):
        token = _run_behind(sent[i]["w_out"], token)
    last_token = _run_behind(sent[0]["w_in"], token)
    grad, delta, new_m, new_v = {}, {}, {}, {}
    for k in sorted(WEIGHTS, key=lambda name: name == "a_w_in"):
        if k == "a_w_in":
            upper = [None] * 4
            for j in reversed(range(1, n_a)):
                upper = _adamw_pieces([pieces[k][j]], wts[k], mom[k], var[k], token, j, [a for a in upper if a is not None])
            grad[k], delta[k], new_m[k], new_v[k] = _adamw_pieces(
                [pieces[k][0]], wts[k], mom[k], var[k], last_token, 0, upper)
        elif k in pieces:
            grad[k], delta[k], new_m[k], new_v[k] = _adamw_pieces(pieces[k], wts[k], mom[k], var[k], token)
        else:
            grad[k] = gsum[k]
            delta[k], new_m[k], new_v[k] = _adamw_small(gsum[k], wts[k], mom[k], var[k])
    return (loss, dx[None], *[grad[k] for k in WEIGHTS], *[delta[k] for k in WEIGHTS],
            *[new_m[k] for k in WEIGHTS], *[new_v[k] for k in WEIGHTS])
```

```python
import jax
import jax.numpy as jnp
from jax import lax
from jax.experimental import pallas as pl
from jax.experimental.pallas import tpu as pltpu
from jax.experimental.pallas import tpu_sc as plsc

F32 = jnp.float32
BF16 = jnp.bfloat16
MESH = pl.DeviceIdType.MESH

RMS_EPS = 1e-6
POOL_WINDOWS = (2, 4, 8, 16)
N_POOL_GROUPS = len(POOL_WINDOWS)
ADAM_LR = 0.001
ADAM_B1 = 0.9
ADAM_B2 = 0.999
ADAM_EPS = 1e-08
ADAM_WD = 0.01
ADAM_STEP = 10
N_DEV = 8

HALO = 16
FWD_ROW_TILE = 512
POOL_BWD_ROW_TILE = 512
CONV_BWD_ROW_TILE = 256
ADAMW_BLOCK_ROWS = 512
BWD_ROW_TILE = 512
VMEM_LIMIT = 56 * 1024 * 1024


def _cparams(*sem):
    return pltpu.CompilerParams(dimension_semantics=sem, vmem_limit_bytes=VMEM_LIMIT)


def _dot(a, b):
    return jnp.dot(a, b, preferred_element_type=F32)


def _dot_nt(a, b):
    return lax.dot_general(a, b, (((1,), (1,)), ((), ())), preferred_element_type=F32)


def _dot_tn(a, b):
    return lax.dot_general(a, b, (((0,), (0,)), ((), ())), preferred_element_type=F32)


def _rms_stats(x):
    r = lax.rsqrt(jnp.mean(x * x, axis=-1, keepdims=True) + RMS_EPS)
    return x * r, r


def _rms_bwd(dy, xh, r, g):
    a = dy * g
    return r * (a - xh * jnp.mean(a * xh, axis=-1, keepdims=True))


def _sigmoid(x):
    return 1.0 / (1.0 + jnp.exp(-x))


def _shift_down(x, k):
    return pltpu.roll(x, k, 0)


def _shift_up(x, k):
    return pltpu.roll(x, x.shape[0] - k, 0)


def _conv_taps(u, u_prev):
    uu = jnp.concatenate([u_prev, u], axis=0)
    return _shift_down(uu, 1)[HALO:], _shift_down(uu, 2)[HALO:]


def _window_mean_minus(uu, row0, window):
    acc = uu
    span = 1
    while span < window:
        acc = acc + _shift_down(acc, span)
        span *= 2
    return acc[HALO:] * _inv_count(uu.shape[0] - HALO, row0, window) - uu[HALO:]


def _inv_count(rows, row0, window):
    t = row0 + lax.broadcasted_iota(jnp.int32, (rows, 1), 0)
    return 1.0 / jnp.minimum(t + 1, window).astype(F32)


def _whole(*shape):
    return pl.BlockSpec(shape, lambda i: (0,) * len(shape), pipeline_mode=pl.Buffered(1))


def _layer_fwd(h, g, w_in, mixer, w_out, pn, w_gate, p_all, w_proj, layer, head=None):
    s, d = h.shape
    n = w_in.shape[1]
    e = w_out.shape[0]
    nsplit = n // e
    gdim = e // N_POOL_GROUPS
    pdim = p_all.shape[2]
    is_conv = mixer[0] == "conv"
    ts = min(FWD_ROW_TILE, s)
    params = mixer[1:]
    head = tuple(head or ())

    def body(*refs):
        h_ref, g_ref, win_ref = refs[:3]
        mix_refs = refs[3:3 + len(params)]
        wo_ref, pn_ref, wg_ref, p_ref, wp_ref = refs[3 + len(params):8 + len(params)]
        head_refs = refs[8 + len(params):8 + len(params) + len(head)]
        proj_ref, hn_ref, o_ref, h1_ref, h2_ref, gl_ref, pp_ref = refs[8 + len(params) + len(head):][:7]
        carry_ref = refs[-1]
        i = pl.program_id(0)

        @pl.when(i == 0)
        def _():
            carry_ref[...] = jnp.zeros_like(carry_ref)

        x = h_ref[...]
        xh, _ = _rms_stats(x)
        hn = (xh * g_ref[...]).astype(BF16)
        hn_ref[...] = hn
        parts = []
        for k in range(nsplit):
            part = _dot(hn, win_ref[:, k * e:(k + 1) * e])
            proj_ref[k] = part.astype(BF16)
            parts.append(part)
        prev = carry_ref[...]
        if is_conv:
            b, c, v, z = parts
            w_ref, = mix_refs
            u = c * v
            u1, u2 = _conv_taps(u, prev)
            mixed = b * (w_ref[0:1, :] * u2 + w_ref[1:2, :] * u1 + w_ref[2:3, :] * u)
        else:
            u, z = parts
            wgrp_ref, sc_ref = mix_refs
            uu = jnp.concatenate([prev, u], axis=0)
            cols = []
            for gi, window in enumerate(POOL_WINDOWS):
                dg = _window_mean_minus(uu[:, gi * gdim:(gi + 1) * gdim], i * ts, window)
                cols.append(_dot(dg.astype(BF16), wgrp_ref[gi]))
            mixed = jnp.concatenate(cols, axis=1) * sc_ref[...]
        carry_ref[...] = u[ts - HALO:]
        o = ((z * _sigmoid(z)) * mixed).astype(BF16)
        o_ref[...] = o
        h1 = x + _dot(o, wo_ref[...])
        h1_ref[...] = h1
        xh1, _ = _rms_stats(h1)
        gl = _dot((xh1 * pn_ref[...]).astype(BF16), wg_ref[...])
        pp = _dot(p_ref[...].astype(BF16), wp_ref[...])
        gl_ref[...] = gl.astype(BF16)
        pp_ref[...] = pp.astype(BF16)
        h2 = h1 + _sigmoid(gl) * pp
        if not head:
            h2_ref[...] = h2
            return
        t_ref, gain_ref = head_refs
        loss_ref, dgain_ref = refs[-3], refs[-2]

        @pl.when(i == 0)
        def _():
            loss_ref[...] = jnp.zeros_like(loss_ref)
            dgain_ref[...] = jnp.zeros_like(dgain_ref)

        gain = gain_ref[...]
        yh, r = _rms_stats(h2)
        err = yh * gain - t_ref[...]
        loss_ref[...] += jnp.full(loss_ref.shape, (0.5 / d) * jnp.sum(err * err), F32)
        dy = err * (1.0 / d)
        dgain_ref[...] += jnp.sum(dy * yh, axis=0, keepdims=True)
        h2_ref[...] = _rms_bwd(dy, yh, r, gain)

    row = lambda width: pl.BlockSpec((ts, width), lambda i: (i, 0))
    mix_specs = [_whole(*a.shape) for a in params]
    head_specs = [row(d), _whole(1, d)] if head else []
    head_out_specs = [_whole(1, 128), _whole(1, d)] if head else []
    head_out_shape = [jax.ShapeDtypeStruct((1, 128), F32), jax.ShapeDtypeStruct((1, d), F32)] if head else []
    outs = pl.pallas_call(
        body, name="layer_fwd",
        grid=(s // ts,),
        in_specs=[row(d), _whole(1, d), _whole(d, n)] + mix_specs
        + [_whole(e, d), _whole(1, d), _whole(d, d),
           pl.BlockSpec((None, ts, pdim), lambda i: (layer, i, 0)), _whole(pdim, d)] + head_specs,
        out_specs=[pl.BlockSpec((nsplit, ts, e), lambda i: (0, i, 0)),
                   row(d), row(e), row(d), row(d), row(d), row(d)] + head_out_specs,
        out_shape=[jax.ShapeDtypeStruct((nsplit, s, e), BF16), jax.ShapeDtypeStruct((s, d), BF16),
                   jax.ShapeDtypeStruct((s, e), BF16), jax.ShapeDtypeStruct((s, d), F32),
                   jax.ShapeDtypeStruct((s, d), F32), jax.ShapeDtypeStruct((s, d), BF16),
                   jax.ShapeDtypeStruct((s, d), BF16)] + head_out_shape,
        scratch_shapes=[pltpu.VMEM((HALO, e), F32)],
        compiler_params=_cparams("arbitrary"),
    )(h, g, w_in, *params, w_out, pn, w_gate, p_all, w_proj, *head)
    proj, hn, o, h1, h2, gl, pp = outs[:7]
    return (h2, *outs[7:]) if head else h2, (proj, hn, o, h1, gl, pp)


def _out_ple_bwd(dh2, gl, pp, h1, p_all, o, pn, wgate, wout, layer):
    s, d = dh2.shape
    e = o.shape[1]
    pdim = p_all.shape[2]
    ts = min(BWD_ROW_TILE, s)
    last = s // ts - 1

    def body(dh2_ref, gl_ref, pp_ref, h1_ref, p_ref, o_ref, pn_ref, wg_ref, wo_ref,
             dh1_ref, do_ref, dwp_ref, dwg_ref, dwo_ref, dpn_ref, awp, awg, awo):
        i = pl.program_id(0)

        @pl.when(i == 0)
        def _():
            awp[...] = jnp.zeros_like(awp)
            awg[...] = jnp.zeros_like(awg)
            awo[...] = jnp.zeros_like(awo)
            dpn_ref[...] = jnp.zeros_like(dpn_ref)

        dh2 = dh2_ref[...]
        gate = _sigmoid(gl_ref[...].astype(F32))
        dpp = (dh2 * gate).astype(BF16)
        dgl = (dh2 * pp_ref[...].astype(F32) * gate * (1.0 - gate)).astype(BF16)
        xh, r = _rms_stats(h1_ref[...])
        pn = pn_ref[...]
        awp[...] += _dot_tn(p_ref[...].astype(BF16), dpp)
        awg[...] += _dot_tn((xh * pn).astype(BF16), dgl)
        dr = _dot_nt(dgl, wg_ref[...])
        dpn_ref[...] += jnp.sum(dr * xh, axis=0, keepdims=True)
        dh1 = dh2 + _rms_bwd(dr, xh, r, pn)
        dh1_ref[...] = dh1
        dh1b = dh1.astype(BF16)
        do_ref[...] = _dot_nt(dh1b, wo_ref[...]).astype(BF16)
        awo[...] += _dot_tn(o_ref[...], dh1b)

        @pl.when(i == last)
        def _():
            dwp_ref[...] = awp[...].astype(BF16)
            dwg_ref[...] = awg[...].astype(BF16)
            dwo_ref[...] = awo[...].astype(BF16)

    row = lambda width: pl.BlockSpec((ts, width), lambda i: (i, 0))
    return pl.pallas_call(
        body, name="out_ple_bwd",
        grid=(s // ts,),
        in_specs=[row(d), row(d), row(d), row(d),
                  pl.BlockSpec((None, ts, pdim), lambda i: (layer, i, 0)),
                  row(e), _whole(1, d), _whole(d, d), _whole(e, d)],
        out_specs=[row(d), row(e), _whole(pdim, d), _whole(d, d), _whole(e, d), _whole(1, d)],
        out_shape=[jax.ShapeDtypeStruct((s, d), F32), jax.ShapeDtypeStruct((s, e), BF16),
                   jax.ShapeDtypeStruct((pdim, d), BF16), jax.ShapeDtypeStruct((d, d), BF16),
                   jax.ShapeDtypeStruct((e, d), BF16), jax.ShapeDtypeStruct((1, d), F32)],
        scratch_shapes=[pltpu.VMEM((pdim, d), F32), pltpu.VMEM((d, d), F32), pltpu.VMEM((e, d), F32)],
        compiler_params=_cparams("arbitrary"),
    )(dh2, gl, pp, h1, p_all, o, pn, wgate, wout)


def _mixer_bwd(do, proj, hn, mixer, w_in, h, g, dh1):
    s, d = h.shape
    nsplit, _, e = proj.shape
    gdim = e // N_POOL_GROUPS
    is_conv = mixer[0] == "conv"
    ts = min(CONV_BWD_ROW_TILE if is_conv else POOL_BWD_ROW_TILE, s)
    nt = s // ts
    params = mixer[1:]
    n_mix_out = 1 if is_conv else 2

    def body(*refs):
        refs = list(refs)
        take = lambda n: [refs.pop(0) for _ in range(n)]
        do_ref, p_ref, ph_ref = take(3)
        mix_refs = take(len(params))
        win_ref, h_ref, g_ref, dh1_ref, hn_ref = take(5)
        dwin_ref, dh_ref, dg_ref = take(3)
        mix_out = take(n_mix_out)
        carry_ref, dp_ref, acc_ref = take(3)
        i = pl.program_id(0)
        tile = nt - 1 - i

        @pl.when(i == 0)
        def _():
            for ref in [carry_ref, dg_ref, acc_ref] + mix_out[-1:] + refs:
                ref[...] = jnp.zeros_like(ref)

        dof = do_ref[...].astype(F32)
        nxt = carry_ref[...]
        if is_conv:
            w_ref, = mix_refs
            dw_ref, = mix_out
            w0, w1, w2 = w_ref[0:1, :], w_ref[1:2, :], w_ref[2:3, :]
            b, c, v, z = [p_ref[k].astype(F32) for k in range(4)]
            u = c * v
            u_prev = jnp.where(tile == 0, 0.0, ph_ref[1].astype(F32) * ph_ref[2].astype(F32))
            u1, u2 = _conv_taps(u, u_prev)
            conv = w0 * u2 + w1 * u1 + w2 * u
            sig = _sigmoid(z)
            sz = z * sig
            dy = dof * sz
            dp_ref[3] = (dof * (b * conv) * (sig + sz * (1.0 - sig))).astype(BF16)
            dp_ref[0] = (dy * conv).astype(BF16)
            dconv = dy * b
            dw_ref[0:1, :] += jnp.sum(dconv * u2, axis=0, keepdims=True)
            dw_ref[1:2, :] += jnp.sum(dconv * u1, axis=0, keepdims=True)
            dw_ref[2:3, :] += jnp.sum(dconv * u, axis=0, keepdims=True)
            dcc = jnp.concatenate([dconv, nxt], axis=0)
            du = w2 * dconv + w1 * _shift_up(dcc, 1)[:ts] + w0 * _shift_up(dcc, 2)[:ts]
            carry_ref[...] = dconv[:HALO]
            dp_ref[1] = (du * v).astype(BF16)
            dp_ref[2] = (du * c).astype(BF16)
        else:
            wgrp_ref, sc_ref = mix_refs
            dsc_ref = mix_out[1]
            agrp, = refs
            u = p_ref[0].astype(F32)
            z = p_ref[1].astype(F32)
            u_prev = jnp.where(tile == 0, 0.0, ph_ref[0].astype(F32))
            uu = jnp.concatenate([u_prev, u], axis=0)
            sig = _sigmoid(z)
            sz = z * sig
            dm = dof * sz
            dsilu = dof * (sig + sz * (1.0 - sig))
            for gi, window in enumerate(POOL_WINDOWS):
                cols = slice(gi * gdim, (gi + 1) * gdim)
                w = wgrp_ref[gi]
                scale = sc_ref[:, cols]
                db = _window_mean_minus(uu[:, cols], tile * ts, window).astype(BF16)
                mr = _dot(db, w)
                dp_ref[1, :, cols] = (dsilu[:, cols] * (mr * scale)).astype(BF16)
                dmg = dm[:, cols]
                dmr = (dmg * scale).astype(BF16)
                agrp[gi] += _dot_tn(db, dmr)
                dsc_ref[:, cols] += jnp.sum(dmg * mr, axis=0, keepdims=True)
                dd = _dot_nt(dmr, w)
                ddq = dd * _inv_count(ts, tile * ts, window)
                acc = jnp.concatenate([ddq, nxt[:, cols]], axis=0)
                span = 1
                while span < window:
                    acc = acc + _shift_up(acc, span)
                    span *= 2
                carry_ref[:, cols] = ddq[:HALO]
                dp_ref[0, :, cols] = (acc[:ts] - dd).astype(BF16)

        dhn = _dot_nt(dp_ref[0], win_ref[:, 0:e])
        for k in range(1, nsplit):
            dhn += _dot_nt(dp_ref[k], win_ref[:, k * e:(k + 1) * e])
        xh, r = _rms_stats(h_ref[...])
        dg_ref[...] += jnp.sum(dhn * xh, axis=0, keepdims=True)
        dh_ref[...] = dh1_ref[...] + _rms_bwd(dhn, xh, r, g_ref[...])
        hn_tile = hn_ref[...]
        for k in range(nsplit):
            acc_ref[:, k * e:(k + 1) * e] += _dot_tn(hn_tile, dp_ref[k])

        @pl.when(i == nt - 1)
        def _():
            dwin_ref[...] = acc_ref[...].astype(BF16)
            if not is_conv:
                mix_out[0][...] = refs[0][...].astype(BF16)

    rev = lambda width: pl.BlockSpec((ts, width), lambda i: (nt - 1 - i, 0))
    halo_blocks = ts // HALO
    in_specs = [rev(e),
                pl.BlockSpec((nsplit, ts, e), lambda i: (0, nt - 1 - i, 0)),
                pl.BlockSpec((nsplit, HALO, e), lambda i: (0, jnp.maximum((nt - 1 - i) * halo_blocks - 1, 0), 0))]
    in_specs += [_whole(*a.shape) for a in params]
    in_specs += [_whole(d, nsplit * e), rev(d), _whole(1, d), rev(d), rev(d)]
    out_specs = [_whole(d, nsplit * e), rev(d), _whole(1, d)]
    out_shape = [jax.ShapeDtypeStruct((d, nsplit * e), BF16), jax.ShapeDtypeStruct((s, d), F32),
                 jax.ShapeDtypeStruct((1, d), F32)]
    scratch = [pltpu.VMEM((HALO, e), F32), pltpu.VMEM((nsplit, ts, e), BF16), pltpu.VMEM((d, nsplit * e), F32)]
    if is_conv:
        out_specs += [_whole(3, e)]
        out_shape += [jax.ShapeDtypeStruct((3, e), F32)]
    else:
        out_specs += [_whole(N_POOL_GROUPS, gdim, gdim), _whole(1, e)]
        out_shape += [jax.ShapeDtypeStruct((N_POOL_GROUPS, gdim, gdim), BF16), jax.ShapeDtypeStruct((1, e), F32)]
        scratch += [pltpu.VMEM((N_POOL_GROUPS, gdim, gdim), F32)]
    return pl.pallas_call(
        body, name="mixer_bwd",
        grid=(nt,),
        in_specs=in_specs, out_specs=out_specs, out_shape=out_shape, scratch_shapes=scratch,
        compiler_params=_cparams("arbitrary"),
    )(do, proj, proj, *params, w_in, h, g, dh1, hn)


def _forward_backward(xs, ps, target, full, conv_w, scale_w, norm_mix, ple_norm, final_norm, exchange):
    depth = len(full)
    row = lambda a, i: a[i][None, :]
    mixer_of = lambda i: ("conv", conv_w[i // 2]) if i % 2 == 0 else ("pool", full[i]["w_grp"], row(scale_w, i // 2))

    saved = []
    h = xs
    for i in range(depth):
        w = full[i]
        head = (target, final_norm[None, :]) if i == depth - 1 else None
        h_next, acts = _layer_fwd(h, row(norm_mix, i), w["w_in"], mixer_of(i), w["w_out"], row(ple_norm, i),
                                  w["gate"], ps, w["proj"], i, head)
        saved.append((h, *acts))
        h = h_next
    dh, loss_row, d_final = h

    d_norm, d_ple_norm, d_conv, d_scale, sent = [None] * depth, [None] * depth, [], [], [None] * depth
    for i in reversed(range(depth)):
        w = full[i]
        h_in, proj, hn, o, h1, gl, pp = saved[i]
        g = {}
        dh1, do, g["proj"], g["gate"], g["w_out"], d_ple_norm[i] = _out_ple_bwd(
            dh, gl, pp, h1, ps, o, row(ple_norm, i), w["gate"], w["w_out"], i)
        g["w_in"], dh, d_norm[i], *mixer_grads = _mixer_bwd(
            do, proj, hn, mixer_of(i), w["w_in"], h_in, row(norm_mix, i), dh1)
        if i % 2 == 0:
            d_conv.insert(0, mixer_grads[0])
        else:
            g["w_grp"] = mixer_grads[0]
            d_scale.insert(0, mixer_grads[1])
        sent[i] = exchange(i, g)
    return loss_row, dh, sent, (d_norm, d_ple_norm, d_final, d_conv, d_scale)


VMEM_SPEC = pl.BlockSpec(memory_space=pltpu.VMEM)

FLIPS = [(fx, fy, fc) for fx in (0, 1) for fy in (0, 1) for fc in (0, 1)][1:]
SHARD_AXIS = {"w_in": 1, "w_out": 0, "w_grp": 1, "gate": 0, "proj": 1}


def _my_place():
    return lax.axis_index("x"), lax.axis_index("y"), lax.axis_index("c")


def _position(place):
    x, y, c = place
    return 4 * x + 2 * y + c


def _flip(place, flips):
    return tuple(1 - v if f else v for v, f in zip(place, flips))


def _shard_of(ref, axis, pos, n):
    idx = [slice(None)] * len(ref.shape)
    idx[axis] = pl.ds(pl.multiple_of(pos * n, n), n)
    return ref.at[tuple(idx)]


def _sequencer_mesh():
    return plsc.ScalarSubcoreMesh(axis_name="sequencer", num_cores=1)


def _handshake(peers):
    barrier = pltpu.get_barrier_semaphore()
    for peer in peers:
        pl.semaphore_signal(barrier, inc=1, device_id=peer, device_id_type=MESH)
    pl.semaphore_wait(barrier, len(peers))


def _all_gather_layer(stacked, collective_id):
    names = list(stacked)
    nt = len(names)
    axes = [SHARD_AXIS[k] for k in names]
    shapes = [stacked[k][0].shape[1:] for k in names]
    widths = [shp[ax] for shp, ax in zip(shapes, axes)]

    def full_shape(t):
        shp = list(shapes[t])
        shp[axes[t]] *= N_DEV
        return tuple(shp)

    def body(*refs):
        ins = [ref.at[stacked[k][1]] for ref, k in zip(refs[:nt], names)]
        outs = refs[nt:2 * nt]
        send_sems, recv_sems, local_sem = refs[2 * nt:]
        me = _my_place()
        x, y, c = me
        sibling = (x, y, 1 - c)
        flip = lambda v, f: v + f - 2 * v * f
        neighbour = lambda core, fx: (flip(x, fx), flip(y, 1 - fx), core)
        first, second = neighbour(c, c), neighbour(c, 1 - c)
        diagonal = (1 - x, 1 - y, c)
        _handshake([sibling, first, second])

        def block(t, place):
            return _shard_of(outs[t], axes[t], _position(place), widths[t])

        def copy(t, k, place, to, src=None):
            return pltpu.make_async_remote_copy(
                src_ref=block(t, place) if src is None else src, dst_ref=block(t, place),
                send_sem=send_sems.at[k], recv_sem=recv_sems.at[k], device_id=to, device_id_type=MESH)

        everything = lambda make: [make(t) for t in range(nt)]
        mine = everything(lambda t: pltpu.make_async_copy(ins[t], block(t, me), local_sem))
        sent = (everything(lambda t: copy(t, 1, me, first, src=ins[t]))
                + everything(lambda t: copy(t, 2, me, second, src=ins[t]))
                + everything(lambda t: copy(t, 0, me, sibling, src=ins[t])))
        for cp in mine + sent:
            cp.start()
        for k, arrived, onward in ((1, first, second), (2, second, None), (3, diagonal, None)):
            for t in range(nt):
                copy(t, k, arrived, me).wait_recv()
            if onward is not None:
                sent += everything(lambda t: copy(t, 3, arrived, onward))
            sent += everything(lambda t: copy(t, 3 + k, arrived, sibling))
            for cp in sent[-nt * (1 + (onward is not None)):]:
                cp.start()
        for k, place in ((0, sibling), (4, neighbour(1 - c, 1 - c)), (5, neighbour(1 - c, c)), (6, (1 - x, 1 - y, 1 - c))):
            for t in range(nt):
                copy(t, k, place, me).wait_recv()
        for cp in sent:
            cp.wait_send()
        for cp in mine:
            cp.wait()

    outs = pl.kernel(
        body, name=f"all_gather_layer_{collective_id}",
        out_type=[jax.ShapeDtypeStruct(full_shape(t), stacked[k][0].dtype) for t, k in enumerate(names)],
        mesh=_sequencer_mesh(),
        scratch_types=[pltpu.SemaphoreType.DMA((7,)), pltpu.SemaphoreType.DMA((7,)), pltpu.SemaphoreType.DMA],
        compiler_params=pltpu.CompilerParams(collective_id=collective_id),
    )(*[stacked[k][0] for k in names])
    return dict(zip(names, outs))


def _exchange_layer(grads, collective_id):
    names = list(grads)
    nt = len(names)
    axes = [SHARD_AXIS[k] for k in names]
    widths = [grads[k].shape[SHARD_AXIS[k]] // N_DEV for k in names]

    def slot_shape(t):
        shp = list(grads[names[t]].shape)
        shp[axes[t]] = widths[t]
        return (N_DEV, *shp)

    def body(*refs):
        ins, outs = refs[:nt], refs[nt:2 * nt]
        send_sems, recv_sems, local_sem = refs[2 * nt:]
        me = _my_place()
        mine = _position(me)
        _handshake([_flip(me, flips) for flips in FLIPS])
        local = [pltpu.make_async_copy(_shard_of(ins[t], axes[t], mine, widths[t]), outs[t].at[mine], local_sem)
                 for t in range(nt)]
        for cp in local:
            cp.start()
        copies = []
        for k, flips in enumerate(FLIPS):
            peer = _flip(me, flips)
            for t in range(nt):
                cp = pltpu.make_async_remote_copy(
                    src_ref=_shard_of(ins[t], axes[t], _position(peer), widths[t]), dst_ref=outs[t].at[mine],
                    send_sem=send_sems.at[k], recv_sem=recv_sems.at[k], device_id=peer, device_id_type=MESH)
                cp.start()
                copies.append(cp)
        for cp in copies:
            cp.wait()
        for cp in local:
            cp.wait()

    outs = pl.kernel(
        body, name=f"exchange_layer_{collective_id}",
        out_type=[jax.ShapeDtypeStruct(slot_shape(t), BF16) for t in range(nt)],
        mesh=_sequencer_mesh(),
        scratch_types=[pltpu.SemaphoreType.DMA((7,)), pltpu.SemaphoreType.DMA((7,)), pltpu.SemaphoreType.DMA],
        compiler_params=pltpu.CompilerParams(collective_id=collective_id),
    )(*[grads[k] for k in names])
    return dict(zip(names, outs))


def _gather_rows(buf, reduce):
    r, c = buf.shape

    def body(in_ref, out_ref, *scratch):
        if reduce:
            all_ref, send_sems, recv_sems = scratch
        else:
            all_ref = out_ref
            send_sems, recv_sems = scratch
        me = _my_place()
        all_ref[_position(me)] = in_ref[...]
        copies = []
        for k, flips in enumerate(FLIPS):
            cp = pltpu.make_async_remote_copy(
                src_ref=in_ref, dst_ref=all_ref.at[_position(me)],
                send_sem=send_sems.at[k], recv_sem=recv_sems.at[k], device_id=_flip(me, flips), device_id_type=MESH)
            cp.start()
            copies.append(cp)
        for cp in copies:
            cp.wait()
        if reduce:
            total = all_ref[0]
            for j in range(1, N_DEV):
                total = total + all_ref[j]
            out_ref[...] = total

    return pl.pallas_call(
        body, name="sum_rows" if reduce else "gather_rows",
        in_specs=[VMEM_SPEC], out_specs=VMEM_SPEC,
        out_shape=jax.ShapeDtypeStruct((r, c) if reduce else (N_DEV, r, c), F32),
        scratch_shapes=([pltpu.VMEM((N_DEV, r, c), F32)] if reduce else [])
        + [pltpu.SemaphoreType.DMA((7,)), pltpu.SemaphoreType.DMA((7,))],
    )(buf)


def _adamw_math(w, g, m, v):
    m = ADAM_B1 * m + (1.0 - ADAM_B1) * g
    v = ADAM_B2 * v + (1.0 - ADAM_B2) * (g * g)
    m_hat = m / (1.0 - ADAM_B1 ** ADAM_STEP)
    v_hat = v / (1.0 - ADAM_B2 ** ADAM_STEP)
    delta = -ADAM_LR * (m_hat / (jnp.sqrt(v_hat) + ADAM_EPS) + ADAM_WD * w)
    return delta, m, v


def _run_behind(x, token):
    def body(x_ref, token_ref, out_ref):
        out_ref[...] = jnp.zeros_like(out_ref)

    any_spec = pl.BlockSpec(memory_space=pl.ANY)
    return pl.pallas_call(
        body, name="run_behind",
        in_specs=[any_spec, any_spec], out_specs=VMEM_SPEC,
        out_shape=jax.ShapeDtypeStruct((8, 128), F32),
    )(x, token)


def _adamw_pieces(pieces, w, m, v, after, layer=None, into=()):
    shape = w.shape
    nl = shape[0]
    cols = shape[-1]
    rows = w.size // (nl * cols)
    first, nh = (0, nl) if layer is None else (layer, 1)
    tr = min(ADAMW_BLOCK_ROWS // nh, rows // 2)
    flat3 = lambda a: a.reshape(nl, rows, cols)
    into = [flat3(a) for a in into]

    def body(*refs):
        p_refs = refs[:nh]
        w_ref, m_ref, v_ref = refs[nh:nh + 3]
        g_ref, d_ref, nm_ref, nv_ref = refs[-4:]
        for l in range(nh):
            g = p_refs[l][0].astype(F32)
            for j in range(1, N_DEV):
                g = g + p_refs[l][j].astype(F32)
            g_ref[l] = g
            d_ref[l], nm_ref[l], nv_ref[l] = _adamw_math(w_ref[l], g, m_ref[l], v_ref[l])

    blk = pl.BlockSpec((nh, tr, cols), lambda i: (first, i, 0))
    any_spec = pl.BlockSpec(memory_space=pl.ANY)
    outs = pl.pallas_call(
        body, name="adamw_pieces",
        grid=(rows // tr,),
        in_specs=[pl.BlockSpec((N_DEV, tr, cols), lambda i: (0, i, 0))] * nh
        + [blk, blk, blk, any_spec] + [any_spec] * len(into),
        out_specs=[blk] * 4,
        out_shape=[jax.ShapeDtypeStruct((nl, rows, cols), F32)] * 4,
        input_output_aliases={nh + 4 + k: k for k in range(len(into))},
        compiler_params=_cparams("parallel"),
    )(*[a.reshape(N_DEV, rows, cols) for a in pieces], flat3(w), flat3(m), flat3(v), after, *into)
    return [a.reshape(shape) for a in outs]


def _adamw_small(g, w, m, v):
    shape = w.shape
    two = lambda a: a.reshape(-1, shape[-1])

    def body(g_ref, w_ref, m_ref, v_ref, d_ref, nm_ref, nv_ref):
        d_ref[...], nm_ref[...], nv_ref[...] = _adamw_math(w_ref[...], g_ref[...], m_ref[...], v_ref[...])

    outs = pl.pallas_call(
        body, name="adamw_small",
        in_specs=[VMEM_SPEC] * 4, out_specs=[VMEM_SPEC] * 3,
        out_shape=[jax.ShapeDtypeStruct(two(w).shape, F32)] * 3,
    )(two(g), two(w), two(m), two(v))
    return [a.reshape(shape) for a in outs]


WEIGHTS = ("norm_mix", "a_w_in", "a_w_conv", "a_w_out", "b_w_in", "b_w_grp", "b_scale", "b_w_out",
           "ple_norm", "ple_w_gate", "ple_w_proj", "final_norm")
SMALL_ROWS = 24
GATHER_ID = 0
EXCHANGE_ID = 4
LAST_EXCHANGE_ID = 8


def kernel(x, p, norm_mix, a_w_in, a_w_conv, a_w_out, b_w_in, b_w_grp, b_scale, b_w_out, ple_norm, ple_w_gate, ple_w_proj, final_norm, loss_target, m_norm_mix, m_a_w_in, m_a_w_conv, m_a_w_out, m_b_w_in, m_b_w_grp, m_b_scale, m_b_w_out, m_ple_norm, m_ple_w_gate, m_ple_w_proj, m_final_norm, v_norm_mix, v_a_w_in, v_a_w_conv, v_a_w_out, v_b_w_in, v_b_w_grp, v_b_scale, v_b_w_out, v_ple_norm, v_ple_w_gate, v_ple_w_proj, v_final_norm):
    wts = dict(norm_mix=norm_mix, a_w_in=a_w_in, a_w_conv=a_w_conv, a_w_out=a_w_out, b_w_in=b_w_in, b_w_grp=b_w_grp,
               b_scale=b_scale, b_w_out=b_w_out, ple_norm=ple_norm, ple_w_gate=ple_w_gate, ple_w_proj=ple_w_proj,
               final_norm=final_norm)
    mom = dict(norm_mix=m_norm_mix, a_w_in=m_a_w_in, a_w_conv=m_a_w_conv, a_w_out=m_a_w_out, b_w_in=m_b_w_in,
               b_w_grp=m_b_w_grp, b_scale=m_b_scale, b_w_out=m_b_w_out, ple_norm=m_ple_norm, ple_w_gate=m_ple_w_gate,
               ple_w_proj=m_ple_w_proj, final_norm=m_final_norm)
    var = dict(norm_mix=v_norm_mix, a_w_in=v_a_w_in, a_w_conv=v_a_w_conv, a_w_out=v_a_w_out, b_w_in=v_b_w_in,
               b_w_grp=v_b_w_grp, b_scale=v_b_scale, b_w_out=v_b_w_out, ple_norm=v_ple_norm, ple_w_gate=v_ple_w_gate,
               ple_w_proj=v_ple_w_proj, final_norm=v_final_norm)
    d = x.shape[2]
    depth = norm_mix.shape[0]
    n_a, n_b = a_w_conv.shape[0], b_scale.shape[0]
    cw = a_w_conv.shape[2]
    pos = _position(_my_place())

    def layer_matrices(i):
        j = i // 2
        mixer = {"w_in": ("a_w_in", j), "w_out": ("a_w_out", j)} if i % 2 == 0 else \
                {"w_in": ("b_w_in", j), "w_grp": ("b_w_grp", j), "w_out": ("b_w_out", j)}
        return {**mixer, "gate": ("ple_w_gate", i), "proj": ("ple_w_proj", i)}

    shards = {name: wts[name].astype(BF16) for name in WEIGHTS if wts[name].ndim >= 3 and name != "a_w_conv"}
    full = [_all_gather_layer({k: (shards[name], idx) for k, (name, idx) in layer_matrices(i).items()},
                              GATHER_ID + i) for i in range(depth)]
    vec_rows = jnp.concatenate([a_w_conv.reshape(-1, cw), b_scale], axis=0)
    vecs = _gather_rows(vec_rows, reduce=False)
    n_conv = 3 * n_a
    conv_w = vecs[:, :n_conv].transpose(1, 0, 2).reshape(n_a, 3, N_DEV * cw)
    scale_w = vecs[:, n_conv:].transpose(1, 0, 2).reshape(n_b, N_DEV * cw)

    def exchange(i, g):
        if i > 0:
            return _exchange_layer(g, EXCHANGE_ID + i)
        early = {k: a for k, a in g.items() if k != "w_in"}
        return {**_exchange_layer(early, EXCHANGE_ID), **_exchange_layer({"w_in": g["w_in"]}, LAST_EXCHANGE_ID)}

    loss_row, dx, sent, (d_norm, d_ple_norm, d_final, d_conv, d_scale) = _forward_backward(
        x[0], p[:, 0], loss_target[0], full, conv_w, scale_w, norm_mix, ple_norm, final_norm, exchange)
    pieces = {name: [None] * wts[name].shape[0] for name in WEIGHTS if wts[name].ndim >= 3 and name != "a_w_conv"}
    for i in range(depth):
        for k, (name, idx) in layer_matrices(i).items():
            pieces[name][idx] = sent[i][k]

    pad = lambda a: jnp.pad(a, ((0, 0), (0, d - a.shape[1])))
    small = jnp.concatenate(d_norm + d_ple_norm + [d_final] + d_conv + d_scale + [pad(loss_row)], axis=0)
    small = jnp.pad(small, ((0, SMALL_ROWS - small.shape[0]), (0, 0)))
    total = _gather_rows(small, reduce=True)
    o = 0
    gsum = {}
    gsum["norm_mix"] = total[o:o + depth]; o += depth
    gsum["ple_norm"] = total[o:o + depth]; o += depth
    gsum["final_norm"] = total[o]; o += 1
    conv_full = total[o:o + n_conv].reshape(n_a, 3, d); o += n_conv
    scale_full = total[o:o + n_b]; o += n_b
    loss = total[o, 0]
    gsum["a_w_conv"] = lax.dynamic_slice_in_dim(conv_full, pos * cw, cw, axis=2)
    gsum["b_scale"] = lax.dynamic_slice_in_dim(scale_full, pos * cw, cw, axis=1)

    token = total
    for i in reversed(range(depth)):
        token = _run_behind(sent[i]["w_out"], token)
    last_token = _run_behind(sent[0]["w_in"], token)
    grad, delta, new_m, new_v = {}, {}, {}, {}
    for k in sorted(WEIGHTS, key=lambda name: name == "a_w_in"):
        if k == "a_w_in":
            upper = [None] * 4
            for j in reversed(range(1, n_a)):
                upper = _adamw_pieces([pieces[k][j]], wts[k], mom[k], var[k], token, j, [a for a in upper if a is not None])
            grad[k], delta[k], new_m[k], new_v[k] = _adamw_pieces(
                [pieces[k][0]], wts[k], mom[k], var[k], last_token, 0, upper)
        elif k in pieces:
            grad[k], delta[k], new_m[k], new_v[k] = _adamw_pieces(pieces[k], wts[k], mom[k], var[k], token)
        else:
            grad[k] = gsum[k]
            delta[k], new_m[k], new_v[k] = _adamw_small(gsum[k], wts[k], mom[k], var[k])
    return (loss, dx[None], *[grad[k] for k in WEIGHTS], *[delta[k] for k in WEIGHTS],
            *[new_m[k] for k in WEIGHTS], *[new_v[k] for k in WEIGHTS])
```

```python
import jax
import jax.numpy as jnp
from jax import lax
from jax.experimental import pallas as pl
from jax.experimental.pallas import tpu as pltpu
from jax.experimental.pallas import tpu_sc as plsc

F32 = jnp.float32
BF16 = jnp.bfloat16
MESH = pl.DeviceIdType.MESH

RMS_EPS = 1e-6
POOL_WINDOWS = (2, 4, 8, 16)
N_POOL_GROUPS = len(POOL_WINDOWS)
ADAM_LR = 0.001
ADAM_B1 = 0.9
ADAM_B2 = 0.999
ADAM_EPS = 1e-08
ADAM_WD = 0.01
ADAM_STEP = 10
N_DEV = 8

HALO = 16
FWD_ROW_TILE = 512
POOL_BWD_ROW_TILE = 512
CONV_BWD_ROW_TILE = 256
ADAMW_BLOCK_ROWS = 512
BWD_ROW_TILE = 512
VMEM_LIMIT = 56 * 1024 * 1024


def _cparams(*sem):
    return pltpu.CompilerParams(dimension_semantics=sem, vmem_limit_bytes=VMEM_LIMIT)


def _dot(a, b):
    return jnp.dot(a, b, preferred_element_type=F32)


def _dot_nt(a, b):
    return lax.dot_general(a, b, (((1,), (1,)), ((), ())), preferred_element_type=F32)


def _dot_tn(a, b):
    return lax.dot_general(a, b, (((0,), (0,)), ((), ())), preferred_element_type=F32)


def _rms_stats(x):
    r = lax.rsqrt(jnp.mean(x * x, axis=-1, keepdims=True) + RMS_EPS)
    return x * r, r


def _rms_bwd(dy, xh, r, g):
    a = dy * g
    return r * (a - xh * jnp.mean(a * xh, axis=-1, keepdims=True))


def _sigmoid(x):
    return 1.0 / (1.0 + jnp.exp(-x))


def _shift_down(x, k):
    return pltpu.roll(x, k, 0)


def _shift_up(x, k):
    return pltpu.roll(x, x.shape[0] - k, 0)


def _conv_taps(u, u_prev):
    uu = jnp.concatenate([u_prev, u], axis=0)
    return _shift_down(uu, 1)[HALO:], _shift_down(uu, 2)[HALO:]


def _window_mean_minus(uu, row0, window):
    acc = uu
    span = 1
    while span < window:
        acc = acc + _shift_down(acc, span)
        span *= 2
    return acc[HALO:] * _inv_count(uu.shape[0] - HALO, row0, window) - uu[HALO:]


def _inv_count(rows, row0, window):
    t = row0 + lax.broadcasted_iota(jnp.int32, (rows, 1), 0)
    return 1.0 / jnp.minimum(t + 1, window).astype(F32)


def _whole(*shape):
    return pl.BlockSpec(shape, lambda i: (0,) * len(shape), pipeline_mode=pl.Buffered(1))


def _layer_fwd(h, g, w_in, mixer, w_out, pn, w_gate, p_all, w_proj, layer, head=None):
    s, d = h.shape
    n = w_in.shape[1]
    e = w_out.shape[0]
    nsplit = n // e
    gdim = e // N_POOL_GROUPS
    pdim = p_all.shape[2]
    is_conv = mixer[0] == "conv"
    ts = min(FWD_ROW_TILE, s)
    params = mixer[1:]
    head = tuple(head or ())

    def body(*refs):
        h_ref, g_ref, win_ref = refs[:3]
        mix_refs = refs[3:3 + len(params)]
        wo_ref, pn_ref, wg_ref, p_ref, wp_ref = refs[3 + len(params):8 + len(params)]
        head_refs = refs[8 + len(params):8 + len(params) + len(head)]
        proj_ref, hn_ref, o_ref, h1_ref, h2_ref, gl_ref, pp_ref = refs[8 + len(params) + len(head):][:7]
        carry_ref = refs[-1]
        i = pl.program_id(0)

        @pl.when(i == 0)
        def _():
            carry_ref[...] = jnp.zeros_like(carry_ref)

        x = h_ref[...]
        xh, _ = _rms_stats(x)
        hn = (xh * g_ref[...]).astype(BF16)
        hn_ref[...] = hn
        parts = []
        for k in range(nsplit):
            part = _dot(hn, win_ref[:, k * e:(k + 1) * e])
            proj_ref[k] = part.astype(BF16)
            parts.append(part)
        prev = carry_ref[...]
        if is_conv:
            b, c, v, z = parts
            w_ref, = mix_refs
            u = c * v
            u1, u2 = _conv_taps(u, prev)
            mixed = b * (w_ref[0:1, :] * u2 + w_ref[1:2, :] * u1 + w_ref[2:3, :] * u)
        else:
            u, z = parts
            wgrp_ref, sc_ref = mix_refs
            uu = jnp.concatenate([prev, u], axis=0)
            cols = []
            for gi, window in enumerate(POOL_WINDOWS):
                dg = _window_mean_minus(uu[:, gi * gdim:(gi + 1) * gdim], i * ts, window)
                cols.append(_dot(dg.astype(BF16), wgrp_ref[gi]))
            mixed = jnp.concatenate(cols, axis=1) * sc_ref[...]
        carry_ref[...] = u[ts - HALO:]
        o = ((z * _sigmoid(z)) * mixed).astype(BF16)
        o_ref[...] = o
        h1 = x + _dot(o, wo_ref[...])
        h1_ref[...] = h1
        xh1, _ = _rms_stats(h1)
        gl = _dot((xh1 * pn_ref[...]).astype(BF16), wg_ref[...])
        pp = _dot(p_ref[...].astype(BF16), wp_ref[...])
        gl_ref[...] = gl.astype(BF16)
        pp_ref[...] = pp.astype(BF16)
        h2 = h1 + _sigmoid(gl) * pp
        if not head:
            h2_ref[...] = h2
            return
        t_ref, gain_ref = head_refs
        loss_ref, dgain_ref = refs[-3], refs[-2]

        @pl.when(i == 0)
        def _():
            loss_ref[...] = jnp.zeros_like(loss_ref)
            dgain_ref[...] = jnp.zeros_like(dgain_ref)

        gain = gain_ref[...]
        yh, r = _rms_stats(h2)
        err = yh * gain - t_ref[...]
        loss_ref[...] += jnp.full(loss_ref.shape, (0.5 / d) * jnp.sum(err * err), F32)
        dy = err * (1.0 / d)
        dgain_ref[...] += jnp.sum(dy * yh, axis=0, keepdims=True)
        h2_ref[...] = _rms_bwd(dy, yh, r, gain)

    row = lambda width: pl.BlockSpec((ts, width), lambda i: (i, 0))
    mix_specs = [_whole(*a.shape) for a in params]
    head_specs = [row(d), _whole(1, d)] if head else []
    head_out_specs = [_whole(1, 128), _whole(1, d)] if head else []
    head_out_shape = [jax.ShapeDtypeStruct((1, 128), F32), jax.ShapeDtypeStruct((1, d), F32)] if head else []
    outs = pl.pallas_call(
        body, name="layer_fwd",
        grid=(s // ts,),
        in_specs=[row(d), _whole(1, d), _whole(d, n)] + mix_specs
        + [_whole(e, d), _whole(1, d), _whole(d, d),
           pl.BlockSpec((None, ts, pdim), lambda i: (layer, i, 0)), _whole(pdim, d)] + head_specs,
        out_specs=[pl.BlockSpec((nsplit, ts, e), lambda i: (0, i, 0)),
                   row(d), row(e), row(d), row(d), row(d), row(d)] + head_out_specs,
        out_shape=[jax.ShapeDtypeStruct((nsplit, s, e), BF16), jax.ShapeDtypeStruct((s, d), BF16),
                   jax.ShapeDtypeStruct((s, e), BF16), jax.ShapeDtypeStruct((s, d), F32),
                   jax.ShapeDtypeStruct((s, d), F32), jax.ShapeDtypeStruct((s, d), BF16),
                   jax.ShapeDtypeStruct((s, d), BF16)] + head_out_shape,
        scratch_shapes=[pltpu.VMEM((HALO, e), F32)],
        compiler_params=_cparams("arbitrary"),
    )(h, g, w_in, *params, w_out, pn, w_gate, p_all, w_proj, *head)
    proj, hn, o, h1, h2, gl, pp = outs[:7]
    return (h2, *outs[7:]) if head else h2, (proj, hn, o, h1, gl, pp)


def _out_ple_bwd(dh2, gl, pp, h1, p_all, o, pn, wgate, wout, layer):
    s, d = dh2.shape
    e = o.shape[1]
    pdim = p_all.shape[2]
    ts = min(BWD_ROW_TILE, s)
    last = s // ts - 1

    def body(dh2_ref, gl_ref, pp_ref, h1_ref, p_ref, o_ref, pn_ref, wg_ref, wo_ref,
             dh1_ref, do_ref, dwp_ref, dwg_ref, dwo_ref, dpn_ref, awp, awg, awo):
        i = pl.program_id(0)

        @pl.when(i == 0)
        def _():
            awp[...] = jnp.zeros_like(awp)
            awg[...] = jnp.zeros_like(awg)
            awo[...] = jnp.zeros_like(awo)
            dpn_ref[...] = jnp.zeros_like(dpn_ref)

        dh2 = dh2_ref[...]
        gate = _sigmoid(gl_ref[...].astype(F32))
        dpp = (dh2 * gate).astype(BF16)
        dgl = (dh2 * pp_ref[...].astype(F32) * gate * (1.0 - gate)).astype(BF16)
        xh, r = _rms_stats(h1_ref[...])
        pn = pn_ref[...]
        awp[...] += _dot_tn(p_ref[...].astype(BF16), dpp)
        awg[...] += _dot_tn((xh * pn).astype(BF16), dgl)
        dr = _dot_nt(dgl, wg_ref[...])
        dpn_ref[...] += jnp.sum(dr * xh, axis=0, keepdims=True)
        dh1 = dh2 + _rms_bwd(dr, xh, r, pn)
        dh1_ref[...] = dh1
        dh1b = dh1.astype(BF16)
        do_ref[...] = _dot_nt(dh1b, wo_ref[...]).astype(BF16)
        awo[...] += _dot_tn(o_ref[...], dh1b)

        @pl.when(i == last)
        def _():
            dwp_ref[...] = awp[...].astype(BF16)
            dwg_ref[...] = awg[...].astype(BF16)
            dwo_ref[...] = awo[...].astype(BF16)

    row = lambda width: pl.BlockSpec((ts, width), lambda i: (i, 0))
    return pl.pallas_call(
        body, name="out_ple_bwd",
        grid=(s // ts,),
        in_specs=[row(d), row(d), row(d), row(d),
                  pl.BlockSpec((None, ts, pdim), lambda i: (layer, i, 0)),
                  row(e), _whole(1, d), _whole(d, d), _whole(e, d)],
        out_specs=[row(d), row(e), _whole(pdim, d), _whole(d, d), _whole(e, d), _whole(1, d)],
        out_shape=[jax.ShapeDtypeStruct((s, d), F32), jax.ShapeDtypeStruct((s, e), BF16),
                   jax.ShapeDtypeStruct((pdim, d), BF16), jax.ShapeDtypeStruct((d, d), BF16),
                   jax.ShapeDtypeStruct((e, d), BF16), jax.ShapeDtypeStruct((1, d), F32)],
        scratch_shapes=[pltpu.VMEM((pdim, d), F32), pltpu.VMEM((d, d), F32), pltpu.VMEM((e, d), F32)],
        compiler_params=_cparams("arbitrary"),
    )(dh2, gl, pp, h1, p_all, o, pn, wgate, wout)


def _mixer_bwd(do, proj, hn, mixer, w_in, h, g, dh1):
    s, d = h.shape
    nsplit, _, e = proj.shape
    gdim = e // N_POOL_GROUPS
    is_conv = mixer[0] == "conv"
    ts = min(CONV_BWD_ROW_TILE if is_conv else POOL_BWD_ROW_TILE, s)
    nt = s // ts
    params = mixer[1:]
    n_mix_out = 1 if is_conv else 2

    def body(*refs):
        refs = list(refs)
        take = lambda n: [refs.pop(0) for _ in range(n)]
        do_ref, p_ref, ph_ref = take(3)
        mix_refs = take(len(params))
        win_ref, h_ref, g_ref, dh1_ref, hn_ref = take(5)
        dwin_ref, dh_ref, dg_ref = take(3)
        mix_out = take(n_mix_out)
        carry_ref, dp_ref, acc_ref = take(3)
        i = pl.program_id(0)
        tile = nt - 1 - i

        @pl.when(i == 0)
        def _():
            for ref in [carry_ref, dg_ref, acc_ref] + mix_out[-1:] + refs:
                ref[...] = jnp.zeros_like(ref)

        dof = do_ref[...].astype(F32)
        nxt = carry_ref[...]
        if is_conv:
            w_ref, = mix_refs
            dw_ref, = mix_out
            w0, w1, w2 = w_ref[0:1, :], w_ref[1:2, :], w_ref[2:3, :]
            b, c, v, z = [p_ref[k].astype(F32) for k in range(4)]
            u = c * v
            u_prev = jnp.where(tile == 0, 0.0, ph_ref[1].astype(F32) * ph_ref[2].astype(F32))
            u1, u2 = _conv_taps(u, u_prev)
            conv = w0 * u2 + w1 * u1 + w2 * u
            sig = _sigmoid(z)
            sz = z * sig
            dy = dof * sz
            dp_ref[3] = (dof * (b * conv) * (sig + sz * (1.0 - sig))).astype(BF16)
            dp_ref[0] = (dy * conv).astype(BF16)
            dconv = dy * b
            dw_ref[0:1, :] += jnp.sum(dconv * u2, axis=0, keepdims=True)
            dw_ref[1:2, :] += jnp.sum(dconv * u1, axis=0, keepdims=True)
            dw_ref[2:3, :] += jnp.sum(dconv * u, axis=0, keepdims=True)
            dcc = jnp.concatenate([dconv, nxt], axis=0)
            du = w2 * dconv + w1 * _shift_up(dcc, 1)[:ts] + w0 * _shift_up(dcc, 2)[:ts]
            carry_ref[...] = dconv[:HALO]
            dp_ref[1] = (du * v).astype(BF16)
            dp_ref[2] = (du * c).astype(BF16)
        else:
            wgrp_ref, sc_ref = mix_refs
            dsc_ref = mix_out[1]
            agrp, = refs
            u = p_ref[0].astype(F32)
            z = p_ref[1].astype(F32)
            u_prev = jnp.where(tile == 0, 0.0, ph_ref[0].astype(F32))
            uu = jnp.concatenate([u_prev, u], axis=0)
            sig = _sigmoid(z)
            sz = z * sig
            dm = dof * sz
            dsilu = dof * (sig + sz * (1.0 - sig))
            for gi, window in enumerate(POOL_WINDOWS):
                cols = slice(gi * gdim, (gi + 1) * gdim)
                w = wgrp_ref[gi]
                scale = sc_ref[:, cols]
                db = _window_mean_minus(uu[:, cols], tile * ts, window).astype(BF16)
                mr = _dot(db, w)
                dp_ref[1, :, cols] = (dsilu[:, cols] * (mr * scale)).astype(BF16)
                dmg = dm[:, cols]
                dmr = (dmg * scale).astype(BF16)
                agrp[gi] += _dot_tn(db, dmr)
                dsc_ref[:, cols] += jnp.sum(dmg * mr, axis=0, keepdims=True)
                dd = _dot_nt(dmr, w)
                ddq = dd * _inv_count(ts, tile * ts, window)
                acc = jnp.concatenate([ddq, nxt[:, cols]], axis=0)
                span = 1
                while span < window:
                    acc = acc + _shift_up(acc, span)
                    span *= 2
                carry_ref[:, cols] = ddq[:HALO]
                dp_ref[0, :, cols] = (acc[:ts] - dd).astype(BF16)

        dhn = _dot_nt(dp_ref[0], win_ref[:, 0:e])
        for k in range(1, nsplit):
            dhn += _dot_nt(dp_ref[k], win_ref[:, k * e:(k + 1) * e])
        xh, r = _rms_stats(h_ref[...])
        dg_ref[...] += jnp.sum(dhn * xh, axis=0, keepdims=True)
        dh_ref[...] = dh1_ref[...] + _rms_bwd(dhn, xh, r, g_ref[...])
        hn_tile = hn_ref[...]
        for k in range(nsplit):
            acc_ref[:, k * e:(k + 1) * e] += _dot_tn(hn_tile, dp_ref[k])

        @pl.when(i == nt - 1)
        def _():
            dwin_ref[...] = acc_ref[...].astype(BF16)
            if not is_conv:
                mix_out[0][...] = refs[0][...].astype(BF16)

    rev = lambda width: pl.BlockSpec((ts, width), lambda i: (nt - 1 - i, 0))
    halo_blocks = ts // HALO
    in_specs = [rev(e),
                pl.BlockSpec((nsplit, ts, e), lambda i: (0, nt - 1 - i, 0)),
                pl.BlockSpec((nsplit, HALO, e), lambda i: (0, jnp.maximum((nt - 1 - i) * halo_blocks - 1, 0), 0))]
    in_specs += [_whole(*a.shape) for a in params]
    in_specs += [_whole(d, nsplit * e), rev(d), _whole(1, d), rev(d), rev(d)]
    out_specs = [_whole(d, nsplit * e), rev(d), _whole(1, d)]
    out_shape = [jax.ShapeDtypeStruct((d, nsplit * e), BF16), jax.ShapeDtypeStruct((s, d), F32),
                 jax.ShapeDtypeStruct((1, d), F32)]
    scratch = [pltpu.VMEM((HALO, e), F32), pltpu.VMEM((nsplit, ts, e), BF16), pltpu.VMEM((d, nsplit * e), F32)]
    if is_conv:
        out_specs += [_whole(3, e)]
        out_shape += [jax.ShapeDtypeStruct((3, e), F32)]
    else:
        out_specs += [_whole(N_POOL_GROUPS, gdim, gdim), _whole(1, e)]
        out_shape += [jax.ShapeDtypeStruct((N_POOL_GROUPS, gdim, gdim), BF16), jax.ShapeDtypeStruct((1, e), F32)]
        scratch += [pltpu.VMEM((N_POOL_GROUPS, gdim, gdim), F32)]
    return pl.pallas_call(
        body, name="mixer_bwd",
        grid=(nt,),
        in_specs=in_specs, out_specs=out_specs, out_shape=out_shape, scratch_shapes=scratch,
        compiler_params=_cparams("arbitrary"),
    )(do, proj, proj, *params, w_in, h, g, dh1, hn)


def _forward_backward(xs, ps, target, full, conv_w, scale_w, norm_mix, ple_norm, final_norm, exchange):
    depth = len(full)
    row = lambda a, i: a[i][None, :]
    mixer_of = lambda i: ("conv", conv_w[i // 2]) if i % 2 == 0 else ("pool", full[i]["w_grp"], row(scale_w, i // 2))

    saved = []
    h = xs
    for i in range(depth):
        w = full[i]
        head = (target, final_norm[None, :]) if i == depth - 1 else None
        h_next, acts = _layer_fwd(h, row(norm_mix, i), w["w_in"], mixer_of(i), w["w_out"], row(ple_norm, i),
                                  w["gate"], ps, w["proj"], i, head)
        saved.append((h, *acts))
        h = h_next
    dh, loss_row, d_final = h

    d_norm, d_ple_norm, d_conv, d_scale, sent = [None] * depth, [None] * depth, [], [], [None] * depth
    for i in reversed(range(depth)):
        w = full[i]
        h_in, proj, hn, o, h1, gl, pp = saved[i]
        g = {}
        dh1, do, g["proj"], g["gate"], g["w_out"], d_ple_norm[i] = _out_ple_bwd(
            dh, gl, pp, h1, ps, o, row(ple_norm, i), w["gate"], w["w_out"], i)
        g["w_in"], dh, d_norm[i], *mixer_grads = _mixer_bwd(
            do, proj, hn, mixer_of(i), w["w_in"], h_in, row(norm_mix, i), dh1)
        if i % 2 == 0:
            d_conv.insert(0, mixer_grads[0])
        else:
            g["w_grp"] = mixer_grads[0]
            d_scale.insert(0, mixer_grads[1])
        sent[i] = exchange(i, g)
    return loss_row, dh, sent, (d_norm, d_ple_norm, d_final, d_conv, d_scale)


VMEM_SPEC = pl.BlockSpec(memory_space=pltpu.VMEM)

FLIPS = [(fx, fy, fc) for fx in (0, 1) for fy in (0, 1) for fc in (0, 1)][1:]
SHARD_AXIS = {"w_in": 1, "w_out": 0, "w_grp": 1, "gate": 0, "proj": 1}


def _my_place():
    return lax.axis_index("x"), lax.axis_index("y"), lax.axis_index("c")


def _position(place):
    x, y, c = place
    return 4 * x + 2 * y + c


def _flip(place, flips):
    return tuple(1 - v if f else v for v, f in zip(place, flips))


def _shard_of(ref, axis, pos, n):
    idx = [slice(None)] * len(ref.shape)
    idx[axis] = pl.ds(pl.multiple_of(pos * n, n), n)
    return ref.at[tuple(idx)]


def _sequencer_mesh():
    return plsc.ScalarSubcoreMesh(axis_name="sequencer", num_cores=1)


def _handshake(peers):
    barrier = pltpu.get_barrier_semaphore()
    for peer in peers:
        pl.semaphore_signal(barrier, inc=1, device_id=peer, device_id_type=MESH)
    pl.semaphore_wait(barrier, len(peers))


def _all_gather_layer(shards, collective_id):
    names = list(shards)
    nt = len(names)
    axes = [SHARD_AXIS[k] for k in names]
    widths = [shards[k].shape[SHARD_AXIS[k]] for k in names]

    def full_shape(k):
        shp = list(shards[k].shape)
        shp[SHARD_AXIS[k]] *= N_DEV
        return tuple(shp)

    def body(*refs):
        ins, outs = refs[:nt], refs[nt:2 * nt]
        send_sems, recv_sems, local_sem = refs[2 * nt:]
        me = _my_place()
        x, y, c = me
        sibling = (x, y, 1 - c)
        flip = lambda v, f: v + f - 2 * v * f
        neighbour = lambda core, fx: (flip(x, fx), flip(y, 1 - fx), core)
        first, second = neighbour(c, c), neighbour(c, 1 - c)
        diagonal = (1 - x, 1 - y, c)
        _handshake([sibling, first, second])

        def block(t, place):
            return _shard_of(outs[t], axes[t], _position(place), widths[t])

        def copy(t, k, place, to, src=None):
            return pltpu.make_async_remote_copy(
                src_ref=block(t, place) if src is None else src, dst_ref=block(t, place),
                send_sem=send_sems.at[k], recv_sem=recv_sems.at[k], device_id=to, device_id_type=MESH)

        everything = lambda make: [make(t) for t in range(nt)]
        mine = everything(lambda t: pltpu.make_async_copy(ins[t], block(t, me), local_sem))
        sent = (everything(lambda t: copy(t, 1, me, first, src=ins[t]))
                + everything(lambda t: copy(t, 2, me, second, src=ins[t]))
                + everything(lambda t: copy(t, 0, me, sibling, src=ins[t])))
        for cp in mine + sent:
            cp.start()
        for k, arrived, onward in ((1, first, second), (2, second, None), (3, diagonal, None)):
            for t in range(nt):
                copy(t, k, arrived, me).wait_recv()
            if onward is not None:
                sent += everything(lambda t: copy(t, 3, arrived, onward))
            sent += everything(lambda t: copy(t, 3 + k, arrived, sibling))
            for cp in sent[-nt * (1 + (onward is not None)):]:
                cp.start()
        for k, place in ((0, sibling), (4, neighbour(1 - c, 1 - c)), (5, neighbour(1 - c, c)), (6, (1 - x, 1 - y, 1 - c))):
            for t in range(nt):
                copy(t, k, place, me).wait_recv()
        for cp in sent:
            cp.wait_send()
        for cp in mine:
            cp.wait()

    outs = pl.kernel(
        body, name=f"all_gather_layer_{collective_id}",
        out_type=[jax.ShapeDtypeStruct(full_shape(k), shards[k].dtype) for k in names],
        mesh=_sequencer_mesh(),
        scratch_types=[pltpu.SemaphoreType.DMA((7,)), pltpu.SemaphoreType.DMA((7,)), pltpu.SemaphoreType.DMA],
        compiler_params=pltpu.CompilerParams(collective_id=collective_id),
    )(*[shards[k] for k in names])
    return dict(zip(names, outs))


def _exchange_layer(grads, collective_id):
    names = list(grads)
    nt = len(names)
    axes = [SHARD_AXIS[k] for k in names]
    widths = [grads[k].shape[SHARD_AXIS[k]] // N_DEV for k in names]

    def slot_shape(t):
        shp = list(grads[names[t]].shape)
        shp[axes[t]] = widths[t]
        return (N_DEV, *shp)

    def body(*refs):
        ins, outs = refs[:nt], refs[nt:2 * nt]
        send_sems, recv_sems, local_sem = refs[2 * nt:]
        me = _my_place()
        mine = _position(me)
        _handshake([_flip(me, flips) for flips in FLIPS])
        local = [pltpu.make_async_copy(_shard_of(ins[t], axes[t], mine, widths[t]), outs[t].at[mine], local_sem)
                 for t in range(nt)]
        for cp in local:
            cp.start()
        copies = []
        for k, flips in enumerate(FLIPS):
            peer = _flip(me, flips)
            for t in range(nt):
                cp = pltpu.make_async_remote_copy(
                    src_ref=_shard_of(ins[t], axes[t], _position(peer), widths[t]), dst_ref=outs[t].at[mine],
                    send_sem=send_sems.at[k], recv_sem=recv_sems.at[k], device_id=peer, device_id_type=MESH)
                cp.start()
                copies.append(cp)
        for cp in copies:
            cp.wait()
        for cp in local:
            cp.wait()

    outs = pl.kernel(
        body, name=f"exchange_layer_{collective_id}",
        out_type=[jax.ShapeDtypeStruct(slot_shape(t), BF16) for t in range(nt)],
        mesh=_sequencer_mesh(),
        scratch_types=[pltpu.SemaphoreType.DMA((7,)), pltpu.SemaphoreType.DMA((7,)), pltpu.SemaphoreType.DMA],
        compiler_params=pltpu.CompilerParams(collective_id=collective_id),
    )(*[grads[k] for k in names])
    return dict(zip(names, outs))


def _gather_rows(buf, reduce):
    r, c = buf.shape

    def body(in_ref, out_ref, *scratch):
        if reduce:
            all_ref, send_sems, recv_sems = scratch
        else:
            all_ref = out_ref
            send_sems, recv_sems = scratch
        me = _my_place()
        all_ref[_position(me)] = in_ref[...]
        copies = []
        for k, flips in enumerate(FLIPS):
            cp = pltpu.make_async_remote_copy(
                src_ref=in_ref, dst_ref=all_ref.at[_position(me)],
                send_sem=send_sems.at[k], recv_sem=recv_sems.at[k], device_id=_flip(me, flips), device_id_type=MESH)
            cp.start()
            copies.append(cp)
        for cp in copies:
            cp.wait()
        if reduce:
            total = all_ref[0]
            for j in range(1, N_DEV):
                total = total + all_ref[j]
            out_ref[...] = total

    return pl.pallas_call(
        body, name="sum_rows" if reduce else "gather_rows",
        in_specs=[VMEM_SPEC], out_specs=VMEM_SPEC,
        out_shape=jax.ShapeDtypeStruct((r, c) if reduce else (N_DEV, r, c), F32),
        scratch_shapes=([pltpu.VMEM((N_DEV, r, c), F32)] if reduce else [])
        + [pltpu.SemaphoreType.DMA((7,)), pltpu.SemaphoreType.DMA((7,))],
    )(buf)


def _adamw_math(w, g, m, v):
    m = ADAM_B1 * m + (1.0 - ADAM_B1) * g
    v = ADAM_B2 * v + (1.0 - ADAM_B2) * (g * g)
    m_hat = m / (1.0 - ADAM_B1 ** ADAM_STEP)
    v_hat = v / (1.0 - ADAM_B2 ** ADAM_STEP)
    delta = -ADAM_LR * (m_hat / (jnp.sqrt(v_hat) + ADAM_EPS) + ADAM_WD * w)
    return delta, m, v


def _run_behind(x, token):
    def body(x_ref, token_ref, out_ref):
        out_ref[...] = jnp.zeros_like(out_ref)

    any_spec = pl.BlockSpec(memory_space=pl.ANY)
    return pl.pallas_call(
        body, name="run_behind",
        in_specs=[any_spec, any_spec], out_specs=VMEM_SPEC,
        out_shape=jax.ShapeDtypeStruct((8, 128), F32),
    )(x, token)


def _adamw_pieces(pieces, w, m, v, after, layer=None, into=()):
    shape = w.shape
    nl = shape[0]
    cols = shape[-1]
    rows = w.size // (nl * cols)
    first, nh = (0, nl) if layer is None else (layer, 1)
    tr = min(ADAMW_BLOCK_ROWS // nh, rows // 2)
    flat3 = lambda a: a.reshape(nl, rows, cols)
    into = [flat3(a) for a in into]

    def body(*refs):
        p_refs = refs[:nh]
        w_ref, m_ref, v_ref = refs[nh:nh + 3]
        g_ref, d_ref, nm_ref, nv_ref = refs[-4:]
        for l in range(nh):
            g = p_refs[l][0].astype(F32)
            for j in range(1, N_DEV):
                g = g + p_refs[l][j].astype(F32)
            g_ref[l] = g
            d_ref[l], nm_ref[l], nv_ref[l] = _adamw_math(w_ref[l], g, m_ref[l], v_ref[l])

    blk = pl.BlockSpec((nh, tr, cols), lambda i: (first, i, 0))
    any_spec = pl.BlockSpec(memory_space=pl.ANY)
    outs = pl.pallas_call(
        body, name="adamw_pieces",
        grid=(rows // tr,),
        in_specs=[pl.BlockSpec((N_DEV, tr, cols), lambda i: (0, i, 0))] * nh
        + [blk, blk, blk, any_spec] + [any_spec] * len(into),
        out_specs=[blk] * 4,
        out_shape=[jax.ShapeDtypeStruct((nl, rows, cols), F32)] * 4,
        input_output_aliases={nh + 4 + k: k for k in range(len(into))},
        compiler_params=_cparams("parallel"),
    )(*[a.reshape(N_DEV, rows, cols) for a in pieces], flat3(w), flat3(m), flat3(v), after, *into)
    return [a.reshape(shape) for a in outs]


def _adamw_small(g, w, m, v):
    shape = w.shape
    two = lambda a: a.reshape(-1, shape[-1])

    def body(g_ref, w_ref, m_ref, v_ref, d_ref, nm_ref, nv_ref):
        d_ref[...], nm_ref[...], nv_ref[...] = _adamw_math(w_ref[...], g_ref[...], m_ref[...], v_ref[...])

    outs = pl.pallas_call(
        body, name="adamw_small",
        in_specs=[VMEM_SPEC] * 4, out_specs=[VMEM_SPEC] * 3,
        out_shape=[jax.ShapeDtypeStruct(two(w).shape, F32)] * 3,
    )(two(g), two(w), two(m), two(v))
    return [a.reshape(shape) for a in outs]


WEIGHTS = ("norm_mix", "a_w_in", "a_w_conv", "a_w_out", "b_w_in", "b_w_grp", "b_scale", "b_w_out",
           "ple_norm", "ple_w_gate", "ple_w_proj", "final_norm")
SMALL_ROWS = 24
GATHER_ID = 0
EXCHANGE_ID = 4
LAST_EXCHANGE_ID = 8


def kernel(x, p, norm_mix, a_w_in, a_w_conv, a_w_out, b_w_in, b_w_grp, b_scale, b_w_out, ple_norm, ple_w_gate, ple_w_proj, final_norm, loss_target, m_norm_mix, m_a_w_in, m_a_w_conv, m_a_w_out, m_b_w_in, m_b_w_grp, m_b_scale, m_b_w_out, m_ple_norm, m_ple_w_gate, m_ple_w_proj, m_final_norm, v_norm_mix, v_a_w_in, v_a_w_conv, v_a_w_out, v_b_w_in, v_b_w_grp, v_b_scale, v_b_w_out, v_ple_norm, v_ple_w_gate, v_ple_w_proj, v_final_norm):
    wts = dict(norm_mix=norm_mix, a_w_in=a_w_in, a_w_conv=a_w_conv, a_w_out=a_w_out, b_w_in=b_w_in, b_w_grp=b_w_grp,
               b_scale=b_scale, b_w_out=b_w_out, ple_norm=ple_norm, ple_w_gate=ple_w_gate, ple_w_proj=ple_w_proj,
               final_norm=final_norm)
    mom = dict(norm_mix=m_norm_mix, a_w_in=m_a_w_in, a_w_conv=m_a_w_conv, a_w_out=m_a_w_out, b_w_in=m_b_w_in,
               b_w_grp=m_b_w_grp, b_scale=m_b_scale, b_w_out=m_b_w_out, ple_norm=m_ple_norm, ple_w_gate=m_ple_w_gate,
               ple_w_proj=m_ple_w_proj, final_norm=m_final_norm)
    var = dict(norm_mix=v_norm_mix, a_w_in=v_a_w_in, a_w_conv=v_a_w_conv, a_w_out=v_a_w_out, b_w_in=v_b_w_in,
               b_w_grp=v_b_w_grp, b_scale=v_b_scale, b_w_out=v_b_w_out, ple_norm=v_ple_norm, ple_w_gate=v_ple_w_gate,
               ple_w_proj=v_ple_w_proj, final_norm=v_final_norm)
    d = x.shape[2]
    depth = norm_mix.shape[0]
    n_a, n_b = a_w_conv.shape[0], b_scale.shape[0]
    cw = a_w_conv.shape[2]
    pos = _position(_my_place())

    def layer_matrices(i):
        j = i // 2
        mixer = {"w_in": ("a_w_in", j), "w_out": ("a_w_out", j)} if i % 2 == 0 else \
                {"w_in": ("b_w_in", j), "w_grp": ("b_w_grp", j), "w_out": ("b_w_out", j)}
        return {**mixer, "gate": ("ple_w_gate", i), "proj": ("ple_w_proj", i)}

    def bf16_shards(i):
        mine = {k: wts[name][idx] for k, (name, idx) in layer_matrices(i).items()}
        if i == 0:
            mine = lax.optimization_barrier(mine)
        return {k: a.astype(BF16) for k, a in mine.items()}

    full = [_all_gather_layer(bf16_shards(i), GATHER_ID + i) for i in range(depth)]
    vec_rows = jnp.concatenate([a_w_conv.reshape(-1, cw), b_scale], axis=0)
    vecs = _gather_rows(vec_rows, reduce=False)
    n_conv = 3 * n_a
    conv_w = vecs[:, :n_conv].transpose(1, 0, 2).reshape(n_a, 3, N_DEV * cw)
    scale_w = vecs[:, n_conv:].transpose(1, 0, 2).reshape(n_b, N_DEV * cw)

    def exchange(i, g):
        if i > 0:
            return _exchange_layer(g, EXCHANGE_ID + i)
        early = {k: a for k, a in g.items() if k != "w_in"}
        return {**_exchange_layer(early, EXCHANGE_ID), **_exchange_layer({"w_in": g["w_in"]}, LAST_EXCHANGE_ID)}

    loss_row, dx, sent, (d_norm, d_ple_norm, d_final, d_conv, d_scale) = _forward_backward(
        x[0], p[:, 0], loss_target[0], full, conv_w, scale_w, norm_mix, ple_norm, final_norm, exchange)
    pieces = {name: [None] * wts[name].shape[0] for name in WEIGHTS if wts[name].ndim >= 3 and name != "a_w_conv"}
    for i in range(depth):
        for k, (name, idx) in layer_matrices(i).items():
            pieces[name][idx] = sent[i][k]

    pad = lambda a: jnp.pad(a, ((0, 0), (0, d - a.shape[1])))
    small = jnp.concatenate(d_norm + d_ple_norm + [d_final] + d_conv + d_scale + [pad(loss_row)], axis=0)
    small = jnp.pad(small, ((0, SMALL_ROWS - small.shape[0]), (0, 0)))
    total = _gather_rows(small, reduce=True)
    o = 0
    gsum = {}
    gsum["norm_mix"] = total[o:o + depth]; o += depth
    gsum["ple_norm"] = total[o:o + depth]; o += depth
    gsum["final_norm"] = total[o]; o += 1
    conv_full = total[o:o + n_conv].reshape(n_a, 3, d); o += n_conv
    scale_full = total[o:o + n_b]; o += n_b
    loss = total[o, 0]
    gsum["a_w_conv"] = lax.dynamic_slice_in_dim(conv_full, pos * cw, cw, axis=2)
    gsum["b_scale"] = lax.dynamic_slice_in_dim(scale_full, pos * cw, cw, axis=1)

    token = total
    for i in reversed(range(depth)):
        token = _run_behind(sent[i]["w_out"], token)
    last_token = _run_behind(sent[0]["w_in"], token)
    grad, delta, new_m, new_v = {}, {}, {}, {}
    for k in sorted(WEIGHTS, key=lambda name: name == "a_w_in"):
        if k == "a_w_in":
            upper = [None] * 4
            for j in reversed(range(1, n_a)):
                upper = _adamw_pieces([pieces[k][j]], wts[k], mom[k], var[k], token, j, [a for a in upper if a is not None])
            grad[k], delta[k], new_m[k], new_v[k] = _adamw_pieces(
                [pieces[k][0]], wts[k], mom[k], var[k], last_token, 0, upper)
        elif k in pieces:
            grad[k], delta[k], new_m[k], new_v[k] = _adamw_pieces(pieces[k], wts[k], mom[k], var[k], token)
        else:
            grad[k] = gsum[k]
            delta[k], new_m[k], new_v[k] = _adamw_small(gsum[k], wts[k], mom[k], var[k])
    return (loss, dx[None], *[grad[k] for k in WEIGHTS], *[delta[k] for k in WEIGHTS],
            *[new_m[k] for k in WEIGHTS], *[new_v[k] for k in WEIGHTS])
```

```python
import jax
import jax.numpy as jnp
from jax import lax
from jax.experimental import pallas as pl
from jax.experimental.pallas import tpu as pltpu
from jax.experimental.pallas import tpu_sc as plsc

F32 = jnp.float32
BF16 = jnp.bfloat16
MESH = pl.DeviceIdType.MESH

RMS_EPS = 1e-6
POOL_WINDOWS = (2, 4, 8, 16)
N_POOL_GROUPS = len(POOL_WINDOWS)
ADAM_LR = 0.001
ADAM_B1 = 0.9
ADAM_B2 = 0.999
ADAM_EPS = 1e-08
ADAM_WD = 0.01
ADAM_STEP = 10
N_DEV = 8

HALO = 16
FWD_ROW_TILE = 512
POOL_BWD_ROW_TILE = 512
CONV_BWD_ROW_TILE = 256
ADAMW_BLOCK_ROWS = 512
BWD_ROW_TILE = 512
VMEM_LIMIT = 56 * 1024 * 1024


def _cparams(*sem):
    return pltpu.CompilerParams(dimension_semantics=sem, vmem_limit_bytes=VMEM_LIMIT)


def _dot(a, b):
    return jnp.dot(a, b, preferred_element_type=F32)


def _dot_nt(a, b):
    return lax.dot_general(a, b, (((1,), (1,)), ((), ())), preferred_element_type=F32)


def _dot_tn(a, b):
    return lax.dot_general(a, b, (((0,), (0,)), ((), ())), preferred_element_type=F32)


def _rms_stats(x):
    r = lax.rsqrt(jnp.mean(x * x, axis=-1, keepdims=True) + RMS_EPS)
    return x * r, r


def _rms_bwd(dy, xh, r, g):
    a = dy * g
    return r * (a - xh * jnp.mean(a * xh, axis=-1, keepdims=True))


def _sigmoid(x):
    return 0.5 * jnp.tanh(0.5 * x) + 0.5


def _shift_down(x, k):
    return pltpu.roll(x, k, 0)


def _shift_up(x, k):
    return pltpu.roll(x, x.shape[0] - k, 0)


def _conv_taps(u, u_prev):
    uu = jnp.concatenate([u_prev, u], axis=0)
    return _shift_down(uu, 1)[HALO:], _shift_down(uu, 2)[HALO:]


def _window_mean_minus(uu, row0, window):
    acc = uu
    span = 1
    while span < window:
        acc = acc + _shift_down(acc, span)
        span *= 2
    return acc[HALO:] * _inv_count(uu.shape[0] - HALO, row0, window) - uu[HALO:]


def _inv_count(rows, row0, window):
    t = row0 + lax.broadcasted_iota(jnp.int32, (rows, 1), 0)
    return 1.0 / jnp.minimum(t + 1, window).astype(F32)


def _whole(*shape):
    return pl.BlockSpec(shape, lambda i: (0,) * len(shape), pipeline_mode=pl.Buffered(1))


def _layer_fwd(h, g, w_in, mixer, w_out, pn, w_gate, p_all, w_proj, layer, head=None):
    s, d = h.shape
    n = w_in.shape[1]
    e = w_out.shape[0]
    nsplit = n // e
    gdim = e // N_POOL_GROUPS
    pdim = p_all.shape[2]
    is_conv = mixer[0] == "conv"
    ts = min(FWD_ROW_TILE, s)
    params = mixer[1:]
    head = tuple(head or ())

    def body(*refs):
        h_ref, g_ref, win_ref = refs[:3]
        mix_refs = refs[3:3 + len(params)]
        wo_ref, pn_ref, wg_ref, p_ref, wp_ref = refs[3 + len(params):8 + len(params)]
        head_refs = refs[8 + len(params):8 + len(params) + len(head)]
        proj_ref, hn_ref, o_ref, h1_ref, h2_ref, gl_ref, pp_ref = refs[8 + len(params) + len(head):][:7]
        carry_ref = refs[-1]
        i = pl.program_id(0)

        @pl.when(i == 0)
        def _():
            carry_ref[...] = jnp.zeros_like(carry_ref)

        x = h_ref[...]
        xh, _ = _rms_stats(x)
        hn = (xh * g_ref[...]).astype(BF16)
        hn_ref[...] = hn
        parts = []
        for k in range(nsplit):
            part = _dot(hn, win_ref[:, k * e:(k + 1) * e])
            proj_ref[k] = part.astype(BF16)
            parts.append(part)
        prev = carry_ref[...]
        if is_conv:
            b, c, v, z = parts
            w_ref, = mix_refs
            u = c * v
            u1, u2 = _conv_taps(u, prev)
            mixed = b * (w_ref[0:1, :] * u2 + w_ref[1:2, :] * u1 + w_ref[2:3, :] * u)
        else:
            u, z = parts
            wgrp_ref, sc_ref = mix_refs
            uu = jnp.concatenate([prev, u], axis=0)
            cols = []
            for gi, window in enumerate(POOL_WINDOWS):
                dg = _window_mean_minus(uu[:, gi * gdim:(gi + 1) * gdim], i * ts, window)
                cols.append(_dot(dg.astype(BF16), wgrp_ref[gi]))
            mixed = jnp.concatenate(cols, axis=1) * sc_ref[...]
        carry_ref[...] = u[ts - HALO:]
        o = ((z * _sigmoid(z)) * mixed).astype(BF16)
        o_ref[...] = o
        h1 = x + _dot(o, wo_ref[...])
        h1_ref[...] = h1
        xh1, _ = _rms_stats(h1)
        gl = _dot((xh1 * pn_ref[...]).astype(BF16), wg_ref[...])
        pp = _dot(p_ref[...].astype(BF16), wp_ref[...])
        gl_ref[...] = gl.astype(BF16)
        pp_ref[...] = pp.astype(BF16)
        h2 = h1 + _sigmoid(gl) * pp
        if not head:
            h2_ref[...] = h2
            return
        t_ref, gain_ref = head_refs
        loss_ref, dgain_ref = refs[-3], refs[-2]

        @pl.when(i == 0)
        def _():
            loss_ref[...] = jnp.zeros_like(loss_ref)
            dgain_ref[...] = jnp.zeros_like(dgain_ref)

        gain = gain_ref[...]
        yh, r = _rms_stats(h2)
        err = yh * gain - t_ref[...]
        loss_ref[...] += jnp.full(loss_ref.shape, (0.5 / d) * jnp.sum(err * err), F32)
        dy = err * (1.0 / d)
        dgain_ref[...] += jnp.sum(dy * yh, axis=0, keepdims=True)
        h2_ref[...] = _rms_bwd(dy, yh, r, gain)

    row = lambda width: pl.BlockSpec((ts, width), lambda i: (i, 0))
    mix_specs = [_whole(*a.shape) for a in params]
    head_specs = [row(d), _whole(1, d)] if head else []
    head_out_specs = [_whole(1, 128), _whole(1, d)] if head else []
    head_out_shape = [jax.ShapeDtypeStruct((1, 128), F32), jax.ShapeDtypeStruct((1, d), F32)] if head else []
    outs = pl.pallas_call(
        body, name="layer_fwd",
        grid=(s // ts,),
        in_specs=[row(d), _whole(1, d), _whole(d, n)] + mix_specs
        + [_whole(e, d), _whole(1, d), _whole(d, d),
           pl.BlockSpec((None, ts, pdim), lambda i: (layer, i, 0)), _whole(pdim, d)] + head_specs,
        out_specs=[pl.BlockSpec((nsplit, ts, e), lambda i: (0, i, 0)),
                   row(d), row(e), row(d), row(d), row(d), row(d)] + head_out_specs,
        out_shape=[jax.ShapeDtypeStruct((nsplit, s, e), BF16), jax.ShapeDtypeStruct((s, d), BF16),
                   jax.ShapeDtypeStruct((s, e), BF16), jax.ShapeDtypeStruct((s, d), F32),
                   jax.ShapeDtypeStruct((s, d), F32), jax.ShapeDtypeStruct((s, d), BF16),
                   jax.ShapeDtypeStruct((s, d), BF16)] + head_out_shape,
        scratch_shapes=[pltpu.VMEM((HALO, e), F32)],
        compiler_params=_cparams("arbitrary"),
    )(h, g, w_in, *params, w_out, pn, w_gate, p_all, w_proj, *head)
    proj, hn, o, h1, h2, gl, pp = outs[:7]
    return (h2, *outs[7:]) if head else h2, (proj, hn, o, h1, gl, pp)


def _out_ple_bwd(dh2, gl, pp, h1, p_all, o, pn, wgate, wout, layer):
    s, d = dh2.shape
    e = o.shape[1]
    pdim = p_all.shape[2]
    ts = min(BWD_ROW_TILE, s)
    last = s // ts - 1

    def body(dh2_ref, gl_ref, pp_ref, h1_ref, p_ref, o_ref, pn_ref, wg_ref, wo_ref,
             dh1_ref, do_ref, dwp_ref, dwg_ref, dwo_ref, dpn_ref, awp, awg, awo):
        i = pl.program_id(0)

        @pl.when(i == 0)
        def _():
            awp[...] = jnp.zeros_like(awp)
            awg[...] = jnp.zeros_like(awg)
            awo[...] = jnp.zeros_like(awo)
            dpn_ref[...] = jnp.zeros_like(dpn_ref)

        dh2 = dh2_ref[...]
        gate = _sigmoid(gl_ref[...].astype(F32))
        dpp = (dh2 * gate).astype(BF16)
        dgl = (dh2 * pp_ref[...].astype(F32) * gate * (1.0 - gate)).astype(BF16)
        xh, r = _rms_stats(h1_ref[...])
        pn = pn_ref[...]
        awp[...] += _dot_tn(p_ref[...].astype(BF16), dpp)
        awg[...] += _dot_tn((xh * pn).astype(BF16), dgl)
        dr = _dot_nt(dgl, wg_ref[...])
        dpn_ref[...] += jnp.sum(dr * xh, axis=0, keepdims=True)
        dh1 = dh2 + _rms_bwd(dr, xh, r, pn)
        dh1_ref[...] = dh1
        dh1b = dh1.astype(BF16)
        do_ref[...] = _dot_nt(dh1b, wo_ref[...]).astype(BF16)
        awo[...] += _dot_tn(o_ref[...], dh1b)

        @pl.when(i == last)
        def _():
            dwp_ref[...] = awp[...].astype(BF16)
            dwg_ref[...] = awg[...].astype(BF16)
            dwo_ref[...] = awo[...].astype(BF16)

    row = lambda width: pl.BlockSpec((ts, width), lambda i: (i, 0))
    return pl.pallas_call(
        body, name="out_ple_bwd",
        grid=(s // ts,),
        in_specs=[row(d), row(d), row(d), row(d),
                  pl.BlockSpec((None, ts, pdim), lambda i: (layer, i, 0)),
                  row(e), _whole(1, d), _whole(d, d), _whole(e, d)],
        out_specs=[row(d), row(e), _whole(pdim, d), _whole(d, d), _whole(e, d), _whole(1, d)],
        out_shape=[jax.ShapeDtypeStruct((s, d), F32), jax.ShapeDtypeStruct((s, e), BF16),
                   jax.ShapeDtypeStruct((pdim, d), BF16), jax.ShapeDtypeStruct((d, d), BF16),
                   jax.ShapeDtypeStruct((e, d), BF16), jax.ShapeDtypeStruct((1, d), F32)],
        scratch_shapes=[pltpu.VMEM((pdim, d), F32), pltpu.VMEM((d, d), F32), pltpu.VMEM((e, d), F32)],
        compiler_params=_cparams("arbitrary"),
    )(dh2, gl, pp, h1, p_all, o, pn, wgate, wout)


def _mixer_bwd(do, proj, hn, mixer, w_in, h, g, dh1):
    s, d = h.shape
    nsplit, _, e = proj.shape
    gdim = e // N_POOL_GROUPS
    is_conv = mixer[0] == "conv"
    ts = min(CONV_BWD_ROW_TILE if is_conv else POOL_BWD_ROW_TILE, s)
    nt = s // ts
    params = mixer[1:]
    n_mix_out = 1 if is_conv else 2

    def body(*refs):
        refs = list(refs)
        take = lambda n: [refs.pop(0) for _ in range(n)]
        do_ref, p_ref, ph_ref = take(3)
        mix_refs = take(len(params))
        win_ref, h_ref, g_ref, dh1_ref, hn_ref = take(5)
        dwin_ref, dh_ref, dg_ref = take(3)
        mix_out = take(n_mix_out)
        carry_ref, dp_ref, acc_ref = take(3)
        i = pl.program_id(0)
        tile = nt - 1 - i

        @pl.when(i == 0)
        def _():
            for ref in [carry_ref, dg_ref, acc_ref] + mix_out[-1:] + refs:
                ref[...] = jnp.zeros_like(ref)

        dof = do_ref[...].astype(F32)
        nxt = carry_ref[...]
        if is_conv:
            w_ref, = mix_refs
            dw_ref, = mix_out
            w0, w1, w2 = w_ref[0:1, :], w_ref[1:2, :], w_ref[2:3, :]
            b, c, v, z = [p_ref[k].astype(F32) for k in range(4)]
            u = c * v
            u_prev = jnp.where(tile == 0, 0.0, ph_ref[1].astype(F32) * ph_ref[2].astype(F32))
            u1, u2 = _conv_taps(u, u_prev)
            conv = w0 * u2 + w1 * u1 + w2 * u
            sig = _sigmoid(z)
            sz = z * sig
            dy = dof * sz
            dp_ref[3] = (dof * (b * conv) * (sig + sz * (1.0 - sig))).astype(BF16)
            dp_ref[0] = (dy * conv).astype(BF16)
            dconv = dy * b
            dw_ref[0:1, :] += jnp.sum(dconv * u2, axis=0, keepdims=True)
            dw_ref[1:2, :] += jnp.sum(dconv * u1, axis=0, keepdims=True)
            dw_ref[2:3, :] += jnp.sum(dconv * u, axis=0, keepdims=True)
            dcc = jnp.concatenate([dconv, nxt], axis=0)
            du = w2 * dconv + w1 * _shift_up(dcc, 1)[:ts] + w0 * _shift_up(dcc, 2)[:ts]
            carry_ref[...] = dconv[:HALO]
            dp_ref[1] = (du * v).astype(BF16)
            dp_ref[2] = (du * c).astype(BF16)
        else:
            wgrp_ref, sc_ref = mix_refs
            dsc_ref = mix_out[1]
            agrp, = refs
            u = p_ref[0].astype(F32)
            z = p_ref[1].astype(F32)
            u_prev = jnp.where(tile == 0, 0.0, ph_ref[0].astype(F32))
            uu = jnp.concatenate([u_prev, u], axis=0)
            sig = _sigmoid(z)
            sz = z * sig
            dm = dof * sz
            dsilu = dof * (sig + sz * (1.0 - sig))
            for gi, window in enumerate(POOL_WINDOWS):
                cols = slice(gi * gdim, (gi + 1) * gdim)
                w = wgrp_ref[gi]
                scale = sc_ref[:, cols]
                db = _window_mean_minus(uu[:, cols], tile * ts, window).astype(BF16)
                mr = _dot(db, w)
                dp_ref[1, :, cols] = (dsilu[:, cols] * (mr * scale)).astype(BF16)
                dmg = dm[:, cols]
                dmr = (dmg * scale).astype(BF16)
                agrp[gi] += _dot_tn(db, dmr)
                dsc_ref[:, cols] += jnp.sum(dmg * mr, axis=0, keepdims=True)
                dd = _dot_nt(dmr, w)
                ddq = dd * _inv_count(ts, tile * ts, window)
                acc = jnp.concatenate([ddq, nxt[:, cols]], axis=0)
                span = 1
                while span < window:
                    acc = acc + _shift_up(acc, span)
                    span *= 2
                carry_ref[:, cols] = ddq[:HALO]
                dp_ref[0, :, cols] = (acc[:ts] - dd).astype(BF16)

        dhn = _dot_nt(dp_ref[0], win_ref[:, 0:e])
        for k in range(1, nsplit):
            dhn += _dot_nt(dp_ref[k], win_ref[:, k * e:(k + 1) * e])
        xh, r = _rms_stats(h_ref[...])
        dg_ref[...] += jnp.sum(dhn * xh, axis=0, keepdims=True)
        dh_ref[...] = dh1_ref[...] + _rms_bwd(dhn, xh, r, g_ref[...])
        hn_tile = hn_ref[...]
        for k in range(nsplit):
            acc_ref[:, k * e:(k + 1) * e] += _dot_tn(hn_tile, dp_ref[k])

        @pl.when(i == nt - 1)
        def _():
            dwin_ref[...] = acc_ref[...].astype(BF16)
            if not is_conv:
                mix_out[0][...] = refs[0][...].astype(BF16)

    rev = lambda width: pl.BlockSpec((ts, width), lambda i: (nt - 1 - i, 0))
    halo_blocks = ts // HALO
    in_specs = [rev(e),
                pl.BlockSpec((nsplit, ts, e), lambda i: (0, nt - 1 - i, 0)),
                pl.BlockSpec((nsplit, HALO, e), lambda i: (0, jnp.maximum((nt - 1 - i) * halo_blocks - 1, 0), 0))]
    in_specs += [_whole(*a.shape) for a in params]
    in_specs += [_whole(d, nsplit * e), rev(d), _whole(1, d), rev(d), rev(d)]
    out_specs = [_whole(d, nsplit * e), rev(d), _whole(1, d)]
    out_shape = [jax.ShapeDtypeStruct((d, nsplit * e), BF16), jax.ShapeDtypeStruct((s, d), F32),
                 jax.ShapeDtypeStruct((1, d), F32)]
    scratch = [pltpu.VMEM((HALO, e), F32), pltpu.VMEM((nsplit, ts, e), BF16), pltpu.VMEM((d, nsplit * e), F32)]
    if is_conv:
        out_specs += [_whole(3, e)]
        out_shape += [jax.ShapeDtypeStruct((3, e), F32)]
    else:
        out_specs += [_whole(N_POOL_GROUPS, gdim, gdim), _whole(1, e)]
        out_shape += [jax.ShapeDtypeStruct((N_POOL_GROUPS, gdim, gdim), BF16), jax.ShapeDtypeStruct((1, e), F32)]
        scratch += [pltpu.VMEM((N_POOL_GROUPS, gdim, gdim), F32)]
    return pl.pallas_call(
        body, name="mixer_bwd",
        grid=(nt,),
        in_specs=in_specs, out_specs=out_specs, out_shape=out_shape, scratch_shapes=scratch,
        compiler_params=_cparams("arbitrary"),
    )(do, proj, proj, *params, w_in, h, g, dh1, hn)


def _forward_backward(xs, ps, target, full, conv_w, scale_w, norm_mix, ple_norm, final_norm, exchange):
    depth = len(full)
    row = lambda a, i: a[i][None, :]
    mixer_of = lambda i: ("conv", conv_w[i // 2]) if i % 2 == 0 else ("pool", full[i]["w_grp"], row(scale_w, i // 2))

    saved = []
    h = xs
    for i in range(depth):
        w = full[i]
        head = (target, final_norm[None, :]) if i == depth - 1 else None
        h_next, acts = _layer_fwd(h, row(norm_mix, i), w["w_in"], mixer_of(i), w["w_out"], row(ple_norm, i),
                                  w["gate"], ps, w["proj"], i, head)
        saved.append((h, *acts))
        h = h_next
    dh, loss_row, d_final = h

    d_norm, d_ple_norm, d_conv, d_scale, sent = [None] * depth, [None] * depth, [], [], [None] * depth
    for i in reversed(range(depth)):
        w = full[i]
        h_in, proj, hn, o, h1, gl, pp = saved[i]
        g = {}
        dh1, do, g["proj"], g["gate"], g["w_out"], d_ple_norm[i] = _out_ple_bwd(
            dh, gl, pp, h1, ps, o, row(ple_norm, i), w["gate"], w["w_out"], i)
        g["w_in"], dh, d_norm[i], *mixer_grads = _mixer_bwd(
            do, proj, hn, mixer_of(i), w["w_in"], h_in, row(norm_mix, i), dh1)
        if i % 2 == 0:
            d_conv.insert(0, mixer_grads[0])
        else:
            g["w_grp"] = mixer_grads[0]
            d_scale.insert(0, mixer_grads[1])
        sent[i] = exchange(i, g)
    return loss_row, dh, sent, (d_norm, d_ple_norm, d_final, d_conv, d_scale)


VMEM_SPEC = pl.BlockSpec(memory_space=pltpu.VMEM)

FLIPS = [(fx, fy, fc) for fx in (0, 1) for fy in (0, 1) for fc in (0, 1)][1:]
SHARD_AXIS = {"w_in": 1, "w_out": 0, "w_grp": 1, "gate": 0, "proj": 1}


def _my_place():
    return lax.axis_index("x"), lax.axis_index("y"), lax.axis_index("c")


def _position(place):
    x, y, c = place
    return 4 * x + 2 * y + c


def _flip(place, flips):
    return tuple(1 - v if f else v for v, f in zip(place, flips))


def _shard_of(ref, axis, pos, n):
    idx = [slice(None)] * len(ref.shape)
    idx[axis] = pl.ds(pl.multiple_of(pos * n, n), n)
    return ref.at[tuple(idx)]


def _sequencer_mesh():
    return plsc.ScalarSubcoreMesh(axis_name="sequencer", num_cores=1)


def _handshake(peers):
    barrier = pltpu.get_barrier_semaphore()
    for peer in peers:
        pl.semaphore_signal(barrier, inc=1, device_id=peer, device_id_type=MESH)
    pl.semaphore_wait(barrier, len(peers))


def _all_gather_layer(shards, collective_id):
    names = list(shards)
    nt = len(names)
    axes = [SHARD_AXIS[k] for k in names]
    widths = [shards[k].shape[SHARD_AXIS[k]] for k in names]

    def full_shape(k):
        shp = list(shards[k].shape)
        shp[SHARD_AXIS[k]] *= N_DEV
        return tuple(shp)

    def body(*refs):
        ins, outs = refs[:nt], refs[nt:2 * nt]
        send_sems, recv_sems, local_sem = refs[2 * nt:]
        me = _my_place()
        x, y, c = me
        sibling = (x, y, 1 - c)
        flip = lambda v, f: v + f - 2 * v * f
        neighbour = lambda core, fx: (flip(x, fx), flip(y, 1 - fx), core)
        first, second = neighbour(c, c), neighbour(c, 1 - c)
        diagonal = (1 - x, 1 - y, c)
        _handshake([sibling, first, second])

        def block(t, place):
            return _shard_of(outs[t], axes[t], _position(place), widths[t])

        def copy(t, k, place, to, src=None):
            return pltpu.make_async_remote_copy(
                src_ref=block(t, place) if src is None else src, dst_ref=block(t, place),
                send_sem=send_sems.at[k], recv_sem=recv_sems.at[k], device_id=to, device_id_type=MESH)

        everything = lambda make: [make(t) for t in range(nt)]
        mine = everything(lambda t: pltpu.make_async_copy(ins[t], block(t, me), local_sem))
        sent = (everything(lambda t: copy(t, 1, me, first, src=ins[t]))
                + everything(lambda t: copy(t, 2, me, second, src=ins[t]))
                + everything(lambda t: copy(t, 0, me, sibling, src=ins[t])))
        for cp in mine + sent:
            cp.start()
        for k, arrived, onward in ((1, first, second), (2, second, None), (3, diagonal, None)):
            for t in range(nt):
                copy(t, k, arrived, me).wait_recv()
            if onward is not None:
                sent += everything(lambda t: copy(t, 3, arrived, onward))
            sent += everything(lambda t: copy(t, 3 + k, arrived, sibling))
            for cp in sent[-nt * (1 + (onward is not None)):]:
                cp.start()
        for k, place in ((0, sibling), (4, neighbour(1 - c, 1 - c)), (5, neighbour(1 - c, c)), (6, (1 - x, 1 - y, 1 - c))):
            for t in range(nt):
                copy(t, k, place, me).wait_recv()
        for cp in sent:
            cp.wait_send()
        for cp in mine:
            cp.wait()

    outs = pl.kernel(
        body, name=f"all_gather_layer_{collective_id}",
        out_type=[jax.ShapeDtypeStruct(full_shape(k), shards[k].dtype) for k in names],
        mesh=_sequencer_mesh(),
        scratch_types=[pltpu.SemaphoreType.DMA((7,)), pltpu.SemaphoreType.DMA((7,)), pltpu.SemaphoreType.DMA],
        compiler_params=pltpu.CompilerParams(collective_id=collective_id),
    )(*[shards[k] for k in names])
    return dict(zip(names, outs))


def _exchange_layer(grads, collective_id):
    names = list(grads)
    nt = len(names)
    axes = [SHARD_AXIS[k] for k in names]
    widths = [grads[k].shape[SHARD_AXIS[k]] // N_DEV for k in names]

    def slot_shape(t):
        shp = list(grads[names[t]].shape)
        shp[axes[t]] = widths[t]
        return (N_DEV, *shp)

    def body(*refs):
        ins, outs = refs[:nt], refs[nt:2 * nt]
        send_sems, recv_sems, local_sem = refs[2 * nt:]
        me = _my_place()
        mine = _position(me)
        _handshake([_flip(me, flips) for flips in FLIPS])
        local = [pltpu.make_async_copy(_shard_of(ins[t], axes[t], mine, widths[t]), outs[t].at[mine], local_sem)
                 for t in range(nt)]
        for cp in local:
            cp.start()
        copies = []
        for k, flips in enumerate(FLIPS):
            peer = _flip(me, flips)
            for t in range(nt):
                cp = pltpu.make_async_remote_copy(
                    src_ref=_shard_of(ins[t], axes[t], _position(peer), widths[t]), dst_ref=outs[t].at[mine],
                    send_sem=send_sems.at[k], recv_sem=recv_sems.at[k], device_id=peer, device_id_type=MESH)
                cp.start()
                copies.append(cp)
        for cp in copies:
            cp.wait()
        for cp in local:
            cp.wait()

    outs = pl.kernel(
        body, name=f"exchange_layer_{collective_id}",
        out_type=[jax.ShapeDtypeStruct(slot_shape(t), BF16) for t in range(nt)],
        mesh=_sequencer_mesh(),
        scratch_types=[pltpu.SemaphoreType.DMA((7,)), pltpu.SemaphoreType.DMA((7,)), pltpu.SemaphoreType.DMA],
        compiler_params=pltpu.CompilerParams(collective_id=collective_id),
    )(*[grads[k] for k in names])
    return dict(zip(names, outs))


def _gather_rows(buf, reduce):
    r, c = buf.shape

    def body(in_ref, out_ref, *scratch):
        if reduce:
            all_ref, send_sems, recv_sems = scratch
        else:
            all_ref = out_ref
            send_sems, recv_sems = scratch
        me = _my_place()
        all_ref[_position(me)] = in_ref[...]
        copies = []
        for k, flips in enumerate(FLIPS):
            cp = pltpu.make_async_remote_copy(
                src_ref=in_ref, dst_ref=all_ref.at[_position(me)],
                send_sem=send_sems.at[k], recv_sem=recv_sems.at[k], device_id=_flip(me, flips), device_id_type=MESH)
            cp.start()
            copies.append(cp)
        for cp in copies:
            cp.wait()
        if reduce:
            total = all_ref[0]
            for j in range(1, N_DEV):
                total = total + all_ref[j]
            out_ref[...] = total

    return pl.pallas_call(
        body, name="sum_rows" if reduce else "gather_rows",
        in_specs=[VMEM_SPEC], out_specs=VMEM_SPEC,
        out_shape=jax.ShapeDtypeStruct((r, c) if reduce else (N_DEV, r, c), F32),
        scratch_shapes=([pltpu.VMEM((N_DEV, r, c), F32)] if reduce else [])
        + [pltpu.SemaphoreType.DMA((7,)), pltpu.SemaphoreType.DMA((7,))],
    )(buf)


def _adamw_math(w, g, m, v):
    m = ADAM_B1 * m + (1.0 - ADAM_B1) * g
    v = ADAM_B2 * v + (1.0 - ADAM_B2) * (g * g)
    m_hat = m / (1.0 - ADAM_B1 ** ADAM_STEP)
    v_hat = v / (1.0 - ADAM_B2 ** ADAM_STEP)
    delta = -ADAM_LR * (m_hat / (jnp.sqrt(v_hat) + ADAM_EPS) + ADAM_WD * w)
    return delta, m, v


def _run_behind(x, token):
    def body(x_ref, token_ref, out_ref):
        out_ref[...] = jnp.zeros_like(out_ref)

    any_spec = pl.BlockSpec(memory_space=pl.ANY)
    return pl.pallas_call(
        body, name="run_behind",
        in_specs=[any_spec, any_spec], out_specs=VMEM_SPEC,
        out_shape=jax.ShapeDtypeStruct((8, 128), F32),
    )(x, token)


def _adamw_pieces(pieces, w, m, v, after, layer=None, into=()):
    shape = w.shape
    nl = shape[0]
    cols = shape[-1]
    rows = w.size // (nl * cols)
    first, nh = (0, nl) if layer is None else (layer, 1)
    tr = min(ADAMW_BLOCK_ROWS // nh, rows // 2)
    flat3 = lambda a: a.reshape(nl, rows, cols)
    into = [flat3(a) for a in into]

    def body(*refs):
        p_refs = refs[:nh]
        w_ref, m_ref, v_ref = refs[nh:nh + 3]
        g_ref, d_ref, nm_ref, nv_ref = refs[-4:]
        for l in range(nh):
            g = p_refs[l][0].astype(F32)
            for j in range(1, N_DEV):
                g = g + p_refs[l][j].astype(F32)
            g_ref[l] = g
            d_ref[l], nm_ref[l], nv_ref[l] = _adamw_math(w_ref[l], g, m_ref[l], v_ref[l])

    blk = pl.BlockSpec((nh, tr, cols), lambda i: (first, i, 0))
    any_spec = pl.BlockSpec(memory_space=pl.ANY)
    outs = pl.pallas_call(
        body, name="adamw_pieces",
        grid=(rows // tr,),
        in_specs=[pl.BlockSpec((N_DEV, tr, cols), lambda i: (0, i, 0))] * nh
        + [blk, blk, blk, any_spec] + [any_spec] * len(into),
        out_specs=[blk] * 4,
        out_shape=[jax.ShapeDtypeStruct((nl, rows, cols), F32)] * 4,
        input_output_aliases={nh + 4 + k: k for k in range(len(into))},
        compiler_params=_cparams("parallel"),
    )(*[a.reshape(N_DEV, rows, cols) for a in pieces], flat3(w), flat3(m), flat3(v), after, *into)
    return [a.reshape(shape) for a in outs]


def _adamw_small(g, w, m, v):
    shape = w.shape
    two = lambda a: a.reshape(-1, shape[-1])

    def body(g_ref, w_ref, m_ref, v_ref, d_ref, nm_ref, nv_ref):
        d_ref[...], nm_ref[...], nv_ref[...] = _adamw_math(w_ref[...], g_ref[...], m_ref[...], v_ref[...])

    outs = pl.pallas_call(
        body, name="adamw_small",
        in_specs=[VMEM_SPEC] * 4, out_specs=[VMEM_SPEC] * 3,
        out_shape=[jax.ShapeDtypeStruct(two(w).shape, F32)] * 3,
    )(two(g), two(w), two(m), two(v))
    return [a.reshape(shape) for a in outs]


WEIGHTS = ("norm_mix", "a_w_in", "a_w_conv", "a_w_out", "b_w_in", "b_w_grp", "b_scale", "b_w_out",
           "ple_norm", "ple_w_gate", "ple_w_proj", "final_norm")
SMALL_ROWS = 24
GATHER_ID = 0
EXCHANGE_ID = 4
LAST_EXCHANGE_ID = 8


def kernel(x, p, norm_mix, a_w_in, a_w_conv, a_w_out, b_w_in, b_w_grp, b_scale, b_w_out, ple_norm, ple_w_gate, ple_w_proj, final_norm, loss_target, m_norm_mix, m_a_w_in, m_a_w_conv, m_a_w_out, m_b_w_in, m_b_w_grp, m_b_scale, m_b_w_out, m_ple_norm, m_ple_w_gate, m_ple_w_proj, m_final_norm, v_norm_mix, v_a_w_in, v_a_w_conv, v_a_w_out, v_b_w_in, v_b_w_grp, v_b_scale, v_b_w_out, v_ple_norm, v_ple_w_gate, v_ple_w_proj, v_final_norm):
    wts = dict(norm_mix=norm_mix, a_w_in=a_w_in, a_w_conv=a_w_conv, a_w_out=a_w_out, b_w_in=b_w_in, b_w_grp=b_w_grp,
               b_scale=b_scale, b_w_out=b_w_out, ple_norm=ple_norm, ple_w_gate=ple_w_gate, ple_w_proj=ple_w_proj,
               final_norm=final_norm)
    mom = dict(norm_mix=m_norm_mix, a_w_in=m_a_w_in, a_w_conv=m_a_w_conv, a_w_out=m_a_w_out, b_w_in=m_b_w_in,
               b_w_grp=m_b_w_grp, b_scale=m_b_scale, b_w_out=m_b_w_out, ple_norm=m_ple_norm, ple_w_gate=m_ple_w_gate,
               ple_w_proj=m_ple_w_proj, final_norm=m_final_norm)
    var = dict(norm_mix=v_norm_mix, a_w_in=v_a_w_in, a_w_conv=v_a_w_conv, a_w_out=v_a_w_out, b_w_in=v_b_w_in,
               b_w_grp=v_b_w_grp, b_scale=v_b_scale, b_w_out=v_b_w_out, ple_norm=v_ple_norm, ple_w_gate=v_ple_w_gate,
               ple_w_proj=v_ple_w_proj, final_norm=v_final_norm)
    d = x.shape[2]
    depth = norm_mix.shape[0]
    n_a, n_b = a_w_conv.shape[0], b_scale.shape[0]
    cw = a_w_conv.shape[2]
    pos = _position(_my_place())

    def layer_matrices(i):
        j = i // 2
        mixer = {"w_in": ("a_w_in", j), "w_out": ("a_w_out", j)} if i % 2 == 0 else \
                {"w_in": ("b_w_in", j), "w_grp": ("b_w_grp", j), "w_out": ("b_w_out", j)}
        return {**mixer, "gate": ("ple_w_gate", i), "proj": ("ple_w_proj", i)}

    full = [_all_gather_layer({k: wts[name][idx].astype(BF16) for k, (name, idx) in layer_matrices(i).items()},
                              GATHER_ID + i) for i in range(depth)]
    vec_rows = jnp.concatenate([a_w_conv.reshape(-1, cw), b_scale], axis=0)
    vecs = _gather_rows(vec_rows, reduce=False)
    n_conv = 3 * n_a
    conv_w = vecs[:, :n_conv].transpose(1, 0, 2).reshape(n_a, 3, N_DEV * cw)
    scale_w = vecs[:, n_conv:].transpose(1, 0, 2).reshape(n_b, N_DEV * cw)

    def exchange(i, g):
        if i > 0:
            return _exchange_layer(g, EXCHANGE_ID + i)
        early = {k: a for k, a in g.items() if k != "w_in"}
        return {**_exchange_layer(early, EXCHANGE_ID), **_exchange_layer({"w_in": g["w_in"]}, LAST_EXCHANGE_ID)}

    loss_row, dx, sent, (d_norm, d_ple_norm, d_final, d_conv, d_scale) = _forward_backward(
        x[0], p[:, 0], loss_target[0], full, conv_w, scale_w, norm_mix, ple_norm, final_norm, exchange)
    pieces = {name: [None] * wts[name].shape[0] for name in WEIGHTS if wts[name].ndim >= 3 and name != "a_w_conv"}
    for i in range(depth):
        for k, (name, idx) in layer_matrices(i).items():
            pieces[name][idx] = sent[i][k]

    pad = lambda a: jnp.pad(a, ((0, 0), (0, d - a.shape[1])))
    small = jnp.concatenate(d_norm + d_ple_norm + [d_final] + d_conv + d_scale + [pad(loss_row)], axis=0)
    small = jnp.pad(small, ((0, SMALL_ROWS - small.shape[0]), (0, 0)))
    total = _gather_rows(small, reduce=True)
    o = 0
    gsum = {}
    gsum["norm_mix"] = total[o:o + depth]; o += depth
    gsum["ple_norm"] = total[o:o + depth]; o += depth
    gsum["final_norm"] = total[o]; o += 1
    conv_full = total[o:o + n_conv].reshape(n_a, 3, d); o += n_conv
    scale_full = total[o:o + n_b]; o += n_b
    loss = total[o, 0]
    gsum["a_w_conv"] = lax.dynamic_slice_in_dim(conv_full, pos * cw, cw, axis=2)
    gsum["b_scale"] = lax.dynamic_slice_in_dim(scale_full, pos * cw, cw, axis=1)

    token = total
    for i in reversed(range(depth)):
        token = _run_behind(sent[i]["w_out"], token)
    last_token = _run_behind(sent[0]["w_in"], token)
    grad, delta, new_m, new_v = {}, {}, {}, {}
    for k in sorted(WEIGHTS, key=lambda name: name == "a_w_in"):
        if k == "a_w_in":
            upper = [None] * 4
            for j in reversed(range(1, n_a)):
                upper = _adamw_pieces([pieces[k][j]], wts[k], mom[k], var[k], token, j, [a for a in upper if a is not None])
            grad[k], delta[k], new_m[k], new_v[k] = _adamw_pieces(
                [pieces[k][0]], wts[k], mom[k], var[k], last_token, 0, upper)
        elif k in pieces:
            grad[k], delta[k], new_m[k], new_v[k] = _adamw_pieces(pieces[k], wts[k], mom[k], var[k], token)
        else:
            grad[k] = gsum[k]
            delta[k], new_m[k], new_v[k] = _adamw_small(gsum[k], wts[k], mom[k], var[k])
    return (loss, dx[None], *[grad[k] for k in WEIGHTS], *[delta[k] for k in WEIGHTS],
            *[new_m[k] for k in WEIGHTS], *[new_v[k] for k in WEIGHTS])
```

```python
import jax
import jax.numpy as jnp
from jax import lax
from jax.experimental import pallas as pl
from jax.experimental.pallas import tpu as pltpu
from jax.experimental.pallas import tpu_sc as plsc

F32 = jnp.float32
BF16 = jnp.bfloat16
MESH = pl.DeviceIdType.MESH

RMS_EPS = 1e-6
POOL_WINDOWS = (2, 4, 8, 16)
N_POOL_GROUPS = len(POOL_WINDOWS)
ADAM_LR = 0.001
ADAM_B1 = 0.9
ADAM_B2 = 0.999
ADAM_EPS = 1e-08
ADAM_WD = 0.01
ADAM_STEP = 10
N_DEV = 8

HALO = 16
FWD_ROW_TILE = 512
POOL_BWD_ROW_TILE = 512
CONV_BWD_ROW_TILE = 256
ADAMW_BLOCK_ROWS = 512
BWD_ROW_TILE = 512
VMEM_LIMIT = 56 * 1024 * 1024


def _cparams(*sem):
    return pltpu.CompilerParams(dimension_semantics=sem, vmem_limit_bytes=VMEM_LIMIT)


def _dot(a, b):
    return jnp.dot(a, b, preferred_element_type=F32)


def _dot_nt(a, b):
    return lax.dot_general(a, b, (((1,), (1,)), ((), ())), preferred_element_type=F32)


def _dot_tn(a, b):
    return lax.dot_general(a, b, (((0,), (0,)), ((), ())), preferred_element_type=F32)


def _rms_stats(x):
    r = lax.rsqrt(jnp.mean(x * x, axis=-1, keepdims=True) + RMS_EPS)
    return x * r, r


def _rms_bwd(dy, xh, r, g):
    a = dy * g
    return r * (a - xh * jnp.mean(a * xh, axis=-1, keepdims=True))


def _sigmoid(x):
    return 0.5 * jnp.tanh(0.5 * x) + 0.5


def _shift_down(x, k):
    return pltpu.roll(x, k, 0)


def _shift_up(x, k):
    return pltpu.roll(x, x.shape[0] - k, 0)


def _conv_taps(u, u_prev):
    uu = jnp.concatenate([u_prev, u], axis=0)
    return _shift_down(uu, 1)[HALO:], _shift_down(uu, 2)[HALO:]


def _window_mean_minus(uu, row0, window):
    acc = uu
    span = 1
    while span < window:
        acc = acc + _shift_down(acc, span)
        span *= 2
    return acc[HALO:] * _inv_count(uu.shape[0] - HALO, row0, window) - uu[HALO:]


def _inv_count(rows, row0, window):
    t = row0 + lax.broadcasted_iota(jnp.int32, (rows, 1), 0)
    return 1.0 / jnp.minimum(t + 1, window).astype(F32)


def _whole(*shape):
    return pl.BlockSpec(shape, lambda i: (0,) * len(shape), pipeline_mode=pl.Buffered(1))


def _mixer_tile(i, x, g_ref, win_ref, mix_refs, carry_ref, proj_ref, hn_ref, is_conv):
    ts = x.shape[0]
    nsplit, _, e = proj_ref.shape
    gdim = e // N_POOL_GROUPS
    xh, _ = _rms_stats(x)
    hn = (xh * g_ref[...]).astype(BF16)
    hn_ref[...] = hn
    parts = []
    for k in range(nsplit):
        part = _dot(hn, win_ref[:, k * e:(k + 1) * e])
        proj_ref[k] = part.astype(BF16)
        parts.append(part)
    prev = carry_ref[...]
    if is_conv:
        b, c, v, z = parts
        w_ref, = mix_refs
        u = c * v
        u1, u2 = _conv_taps(u, prev)
        mixed = b * (w_ref[0:1, :] * u2 + w_ref[1:2, :] * u1 + w_ref[2:3, :] * u)
    else:
        u, z = parts
        wgrp_ref, sc_ref = mix_refs
        uu = jnp.concatenate([prev, u], axis=0)
        cols = []
        for gi, window in enumerate(POOL_WINDOWS):
            dg = _window_mean_minus(uu[:, gi * gdim:(gi + 1) * gdim], i * ts, window)
            cols.append(_dot(dg.astype(BF16), wgrp_ref[gi]))
        mixed = jnp.concatenate(cols, axis=1) * sc_ref[...]
    carry_ref[...] = u[ts - HALO:]
    return ((z * _sigmoid(z)) * mixed).astype(BF16)


def _out_ple_tile(x, o, wo_ref, pn_ref, wg_ref, p_ref, wp_ref, h1_ref, gl_ref, pp_ref):
    h1 = x + _dot(o, wo_ref[...])
    h1_ref[...] = h1
    xh1, _ = _rms_stats(h1)
    gl = _dot((xh1 * pn_ref[...]).astype(BF16), wg_ref[...])
    pp = _dot(p_ref[...].astype(BF16), wp_ref[...])
    gl_ref[...] = gl.astype(BF16)
    pp_ref[...] = pp.astype(BF16)
    return h1 + _sigmoid(gl) * pp


def _layer_fwd_halves(h, g, w_in, mixer, w_out, pn, w_gate, p_all, w_proj, layer):
    s, d = h.shape
    n = w_in.shape[1]
    e = w_out.shape[0]
    nsplit = n // e
    pdim = p_all.shape[2]
    is_conv = mixer[0] == "conv"
    ts = min(FWD_ROW_TILE, s)
    params = mixer[1:]
    row = lambda width: pl.BlockSpec((ts, width), lambda i: (i, 0))

    def mixer_body(*refs):
        h_ref, g_ref, win_ref = refs[:3]
        mix_refs = refs[3:3 + len(params)]
        proj_ref, hn_ref, o_ref, carry_ref = refs[3 + len(params):]
        i = pl.program_id(0)

        @pl.when(i == 0)
        def _():
            carry_ref[...] = jnp.zeros_like(carry_ref)

        o_ref[...] = _mixer_tile(i, h_ref[...], g_ref, win_ref, mix_refs, carry_ref, proj_ref, hn_ref, is_conv)

    proj, hn, o = pl.pallas_call(
        mixer_body, name="mixer_fwd",
        grid=(s // ts,),
        in_specs=[row(d), _whole(1, d), _whole(d, n)] + [_whole(*a.shape) for a in params],
        out_specs=[pl.BlockSpec((nsplit, ts, e), lambda i: (0, i, 0)), row(d), row(e)],
        out_shape=[jax.ShapeDtypeStruct((nsplit, s, e), BF16), jax.ShapeDtypeStruct((s, d), BF16),
                   jax.ShapeDtypeStruct((s, e), BF16)],
        scratch_shapes=[pltpu.VMEM((HALO, e), F32)],
        compiler_params=_cparams("arbitrary"),
    )(h, g, w_in, *params)

    def out_body(h_ref, o_ref, wo_ref, pn_ref, wg_ref, p_ref, wp_ref, h1_ref, h2_ref, gl_ref, pp_ref):
        h2_ref[...] = _out_ple_tile(h_ref[...], o_ref[...], wo_ref, pn_ref, wg_ref, p_ref, wp_ref,
                                    h1_ref, gl_ref, pp_ref)

    h1, h2, gl, pp = pl.pallas_call(
        out_body, name="out_ple_fwd",
        grid=(s // ts,),
        in_specs=[row(d), row(e), _whole(e, d), _whole(1, d), _whole(d, d),
                  pl.BlockSpec((None, ts, pdim), lambda i: (layer, i, 0)), _whole(pdim, d)],
        out_specs=[row(d), row(d), row(d), row(d)],
        out_shape=[jax.ShapeDtypeStruct((s, d), F32), jax.ShapeDtypeStruct((s, d), F32),
                   jax.ShapeDtypeStruct((s, d), BF16), jax.ShapeDtypeStruct((s, d), BF16)],
        compiler_params=_cparams("parallel"),
    )(h, o, w_out, pn, w_gate, p_all, w_proj)
    return h2, (proj, hn, o, h1, gl, pp)


def _layer_fwd(h, g, w_in, mixer, w_out, pn, w_gate, p_all, w_proj, layer, head=None):
    s, d = h.shape
    n = w_in.shape[1]
    e = w_out.shape[0]
    nsplit = n // e
    pdim = p_all.shape[2]
    is_conv = mixer[0] == "conv"
    ts = min(FWD_ROW_TILE, s)
    params = mixer[1:]
    head = tuple(head or ())

    def body(*refs):
        h_ref, g_ref, win_ref = refs[:3]
        mix_refs = refs[3:3 + len(params)]
        wo_ref, pn_ref, wg_ref, p_ref, wp_ref = refs[3 + len(params):8 + len(params)]
        head_refs = refs[8 + len(params):8 + len(params) + len(head)]
        proj_ref, hn_ref, o_ref, h1_ref, h2_ref, gl_ref, pp_ref = refs[8 + len(params) + len(head):][:7]
        carry_ref = refs[-1]
        i = pl.program_id(0)

        @pl.when(i == 0)
        def _():
            carry_ref[...] = jnp.zeros_like(carry_ref)

        x = h_ref[...]
        o = _mixer_tile(i, x, g_ref, win_ref, mix_refs, carry_ref, proj_ref, hn_ref, is_conv)
        o_ref[...] = o
        h2 = _out_ple_tile(x, o, wo_ref, pn_ref, wg_ref, p_ref, wp_ref, h1_ref, gl_ref, pp_ref)
        if not head:
            h2_ref[...] = h2
            return
        t_ref, gain_ref = head_refs
        loss_ref, dgain_ref = refs[-3], refs[-2]

        @pl.when(i == 0)
        def _():
            loss_ref[...] = jnp.zeros_like(loss_ref)
            dgain_ref[...] = jnp.zeros_like(dgain_ref)

        gain = gain_ref[...]
        yh, r = _rms_stats(h2)
        err = yh * gain - t_ref[...]
        loss_ref[...] += jnp.full(loss_ref.shape, (0.5 / d) * jnp.sum(err * err), F32)
        dy = err * (1.0 / d)
        dgain_ref[...] += jnp.sum(dy * yh, axis=0, keepdims=True)
        h2_ref[...] = _rms_bwd(dy, yh, r, gain)

    row = lambda width: pl.BlockSpec((ts, width), lambda i: (i, 0))
    mix_specs = [_whole(*a.shape) for a in params]
    head_specs = [row(d), _whole(1, d)] if head else []
    head_out_specs = [_whole(1, 128), _whole(1, d)] if head else []
    head_out_shape = [jax.ShapeDtypeStruct((1, 128), F32), jax.ShapeDtypeStruct((1, d), F32)] if head else []
    outs = pl.pallas_call(
        body, name="layer_fwd",
        grid=(s // ts,),
        in_specs=[row(d), _whole(1, d), _whole(d, n)] + mix_specs
        + [_whole(e, d), _whole(1, d), _whole(d, d),
           pl.BlockSpec((None, ts, pdim), lambda i: (layer, i, 0)), _whole(pdim, d)] + head_specs,
        out_specs=[pl.BlockSpec((nsplit, ts, e), lambda i: (0, i, 0)),
                   row(d), row(e), row(d), row(d), row(d), row(d)] + head_out_specs,
        out_shape=[jax.ShapeDtypeStruct((nsplit, s, e), BF16), jax.ShapeDtypeStruct((s, d), BF16),
                   jax.ShapeDtypeStruct((s, e), BF16), jax.ShapeDtypeStruct((s, d), F32),
                   jax.ShapeDtypeStruct((s, d), F32), jax.ShapeDtypeStruct((s, d), BF16),
                   jax.ShapeDtypeStruct((s, d), BF16)] + head_out_shape,
        scratch_shapes=[pltpu.VMEM((HALO, e), F32)],
        compiler_params=_cparams("arbitrary"),
    )(h, g, w_in, *params, w_out, pn, w_gate, p_all, w_proj, *head)
    proj, hn, o, h1, h2, gl, pp = outs[:7]
    return (h2, *outs[7:]) if head else h2, (proj, hn, o, h1, gl, pp)


def _out_ple_bwd(dh2, gl, pp, h1, p_all, o, pn, wgate, wout, layer):
    s, d = dh2.shape
    e = o.shape[1]
    pdim = p_all.shape[2]
    ts = min(BWD_ROW_TILE, s)
    last = s // ts - 1

    def body(dh2_ref, gl_ref, pp_ref, h1_ref, p_ref, o_ref, pn_ref, wg_ref, wo_ref,
             dh1_ref, do_ref, dwp_ref, dwg_ref, dwo_ref, dpn_ref, awp, awg, awo):
        i = pl.program_id(0)

        @pl.when(i == 0)
        def _():
            awp[...] = jnp.zeros_like(awp)
            awg[...] = jnp.zeros_like(awg)
            awo[...] = jnp.zeros_like(awo)
            dpn_ref[...] = jnp.zeros_like(dpn_ref)

        dh2 = dh2_ref[...]
        gate = _sigmoid(gl_ref[...].astype(F32))
        dpp = (dh2 * gate).astype(BF16)
        dgl = (dh2 * pp_ref[...].astype(F32) * gate * (1.0 - gate)).astype(BF16)
        xh, r = _rms_stats(h1_ref[...])
        pn = pn_ref[...]
        awp[...] += _dot_tn(p_ref[...].astype(BF16), dpp)
        awg[...] += _dot_tn((xh * pn).astype(BF16), dgl)
        dr = _dot_nt(dgl, wg_ref[...])
        dpn_ref[...] += jnp.sum(dr * xh, axis=0, keepdims=True)
        dh1 = dh2 + _rms_bwd(dr, xh, r, pn)
        dh1_ref[...] = dh1
        dh1b = dh1.astype(BF16)
        do_ref[...] = _dot_nt(dh1b, wo_ref[...]).astype(BF16)
        awo[...] += _dot_tn(o_ref[...], dh1b)

        @pl.when(i == last)
        def _():
            dwp_ref[...] = awp[...].astype(BF16)
            dwg_ref[...] = awg[...].astype(BF16)
            dwo_ref[...] = awo[...].astype(BF16)

    row = lambda width: pl.BlockSpec((ts, width), lambda i: (i, 0))
    return pl.pallas_call(
        body, name="out_ple_bwd",
        grid=(s // ts,),
        in_specs=[row(d), row(d), row(d), row(d),
                  pl.BlockSpec((None, ts, pdim), lambda i: (layer, i, 0)),
                  row(e), _whole(1, d), _whole(d, d), _whole(e, d)],
        out_specs=[row(d), row(e), _whole(pdim, d), _whole(d, d), _whole(e, d), _whole(1, d)],
        out_shape=[jax.ShapeDtypeStruct((s, d), F32), jax.ShapeDtypeStruct((s, e), BF16),
                   jax.ShapeDtypeStruct((pdim, d), BF16), jax.ShapeDtypeStruct((d, d), BF16),
                   jax.ShapeDtypeStruct((e, d), BF16), jax.ShapeDtypeStruct((1, d), F32)],
        scratch_shapes=[pltpu.VMEM((pdim, d), F32), pltpu.VMEM((d, d), F32), pltpu.VMEM((e, d), F32)],
        compiler_params=_cparams("arbitrary"),
    )(dh2, gl, pp, h1, p_all, o, pn, wgate, wout)


def _mixer_bwd(do, proj, hn, mixer, w_in, h, g, dh1):
    s, d = h.shape
    nsplit, _, e = proj.shape
    gdim = e // N_POOL_GROUPS
    is_conv = mixer[0] == "conv"
    ts = min(CONV_BWD_ROW_TILE if is_conv else POOL_BWD_ROW_TILE, s)
    nt = s // ts
    params = mixer[1:]
    n_mix_out = 1 if is_conv else 2

    def body(*refs):
        refs = list(refs)
        take = lambda n: [refs.pop(0) for _ in range(n)]
        do_ref, p_ref, ph_ref = take(3)
        mix_refs = take(len(params))
        win_ref, h_ref, g_ref, dh1_ref, hn_ref = take(5)
        dwin_ref, dh_ref, dg_ref = take(3)
        mix_out = take(n_mix_out)
        carry_ref, dp_ref, acc_ref = take(3)
        i = pl.program_id(0)
        tile = nt - 1 - i

        @pl.when(i == 0)
        def _():
            for ref in [carry_ref, dg_ref, acc_ref] + mix_out[-1:] + refs:
                ref[...] = jnp.zeros_like(ref)

        dof = do_ref[...].astype(F32)
        nxt = carry_ref[...]
        if is_conv:
            w_ref, = mix_refs
            dw_ref, = mix_out
            w0, w1, w2 = w_ref[0:1, :], w_ref[1:2, :], w_ref[2:3, :]
            b, c, v, z = [p_ref[k].astype(F32) for k in range(4)]
            u = c * v
            u_prev = jnp.where(tile == 0, 0.0, ph_ref[1].astype(F32) * ph_ref[2].astype(F32))
            u1, u2 = _conv_taps(u, u_prev)
            conv = w0 * u2 + w1 * u1 + w2 * u
            sig = _sigmoid(z)
            sz = z * sig
            dy = dof * sz
            dp_ref[3] = (dof * (b * conv) * (sig + sz * (1.0 - sig))).astype(BF16)
            dp_ref[0] = (dy * conv).astype(BF16)
            dconv = dy * b
            dw_ref[0:1, :] += jnp.sum(dconv * u2, axis=0, keepdims=True)
            dw_ref[1:2, :] += jnp.sum(dconv * u1, axis=0, keepdims=True)
            dw_ref[2:3, :] += jnp.sum(dconv * u, axis=0, keepdims=True)
            dcc = jnp.concatenate([dconv, nxt], axis=0)
            du = w2 * dconv + w1 * _shift_up(dcc, 1)[:ts] + w0 * _shift_up(dcc, 2)[:ts]
            carry_ref[...] = dconv[:HALO]
            dp_ref[1] = (du * v).astype(BF16)
            dp_ref[2] = (du * c).astype(BF16)
        else:
            wgrp_ref, sc_ref = mix_refs
            dsc_ref = mix_out[1]
            agrp, = refs
            u = p_ref[0].astype(F32)
            z = p_ref[1].astype(F32)
            u_prev = jnp.where(tile == 0, 0.0, ph_ref[0].astype(F32))
            uu = jnp.concatenate([u_prev, u], axis=0)
            sig = _sigmoid(z)
            sz = z * sig
            dm = dof * sz
            dsilu = dof * (sig + sz * (1.0 - sig))
            for gi, window in enumerate(POOL_WINDOWS):
                cols = slice(gi * gdim, (gi + 1) * gdim)
                w = wgrp_ref[gi]
                scale = sc_ref[:, cols]
                db = _window_mean_minus(uu[:, cols], tile * ts, window).astype(BF16)
                mr = _dot(db, w)
                dp_ref[1, :, cols] = (dsilu[:, cols] * (mr * scale)).astype(BF16)
                dmg = dm[:, cols]
                dmr = (dmg * scale).astype(BF16)
                agrp[gi] += _dot_tn(db, dmr)
                dsc_ref[:, cols] += jnp.sum(dmg * mr, axis=0, keepdims=True)
                dd = _dot_nt(dmr, w)
                ddq = dd * _inv_count(ts, tile * ts, window)
                acc = jnp.concatenate([ddq, nxt[:, cols]], axis=0)
                span = 1
                while span < window:
                    acc = acc + _shift_up(acc, span)
                    span *= 2
                carry_ref[:, cols] = ddq[:HALO]
                dp_ref[0, :, cols] = (acc[:ts] - dd).astype(BF16)

        dhn = _dot_nt(dp_ref[0], win_ref[:, 0:e])
        for k in range(1, nsplit):
            dhn += _dot_nt(dp_ref[k], win_ref[:, k * e:(k + 1) * e])
        xh, r = _rms_stats(h_ref[...])
        dg_ref[...] += jnp.sum(dhn * xh, axis=0, keepdims=True)
        dh_ref[...] = dh1_ref[...] + _rms_bwd(dhn, xh, r, g_ref[...])
        hn_tile = hn_ref[...]
        for k in range(nsplit):
            acc_ref[:, k * e:(k + 1) * e] += _dot_tn(hn_tile, dp_ref[k])

        @pl.when(i == nt - 1)
        def _():
            dwin_ref[...] = acc_ref[...].astype(BF16)
            if not is_conv:
                mix_out[0][...] = refs[0][...].astype(BF16)

    rev = lambda width: pl.BlockSpec((ts, width), lambda i: (nt - 1 - i, 0))
    halo_blocks = ts // HALO
    in_specs = [rev(e),
                pl.BlockSpec((nsplit, ts, e), lambda i: (0, nt - 1 - i, 0)),
                pl.BlockSpec((nsplit, HALO, e), lambda i: (0, jnp.maximum((nt - 1 - i) * halo_blocks - 1, 0), 0))]
    in_specs += [_whole(*a.shape) for a in params]
    in_specs += [_whole(d, nsplit * e), rev(d), _whole(1, d), rev(d), rev(d)]
    out_specs = [_whole(d, nsplit * e), rev(d), _whole(1, d)]
    out_shape = [jax.ShapeDtypeStruct((d, nsplit * e), BF16), jax.ShapeDtypeStruct((s, d), F32),
                 jax.ShapeDtypeStruct((1, d), F32)]
    scratch = [pltpu.VMEM((HALO, e), F32), pltpu.VMEM((nsplit, ts, e), BF16), pltpu.VMEM((d, nsplit * e), F32)]
    if is_conv:
        out_specs += [_whole(3, e)]
        out_shape += [jax.ShapeDtypeStruct((3, e), F32)]
    else:
        out_specs += [_whole(N_POOL_GROUPS, gdim, gdim), _whole(1, e)]
        out_shape += [jax.ShapeDtypeStruct((N_POOL_GROUPS, gdim, gdim), BF16), jax.ShapeDtypeStruct((1, e), F32)]
        scratch += [pltpu.VMEM((N_POOL_GROUPS, gdim, gdim), F32)]
    return pl.pallas_call(
        body, name="mixer_bwd",
        grid=(nt,),
        in_specs=in_specs, out_specs=out_specs, out_shape=out_shape, scratch_shapes=scratch,
        compiler_params=_cparams("arbitrary"),
    )(do, proj, proj, *params, w_in, h, g, dh1, hn)


def _forward_backward(xs, ps, target, full, conv_w, scale_w, norm_mix, ple_norm, final_norm, exchange):
    depth = len(full)
    row = lambda a, i: a[i][None, :]
    mixer_of = lambda i: ("conv", conv_w[i // 2]) if i % 2 == 0 else ("pool", full[i]["w_grp"], row(scale_w, i // 2))

    saved = []
    h = xs
    for i in range(depth):
        w = full[i]
        head = (target, final_norm[None, :]) if i == depth - 1 else None
        args = (h, row(norm_mix, i), w["w_in"], mixer_of(i), w["w_out"], row(ple_norm, i), w["gate"], ps, w["proj"], i)
        h_next, acts = _layer_fwd_halves(*args) if i == 0 and not head else _layer_fwd(*args, head)
        saved.append((h, *acts))
        h = h_next
    dh, loss_row, d_final = h

    d_norm, d_ple_norm, d_conv, d_scale, sent = [None] * depth, [None] * depth, [], [], [None] * depth
    for i in reversed(range(depth)):
        w = full[i]
        h_in, proj, hn, o, h1, gl, pp = saved[i]
        g = {}
        dh1, do, g["proj"], g["gate"], g["w_out"], d_ple_norm[i] = _out_ple_bwd(
            dh, gl, pp, h1, ps, o, row(ple_norm, i), w["gate"], w["w_out"], i)
        g["w_in"], dh, d_norm[i], *mixer_grads = _mixer_bwd(
            do, proj, hn, mixer_of(i), w["w_in"], h_in, row(norm_mix, i), dh1)
        if i % 2 == 0:
            d_conv.insert(0, mixer_grads[0])
        else:
            g["w_grp"] = mixer_grads[0]
            d_scale.insert(0, mixer_grads[1])
        sent[i] = exchange(i, g)
    return loss_row, dh, sent, (d_norm, d_ple_norm, d_final, d_conv, d_scale)


VMEM_SPEC = pl.BlockSpec(memory_space=pltpu.VMEM)

FLIPS = [(fx, fy, fc) for fx in (0, 1) for fy in (0, 1) for fc in (0, 1)][1:]
SHARD_AXIS = {"w_in": 1, "w_out": 0, "w_grp": 1, "gate": 0, "proj": 1}


def _my_place():
    return lax.axis_index("x"), lax.axis_index("y"), lax.axis_index("c")


def _position(place):
    x, y, c = place
    return 4 * x + 2 * y + c


def _flip(place, flips):
    return tuple(1 - v if f else v for v, f in zip(place, flips))


def _shard_of(ref, axis, pos, n):
    idx = [slice(None)] * len(ref.shape)
    idx[axis] = pl.ds(pl.multiple_of(pos * n, n), n)
    return ref.at[tuple(idx)]


def _sequencer_mesh():
    return plsc.ScalarSubcoreMesh(axis_name="sequencer", num_cores=1)


def _handshake(peers):
    barrier = pltpu.get_barrier_semaphore()
    for peer in peers:
        pl.semaphore_signal(barrier, inc=1, device_id=peer, device_id_type=MESH)
    pl.semaphore_wait(barrier, len(peers))


def _all_gather_layer(shards, collective_id):
    names = list(shards)
    nt = len(names)
    axes = [SHARD_AXIS[k] for k in names]
    widths = [shards[k].shape[SHARD_AXIS[k]] for k in names]

    def full_shape(k):
        shp = list(shards[k].shape)
        shp[SHARD_AXIS[k]] *= N_DEV
        return tuple(shp)

    def body(*refs):
        ins, outs = refs[:nt], refs[nt:2 * nt]
        send_sems, recv_sems, local_sem = refs[2 * nt:]
        me = _my_place()
        x, y, c = me
        sibling = (x, y, 1 - c)
        flip = lambda v, f: v + f - 2 * v * f
        neighbour = lambda core, fx: (flip(x, fx), flip(y, 1 - fx), core)
        first, second = neighbour(c, c), neighbour(c, 1 - c)
        diagonal = (1 - x, 1 - y, c)
        _handshake([sibling, first, second])

        def block(t, place):
            return _shard_of(outs[t], axes[t], _position(place), widths[t])

        def copy(t, k, place, to, src=None):
            return pltpu.make_async_remote_copy(
                src_ref=block(t, place) if src is None else src, dst_ref=block(t, place),
                send_sem=send_sems.at[k], recv_sem=recv_sems.at[k], device_id=to, device_id_type=MESH)

        everything = lambda make: [make(t) for t in range(nt)]
        mine = everything(lambda t: pltpu.make_async_copy(ins[t], block(t, me), local_sem))
        sent = (everything(lambda t: copy(t, 1, me, first, src=ins[t]))
                + everything(lambda t: copy(t, 2, me, second, src=ins[t]))
                + everything(lambda t: copy(t, 0, me, sibling, src=ins[t])))
        for cp in mine + sent:
            cp.start()
        for k, arrived, onward in ((1, first, second), (2, second, None), (3, diagonal, None)):
            for t in range(nt):
                copy(t, k, arrived, me).wait_recv()
            if onward is not None:
                sent += everything(lambda t: copy(t, 3, arrived, onward))
            sent += everything(lambda t: copy(t, 3 + k, arrived, sibling))
            for cp in sent[-nt * (1 + (onward is not None)):]:
                cp.start()
        for k, place in ((0, sibling), (4, neighbour(1 - c, 1 - c)), (5, neighbour(1 - c, c)), (6, (1 - x, 1 - y, 1 - c))):
            for t in range(nt):
                copy(t, k, place, me).wait_recv()
        for cp in sent:
            cp.wait_send()
        for cp in mine:
            cp.wait()

    outs = pl.kernel(
        body, name=f"all_gather_layer_{collective_id}",
        out_type=[jax.ShapeDtypeStruct(full_shape(k), shards[k].dtype) for k in names],
        mesh=_sequencer_mesh(),
        scratch_types=[pltpu.SemaphoreType.DMA((7,)), pltpu.SemaphoreType.DMA((7,)), pltpu.SemaphoreType.DMA],
        compiler_params=pltpu.CompilerParams(collective_id=collective_id),
    )(*[shards[k] for k in names])
    return dict(zip(names, outs))


def _exchange_layer(grads, collective_id):
    names = list(grads)
    nt = len(names)
    axes = [SHARD_AXIS[k] for k in names]
    widths = [grads[k].shape[SHARD_AXIS[k]] // N_DEV for k in names]

    def slot_shape(t):
        shp = list(grads[names[t]].shape)
        shp[axes[t]] = widths[t]
        return (N_DEV, *shp)

    def body(*refs):
        ins, outs = refs[:nt], refs[nt:2 * nt]
        send_sems, recv_sems, local_sem = refs[2 * nt:]
        me = _my_place()
        mine = _position(me)
        _handshake([_flip(me, flips) for flips in FLIPS])
        local = [pltpu.make_async_copy(_shard_of(ins[t], axes[t], mine, widths[t]), outs[t].at[mine], local_sem)
                 for t in range(nt)]
        for cp in local:
            cp.start()
        copies = []
        for k, flips in enumerate(FLIPS):
            peer = _flip(me, flips)
            for t in range(nt):
                cp = pltpu.make_async_remote_copy(
                    src_ref=_shard_of(ins[t], axes[t], _position(peer), widths[t]), dst_ref=outs[t].at[mine],
                    send_sem=send_sems.at[k], recv_sem=recv_sems.at[k], device_id=peer, device_id_type=MESH)
                cp.start()
                copies.append(cp)
        for cp in copies:
            cp.wait()
        for cp in local:
            cp.wait()

    outs = pl.kernel(
        body, name=f"exchange_layer_{collective_id}",
        out_type=[jax.ShapeDtypeStruct(slot_shape(t), BF16) for t in range(nt)],
        mesh=_sequencer_mesh(),
        scratch_types=[pltpu.SemaphoreType.DMA((7,)), pltpu.SemaphoreType.DMA((7,)), pltpu.SemaphoreType.DMA],
        compiler_params=pltpu.CompilerParams(collective_id=collective_id),
    )(*[grads[k] for k in names])
    return dict(zip(names, outs))


def _gather_rows(buf, reduce):
    r, c = buf.shape

    def body(in_ref, out_ref, *scratch):
        if reduce:
            all_ref, send_sems, recv_sems = scratch
        else:
            all_ref = out_ref
            send_sems, recv_sems = scratch
        me = _my_place()
        all_ref[_position(me)] = in_ref[...]
        copies = []
        for k, flips in enumerate(FLIPS):
            cp = pltpu.make_async_remote_copy(
                src_ref=in_ref, dst_ref=all_ref.at[_position(me)],
                send_sem=send_sems.at[k], recv_sem=recv_sems.at[k], device_id=_flip(me, flips), device_id_type=MESH)
            cp.start()
            copies.append(cp)
        for cp in copies:
            cp.wait()
        if reduce:
            total = all_ref[0]
            for j in range(1, N_DEV):
                total = total + all_ref[j]
            out_ref[...] = total

    return pl.pallas_call(
        body, name="sum_rows" if reduce else "gather_rows",
        in_specs=[VMEM_SPEC], out_specs=VMEM_SPEC,
        out_shape=jax.ShapeDtypeStruct((r, c) if reduce else (N_DEV, r, c), F32),
        scratch_shapes=([pltpu.VMEM((N_DEV, r, c), F32)] if reduce else [])
        + [pltpu.SemaphoreType.DMA((7,)), pltpu.SemaphoreType.DMA((7,))],
    )(buf)


def _adamw_math(w, g, m, v):
    m = ADAM_B1 * m + (1.0 - ADAM_B1) * g
    v = ADAM_B2 * v + (1.0 - ADAM_B2) * (g * g)
    m_hat = m / (1.0 - ADAM_B1 ** ADAM_STEP)
    v_hat = v / (1.0 - ADAM_B2 ** ADAM_STEP)
    delta = -ADAM_LR * (m_hat / (jnp.sqrt(v_hat) + ADAM_EPS) + ADAM_WD * w)
    return delta, m, v


def _run_behind(x, token):
    def body(x_ref, token_ref, out_ref):
        out_ref[...] = jnp.zeros_like(out_ref)

    any_spec = pl.BlockSpec(memory_space=pl.ANY)
    return pl.pallas_call(
        body, name="run_behind",
        in_specs=[any_spec, any_spec], out_specs=VMEM_SPEC,
        out_shape=jax.ShapeDtypeStruct((8, 128), F32),
    )(x, token)


def _adamw_pieces(pieces, w, m, v, after, layer=None, into=()):
    shape = w.shape
    nl = shape[0]
    cols = shape[-1]
    rows = w.size // (nl * cols)
    first, nh = (0, nl) if layer is None else (layer, 1)
    tr = min(ADAMW_BLOCK_ROWS // nh, rows // 2)
    flat3 = lambda a: a.reshape(nl, rows, cols)
    into = [flat3(a) for a in into]

    def body(*refs):
        p_refs = refs[:nh]
        w_ref, m_ref, v_ref = refs[nh:nh + 3]
        g_ref, d_ref, nm_ref, nv_ref = refs[-4:]
        for l in range(nh):
            g = p_refs[l][0].astype(F32)
            for j in range(1, N_DEV):
                g = g + p_refs[l][j].astype(F32)
            g_ref[l] = g
            d_ref[l], nm_ref[l], nv_ref[l] = _adamw_math(w_ref[l], g, m_ref[l], v_ref[l])

    blk = pl.BlockSpec((nh, tr, cols), lambda i: (first, i, 0))
    any_spec = pl.BlockSpec(memory_space=pl.ANY)
    outs = pl.pallas_call(
        body, name="adamw_pieces",
        grid=(rows // tr,),
        in_specs=[pl.BlockSpec((N_DEV, tr, cols), lambda i: (0, i, 0))] * nh
        + [blk, blk, blk, any_spec] + [any_spec] * len(into),
        out_specs=[blk] * 4,
        out_shape=[jax.ShapeDtypeStruct((nl, rows, cols), F32)] * 4,
        input_output_aliases={nh + 4 + k: k for k in range(len(into))},
        compiler_params=_cparams("parallel"),
    )(*[a.reshape(N_DEV, rows, cols) for a in pieces], flat3(w), flat3(m), flat3(v), after, *into)
    return [a.reshape(shape) for a in outs]


def _adamw_small(g, w, m, v):
    shape = w.shape
    two = lambda a: a.reshape(-1, shape[-1])

    def body(g_ref, w_ref, m_ref, v_ref, d_ref, nm_ref, nv_ref):
        d_ref[...], nm_ref[...], nv_ref[...] = _adamw_math(w_ref[...], g_ref[...], m_ref[...], v_ref[...])

    outs = pl.pallas_call(
        body, name="adamw_small",
        in_specs=[VMEM_SPEC] * 4, out_specs=[VMEM_SPEC] * 3,
        out_shape=[jax.ShapeDtypeStruct(two(w).shape, F32)] * 3,
    )(two(g), two(w), two(m), two(v))
    return [a.reshape(shape) for a in outs]


WEIGHTS = ("norm_mix", "a_w_in", "a_w_conv", "a_w_out", "b_w_in", "b_w_grp", "b_scale", "b_w_out",
           "ple_norm", "ple_w_gate", "ple_w_proj", "final_norm")
SMALL_ROWS = 24
GATHER_ID = 0
EXCHANGE_ID = 4
LAST_EXCHANGE_ID = 8
REST_GATHER_ID = 9


def kernel(x, p, norm_mix, a_w_in, a_w_conv, a_w_out, b_w_in, b_w_grp, b_scale, b_w_out, ple_norm, ple_w_gate, ple_w_proj, final_norm, loss_target, m_norm_mix, m_a_w_in, m_a_w_conv, m_a_w_out, m_b_w_in, m_b_w_grp, m_b_scale, m_b_w_out, m_ple_norm, m_ple_w_gate, m_ple_w_proj, m_final_norm, v_norm_mix, v_a_w_in, v_a_w_conv, v_a_w_out, v_b_w_in, v_b_w_grp, v_b_scale, v_b_w_out, v_ple_norm, v_ple_w_gate, v_ple_w_proj, v_final_norm):
    wts = dict(norm_mix=norm_mix, a_w_in=a_w_in, a_w_conv=a_w_conv, a_w_out=a_w_out, b_w_in=b_w_in, b_w_grp=b_w_grp,
               b_scale=b_scale, b_w_out=b_w_out, ple_norm=ple_norm, ple_w_gate=ple_w_gate, ple_w_proj=ple_w_proj,
               final_norm=final_norm)
    mom = dict(norm_mix=m_norm_mix, a_w_in=m_a_w_in, a_w_conv=m_a_w_conv, a_w_out=m_a_w_out, b_w_in=m_b_w_in,
               b_w_grp=m_b_w_grp, b_scale=m_b_scale, b_w_out=m_b_w_out, ple_norm=m_ple_norm, ple_w_gate=m_ple_w_gate,
               ple_w_proj=m_ple_w_proj, final_norm=m_final_norm)
    var = dict(norm_mix=v_norm_mix, a_w_in=v_a_w_in, a_w_conv=v_a_w_conv, a_w_out=v_a_w_out, b_w_in=v_b_w_in,
               b_w_grp=v_b_w_grp, b_scale=v_b_scale, b_w_out=v_b_w_out, ple_norm=v_ple_norm, ple_w_gate=v_ple_w_gate,
               ple_w_proj=v_ple_w_proj, final_norm=v_final_norm)
    d = x.shape[2]
    depth = norm_mix.shape[0]
    n_a, n_b = a_w_conv.shape[0], b_scale.shape[0]
    cw = a_w_conv.shape[2]
    pos = _position(_my_place())

    def layer_matrices(i):
        j = i // 2
        mixer = {"w_in": ("a_w_in", j), "w_out": ("a_w_out", j)} if i % 2 == 0 else \
                {"w_in": ("b_w_in", j), "w_grp": ("b_w_grp", j), "w_out": ("b_w_out", j)}
        return {**mixer, "gate": ("ple_w_gate", i), "proj": ("ple_w_proj", i)}

    def gathered(i):
        shards = {k: wts[name][idx].astype(BF16) for k, (name, idx) in layer_matrices(i).items()}
        if i > 0:
            return _all_gather_layer(shards, GATHER_ID + i)
        first = {"w_in": shards.pop("w_in")}
        return {**_all_gather_layer(first, GATHER_ID), **_all_gather_layer(shards, REST_GATHER_ID)}

    full = [gathered(i) for i in range(depth)]
    vec_rows = jnp.concatenate([a_w_conv.reshape(-1, cw), b_scale], axis=0)
    vecs = _gather_rows(vec_rows, reduce=False)
    n_conv = 3 * n_a
    conv_w = vecs[:, :n_conv].transpose(1, 0, 2).reshape(n_a, 3, N_DEV * cw)
    scale_w = vecs[:, n_conv:].transpose(1, 0, 2).reshape(n_b, N_DEV * cw)

    def exchange(i, g):
        if i > 0:
            return _exchange_layer(g, EXCHANGE_ID + i)
        early = {k: a for k, a in g.items() if k != "w_in"}
        return {**_exchange_layer(early, EXCHANGE_ID), **_exchange_layer({"w_in": g["w_in"]}, LAST_EXCHANGE_ID)}

    loss_row, dx, sent, (d_norm, d_ple_norm, d_final, d_conv, d_scale) = _forward_backward(
        x[0], p[:, 0], loss_target[0], full, conv_w, scale_w, norm_mix, ple_norm, final_norm, exchange)
    pieces = {name: [None] * wts[name].shape[0] for name in WEIGHTS if wts[name].ndim >= 3 and name != "a_w_conv"}
    for i in range(depth):
        for k, (name, idx) in layer_matrices(i).items():
            pieces[name][idx] = sent[i][k]

    pad = lambda a: jnp.pad(a, ((0, 0), (0, d - a.shape[1])))
    small = jnp.concatenate(d_norm + d_ple_norm + [d_final] + d_conv + d_scale + [pad(loss_row)], axis=0)
    small = jnp.pad(small, ((0, SMALL_ROWS - small.shape[0]), (0, 0)))
    total = _gather_rows(small, reduce=True)
    o = 0
    gsum = {}
    gsum["norm_mix"] = total[o:o + depth]; o += depth
    gsum["ple_norm"] = total[o:o + depth]; o += depth
    gsum["final_norm"] = total[o]; o += 1
    conv_full = total[o:o + n_conv].reshape(n_a, 3, d); o += n_conv
    scale_full = total[o:o + n_b]; o += n_b
    loss = total[o, 0]
    gsum["a_w_conv"] = lax.dynamic_slice_in_dim(conv_full, pos * cw, cw, axis=2)
    gsum["b_scale"] = lax.dynamic_slice_in_dim(scale_full, pos * cw, cw, axis=1)

    token = total
    for i in reversed(range(depth)):
        token = _run_behind(sent[i]["w_out"], token)
    last_token = _run_behind(sent[0]["w_in"], token)
    grad, delta, new_m, new_v = {}, {}, {}, {}
    for k in sorted(WEIGHTS, key=lambda name: name == "a_w_in"):
        if k == "a_w_in":
            upper = [None] * 4
            for j in reversed(range(1, n_a)):
                upper = _adamw_pieces([pieces[k][j]], wts[k], mom[k], var[k], token, j, [a for a in upper if a is not None])
            grad[k], delta[k], new_m[k], new_v[k] = _adamw_pieces(
                [pieces[k][0]], wts[k], mom[k], var[k], last_token, 0, upper)
        elif k in pieces:
            grad[k], delta[k], new_m[k], new_v[k] = _adamw_pieces(pieces[k], wts[k], mom[k], var[k], token)
        else:
            grad[k] = gsum[k]
            delta[k], new_m[k], new_v[k] = _adamw_small(gsum[k], wts[k], mom[k], var[k])
    return (loss, dx[None], *[grad[k] for k in WEIGHTS], *[delta[k] for k in WEIGHTS],
            *[new_m[k] for k in WEIGHTS], *[new_v[k] for k in WEIGHTS])
```

```python
import jax
import jax.numpy as jnp
from jax import lax
from jax.experimental import pallas as pl
from jax.experimental.pallas import tpu as pltpu
from jax.experimental.pallas import tpu_sc as plsc

F32 = jnp.float32
BF16 = jnp.bfloat16
MESH = pl.DeviceIdType.MESH

RMS_EPS = 1e-6
POOL_WINDOWS = (2, 4, 8, 16)
N_POOL_GROUPS = len(POOL_WINDOWS)
ADAM_LR = 0.001
ADAM_B1 = 0.9
ADAM_B2 = 0.999
ADAM_EPS = 1e-08
ADAM_WD = 0.01
ADAM_STEP = 10
N_DEV = 8

HALO = 16
FWD_ROW_TILE = 512
POOL_BWD_ROW_TILE = 512
CONV_BWD_ROW_TILE = 256
ADAMW_BLOCK_ROWS = 512
BWD_ROW_TILE = 512
VMEM_LIMIT = 56 * 1024 * 1024


def _cparams(*sem):
    return pltpu.CompilerParams(dimension_semantics=sem, vmem_limit_bytes=VMEM_LIMIT)


def _dot(a, b):
    return jnp.dot(a, b, preferred_element_type=F32)


def _dot_nt(a, b):
    return lax.dot_general(a, b, (((1,), (1,)), ((), ())), preferred_element_type=F32)


def _dot_tn(a, b):
    return lax.dot_general(a, b, (((0,), (0,)), ((), ())), preferred_element_type=F32)


def _rms_stats(x):
    r = lax.rsqrt(jnp.mean(x * x, axis=-1, keepdims=True) + RMS_EPS)
    return x * r, r


def _rms_bwd(dy, xh, r, g):
    a = dy * g
    return r * (a - xh * jnp.mean(a * xh, axis=-1, keepdims=True))


def _sigmoid(x):
    return 0.5 * jnp.tanh(0.5 * x) + 0.5


def _silu(x):
    half = 0.5 * x
    return half * jnp.tanh(half) + half


def _shift_down(x, k):
    return pltpu.roll(x, k, 0)


def _shift_up(x, k):
    return pltpu.roll(x, x.shape[0] - k, 0)


def _conv_taps(u, u_prev):
    uu = jnp.concatenate([u_prev, u], axis=0)
    return _shift_down(uu, 1)[HALO:], _shift_down(uu, 2)[HALO:]


def _window_mean_minus(uu, row0, window):
    acc = uu
    span = 1
    while span < window:
        acc = acc + _shift_down(acc, span)
        span *= 2
    return acc[HALO:] * _inv_count(uu.shape[0] - HALO, row0, window) - uu[HALO:]


def _inv_count(rows, row0, window):
    t = row0 + lax.broadcasted_iota(jnp.int32, (rows, 1), 0)
    return 1.0 / jnp.minimum(t + 1, window).astype(F32)


def _whole(*shape):
    return pl.BlockSpec(shape, lambda i: (0,) * len(shape), pipeline_mode=pl.Buffered(1))


def _mixer_tile(i, x, g_ref, win_ref, mix_refs, carry_ref, proj_ref, hn_ref, is_conv):
    ts = x.shape[0]
    nsplit, _, e = proj_ref.shape
    gdim = e // N_POOL_GROUPS
    xh, _ = _rms_stats(x)
    hn = (xh * g_ref[...]).astype(BF16)
    hn_ref[...] = hn
    parts = []
    for k in range(nsplit):
        part = _dot(hn, win_ref[:, k * e:(k + 1) * e])
        proj_ref[k] = part.astype(BF16)
        parts.append(part)
    prev = carry_ref[...]
    if is_conv:
        b, c, v, z = parts
        w_ref, = mix_refs
        u = c * v
        u1, u2 = _conv_taps(u, prev)
        mixed = b * (w_ref[0:1, :] * u2 + w_ref[1:2, :] * u1 + w_ref[2:3, :] * u)
    else:
        u, z = parts
        wgrp_ref, sc_ref = mix_refs
        uu = jnp.concatenate([prev, u], axis=0)
        cols = []
        for gi, window in enumerate(POOL_WINDOWS):
            dg = _window_mean_minus(uu[:, gi * gdim:(gi + 1) * gdim], i * ts, window)
            cols.append(_dot(dg.astype(BF16), wgrp_ref[gi]))
        mixed = jnp.concatenate(cols, axis=1) * sc_ref[...]
    carry_ref[...] = u[ts - HALO:]
    return (_silu(z) * mixed).astype(BF16)


def _out_ple_tile(x, o, wo_ref, pn_ref, wg_ref, p_ref, wp_ref, h1_ref, gl_ref, pp_ref):
    h1 = x + _dot(o, wo_ref[...])
    h1_ref[...] = h1
    xh1, _ = _rms_stats(h1)
    gl = _dot((xh1 * pn_ref[...]).astype(BF16), wg_ref[...])
    pp = _dot(p_ref[...].astype(BF16), wp_ref[...])
    gl_ref[...] = gl.astype(BF16)
    pp_ref[...] = pp.astype(BF16)
    return h1 + _sigmoid(gl) * pp


def _layer_fwd_halves(h, g, w_in, mixer, w_out, pn, w_gate, p_all, w_proj, layer):
    s, d = h.shape
    n = w_in.shape[1]
    e = w_out.shape[0]
    nsplit = n // e
    pdim = p_all.shape[2]
    is_conv = mixer[0] == "conv"
    ts = min(FWD_ROW_TILE, s)
    params = mixer[1:]
    row = lambda width: pl.BlockSpec((ts, width), lambda i: (i, 0))

    def mixer_body(*refs):
        h_ref, g_ref, win_ref = refs[:3]
        mix_refs = refs[3:3 + len(params)]
        proj_ref, hn_ref, o_ref, carry_ref = refs[3 + len(params):]
        i = pl.program_id(0)

        @pl.when(i == 0)
        def _():
            carry_ref[...] = jnp.zeros_like(carry_ref)

        o_ref[...] = _mixer_tile(i, h_ref[...], g_ref, win_ref, mix_refs, carry_ref, proj_ref, hn_ref, is_conv)

    proj, hn, o = pl.pallas_call(
        mixer_body, name="mixer_fwd",
        grid=(s // ts,),
        in_specs=[row(d), _whole(1, d), _whole(d, n)] + [_whole(*a.shape) for a in params],
        out_specs=[pl.BlockSpec((nsplit, ts, e), lambda i: (0, i, 0)), row(d), row(e)],
        out_shape=[jax.ShapeDtypeStruct((nsplit, s, e), BF16), jax.ShapeDtypeStruct((s, d), BF16),
                   jax.ShapeDtypeStruct((s, e), BF16)],
        scratch_shapes=[pltpu.VMEM((HALO, e), F32)],
        compiler_params=_cparams("arbitrary"),
    )(h, g, w_in, *params)

    def out_body(h_ref, o_ref, wo_ref, pn_ref, wg_ref, p_ref, wp_ref, h1_ref, h2_ref, gl_ref, pp_ref):
        h2_ref[...] = _out_ple_tile(h_ref[...], o_ref[...], wo_ref, pn_ref, wg_ref, p_ref, wp_ref,
                                    h1_ref, gl_ref, pp_ref)

    h1, h2, gl, pp = pl.pallas_call(
        out_body, name="out_ple_fwd",
        grid=(s // ts,),
        in_specs=[row(d), row(e), _whole(e, d), _whole(1, d), _whole(d, d),
                  pl.BlockSpec((None, ts, pdim), lambda i: (layer, i, 0)), _whole(pdim, d)],
        out_specs=[row(d), row(d), row(d), row(d)],
        out_shape=[jax.ShapeDtypeStruct((s, d), F32), jax.ShapeDtypeStruct((s, d), F32),
                   jax.ShapeDtypeStruct((s, d), BF16), jax.ShapeDtypeStruct((s, d), BF16)],
        compiler_params=_cparams("parallel"),
    )(h, o, w_out, pn, w_gate, p_all, w_proj)
    return h2, (proj, hn, o, h1, gl, pp)


def _layer_fwd(h, g, w_in, mixer, w_out, pn, w_gate, p_all, w_proj, layer, head=None):
    s, d = h.shape
    n = w_in.shape[1]
    e = w_out.shape[0]
    nsplit = n // e
    pdim = p_all.shape[2]
    is_conv = mixer[0] == "conv"
    ts = min(FWD_ROW_TILE, s)
    params = mixer[1:]
    head = tuple(head or ())

    def body(*refs):
        h_ref, g_ref, win_ref = refs[:3]
        mix_refs = refs[3:3 + len(params)]
        wo_ref, pn_ref, wg_ref, p_ref, wp_ref = refs[3 + len(params):8 + len(params)]
        head_refs = refs[8 + len(params):8 + len(params) + len(head)]
        proj_ref, hn_ref, o_ref, h1_ref, h2_ref, gl_ref, pp_ref = refs[8 + len(params) + len(head):][:7]
        carry_ref = refs[-1]
        i = pl.program_id(0)

        @pl.when(i == 0)
        def _():
            carry_ref[...] = jnp.zeros_like(carry_ref)

        x = h_ref[...]
        o = _mixer_tile(i, x, g_ref, win_ref, mix_refs, carry_ref, proj_ref, hn_ref, is_conv)
        o_ref[...] = o
        h2 = _out_ple_tile(x, o, wo_ref, pn_ref, wg_ref, p_ref, wp_ref, h1_ref, gl_ref, pp_ref)
        if not head:
            h2_ref[...] = h2
            return
        t_ref, gain_ref = head_refs
        loss_ref, dgain_ref = refs[-3], refs[-2]

        @pl.when(i == 0)
        def _():
            loss_ref[...] = jnp.zeros_like(loss_ref)
            dgain_ref[...] = jnp.zeros_like(dgain_ref)

        gain = gain_ref[...]
        yh, r = _rms_stats(h2)
        err = yh * gain - t_ref[...]
        loss_ref[...] += jnp.full(loss_ref.shape, (0.5 / d) * jnp.sum(err * err), F32)
        dy = err * (1.0 / d)
        dgain_ref[...] += jnp.sum(dy * yh, axis=0, keepdims=True)
        h2_ref[...] = _rms_bwd(dy, yh, r, gain)

    row = lambda width: pl.BlockSpec((ts, width), lambda i: (i, 0))
    mix_specs = [_whole(*a.shape) for a in params]
    head_specs = [row(d), _whole(1, d)] if head else []
    head_out_specs = [_whole(1, 128), _whole(1, d)] if head else []
    head_out_shape = [jax.ShapeDtypeStruct((1, 128), F32), jax.ShapeDtypeStruct((1, d), F32)] if head else []
    outs = pl.pallas_call(
        body, name="layer_fwd",
        grid=(s // ts,),
        in_specs=[row(d), _whole(1, d), _whole(d, n)] + mix_specs
        + [_whole(e, d), _whole(1, d), _whole(d, d),
           pl.BlockSpec((None, ts, pdim), lambda i: (layer, i, 0)), _whole(pdim, d)] + head_specs,
        out_specs=[pl.BlockSpec((nsplit, ts, e), lambda i: (0, i, 0)),
                   row(d), row(e), row(d), row(d), row(d), row(d)] + head_out_specs,
        out_shape=[jax.ShapeDtypeStruct((nsplit, s, e), BF16), jax.ShapeDtypeStruct((s, d), BF16),
                   jax.ShapeDtypeStruct((s, e), BF16), jax.ShapeDtypeStruct((s, d), F32),
                   jax.ShapeDtypeStruct((s, d), F32), jax.ShapeDtypeStruct((s, d), BF16),
                   jax.ShapeDtypeStruct((s, d), BF16)] + head_out_shape,
        scratch_shapes=[pltpu.VMEM((HALO, e), F32)],
        compiler_params=_cparams("arbitrary"),
    )(h, g, w_in, *params, w_out, pn, w_gate, p_all, w_proj, *head)
    proj, hn, o, h1, h2, gl, pp = outs[:7]
    return (h2, *outs[7:]) if head else h2, (proj, hn, o, h1, gl, pp)


def _out_ple_bwd(dh2, gl, pp, h1, p_all, o, pn, wgate, wout, layer):
    s, d = dh2.shape
    e = o.shape[1]
    pdim = p_all.shape[2]
    ts = min(BWD_ROW_TILE, s)
    last = s // ts - 1

    def body(dh2_ref, gl_ref, pp_ref, h1_ref, p_ref, o_ref, pn_ref, wg_ref, wo_ref,
             dh1_ref, do_ref, dwp_ref, dwg_ref, dwo_ref, dpn_ref, awp, awg, awo):
        i = pl.program_id(0)

        @pl.when(i == 0)
        def _():
            awp[...] = jnp.zeros_like(awp)
            awg[...] = jnp.zeros_like(awg)
            awo[...] = jnp.zeros_like(awo)
            dpn_ref[...] = jnp.zeros_like(dpn_ref)

        dh2 = dh2_ref[...]
        gate = _sigmoid(gl_ref[...].astype(F32))
        dpp = dh2 * gate
        dgl = (dpp * pp_ref[...].astype(F32) * (1.0 - gate)).astype(BF16)
        dpp = dpp.astype(BF16)
        xh, r = _rms_stats(h1_ref[...])
        pn = pn_ref[...]
        awp[...] += _dot_tn(p_ref[...].astype(BF16), dpp)
        awg[...] += _dot_tn((xh * pn).astype(BF16), dgl)
        dr = _dot_nt(dgl, wg_ref[...])
        dpn_ref[...] += jnp.sum(dr * xh, axis=0, keepdims=True)
        dh1 = dh2 + _rms_bwd(dr, xh, r, pn)
        dh1_ref[...] = dh1
        dh1b = dh1.astype(BF16)
        do_ref[...] = _dot_nt(dh1b, wo_ref[...]).astype(BF16)
        awo[...] += _dot_tn(o_ref[...], dh1b)

        @pl.when(i == last)
        def _():
            dwp_ref[...] = awp[...].astype(BF16)
            dwg_ref[...] = awg[...].astype(BF16)
            dwo_ref[...] = awo[...].astype(BF16)

    row = lambda width: pl.BlockSpec((ts, width), lambda i: (i, 0))
    return pl.pallas_call(
        body, name="out_ple_bwd",
        grid=(s // ts,),
        in_specs=[row(d), row(d), row(d), row(d),
                  pl.BlockSpec((None, ts, pdim), lambda i: (layer, i, 0)),
                  row(e), _whole(1, d), _whole(d, d), _whole(e, d)],
        out_specs=[row(d), row(e), _whole(pdim, d), _whole(d, d), _whole(e, d), _whole(1, d)],
        out_shape=[jax.ShapeDtypeStruct((s, d), F32), jax.ShapeDtypeStruct((s, e), BF16),
                   jax.ShapeDtypeStruct((pdim, d), BF16), jax.ShapeDtypeStruct((d, d), BF16),
                   jax.ShapeDtypeStruct((e, d), BF16), jax.ShapeDtypeStruct((1, d), F32)],
        scratch_shapes=[pltpu.VMEM((pdim, d), F32), pltpu.VMEM((d, d), F32), pltpu.VMEM((e, d), F32)],
        compiler_params=_cparams("arbitrary"),
    )(dh2, gl, pp, h1, p_all, o, pn, wgate, wout)


def _mixer_bwd(do, proj, hn, mixer, w_in, h, g, dh1):
    s, d = h.shape
    nsplit, _, e = proj.shape
    gdim = e // N_POOL_GROUPS
    is_conv = mixer[0] == "conv"
    ts = min(CONV_BWD_ROW_TILE if is_conv else POOL_BWD_ROW_TILE, s)
    nt = s // ts
    params = mixer[1:]
    n_mix_out = 1 if is_conv else 2

    def body(*refs):
        refs = list(refs)
        take = lambda n: [refs.pop(0) for _ in range(n)]
        do_ref, p_ref, ph_ref = take(3)
        mix_refs = take(len(params))
        win_ref, h_ref, g_ref, dh1_ref, hn_ref = take(5)
        dwin_ref, dh_ref, dg_ref = take(3)
        mix_out = take(n_mix_out)
        carry_ref, dp_ref, acc_ref = take(3)
        i = pl.program_id(0)
        tile = nt - 1 - i

        @pl.when(i == 0)
        def _():
            for ref in [carry_ref, dg_ref, acc_ref] + mix_out[-1:] + refs:
                ref[...] = jnp.zeros_like(ref)

        dof = do_ref[...].astype(F32)
        nxt = carry_ref[...]
        if is_conv:
            w_ref, = mix_refs
            dw_ref, = mix_out
            w0, w1, w2 = w_ref[0:1, :], w_ref[1:2, :], w_ref[2:3, :]
            b, c, v, z = [p_ref[k].astype(F32) for k in range(4)]
            u = c * v
            u_prev = jnp.where(tile == 0, 0.0, ph_ref[1].astype(F32) * ph_ref[2].astype(F32))
            u1, u2 = _conv_taps(u, u_prev)
            conv = w0 * u2 + w1 * u1 + w2 * u
            sig = _sigmoid(z)
            sz = z * sig
            dy = dof * sz
            dp_ref[3] = (dof * (b * conv) * (sig + sz * (1.0 - sig))).astype(BF16)
            dp_ref[0] = (dy * conv).astype(BF16)
            dconv = dy * b
            dw_ref[0:1, :] += jnp.sum(dconv * u2, axis=0, keepdims=True)
            dw_ref[1:2, :] += jnp.sum(dconv * u1, axis=0, keepdims=True)
            dw_ref[2:3, :] += jnp.sum(dconv * u, axis=0, keepdims=True)
            dcc = jnp.concatenate([dconv, nxt], axis=0)
            du = w2 * dconv + w1 * _shift_up(dcc, 1)[:ts] + w0 * _shift_up(dcc, 2)[:ts]
            carry_ref[...] = dconv[:HALO]
            dp_ref[1] = (du * v).astype(BF16)
            dp_ref[2] = (du * c).astype(BF16)
        else:
            wgrp_ref, sc_ref = mix_refs
            dsc_ref = mix_out[1]
            agrp, = refs
            u = p_ref[0].astype(F32)
            z = p_ref[1].astype(F32)
            u_prev = jnp.where(tile == 0, 0.0, ph_ref[0].astype(F32))
            uu = jnp.concatenate([u_prev, u], axis=0)
            sig = _sigmoid(z)
            sz = z * sig
            dm = dof * sz
            dsilu = dof * (sig + sz * (1.0 - sig))
            for gi, window in enumerate(POOL_WINDOWS):
                cols = slice(gi * gdim, (gi + 1) * gdim)
                w = wgrp_ref[gi]
                scale = sc_ref[:, cols]
                db = _window_mean_minus(uu[:, cols], tile * ts, window).astype(BF16)
                mr = _dot(db, w)
                dp_ref[1, :, cols] = (dsilu[:, cols] * (mr * scale)).astype(BF16)
                dmg = dm[:, cols]
                dmr = (dmg * scale).astype(BF16)
                agrp[gi] += _dot_tn(db, dmr)
                dsc_ref[:, cols] += jnp.sum(dmg * mr, axis=0, keepdims=True)
                dd = _dot_nt(dmr, w)
                ddq = dd * _inv_count(ts, tile * ts, window)
                acc = jnp.concatenate([ddq, nxt[:, cols]], axis=0)
                span = 1
                while span < window:
                    acc = acc + _shift_up(acc, span)
                    span *= 2
                carry_ref[:, cols] = ddq[:HALO]
                dp_ref[0, :, cols] = (acc[:ts] - dd).astype(BF16)

        dhn = _dot_nt(dp_ref[0], win_ref[:, 0:e])
        for k in range(1, nsplit):
            dhn += _dot_nt(dp_ref[k], win_ref[:, k * e:(k + 1) * e])
        xh, r = _rms_stats(h_ref[...])
        dg_ref[...] += jnp.sum(dhn * xh, axis=0, keepdims=True)
        dh_ref[...] = dh1_ref[...] + _rms_bwd(dhn, xh, r, g_ref[...])
        hn_tile = hn_ref[...]
        for k in range(nsplit):
            acc_ref[:, k * e:(k + 1) * e] += _dot_tn(hn_tile, dp_ref[k])

        @pl.when(i == nt - 1)
        def _():
            dwin_ref[...] = acc_ref[...].astype(BF16)
            if not is_conv:
                mix_out[0][...] = refs[0][...].astype(BF16)

    rev = lambda width: pl.BlockSpec((ts, width), lambda i: (nt - 1 - i, 0))
    halo_blocks = ts // HALO
    in_specs = [rev(e),
                pl.BlockSpec((nsplit, ts, e), lambda i: (0, nt - 1 - i, 0)),
                pl.BlockSpec((nsplit, HALO, e), lambda i: (0, jnp.maximum((nt - 1 - i) * halo_blocks - 1, 0), 0))]
    in_specs += [_whole(*a.shape) for a in params]
    in_specs += [_whole(d, nsplit * e), rev(d), _whole(1, d), rev(d), rev(d)]
    out_specs = [_whole(d, nsplit * e), rev(d), _whole(1, d)]
    out_shape = [jax.ShapeDtypeStruct((d, nsplit * e), BF16), jax.ShapeDtypeStruct((s, d), F32),
                 jax.ShapeDtypeStruct((1, d), F32)]
    scratch = [pltpu.VMEM((HALO, e), F32), pltpu.VMEM((nsplit, ts, e), BF16), pltpu.VMEM((d, nsplit * e), F32)]
    if is_conv:
        out_specs += [_whole(3, e)]
        out_shape += [jax.ShapeDtypeStruct((3, e), F32)]
    else:
        out_specs += [_whole(N_POOL_GROUPS, gdim, gdim), _whole(1, e)]
        out_shape += [jax.ShapeDtypeStruct((N_POOL_GROUPS, gdim, gdim), BF16), jax.ShapeDtypeStruct((1, e), F32)]
        scratch += [pltpu.VMEM((N_POOL_GROUPS, gdim, gdim), F32)]
    return pl.pallas_call(
        body, name="mixer_bwd",
        grid=(nt,),
        in_specs=in_specs, out_specs=out_specs, out_shape=out_shape, scratch_shapes=scratch,
        compiler_params=_cparams("arbitrary"),
    )(do, proj, proj, *params, w_in, h, g, dh1, hn)


def _forward_backward(xs, ps, target, full, conv_w, scale_w, norm_mix, ple_norm, final_norm, exchange):
    depth = len(full)
    row = lambda a, i: a[i][None, :]
    mixer_of = lambda i: ("conv", conv_w[i // 2]) if i % 2 == 0 else ("pool", full[i]["w_grp"], row(scale_w, i // 2))

    saved = []
    h = xs
    for i in range(depth):
        w = full[i]
        head = (target, final_norm[None, :]) if i == depth - 1 else None
        args = (h, row(norm_mix, i), w["w_in"], mixer_of(i), w["w_out"], row(ple_norm, i), w["gate"], ps, w["proj"], i)
        h_next, acts = _layer_fwd_halves(*args) if i == 0 and not head else _layer_fwd(*args, head)
        saved.append((h, *acts))
        h = h_next
    dh, loss_row, d_final = h

    d_norm, d_ple_norm, d_conv, d_scale, sent = [None] * depth, [None] * depth, [], [], [None] * depth
    for i in reversed(range(depth)):
        w = full[i]
        h_in, proj, hn, o, h1, gl, pp = saved[i]
        g = {}
        dh1, do, g["proj"], g["gate"], g["w_out"], d_ple_norm[i] = _out_ple_bwd(
            dh, gl, pp, h1, ps, o, row(ple_norm, i), w["gate"], w["w_out"], i)
        g["w_in"], dh, d_norm[i], *mixer_grads = _mixer_bwd(
            do, proj, hn, mixer_of(i), w["w_in"], h_in, row(norm_mix, i), dh1)
        if i % 2 == 0:
            d_conv.insert(0, mixer_grads[0])
        else:
            g["w_grp"] = mixer_grads[0]
            d_scale.insert(0, mixer_grads[1])
        sent[i] = exchange(i, g)
    return loss_row, dh, sent, (d_norm, d_ple_norm, d_final, d_conv, d_scale)


VMEM_SPEC = pl.BlockSpec(memory_space=pltpu.VMEM)

FLIPS = [(fx, fy, fc) for fx in (0, 1) for fy in (0, 1) for fc in (0, 1)][1:]
SHARD_AXIS = {"w_in": 1, "w_out": 0, "w_grp": 1, "gate": 0, "proj": 1}


def _my_place():
    return lax.axis_index("x"), lax.axis_index("y"), lax.axis_index("c")


def _position(place):
    x, y, c = place
    return 4 * x + 2 * y + c


def _flip(place, flips):
    return tuple(1 - v if f else v for v, f in zip(place, flips))


def _shard_of(ref, axis, pos, n):
    idx = [slice(None)] * len(ref.shape)
    idx[axis] = pl.ds(pl.multiple_of(pos * n, n), n)
    return ref.at[tuple(idx)]


def _sequencer_mesh():
    return plsc.ScalarSubcoreMesh(axis_name="sequencer", num_cores=1)


def _handshake(peers):
    barrier = pltpu.get_barrier_semaphore()
    for peer in peers:
        pl.semaphore_signal(barrier, inc=1, device_id=peer, device_id_type=MESH)
    pl.semaphore_wait(barrier, len(peers))


def _all_gather_layer(shards, collective_id):
    names = list(shards)
    nt = len(names)
    axes = [SHARD_AXIS[k] for k in names]
    widths = [shards[k].shape[SHARD_AXIS[k]] for k in names]

    def full_shape(k):
        shp = list(shards[k].shape)
        shp[SHARD_AXIS[k]] *= N_DEV
        return tuple(shp)

    def body(*refs):
        ins, outs = refs[:nt], refs[nt:2 * nt]
        send_sems, recv_sems, local_sem = refs[2 * nt:]
        me = _my_place()
        x, y, c = me
        sibling = (x, y, 1 - c)
        flip = lambda v, f: v + f - 2 * v * f
        neighbour = lambda core, fx: (flip(x, fx), flip(y, 1 - fx), core)
        first, second = neighbour(c, c), neighbour(c, 1 - c)
        diagonal = (1 - x, 1 - y, c)
        _handshake([sibling, first, second])

        def block(t, place):
            return _shard_of(outs[t], axes[t], _position(place), widths[t])

        def copy(t, k, place, to, src=None):
            return pltpu.make_async_remote_copy(
                src_ref=block(t, place) if src is None else src, dst_ref=block(t, place),
                send_sem=send_sems.at[k], recv_sem=recv_sems.at[k], device_id=to, device_id_type=MESH)

        everything = lambda make: [make(t) for t in range(nt)]
        mine = everything(lambda t: pltpu.make_async_copy(ins[t], block(t, me), local_sem))
        sent = (everything(lambda t: copy(t, 1, me, first, src=ins[t]))
                + everything(lambda t: copy(t, 2, me, second, src=ins[t]))
                + everything(lambda t: copy(t, 0, me, sibling, src=ins[t])))
        for cp in mine + sent:
            cp.start()
        for k, arrived, onward in ((1, first, second), (2, second, None), (3, diagonal, None)):
            for t in range(nt):
                copy(t, k, arrived, me).wait_recv()
            if onward is not None:
                sent += everything(lambda t: copy(t, 3, arrived, onward))
            sent += everything(lambda t: copy(t, 3 + k, arrived, sibling))
            for cp in sent[-nt * (1 + (onward is not None)):]:
                cp.start()
        for k, place in ((0, sibling), (4, neighbour(1 - c, 1 - c)), (5, neighbour(1 - c, c)), (6, (1 - x, 1 - y, 1 - c))):
            for t in range(nt):
                copy(t, k, place, me).wait_recv()
        for cp in sent:
            cp.wait_send()
        for cp in mine:
            cp.wait()

    outs = pl.kernel(
        body, name=f"all_gather_layer_{collective_id}",
        out_type=[jax.ShapeDtypeStruct(full_shape(k), shards[k].dtype) for k in names],
        mesh=_sequencer_mesh(),
        scratch_types=[pltpu.SemaphoreType.DMA((7,)), pltpu.SemaphoreType.DMA((7,)), pltpu.SemaphoreType.DMA],
        compiler_params=pltpu.CompilerParams(collective_id=collective_id),
    )(*[shards[k] for k in names])
    return dict(zip(names, outs))


def _exchange_layer(grads, collective_id):
    names = list(grads)
    nt = len(names)
    axes = [SHARD_AXIS[k] for k in names]
    widths = [grads[k].shape[SHARD_AXIS[k]] // N_DEV for k in names]

    def slot_shape(t):
        shp = list(grads[names[t]].shape)
        shp[axes[t]] = widths[t]
        return (N_DEV, *shp)

    def body(*refs):
        ins, outs = refs[:nt], refs[nt:2 * nt]
        send_sems, recv_sems, local_sem = refs[2 * nt:]
        me = _my_place()
        mine = _position(me)
        _handshake([_flip(me, flips) for flips in FLIPS])
        local = [pltpu.make_async_copy(_shard_of(ins[t], axes[t], mine, widths[t]), outs[t].at[mine], local_sem)
                 for t in range(nt)]
        for cp in local:
            cp.start()
        copies = []
        for k, flips in enumerate(FLIPS):
            peer = _flip(me, flips)
            for t in range(nt):
                cp = pltpu.make_async_remote_copy(
                    src_ref=_shard_of(ins[t], axes[t], _position(peer), widths[t]), dst_ref=outs[t].at[mine],
                    send_sem=send_sems.at[k], recv_sem=recv_sems.at[k], device_id=peer, device_id_type=MESH)
                cp.start()
                copies.append(cp)
        for cp in copies:
            cp.wait()
        for cp in local:
            cp.wait()

    outs = pl.kernel(
        body, name=f"exchange_layer_{collective_id}",
        out_type=[jax.ShapeDtypeStruct(slot_shape(t), BF16) for t in range(nt)],
        mesh=_sequencer_mesh(),
        scratch_types=[pltpu.SemaphoreType.DMA((7,)), pltpu.SemaphoreType.DMA((7,)), pltpu.SemaphoreType.DMA],
        compiler_params=pltpu.CompilerParams(collective_id=collective_id),
    )(*[grads[k] for k in names])
    return dict(zip(names, outs))


def _gather_rows(buf, reduce):
    r, c = buf.shape

    def body(in_ref, out_ref, *scratch):
        if reduce:
            all_ref, send_sems, recv_sems = scratch
        else:
            all_ref = out_ref
            send_sems, recv_sems = scratch
        me = _my_place()
        all_ref[_position(me)] = in_ref[...]
        copies = []
        for k, flips in enumerate(FLIPS):
            cp = pltpu.make_async_remote_copy(
                src_ref=in_ref, dst_ref=all_ref.at[_position(me)],
                send_sem=send_sems.at[k], recv_sem=recv_sems.at[k], device_id=_flip(me, flips), device_id_type=MESH)
            cp.start()
            copies.append(cp)
        for cp in copies:
            cp.wait()
        if reduce:
            total = all_ref[0]
            for j in range(1, N_DEV):
                total = total + all_ref[j]
            out_ref[...] = total

    return pl.pallas_call(
        body, name="sum_rows" if reduce else "gather_rows",
        in_specs=[VMEM_SPEC], out_specs=VMEM_SPEC,
        out_shape=jax.ShapeDtypeStruct((r, c) if reduce else (N_DEV, r, c), F32),
        scratch_shapes=([pltpu.VMEM((N_DEV, r, c), F32)] if reduce else [])
        + [pltpu.SemaphoreType.DMA((7,)), pltpu.SemaphoreType.DMA((7,))],
    )(buf)


def _adamw_math(w, g, m, v):
    m = ADAM_B1 * m + (1.0 - ADAM_B1) * g
    v = ADAM_B2 * v + (1.0 - ADAM_B2) * (g * g)
    m_hat = m / (1.0 - ADAM_B1 ** ADAM_STEP)
    v_hat = v / (1.0 - ADAM_B2 ** ADAM_STEP)
    delta = -ADAM_LR * (m_hat / (jnp.sqrt(v_hat) + ADAM_EPS) + ADAM_WD * w)
    return delta, m, v


def _run_behind(x, token):
    def body(x_ref, token_ref, out_ref):
        out_ref[...] = jnp.zeros_like(out_ref)

    any_spec = pl.BlockSpec(memory_space=pl.ANY)
    return pl.pallas_call(
        body, name="run_behind",
        in_specs=[any_spec, any_spec], out_specs=VMEM_SPEC,
        out_shape=jax.ShapeDtypeStruct((8, 128), F32),
    )(x, token)


def _adamw_pieces(pieces, w, m, v, after, layer=None, into=()):
    shape = w.shape
    nl = shape[0]
    cols = shape[-1]
    rows = w.size // (nl * cols)
    first, nh = (0, nl) if layer is None else (layer, 1)
    tr = min(ADAMW_BLOCK_ROWS // nh, rows // 2)
    flat3 = lambda a: a.reshape(nl, rows, cols)
    into = [flat3(a) for a in into]

    def body(*refs):
        p_refs = refs[:nh]
        w_ref, m_ref, v_ref = refs[nh:nh + 3]
        g_ref, d_ref, nm_ref, nv_ref = refs[-4:]
        for l in range(nh):
            g = p_refs[l][0].astype(F32)
            for j in range(1, N_DEV):
                g = g + p_refs[l][j].astype(F32)
            g_ref[l] = g
            d_ref[l], nm_ref[l], nv_ref[l] = _adamw_math(w_ref[l], g, m_ref[l], v_ref[l])

    blk = pl.BlockSpec((nh, tr, cols), lambda i: (first, i, 0))
    any_spec = pl.BlockSpec(memory_space=pl.ANY)
    outs = pl.pallas_call(
        body, name="adamw_pieces",
        grid=(rows // tr,),
        in_specs=[pl.BlockSpec((N_DEV, tr, cols), lambda i: (0, i, 0))] * nh
        + [blk, blk, blk, any_spec] + [any_spec] * len(into),
        out_specs=[blk] * 4,
        out_shape=[jax.ShapeDtypeStruct((nl, rows, cols), F32)] * 4,
        input_output_aliases={nh + 4 + k: k for k in range(len(into))},
        compiler_params=_cparams("parallel"),
    )(*[a.reshape(N_DEV, rows, cols) for a in pieces], flat3(w), flat3(m), flat3(v), after, *into)
    return [a.reshape(shape) for a in outs]


def _adamw_small(g, w, m, v):
    shape = w.shape
    two = lambda a: a.reshape(-1, shape[-1])

    def body(g_ref, w_ref, m_ref, v_ref, d_ref, nm_ref, nv_ref):
        d_ref[...], nm_ref[...], nv_ref[...] = _adamw_math(w_ref[...], g_ref[...], m_ref[...], v_ref[...])

    outs = pl.pallas_call(
        body, name="adamw_small",
        in_specs=[VMEM_SPEC] * 4, out_specs=[VMEM_SPEC] * 3,
        out_shape=[jax.ShapeDtypeStruct(two(w).shape, F32)] * 3,
    )(two(g), two(w), two(m), two(v))
    return [a.reshape(shape) for a in outs]


WEIGHTS = ("norm_mix", "a_w_in", "a_w_conv", "a_w_out", "b_w_in", "b_w_grp", "b_scale", "b_w_out",
           "ple_norm", "ple_w_gate", "ple_w_proj", "final_norm")
SMALL_ROWS = 24
GATHER_ID = 0
EXCHANGE_ID = 4
LAST_EXCHANGE_ID = 8
REST_GATHER_ID = 9


def kernel(x, p, norm_mix, a_w_in, a_w_conv, a_w_out, b_w_in, b_w_grp, b_scale, b_w_out, ple_norm, ple_w_gate, ple_w_proj, final_norm, loss_target, m_norm_mix, m_a_w_in, m_a_w_conv, m_a_w_out, m_b_w_in, m_b_w_grp, m_b_scale, m_b_w_out, m_ple_norm, m_ple_w_gate, m_ple_w_proj, m_final_norm, v_norm_mix, v_a_w_in, v_a_w_conv, v_a_w_out, v_b_w_in, v_b_w_grp, v_b_scale, v_b_w_out, v_ple_norm, v_ple_w_gate, v_ple_w_proj, v_final_norm):
    wts = dict(norm_mix=norm_mix, a_w_in=a_w_in, a_w_conv=a_w_conv, a_w_out=a_w_out, b_w_in=b_w_in, b_w_grp=b_w_grp,
               b_scale=b_scale, b_w_out=b_w_out, ple_norm=ple_norm, ple_w_gate=ple_w_gate, ple_w_proj=ple_w_proj,
               final_norm=final_norm)
    mom = dict(norm_mix=m_norm_mix, a_w_in=m_a_w_in, a_w_conv=m_a_w_conv, a_w_out=m_a_w_out, b_w_in=m_b_w_in,
               b_w_grp=m_b_w_grp, b_scale=m_b_scale, b_w_out=m_b_w_out, ple_norm=m_ple_norm, ple_w_gate=m_ple_w_gate,
               ple_w_proj=m_ple_w_proj, final_norm=m_final_norm)
    var = dict(norm_mix=v_norm_mix, a_w_in=v_a_w_in, a_w_conv=v_a_w_conv, a_w_out=v_a_w_out, b_w_in=v_b_w_in,
               b_w_grp=v_b_w_grp, b_scale=v_b_scale, b_w_out=v_b_w_out, ple_norm=v_ple_norm, ple_w_gate=v_ple_w_gate,
               ple_w_proj=v_ple_w_proj, final_norm=v_final_norm)
    d = x.shape[2]
    depth = norm_mix.shape[0]
    n_a, n_b = a_w_conv.shape[0], b_scale.shape[0]
    cw = a_w_conv.shape[2]
    pos = _position(_my_place())

    def layer_matrices(i):
        j = i // 2
        mixer = {"w_in": ("a_w_in", j), "w_out": ("a_w_out", j)} if i % 2 == 0 else \
                {"w_in": ("b_w_in", j), "w_grp": ("b_w_grp", j), "w_out": ("b_w_out", j)}
        return {**mixer, "gate": ("ple_w_gate", i), "proj": ("ple_w_proj", i)}

    def gathered(i):
        shards = {k: wts[name][idx].astype(BF16) for k, (name, idx) in layer_matrices(i).items()}
        if i > 0:
            return _all_gather_layer(shards, GATHER_ID + i)
        first = {"w_in": shards.pop("w_in")}
        return {**_all_gather_layer(first, GATHER_ID), **_all_gather_layer(shards, REST_GATHER_ID)}

    full = [gathered(i) for i in range(depth)]
    vec_rows = jnp.concatenate([a_w_conv.reshape(-1, cw), b_scale], axis=0)
    vecs = _gather_rows(vec_rows, reduce=False)
    n_conv = 3 * n_a
    conv_w = vecs[:, :n_conv].transpose(1, 0, 2).reshape(n_a, 3, N_DEV * cw)
    scale_w = vecs[:, n_conv:].transpose(1, 0, 2).reshape(n_b, N_DEV * cw)

    def exchange(i, g):
        if i > 0:
            return _exchange_layer(g, EXCHANGE_ID + i)
        early = {k: a for k, a in g.items() if k != "w_in"}
        return {**_exchange_layer(early, EXCHANGE_ID), **_exchange_layer({"w_in": g["w_in"]}, LAST_EXCHANGE_ID)}

    loss_row, dx, sent, (d_norm, d_ple_norm, d_final, d_conv, d_scale) = _forward_backward(
        x[0], p[:, 0], loss_target[0], full, conv_w, scale_w, norm_mix, ple_norm, final_norm, exchange)
    pieces = {name: [None] * wts[name].shape[0] for name in WEIGHTS if wts[name].ndim >= 3 and name != "a_w_conv"}
    for i in range(depth):
        for k, (name, idx) in layer_matrices(i).items():
            pieces[name][idx] = sent[i][k]

    pad = lambda a: jnp.pad(a, ((0, 0), (0, d - a.shape[1])))
    small = jnp.concatenate(d_norm + d_ple_norm + [d_final] + d_conv + d_scale + [pad(loss_row)], axis=0)
    small = jnp.pad(small, ((0, SMALL_ROWS - small.shape[0]), (0, 0)))
    total = _gather_rows(small, reduce=True)
    o = 0
    gsum = {}
    gsum["norm_mix"] = total[o:o + depth]; o += depth
    gsum["ple_norm"] = total[o:o + depth]; o += depth
    gsum["final_norm"] = total[o]; o += 1
    conv_full = total[o:o + n_conv].reshape(n_a, 3, d); o += n_conv
    scale_full = total[o:o + n_b]; o += n_b
    loss = total[o, 0]
    gsum["a_w_conv"] = lax.dynamic_slice_in_dim(conv_full, pos * cw, cw, axis=2)
    gsum["b_scale"] = lax.dynamic_slice_in_dim(scale_full, pos * cw, cw, axis=1)

    token = total
    for i in reversed(range(depth)):
        token = _run_behind(sent[i]["w_out"], token)
    last_token = _run_behind(sent[0]["w_in"], token)
    grad, delta, new_m, new_v = {}, {}, {}, {}
    for k in sorted(WEIGHTS, key=lambda name: name == "a_w_in"):
        if k == "a_w_in":
            upper = [None] * 4
            for j in reversed(range(1, n_a)):
                upper = _adamw_pieces([pieces[k][j]], wts[k], mom[k], var[k], token, j, [a for a in upper if a is not None])
            grad[k], delta[k], new_m[k], new_v[k] = _adamw_pieces(
                [pieces[k][0]], wts[k], mom[k], var[k], last_token, 0, upper)
        elif k in pieces:
            grad[k], delta[k], new_m[k], new_v[k] = _adamw_pieces(pieces[k], wts[k], mom[k], var[k], token)
        else:
            grad[k] = gsum[k]
            delta[k], new_m[k], new_v[k] = _adamw_small(gsum[k], wts[k], mom[k], var[k])
    return (loss, dx[None], *[grad[k] for k in WEIGHTS], *[delta[k] for k in WEIGHTS],
            *[new_m[k] for k in WEIGHTS], *[new_v[k] for k in WEIGHTS])
```

```python
import jax
import jax.numpy as jnp
from jax import lax
from jax.experimental import pallas as pl
from jax.experimental.pallas import tpu as pltpu
from jax.experimental.pallas import tpu_sc as plsc

F32 = jnp.float32
BF16 = jnp.bfloat16
MESH = pl.DeviceIdType.MESH

RMS_EPS = 1e-6
POOL_WINDOWS = (2, 4, 8, 16)
N_POOL_GROUPS = len(POOL_WINDOWS)
ADAM_LR = 0.001
ADAM_B1 = 0.9
ADAM_B2 = 0.999
ADAM_EPS = 1e-08
ADAM_WD = 0.01
ADAM_STEP = 10
N_DEV = 8

HALO = 16
FWD_ROW_TILE = 512
POOL_BWD_ROW_TILE = 512
CONV_BWD_ROW_TILE = 256
ADAMW_BLOCK_ROWS = 256
BWD_ROW_TILE = 512
VMEM_LIMIT = 56 * 1024 * 1024


def _cparams(*sem):
    return pltpu.CompilerParams(dimension_semantics=sem, vmem_limit_bytes=VMEM_LIMIT)


def _dot(a, b):
    return jnp.dot(a, b, preferred_element_type=F32)


def _dot_nt(a, b):
    return lax.dot_general(a, b, (((1,), (1,)), ((), ())), preferred_element_type=F32)


def _dot_tn(a, b):
    return lax.dot_general(a, b, (((0,), (0,)), ((), ())), preferred_element_type=F32)


def _rms_stats(x):
    r = lax.rsqrt(jnp.mean(x * x, axis=-1, keepdims=True) + RMS_EPS)
    return x * r, r


def _rms_bwd(dy, xh, r, g):
    a = dy * g
    return r * (a - xh * jnp.mean(a * xh, axis=-1, keepdims=True))


def _sigmoid(x):
    return 0.5 * jnp.tanh(0.5 * x) + 0.5


def _shift_down(x, k):
    return pltpu.roll(x, k, 0)


def _shift_up(x, k):
    return pltpu.roll(x, x.shape[0] - k, 0)


def _conv_taps(u, u_prev):
    uu = jnp.concatenate([u_prev, u], axis=0)
    return _shift_down(uu, 1)[HALO:], _shift_down(uu, 2)[HALO:]


def _window_mean_minus(uu, row0, window):
    acc = uu
    span = 1
    while span < window:
        acc = acc + _shift_down(acc, span)
        span *= 2
    return acc[HALO:] * _inv_count(uu.shape[0] - HALO, row0, window) - uu[HALO:]


def _inv_count(rows, row0, window):
    t = row0 + lax.broadcasted_iota(jnp.int32, (rows, 1), 0)
    return 1.0 / jnp.minimum(t + 1, window).astype(F32)


def _whole(*shape):
    return pl.BlockSpec(shape, lambda i: (0,) * len(shape), pipeline_mode=pl.Buffered(1))


def _mixer_tile(i, x, g_ref, win_ref, mix_refs, carry_ref, proj_ref, hn_ref, is_conv):
    ts = x.shape[0]
    nsplit, _, e = proj_ref.shape
    gdim = e // N_POOL_GROUPS
    xh, _ = _rms_stats(x)
    hn = (xh * g_ref[...]).astype(BF16)
    hn_ref[...] = hn
    parts = []
    for k in range(nsplit):
        part = _dot(hn, win_ref[:, k * e:(k + 1) * e])
        proj_ref[k] = part.astype(BF16)
        parts.append(part)
    prev = carry_ref[...]
    if is_conv:
        b, c, v, z = parts
        w_ref, = mix_refs
        u = c * v
        u1, u2 = _conv_taps(u, prev)
        mixed = b * (w_ref[0:1, :] * u2 + w_ref[1:2, :] * u1 + w_ref[2:3, :] * u)
    else:
        u, z = parts
        wgrp_ref, sc_ref = mix_refs
        uu = jnp.concatenate([prev, u], axis=0)
        cols = []
        for gi, window in enumerate(POOL_WINDOWS):
            dg = _window_mean_minus(uu[:, gi * gdim:(gi + 1) * gdim], i * ts, window)
            cols.append(_dot(dg.astype(BF16), wgrp_ref[gi]))
        mixed = jnp.concatenate(cols, axis=1) * sc_ref[...]
    carry_ref[...] = u[ts - HALO:]
    return ((z * _sigmoid(z)) * mixed).astype(BF16)


def _out_ple_tile(x, o, wo_ref, pn_ref, wg_ref, p_ref, wp_ref, h1_ref, gl_ref, pp_ref):
    h1 = x + _dot(o, wo_ref[...])
    h1_ref[...] = h1
    xh1, _ = _rms_stats(h1)
    gl = _dot((xh1 * pn_ref[...]).astype(BF16), wg_ref[...])
    pp = _dot(p_ref[...].astype(BF16), wp_ref[...])
    gl_ref[...] = gl.astype(BF16)
    pp_ref[...] = pp.astype(BF16)
    return h1 + _sigmoid(gl) * pp


def _layer_fwd_halves(h, g, w_in, mixer, w_out, pn, w_gate, p_all, w_proj, layer):
    s, d = h.shape
    n = w_in.shape[1]
    e = w_out.shape[0]
    nsplit = n // e
    pdim = p_all.shape[2]
    is_conv = mixer[0] == "conv"
    ts = min(FWD_ROW_TILE, s)
    params = mixer[1:]
    row = lambda width: pl.BlockSpec((ts, width), lambda i: (i, 0))

    def mixer_body(*refs):
        h_ref, g_ref, win_ref = refs[:3]
        mix_refs = refs[3:3 + len(params)]
        proj_ref, hn_ref, o_ref, carry_ref = refs[3 + len(params):]
        i = pl.program_id(0)

        @pl.when(i == 0)
        def _():
            carry_ref[...] = jnp.zeros_like(carry_ref)

        o_ref[...] = _mixer_tile(i, h_ref[...], g_ref, win_ref, mix_refs, carry_ref, proj_ref, hn_ref, is_conv)

    proj, hn, o = pl.pallas_call(
        mixer_body, name="mixer_fwd",
        grid=(s // ts,),
        in_specs=[row(d), _whole(1, d), _whole(d, n)] + [_whole(*a.shape) for a in params],
        out_specs=[pl.BlockSpec((nsplit, ts, e), lambda i: (0, i, 0)), row(d), row(e)],
        out_shape=[jax.ShapeDtypeStruct((nsplit, s, e), BF16), jax.ShapeDtypeStruct((s, d), BF16),
                   jax.ShapeDtypeStruct((s, e), BF16)],
        scratch_shapes=[pltpu.VMEM((HALO, e), F32)],
        compiler_params=_cparams("arbitrary"),
    )(h, g, w_in, *params)

    def out_body(h_ref, o_ref, wo_ref, pn_ref, wg_ref, p_ref, wp_ref, h1_ref, h2_ref, gl_ref, pp_ref):
        h2_ref[...] = _out_ple_tile(h_ref[...], o_ref[...], wo_ref, pn_ref, wg_ref, p_ref, wp_ref,
                                    h1_ref, gl_ref, pp_ref)

    h1, h2, gl, pp = pl.pallas_call(
        out_body, name="out_ple_fwd",
        grid=(s // ts,),
        in_specs=[row(d), row(e), _whole(e, d), _whole(1, d), _whole(d, d),
                  pl.BlockSpec((None, ts, pdim), lambda i: (layer, i, 0)), _whole(pdim, d)],
        out_specs=[row(d), row(d), row(d), row(d)],
        out_shape=[jax.ShapeDtypeStruct((s, d), F32), jax.ShapeDtypeStruct((s, d), F32),
                   jax.ShapeDtypeStruct((s, d), BF16), jax.ShapeDtypeStruct((s, d), BF16)],
        compiler_params=_cparams("parallel"),
    )(h, o, w_out, pn, w_gate, p_all, w_proj)
    return h2, (proj, hn, o, h1, gl, pp)


def _layer_fwd(h, g, w_in, mixer, w_out, pn, w_gate, p_all, w_proj, layer, head=None):
    s, d = h.shape
    n = w_in.shape[1]
    e = w_out.shape[0]
    nsplit = n // e
    pdim = p_all.shape[2]
    is_conv = mixer[0] == "conv"
    ts = min(FWD_ROW_TILE, s)
    params = mixer[1:]
    head = tuple(head or ())

    def body(*refs):
        h_ref, g_ref, win_ref = refs[:3]
        mix_refs = refs[3:3 + len(params)]
        wo_ref, pn_ref, wg_ref, p_ref, wp_ref = refs[3 + len(params):8 + len(params)]
        head_refs = refs[8 + len(params):8 + len(params) + len(head)]
        proj_ref, hn_ref, o_ref, h1_ref, h2_ref, gl_ref, pp_ref = refs[8 + len(params) + len(head):][:7]
        carry_ref = refs[-1]
        i = pl.program_id(0)

        @pl.when(i == 0)
        def _():
            carry_ref[...] = jnp.zeros_like(carry_ref)

        x = h_ref[...]
        o = _mixer_tile(i, x, g_ref, win_ref, mix_refs, carry_ref, proj_ref, hn_ref, is_conv)
        o_ref[...] = o
        h2 = _out_ple_tile(x, o, wo_ref, pn_ref, wg_ref, p_ref, wp_ref, h1_ref, gl_ref, pp_ref)
        if not head:
            h2_ref[...] = h2
            return
        t_ref, gain_ref = head_refs
        loss_ref, dgain_ref = refs[-3], refs[-2]

        @pl.when(i == 0)
        def _():
            loss_ref[...] = jnp.zeros_like(loss_ref)
            dgain_ref[...] = jnp.zeros_like(dgain_ref)

        gain = gain_ref[...]
        yh, r = _rms_stats(h2)
        err = yh * gain - t_ref[...]
        loss_ref[...] += jnp.full(loss_ref.shape, (0.5 / d) * jnp.sum(err * err), F32)
        dy = err * (1.0 / d)
        dgain_ref[...] += jnp.sum(dy * yh, axis=0, keepdims=True)
        h2_ref[...] = _rms_bwd(dy, yh, r, gain)

    row = lambda width: pl.BlockSpec((ts, width), lambda i: (i, 0))
    mix_specs = [_whole(*a.shape) for a in params]
    head_specs = [row(d), _whole(1, d)] if head else []
    head_out_specs = [_whole(1, 128), _whole(1, d)] if head else []
    head_out_shape = [jax.ShapeDtypeStruct((1, 128), F32), jax.ShapeDtypeStruct((1, d), F32)] if head else []
    outs = pl.pallas_call(
        body, name="layer_fwd",
        grid=(s // ts,),
        in_specs=[row(d), _whole(1, d), _whole(d, n)] + mix_specs
        + [_whole(e, d), _whole(1, d), _whole(d, d),
           pl.BlockSpec((None, ts, pdim), lambda i: (layer, i, 0)), _whole(pdim, d)] + head_specs,
        out_specs=[pl.BlockSpec((nsplit, ts, e), lambda i: (0, i, 0)),
                   row(d), row(e), row(d), row(d), row(d), row(d)] + head_out_specs,
        out_shape=[jax.ShapeDtypeStruct((nsplit, s, e), BF16), jax.ShapeDtypeStruct((s, d), BF16),
                   jax.ShapeDtypeStruct((s, e), BF16), jax.ShapeDtypeStruct((s, d), F32),
                   jax.ShapeDtypeStruct((s, d), F32), jax.ShapeDtypeStruct((s, d), BF16),
                   jax.ShapeDtypeStruct((s, d), BF16)] + head_out_shape,
        scratch_shapes=[pltpu.VMEM((HALO, e), F32)],
        compiler_params=_cparams("arbitrary"),
    )(h, g, w_in, *params, w_out, pn, w_gate, p_all, w_proj, *head)
    proj, hn, o, h1, h2, gl, pp = outs[:7]
    return (h2, *outs[7:]) if head else h2, (proj, hn, o, h1, gl, pp)


def _out_ple_bwd(dh2, gl, pp, h1, p_all, o, pn, wgate, wout, layer):
    s, d = dh2.shape
    e = o.shape[1]
    pdim = p_all.shape[2]
    ts = min(BWD_ROW_TILE, s)
    last = s // ts - 1

    def body(dh2_ref, gl_ref, pp_ref, h1_ref, p_ref, o_ref, pn_ref, wg_ref, wo_ref,
             dh1_ref, do_ref, dwp_ref, dwg_ref, dwo_ref, dpn_ref, awp, awg, awo):
        i = pl.program_id(0)

        @pl.when(i == 0)
        def _():
            awp[...] = jnp.zeros_like(awp)
            awg[...] = jnp.zeros_like(awg)
            awo[...] = jnp.zeros_like(awo)
            dpn_ref[...] = jnp.zeros_like(dpn_ref)

        dh2 = dh2_ref[...]
        gate = _sigmoid(gl_ref[...].astype(F32))
        dpp = (dh2 * gate).astype(BF16)
        dgl = (dh2 * pp_ref[...].astype(F32) * gate * (1.0 - gate)).astype(BF16)
        xh, r = _rms_stats(h1_ref[...])
        pn = pn_ref[...]
        awp[...] += _dot_tn(p_ref[...].astype(BF16), dpp)
        awg[...] += _dot_tn((xh * pn).astype(BF16), dgl)
        dr = _dot_nt(dgl, wg_ref[...])
        dpn_ref[...] += jnp.sum(dr * xh, axis=0, keepdims=True)
        dh1 = dh2 + _rms_bwd(dr, xh, r, pn)
        dh1_ref[...] = dh1
        dh1b = dh1.astype(BF16)
        do_ref[...] = _dot_nt(dh1b, wo_ref[...]).astype(BF16)
        awo[...] += _dot_tn(o_ref[...], dh1b)

        @pl.when(i == last)
        def _():
            dwp_ref[...] = awp[...].astype(BF16)
            dwg_ref[...] = awg[...].astype(BF16)
            dwo_ref[...] = awo[...].astype(BF16)

    row = lambda width: pl.BlockSpec((ts, width), lambda i: (i, 0))
    return pl.pallas_call(
        body, name="out_ple_bwd",
        grid=(s // ts,),
        in_specs=[row(d), row(d), row(d), row(d),
                  pl.BlockSpec((None, ts, pdim), lambda i: (layer, i, 0)),
                  row(e), _whole(1, d), _whole(d, d), _whole(e, d)],
        out_specs=[row(d), row(e), _whole(pdim, d), _whole(d, d), _whole(e, d), _whole(1, d)],
        out_shape=[jax.ShapeDtypeStruct((s, d), F32), jax.ShapeDtypeStruct((s, e), BF16),
                   jax.ShapeDtypeStruct((pdim, d), BF16), jax.ShapeDtypeStruct((d, d), BF16),
                   jax.ShapeDtypeStruct((e, d), BF16), jax.ShapeDtypeStruct((1, d), F32)],
        scratch_shapes=[pltpu.VMEM((pdim, d), F32), pltpu.VMEM((d, d), F32), pltpu.VMEM((e, d), F32)],
        compiler_params=_cparams("arbitrary"),
    )(dh2, gl, pp, h1, p_all, o, pn, wgate, wout)


def _mixer_bwd(do, proj, hn, mixer, w_in, h, g, dh1):
    s, d = h.shape
    nsplit, _, e = proj.shape
    gdim = e // N_POOL_GROUPS
    is_conv = mixer[0] == "conv"
    ts = min(CONV_BWD_ROW_TILE if is_conv else POOL_BWD_ROW_TILE, s)
    nt = s // ts
    params = mixer[1:]
    n_mix_out = 1 if is_conv else 2

    def body(*refs):
        refs = list(refs)
        take = lambda n: [refs.pop(0) for _ in range(n)]
        do_ref, p_ref, ph_ref = take(3)
        mix_refs = take(len(params))
        win_ref, h_ref, g_ref, dh1_ref, hn_ref = take(5)
        dwin_ref, dh_ref, dg_ref = take(3)
        mix_out = take(n_mix_out)
        carry_ref, dp_ref, acc_ref = take(3)
        i = pl.program_id(0)
        tile = nt - 1 - i

        @pl.when(i == 0)
        def _():
            for ref in [carry_ref, dg_ref, acc_ref] + mix_out[-1:] + refs:
                ref[...] = jnp.zeros_like(ref)

        dof = do_ref[...].astype(F32)
        nxt = carry_ref[...]
        if is_conv:
            w_ref, = mix_refs
            dw_ref, = mix_out
            w0, w1, w2 = w_ref[0:1, :], w_ref[1:2, :], w_ref[2:3, :]
            b, c, v, z = [p_ref[k].astype(F32) for k in range(4)]
            u = c * v
            u_prev = jnp.where(tile == 0, 0.0, ph_ref[1].astype(F32) * ph_ref[2].astype(F32))
            u1, u2 = _conv_taps(u, u_prev)
            conv = w0 * u2 + w1 * u1 + w2 * u
            sig = _sigmoid(z)
            sz = z * sig
            dy = dof * sz
            dp_ref[3] = (dof * (b * conv) * (sig + sz * (1.0 - sig))).astype(BF16)
            dp_ref[0] = (dy * conv).astype(BF16)
            dconv = dy * b
            dw_ref[0:1, :] += jnp.sum(dconv * u2, axis=0, keepdims=True)
            dw_ref[1:2, :] += jnp.sum(dconv * u1, axis=0, keepdims=True)
            dw_ref[2:3, :] += jnp.sum(dconv * u, axis=0, keepdims=True)
            dcc = jnp.concatenate([dconv, nxt], axis=0)
            du = w2 * dconv + w1 * _shift_up(dcc, 1)[:ts] + w0 * _shift_up(dcc, 2)[:ts]
            carry_ref[...] = dconv[:HALO]
            dp_ref[1] = (du * v).astype(BF16)
            dp_ref[2] = (du * c).astype(BF16)
        else:
            wgrp_ref, sc_ref = mix_refs
            dsc_ref = mix_out[1]
            agrp, = refs
            u = p_ref[0].astype(F32)
            z = p_ref[1].astype(F32)
            u_prev = jnp.where(tile == 0, 0.0, ph_ref[0].astype(F32))
            uu = jnp.concatenate([u_prev, u], axis=0)
            sig = _sigmoid(z)
            sz = z * sig
            dm = dof * sz
            dsilu = dof * (sig + sz * (1.0 - sig))
            for gi, window in enumerate(POOL_WINDOWS):
                cols = slice(gi * gdim, (gi + 1) * gdim)
                w = wgrp_ref[gi]
                scale = sc_ref[:, cols]
                db = _window_mean_minus(uu[:, cols], tile * ts, window).astype(BF16)
                mr = _dot(db, w)
                dp_ref[1, :, cols] = (dsilu[:, cols] * (mr * scale)).astype(BF16)
                dmg = dm[:, cols]
                dmr = (dmg * scale).astype(BF16)
                agrp[gi] += _dot_tn(db, dmr)
                dsc_ref[:, cols] += jnp.sum(dmg * mr, axis=0, keepdims=True)
                dd = _dot_nt(dmr, w)
                ddq = dd * _inv_count(ts, tile * ts, window)
                acc = jnp.concatenate([ddq, nxt[:, cols]], axis=0)
                span = 1
                while span < window:
                    acc = acc + _shift_up(acc, span)
                    span *= 2
                carry_ref[:, cols] = ddq[:HALO]
                dp_ref[0, :, cols] = (acc[:ts] - dd).astype(BF16)

        dhn = _dot_nt(dp_ref[0], win_ref[:, 0:e])
        for k in range(1, nsplit):
            dhn += _dot_nt(dp_ref[k], win_ref[:, k * e:(k + 1) * e])
        xh, r = _rms_stats(h_ref[...])
        dg_ref[...] += jnp.sum(dhn * xh, axis=0, keepdims=True)
        dh_ref[...] = dh1_ref[...] + _rms_bwd(dhn, xh, r, g_ref[...])
        hn_tile = hn_ref[...]
        for k in range(nsplit):
            acc_ref[:, k * e:(k + 1) * e] += _dot_tn(hn_tile, dp_ref[k])

        @pl.when(i == nt - 1)
        def _():
            dwin_ref[...] = acc_ref[...].astype(BF16)
            if not is_conv:
                mix_out[0][...] = refs[0][...].astype(BF16)

    rev = lambda width: pl.BlockSpec((ts, width), lambda i: (nt - 1 - i, 0))
    halo_blocks = ts // HALO
    in_specs = [rev(e),
                pl.BlockSpec((nsplit, ts, e), lambda i: (0, nt - 1 - i, 0)),
                pl.BlockSpec((nsplit, HALO, e), lambda i: (0, jnp.maximum((nt - 1 - i) * halo_blocks - 1, 0), 0))]
    in_specs += [_whole(*a.shape) for a in params]
    in_specs += [_whole(d, nsplit * e), rev(d), _whole(1, d), rev(d), rev(d)]
    out_specs = [_whole(d, nsplit * e), rev(d), _whole(1, d)]
    out_shape = [jax.ShapeDtypeStruct((d, nsplit * e), BF16), jax.ShapeDtypeStruct((s, d), F32),
                 jax.ShapeDtypeStruct((1, d), F32)]
    scratch = [pltpu.VMEM((HALO, e), F32), pltpu.VMEM((nsplit, ts, e), BF16), pltpu.VMEM((d, nsplit * e), F32)]
    if is_conv:
        out_specs += [_whole(3, e)]
        out_shape += [jax.ShapeDtypeStruct((3, e), F32)]
    else:
        out_specs += [_whole(N_POOL_GROUPS, gdim, gdim), _whole(1, e)]
        out_shape += [jax.ShapeDtypeStruct((N_POOL_GROUPS, gdim, gdim), BF16), jax.ShapeDtypeStruct((1, e), F32)]
        scratch += [pltpu.VMEM((N_POOL_GROUPS, gdim, gdim), F32)]
    return pl.pallas_call(
        body, name="mixer_bwd",
        grid=(nt,),
        in_specs=in_specs, out_specs=out_specs, out_shape=out_shape, scratch_shapes=scratch,
        compiler_params=_cparams("arbitrary"),
    )(do, proj, proj, *params, w_in, h, g, dh1, hn)


def _forward_backward(xs, ps, target, full, conv_w, scale_w, norm_mix, ple_norm, final_norm, exchange):
    depth = len(full)
    row = lambda a, i: a[i][None, :]
    mixer_of = lambda i: ("conv", conv_w[i // 2]) if i % 2 == 0 else ("pool", full[i]["w_grp"], row(scale_w, i // 2))

    saved = []
    h = xs
    for i in range(depth):
        w = full[i]
        head = (target, final_norm[None, :]) if i == depth - 1 else None
        args = (h, row(norm_mix, i), w["w_in"], mixer_of(i), w["w_out"], row(ple_norm, i), w["gate"], ps, w["proj"], i)
        h_next, acts = _layer_fwd_halves(*args) if i == 0 and not head else _layer_fwd(*args, head)
        saved.append((h, *acts))
        h = h_next
    dh, loss_row, d_final = h

    d_norm, d_ple_norm, d_conv, d_scale, sent = [None] * depth, [None] * depth, [], [], [None] * depth
    for i in reversed(range(depth)):
        w = full[i]
        h_in, proj, hn, o, h1, gl, pp = saved[i]
        g = {}
        dh1, do, g["proj"], g["gate"], g["w_out"], d_ple_norm[i] = _out_ple_bwd(
            dh, gl, pp, h1, ps, o, row(ple_norm, i), w["gate"], w["w_out"], i)
        g["w_in"], dh, d_norm[i], *mixer_grads = _mixer_bwd(
            do, proj, hn, mixer_of(i), w["w_in"], h_in, row(norm_mix, i), dh1)
        if i % 2 == 0:
            d_conv.insert(0, mixer_grads[0])
        else:
            g["w_grp"] = mixer_grads[0]
            d_scale.insert(0, mixer_grads[1])
        sent[i] = exchange(i, g)
    return loss_row, dh, sent, (d_norm, d_ple_norm, d_final, d_conv, d_scale)


VMEM_SPEC = pl.BlockSpec(memory_space=pltpu.VMEM)

FLIPS = [(fx, fy, fc) for fx in (0, 1) for fy in (0, 1) for fc in (0, 1)][1:]
SHARD_AXIS = {"w_in": 1, "w_out": 0, "w_grp": 1, "gate": 0, "proj": 1}


def _my_place():
    return lax.axis_index("x"), lax.axis_index("y"), lax.axis_index("c")


def _position(place):
    x, y, c = place
    return 4 * x + 2 * y + c


def _flip(place, flips):
    return tuple(1 - v if f else v for v, f in zip(place, flips))


def _shard_of(ref, axis, pos, n):
    idx = [slice(None)] * len(ref.shape)
    idx[axis] = pl.ds(pl.multiple_of(pos * n, n), n)
    return ref.at[tuple(idx)]


def _sequencer_mesh():
    return plsc.ScalarSubcoreMesh(axis_name="sequencer", num_cores=1)


def _handshake(peers):
    barrier = pltpu.get_barrier_semaphore()
    for peer in peers:
        pl.semaphore_signal(barrier, inc=1, device_id=peer, device_id_type=MESH)
    pl.semaphore_wait(barrier, len(peers))


def _all_gather_layer(shards, collective_id):
    names = list(shards)
    nt = len(names)
    axes = [SHARD_AXIS[k] for k in names]
    widths = [shards[k].shape[SHARD_AXIS[k]] for k in names]

    def full_shape(k):
        shp = list(shards[k].shape)
        shp[SHARD_AXIS[k]] *= N_DEV
        return tuple(shp)

    def body(*refs):
        ins, outs = refs[:nt], refs[nt:2 * nt]
        send_sems, recv_sems, local_sem = refs[2 * nt:]
        me = _my_place()
        x, y, c = me
        sibling = (x, y, 1 - c)
        flip = lambda v, f: v + f - 2 * v * f
        neighbour = lambda core, fx: (flip(x, fx), flip(y, 1 - fx), core)
        first, second = neighbour(c, c), neighbour(c, 1 - c)
        diagonal = (1 - x, 1 - y, c)
        _handshake([sibling, first, second])

        def block(t, place):
            return _shard_of(outs[t], axes[t], _position(place), widths[t])

        def copy(t, k, place, to, src=None):
            return pltpu.make_async_remote_copy(
                src_ref=block(t, place) if src is None else src, dst_ref=block(t, place),
                send_sem=send_sems.at[k], recv_sem=recv_sems.at[k], device_id=to, device_id_type=MESH)

        everything = lambda make: [make(t) for t in range(nt)]
        mine = everything(lambda t: pltpu.make_async_copy(ins[t], block(t, me), local_sem))
        sent = (everything(lambda t: copy(t, 1, me, first, src=ins[t]))
                + everything(lambda t: copy(t, 2, me, second, src=ins[t]))
                + everything(lambda t: copy(t, 0, me, sibling, src=ins[t])))
        for cp in mine + sent:
            cp.start()
        for k, arrived, onward in ((1, first, second), (2, second, None), (3, diagonal, None)):
            for t in range(nt):
                copy(t, k, arrived, me).wait_recv()
            if onward is not None:
                sent += everything(lambda t: copy(t, 3, arrived, onward))
            sent += everything(lambda t: copy(t, 3 + k, arrived, sibling))
            for cp in sent[-nt * (1 + (onward is not None)):]:
                cp.start()
        for k, place in ((0, sibling), (4, neighbour(1 - c, 1 - c)), (5, neighbour(1 - c, c)), (6, (1 - x, 1 - y, 1 - c))):
            for t in range(nt):
                copy(t, k, place, me).wait_recv()
        for cp in sent:
            cp.wait_send()
        for cp in mine:
            cp.wait()

    outs = pl.kernel(
        body, name=f"all_gather_layer_{collective_id}",
        out_type=[jax.ShapeDtypeStruct(full_shape(k), shards[k].dtype) for k in names],
        mesh=_sequencer_mesh(),
        scratch_types=[pltpu.SemaphoreType.DMA((7,)), pltpu.SemaphoreType.DMA((7,)), pltpu.SemaphoreType.DMA],
        compiler_params=pltpu.CompilerParams(collective_id=collective_id),
    )(*[shards[k] for k in names])
    return dict(zip(names, outs))


def _exchange_layer(grads, collective_id):
    names = list(grads)
    nt = len(names)
    axes = [SHARD_AXIS[k] for k in names]
    widths = [grads[k].shape[SHARD_AXIS[k]] // N_DEV for k in names]

    def slot_shape(t):
        shp = list(grads[names[t]].shape)
        shp[axes[t]] = widths[t]
        return (N_DEV, *shp)

    def body(*refs):
        ins, outs = refs[:nt], refs[nt:2 * nt]
        send_sems, recv_sems, local_sem = refs[2 * nt:]
        me = _my_place()
        mine = _position(me)
        _handshake([_flip(me, flips) for flips in FLIPS])
        local = [pltpu.make_async_copy(_shard_of(ins[t], axes[t], mine, widths[t]), outs[t].at[mine], local_sem)
                 for t in range(nt)]
        for cp in local:
            cp.start()
        copies = []
        for k, flips in enumerate(FLIPS):
            peer = _flip(me, flips)
            for t in range(nt):
                cp = pltpu.make_async_remote_copy(
                    src_ref=_shard_of(ins[t], axes[t], _position(peer), widths[t]), dst_ref=outs[t].at[mine],
                    send_sem=send_sems.at[k], recv_sem=recv_sems.at[k], device_id=peer, device_id_type=MESH)
                cp.start()
                copies.append(cp)
        for cp in copies:
            cp.wait()
        for cp in local:
            cp.wait()

    outs = pl.kernel(
        body, name=f"exchange_layer_{collective_id}",
        out_type=[jax.ShapeDtypeStruct(slot_shape(t), BF16) for t in range(nt)],
        mesh=_sequencer_mesh(),
        scratch_types=[pltpu.SemaphoreType.DMA((7,)), pltpu.SemaphoreType.DMA((7,)), pltpu.SemaphoreType.DMA],
        compiler_params=pltpu.CompilerParams(collective_id=collective_id),
    )(*[grads[k] for k in names])
    return dict(zip(names, outs))


def _gather_rows(buf, reduce):
    r, c = buf.shape

    def body(in_ref, out_ref, *scratch):
        if reduce:
            all_ref, send_sems, recv_sems = scratch
        else:
            all_ref = out_ref
            send_sems, recv_sems = scratch
        me = _my_place()
        all_ref[_position(me)] = in_ref[...]
        copies = []
        for k, flips in enumerate(FLIPS):
            cp = pltpu.make_async_remote_copy(
                src_ref=in_ref, dst_ref=all_ref.at[_position(me)],
                send_sem=send_sems.at[k], recv_sem=recv_sems.at[k], device_id=_flip(me, flips), device_id_type=MESH)
            cp.start()
            copies.append(cp)
        for cp in copies:
            cp.wait()
        if reduce:
            total = all_ref[0]
            for j in range(1, N_DEV):
                total = total + all_ref[j]
            out_ref[...] = total

    return pl.pallas_call(
        body, name="sum_rows" if reduce else "gather_rows",
        in_specs=[VMEM_SPEC], out_specs=VMEM_SPEC,
        out_shape=jax.ShapeDtypeStruct((r, c) if reduce else (N_DEV, r, c), F32),
        scratch_shapes=([pltpu.VMEM((N_DEV, r, c), F32)] if reduce else [])
        + [pltpu.SemaphoreType.DMA((7,)), pltpu.SemaphoreType.DMA((7,))],
    )(buf)


def _adamw_math(w, g, m, v):
    m = ADAM_B1 * m + (1.0 - ADAM_B1) * g
    v = ADAM_B2 * v + (1.0 - ADAM_B2) * (g * g)
    m_hat = m / (1.0 - ADAM_B1 ** ADAM_STEP)
    v_hat = v / (1.0 - ADAM_B2 ** ADAM_STEP)
    delta = -ADAM_LR * (m_hat / (jnp.sqrt(v_hat) + ADAM_EPS) + ADAM_WD * w)
    return delta, m, v


def _run_behind(x, token):
    def body(x_ref, token_ref, out_ref):
        out_ref[...] = jnp.zeros_like(out_ref)

    any_spec = pl.BlockSpec(memory_space=pl.ANY)
    return pl.pallas_call(
        body, name="run_behind",
        in_specs=[any_spec, any_spec], out_specs=VMEM_SPEC,
        out_shape=jax.ShapeDtypeStruct((8, 128), F32),
    )(x, token)


def _adamw_pieces(pieces, w, m, v, after, layer=None, into=()):
    shape = w.shape
    nl = shape[0]
    cols = shape[-1]
    rows = w.size // (nl * cols)
    first, nh = (0, nl) if layer is None else (layer, 1)
    tr = min(ADAMW_BLOCK_ROWS // nh, rows // 2)
    flat3 = lambda a: a.reshape(nl, rows, cols)
    into = [flat3(a) for a in into]

    def body(*refs):
        p_refs = refs[:nh]
        w_ref, m_ref, v_ref = refs[nh:nh + 3]
        g_ref, d_ref, nm_ref, nv_ref = refs[-4:]
        for l in range(nh):
            g = p_refs[l][0].astype(F32)
            for j in range(1, N_DEV):
                g = g + p_refs[l][j].astype(F32)
            g_ref[l] = g
            d_ref[l], nm_ref[l], nv_ref[l] = _adamw_math(w_ref[l], g, m_ref[l], v_ref[l])

    blk = pl.BlockSpec((nh, tr, cols), lambda i: (first, i, 0))
    any_spec = pl.BlockSpec(memory_space=pl.ANY)
    outs = pl.pallas_call(
        body, name="adamw_pieces",
        grid=(rows // tr,),
        in_specs=[pl.BlockSpec((N_DEV, tr, cols), lambda i: (0, i, 0))] * nh
        + [blk, blk, blk, any_spec] + [any_spec] * len(into),
        out_specs=[blk] * 4,
        out_shape=[jax.ShapeDtypeStruct((nl, rows, cols), F32)] * 4,
        input_output_aliases={nh + 4 + k: k for k in range(len(into))},
        compiler_params=_cparams("parallel"),
    )(*[a.reshape(N_DEV, rows, cols) for a in pieces], flat3(w), flat3(m), flat3(v), after, *into)
    return [a.reshape(shape) for a in outs]


def _adamw_small(g, w, m, v):
    shape = w.shape
    two = lambda a: a.reshape(-1, shape[-1])

    def body(g_ref, w_ref, m_ref, v_ref, d_ref, nm_ref, nv_ref):
        d_ref[...], nm_ref[...], nv_ref[...] = _adamw_math(w_ref[...], g_ref[...], m_ref[...], v_ref[...])

    outs = pl.pallas_call(
        body, name="adamw_small",
        in_specs=[VMEM_SPEC] * 4, out_specs=[VMEM_SPEC] * 3,
        out_shape=[jax.ShapeDtypeStruct(two(w).shape, F32)] * 3,
    )(two(g), two(w), two(m), two(v))
    return [a.reshape(shape) for a in outs]


WEIGHTS = ("norm_mix", "a_w_in", "a_w_conv", "a_w_out", "b_w_in", "b_w_grp", "b_scale", "b_w_out",
           "ple_norm", "ple_w_gate", "ple_w_proj", "final_norm")
SMALL_ROWS = 24
GATHER_ID = 0
EXCHANGE_ID = 4
LAST_EXCHANGE_ID = 8
REST_GATHER_ID = 9


def kernel(x, p, norm_mix, a_w_in, a_w_conv, a_w_out, b_w_in, b_w_grp, b_scale, b_w_out, ple_norm, ple_w_gate, ple_w_proj, final_norm, loss_target, m_norm_mix, m_a_w_in, m_a_w_conv, m_a_w_out, m_b_w_in, m_b_w_grp, m_b_scale, m_b_w_out, m_ple_norm, m_ple_w_gate, m_ple_w_proj, m_final_norm, v_norm_mix, v_a_w_in, v_a_w_conv, v_a_w_out, v_b_w_in, v_b_w_grp, v_b_scale, v_b_w_out, v_ple_norm, v_ple_w_gate, v_ple_w_proj, v_final_norm):
    wts = dict(norm_mix=norm_mix, a_w_in=a_w_in, a_w_conv=a_w_conv, a_w_out=a_w_out, b_w_in=b_w_in, b_w_grp=b_w_grp,
               b_scale=b_scale, b_w_out=b_w_out, ple_norm=ple_norm, ple_w_gate=ple_w_gate, ple_w_proj=ple_w_proj,
               final_norm=final_norm)
    mom = dict(norm_mix=m_norm_mix, a_w_in=m_a_w_in, a_w_conv=m_a_w_conv, a_w_out=m_a_w_out, b_w_in=m_b_w_in,
               b_w_grp=m_b_w_grp, b_scale=m_b_scale, b_w_out=m_b_w_out, ple_norm=m_ple_norm, ple_w_gate=m_ple_w_gate,
               ple_w_proj=m_ple_w_proj, final_norm=m_final_norm)
    var = dict(norm_mix=v_norm_mix, a_w_in=v_a_w_in, a_w_conv=v_a_w_conv, a_w_out=v_a_w_out, b_w_in=v_b_w_in,
               b_w_grp=v_b_w_grp, b_scale=v_b_scale, b_w_out=v_b_w_out, ple_norm=v_ple_norm, ple_w_gate=v_ple_w_gate,
               ple_w_proj=v_ple_w_proj, final_norm=v_final_norm)
    d = x.shape[2]
    depth = norm_mix.shape[0]
    n_a, n_b = a_w_conv.shape[0], b_scale.shape[0]
    cw = a_w_conv.shape[2]
    pos = _position(_my_place())

    def layer_matrices(i):
        j = i // 2
        mixer = {"w_in": ("a_w_in", j), "w_out": ("a_w_out", j)} if i % 2 == 0 else \
                {"w_in": ("b_w_in", j), "w_grp": ("b_w_grp", j), "w_out": ("b_w_out", j)}
        return {**mixer, "gate": ("ple_w_gate", i), "proj": ("ple_w_proj", i)}

    def gathered(i):
        shards = {k: wts[name][idx].astype(BF16) for k, (name, idx) in layer_matrices(i).items()}
        if i > 0:
            return _all_gather_layer(shards, GATHER_ID + i)
        first = {"w_in": shards.pop("w_in")}
        return {**_all_gather_layer(first, GATHER_ID), **_all_gather_layer(shards, REST_GATHER_ID)}

    full = [gathered(i) for i in range(depth)]
    vec_rows = jnp.concatenate([a_w_conv.reshape(-1, cw), b_scale], axis=0)
    vecs = _gather_rows(vec_rows, reduce=False)
    n_conv = 3 * n_a
    conv_w = vecs[:, :n_conv].transpose(1, 0, 2).reshape(n_a, 3, N_DEV * cw)
    scale_w = vecs[:, n_conv:].transpose(1, 0, 2).reshape(n_b, N_DEV * cw)

    def exchange(i, g):
        if i > 0:
            return _exchange_layer(g, EXCHANGE_ID + i)
        early = {k: a for k, a in g.items() if k != "w_in"}
        return {**_exchange_layer(early, EXCHANGE_ID), **_exchange_layer({"w_in": g["w_in"]}, LAST_EXCHANGE_ID)}

    loss_row, dx, sent, (d_norm, d_ple_norm, d_final, d_conv, d_scale) = _forward_backward(
        x[0], p[:, 0], loss_target[0], full, conv_w, scale_w, norm_mix, ple_norm, final_norm, exchange)
    pieces = {name: [None] * wts[name].shape[0] for name in WEIGHTS if wts[name].ndim >= 3 and name != "a_w_conv"}
    for i in range(depth):
        for k, (name, idx) in layer_matrices(i).items():
            pieces[name][idx] = sent[i][k]

    pad = lambda a: jnp.pad(a, ((0, 0), (0, d - a.shape[1])))
    small = jnp.concatenate(d_norm + d_ple_norm + [d_final] + d_conv + d_scale + [pad(loss_row)], axis=0)
    small = jnp.pad(small, ((0, SMALL_ROWS - small.shape[0]), (0, 0)))
    total = _gather_rows(small, reduce=True)
    o = 0
    gsum = {}
    gsum["norm_mix"] = total[o:o + depth]; o += depth
    gsum["ple_norm"] = total[o:o + depth]; o += depth
    gsum["final_norm"] = total[o]; o += 1
    conv_full = total[o:o + n_conv].reshape(n_a, 3, d); o += n_conv
    scale_full = total[o:o + n_b]; o += n_b
    loss = total[o, 0]
    gsum["a_w_conv"] = lax.dynamic_slice_in_dim(conv_full, pos * cw, cw, axis=2)
    gsum["b_scale"] = lax.dynamic_slice_in_dim(scale_full, pos * cw, cw, axis=1)

    token = total
    for i in reversed(range(depth)):
        token = _run_behind(sent[i]["w_out"], token)
    last_token = _run_behind(sent[0]["w_in"], token)
    grad, delta, new_m, new_v = {}, {}, {}, {}
    for k in sorted(WEIGHTS, key=lambda name: name == "a_w_in"):
        if k == "a_w_in":
            upper = [None] * 4
            for j in reversed(range(1, n_a)):
                upper = _adamw_pieces([pieces[k][j]], wts[k], mom[k], var[k], token, j, [a for a in upper if a is not None])
            grad[k], delta[k], new_m[k], new_v[k] = _adamw_pieces(
                [pieces[k][0]], wts[k], mom[k], var[k], last_token, 0, upper)
        elif k in pieces:
            grad[k], delta[k], new_m[k], new_v[k] = _adamw_pieces(pieces[k], wts[k], mom[k], var[k], token)
        else:
            grad[k] = gsum[k]
            delta[k], new_m[k], new_v[k] = _adamw_small(gsum[k], wts[k], mom[k], var[k])
    return (loss, dx[None], *[grad[k] for k in WEIGHTS], *[delta[k] for k in WEIGHTS],
            *[new_m[k] for k in WEIGHTS], *[new_v[k] for k in WEIGHTS])
```
